```python
import math
import jax, jax.numpy as jnp
from jax import lax
import numpy as np

D_MODEL = 1024
BATCH = 2
SEQ = 8192
DEPTH = 1

SSM_WIDTH = D_MODEL // 2
SSM_GROUP = 16
SSM_GROUPS = SSM_WIDTH // SSM_GROUP
SSM_STATE = 64
DT_MIN = 1e-3
DT_MAX = 1e-1
LAMBDA_RE_MAX = -1e-4
HGRN_WIDTH = D_MODEL // 2
HGRN_HEAD_DIM = 128
HGRN_HEADS = HGRN_WIDTH // HGRN_HEAD_DIM
HGRN_CHUNK = 64
NORM_EPS = 1e-6

IN_WIDTH = 2 * SSM_WIDTH + 5 * HGRN_WIDTH + 2 * D_MODEL
SPLIT_POINTS = (
    SSM_WIDTH,
    2 * SSM_WIDTH,
    2 * SSM_WIDTH + HGRN_WIDTH,
    2 * SSM_WIDTH + 2 * HGRN_WIDTH,
    2 * SSM_WIDTH + 3 * HGRN_WIDTH,
    2 * SSM_WIDTH + 4 * HGRN_WIDTH,
    2 * SSM_WIDTH + 5 * HGRN_WIDTH,
    2 * SSM_WIDTH + 5 * HGRN_WIDTH + D_MODEL,
)

kernel_name = 'hybrid_s5_hgrn2_gated_block'


def rms_norm(x, w):
    xf = x.astype(jnp.float32)
    y = xf * lax.rsqrt(jnp.mean(xf * xf, axis=-1, keepdims=True) + NORM_EPS)
    return (y * w.astype(jnp.float32)).astype(x.dtype)


def s5_branch(u, z, lam_re, lam_im, b_re, b_im, c_re, c_im, d, log_dt, w_glu, b_glu):
    f32 = jnp.float32
    bsz, seqlen, _ = u.shape
    ug = u.astype(f32).reshape(bsz, seqlen, SSM_GROUPS, SSM_GROUP)
    lr = jnp.minimum(lam_re.astype(f32), LAMBDA_RE_MAX)
    li = lam_im.astype(f32)
    dt = jnp.exp(log_dt.astype(f32))[:, None]
    mag = jnp.exp(lr * dt)
    ab_re = mag * jnp.cos(li * dt)
    ab_im = mag * jnp.sin(li * dt)
    den = lr * lr + li * li
    nr = ab_re - 1.0
    coef_re = (nr * lr + ab_im * li) / den
    coef_im = (ab_im * lr - nr * li) / den
    br = b_re.astype(f32)
    bi = b_im.astype(f32)
    bb_re = coef_re[..., None] * br - coef_im[..., None] * bi
    bb_im = coef_re[..., None] * bi + coef_im[..., None] * br
    bu_re = jnp.einsum('blgp,gnp->blgn', ug, bb_re)
    bu_im = jnp.einsum('blgp,gnp->blgn', ug, bb_im)
    a_re = jnp.broadcast_to(ab_re, bu_re.shape)
    a_im = jnp.broadcast_to(ab_im, bu_im.shape)

    def combine(e1, e2):
        a1r, a1i, b1r, b1i = e1
        a2r, a2i, b2r, b2i = e2
        return (a2r * a1r - a2i * a1i,
                a2r * a1i + a2i * a1r,
                a2r * b1r - a2i * b1i + b2r,
                a2r * b1i + a2i * b1r + b2i)

    _, _, h_re, h_im = lax.associative_scan(combine, (a_re, a_im, bu_re, bu_im), axis=1)
    y = (jnp.einsum('blgn,gpn->blgp', h_re, c_re.astype(f32))
         - jnp.einsum('blgn,gpn->blgp', h_im, c_im.astype(f32))
         + d.astype(f32) * ug)
    y = jax.nn.gelu(y.reshape(bsz, seqlen, SSM_WIDTH))
    y = y * jax.nn.sigmoid(y @ w_glu.astype(f32) + b_glu.astype(f32))
    return y * jax.nn.silu(z.astype(f32))


def hgrn2_branch(q, f, i, og, z, lower_bound, head_norm_w):
    f32 = jnp.float32
    bsz, seqlen, _ = q.shape
    n_chunks = seqlen // HGRN_CHUNK

    def to_chunks(t):
        return t.reshape(bsz, n_chunks, HGRN_CHUNK, HGRN_HEADS, HGRN_HEAD_DIM).transpose(1, 0, 3, 2, 4)

    qf = jax.nn.silu(q.astype(f32))
    forget = lower_bound + (1.0 - lower_bound) * jax.nn.sigmoid(f.astype(f32))
    log_f = jnp.log(forget)
    key = 1.0 - forget
    causal = jnp.tril(jnp.ones((HGRN_CHUNK, HGRN_CHUNK), dtype=bool))[:, :, None]

    def step(state, inp):
        qc, kc, vc, gc = inp
        cum = jnp.cumsum(gc, axis=2)
        o_inter = jnp.einsum('bhtk,bhkv->bhtv', qc * jnp.exp(cum), state)
        diff = cum[:, :, :, None, :] - cum[:, :, None, :, :]
        decay = jnp.exp(jnp.where(causal, diff, -jnp.inf))
        scores = jnp.einsum('bhtk,bhsk,bhtsk->bhts', qc, kc, decay)
        o_intra = jnp.einsum('bhts,bhsv->bhtv', scores, vc)
        last = cum[:, :, -1:, :]
        new_state = (jnp.exp(last[:, :, 0, :])[..., None] * state
                     + jnp.einsum('bhsk,bhsv->bhkv', kc * jnp.exp(last - cum), vc))
        return new_state, o_inter + o_intra

    s0 = jnp.zeros((bsz, HGRN_HEADS, HGRN_HEAD_DIM, HGRN_HEAD_DIM), f32)
    _, o = lax.scan(step, s0, (to_chunks(qf), to_chunks(key), to_chunks(i.astype(f32)), to_chunks(log_f)))
    o = o.transpose(1, 0, 3, 2, 4).reshape(bsz, seqlen, HGRN_HEADS, HGRN_HEAD_DIM)
    o = o * jax.nn.sigmoid(og.astype(f32)).reshape(bsz, seqlen, HGRN_HEADS, HGRN_HEAD_DIM)
    o = o * lax.rsqrt(jnp.mean(o * o, axis=-1, keepdims=True) + NORM_EPS)
    o = o * head_norm_w.astype(f32).reshape(HGRN_HEADS, HGRN_HEAD_DIM)
    o = o.reshape(bsz, seqlen, HGRN_WIDTH)
    return o * jax.nn.silu(z.astype(f32))


def setup_inputs(seed: int = 0) -> dict:
    key = jax.random.key(seed)
    ks = jax.random.split(key, 24)
    nrm = jax.random.normal
    f32 = jnp.float32
    n_idx = jnp.arange(SSM_STATE, dtype=f32)
    x = nrm(ks[0], (BATCH, SEQ, D_MODEL), f32)
    norm_w = 1.0 + 0.01 * nrm(ks[1], (DEPTH, D_MODEL), f32)
    w_in = nrm(ks[2], (DEPTH, D_MODEL, IN_WIDTH), f32) * D_MODEL ** -0.5
    ssm_lambda_re = -0.5 + 0.01 * nrm(ks[3], (DEPTH, SSM_GROUPS, SSM_STATE), f32)
    ssm_lambda_im = math.pi * n_idx + 0.01 * nrm(ks[4], (DEPTH, SSM_GROUPS, SSM_STATE), f32)
    ssm_b_re = nrm(ks[5], (DEPTH, SSM_GROUPS, SSM_STATE, SSM_GROUP), f32) * (2 * SSM_GROUP) ** -0.5
    ssm_b_im = nrm(ks[6], (DEPTH, SSM_GROUPS, SSM_STATE, SSM_GROUP), f32) * (2 * SSM_GROUP) ** -0.5
    ssm_c_re = nrm(ks[7], (DEPTH, SSM_GROUPS, SSM_GROUP, SSM_STATE), f32) * (2 * SSM_STATE) ** -0.5
    ssm_c_im = nrm(ks[8], (DEPTH, SSM_GROUPS, SSM_GROUP, SSM_STATE), f32) * (2 * SSM_STATE) ** -0.5
    ssm_d = nrm(ks[9], (DEPTH, SSM_GROUPS, SSM_GROUP), f32)
    ssm_log_dt = jax.random.uniform(ks[10], (DEPTH, SSM_GROUPS), f32,
                                    minval=math.log(DT_MIN), maxval=math.log(DT_MAX))
    ssm_w_glu = nrm(ks[11], (DEPTH, SSM_WIDTH, SSM_WIDTH), f32) * SSM_WIDTH ** -0.5
    ssm_b_glu = 0.01 * nrm(ks[12], (DEPTH, SSM_WIDTH), f32)
    hgrn_lb_logits = 0.1 * nrm(ks[13], (DEPTH + 1, HGRN_WIDTH), f32)
    hgrn_norm_w = 1.0 + 0.01 * nrm(ks[14], (DEPTH, HGRN_WIDTH), f32)
    w_proj_a = nrm(ks[15], (DEPTH, SSM_WIDTH, D_MODEL), f32) * SSM_WIDTH ** -0.5
    w_proj_b = nrm(ks[16], (DEPTH, HGRN_WIDTH, D_MODEL), f32) * HGRN_WIDTH ** -0.5
    w_out = nrm(ks[17], (DEPTH, D_MODEL, D_MODEL), f32) * D_MODEL ** -0.5
    final_norm_w = 1.0 + 0.01 * nrm(ks[18], (D_MODEL,), f32)
    return {'x': x, 'norm_w': norm_w, 'w_in': w_in,
            'ssm_lambda_re': ssm_lambda_re, 'ssm_lambda_im': ssm_lambda_im,
            'ssm_b_re': ssm_b_re, 'ssm_b_im': ssm_b_im,
            'ssm_c_re': ssm_c_re, 'ssm_c_im': ssm_c_im,
            'ssm_d': ssm_d, 'ssm_log_dt': ssm_log_dt,
            'ssm_w_glu': ssm_w_glu, 'ssm_b_glu': ssm_b_glu,
            'hgrn_lb_logits': hgrn_lb_logits, 'hgrn_norm_w': hgrn_norm_w,
            'w_proj_a': w_proj_a, 'w_proj_b': w_proj_b, 'w_out': w_out,
            'final_norm_w': final_norm_w}


def reference(x, norm_w, w_in, ssm_lambda_re, ssm_lambda_im, ssm_b_re, ssm_b_im,
              ssm_c_re, ssm_c_im, ssm_d, ssm_log_dt, ssm_w_glu, ssm_b_glu,
              hgrn_lb_logits, hgrn_norm_w, w_proj_a, w_proj_b, w_out, final_norm_w):
    lower_bounds = jnp.cumsum(jax.nn.softmax(hgrn_lb_logits.astype(jnp.float32), axis=0), axis=0)
    h = x
    for layer in range(DEPTH):
        xn = rms_norm(h, norm_w[layer])
        proj = xn @ w_in[layer]
        u_a, z_a, q_b, f_b, i_b, og_b, z_b, g_a, g_b = jnp.split(proj, SPLIT_POINTS, axis=-1)
        y_a = s5_branch(u_a, z_a, ssm_lambda_re[layer], ssm_lambda_im[layer],
                        ssm_b_re[layer], ssm_b_im[layer], ssm_c_re[layer], ssm_c_im[layer],
                        ssm_d[layer], ssm_log_dt[layer], ssm_w_glu[layer], ssm_b_glu[layer])
        y_b = hgrn2_branch(q_b, f_b, i_b, og_b, z_b, lower_bounds[layer], hgrn_norm_w[layer])
        merged = (jax.nn.sigmoid(g_a.astype(jnp.float32)) * (y_a @ w_proj_a[layer])
                  + jax.nn.sigmoid(g_b.astype(jnp.float32)) * (y_b @ w_proj_b[layer]))
        h = h + (merged @ w_out[layer]).astype(h.dtype)
    return rms_norm(h, final_norm_w)
```

```python
import jax
import jax.numpy as jnp
from jax import lax
from jax.experimental import pallas as pl
from jax.experimental.pallas import tpu as pltpu

F32 = jnp.float32
BF16 = jnp.bfloat16
HIGHEST = lax.Precision.HIGHEST

D_MODEL = 1024
SSM_WIDTH = 512
SSM_GROUP = 16
SSM_GROUPS = 32
SSM_STATE = 64
HGRN_WIDTH = 512
HEAD_DIM = 128
HEADS = 4
MAIN_WIDTH = 2 * SSM_WIDTH + 5 * HGRN_WIDTH
GATE_WIDTH = 2 * D_MODEL
NORM_EPS = 1e-6
LAMBDA_RE_MAX = -1e-4

LANES = 128
OCT = LANES // SSM_GROUP
N_OCT = SSM_GROUPS // OCT
OCT_STATE = OCT * SSM_STATE

TM_IN = 256
S5_Q = 8
S5_TB = 1024
S5_NCH = S5_TB // S5_Q
S5_PT_ROWS = S5_NCH + 8
HG_TB = 512
HG_CH = 64
HG_SUB = 16
TM_OUT = 512

VMEM_LIMIT = 56 * 1024 * 1024


def _sigmoid(x):
    return jax.nn.sigmoid(x)


def _const_spec(shape):
    nd = len(shape)
    return pl.BlockSpec(shape, lambda *_: (0,) * nd, pipeline_mode=pl.Buffered(1))


def _inproj_kernel(x_ref, nw_ref, wm_ref, wg_ref, om_ref, og_ref):
    x = x_ref[...]
    ms = jnp.mean(x * x, axis=-1, keepdims=True)
    xn = (x * lax.rsqrt(ms + NORM_EPS) * nw_ref[...]).astype(BF16)
    om_ref[...] = jnp.dot(xn, wm_ref[...], preferred_element_type=F32)
    og_ref[...] = jnp.dot(xn, wg_ref[...], preferred_element_type=F32)


def _inproj(x2, norm_w, w_main, w_gate):
    t = x2.shape[0]
    return pl.pallas_call(
        _inproj_kernel,
        grid=(t // TM_IN,),
        in_specs=[
            pl.BlockSpec((TM_IN, D_MODEL), lambda i: (i, 0)),
            _const_spec((1, D_MODEL)),
            _const_spec((D_MODEL, MAIN_WIDTH)),
            _const_spec((D_MODEL, GATE_WIDTH)),
        ],
        out_specs=[
            pl.BlockSpec((TM_IN, MAIN_WIDTH), lambda i: (i, 0)),
            pl.BlockSpec((TM_IN, GATE_WIDTH), lambda i: (i, 0)),
        ],
        out_shape=[
            jax.ShapeDtypeStruct((t, MAIN_WIDTH), F32),
            jax.ShapeDtypeStruct((t, GATE_WIDTH), F32),
        ],
        compiler_params=pltpu.CompilerParams(
            dimension_semantics=("arbitrary",), vmem_limit_bytes=VMEM_LIMIT),
        name="inproj",
    )(x2, norm_w, w_main, w_gate)


def _disc(lam_re, lam_im, log_dt):
    return jnp.minimum(lam_re, LAMBDA_RE_MAX), lam_im, jnp.exp(log_dt)


def _cpow(lr, li, dt, k):
    mag = jnp.exp(k * (lr * dt))
    ang = k * (li * dt)
    return mag * jnp.cos(ang), mag * jnp.sin(ang)


def _s5_prep_kernel(lre, lim, ldt, bre_t, bim_t, cre_t, cim_t, ctre, ctim,
                    kb_ref, win_ref, woutt_ref, pw_s):
    tau = pl.program_id(0).astype(F32)
    lr, li, dt = _disc(lre[...], lim[...], ldt[...])
    ab_re, ab_im = _cpow(lr, li, dt, 1.0)
    den = lr * lr + li * li
    nr = ab_re - 1.0
    pw_s[0] = (nr * lr + ab_im * li) / den
    pw_s[1] = (ab_im * lr - nr * li) / den
    pw_s[2], pw_s[3] = _cpow(lr, li, dt, tau)
    pw_s[4], pw_s[5] = _cpow(lr, li, dt, tau + 1.0)

    col = lax.broadcasted_iota(jnp.int32, (SSM_GROUP, SSM_WIDTH), 1)

    def strip(g, carry):
        rows = pl.ds(pl.multiple_of(g * SSM_GROUP, SSM_GROUP), SSM_GROUP)
        grow = pl.ds(g, 1)
        coef_re, coef_im = pw_s[0, grow, :], pw_s[1, grow, :]
        p_re, p_im = pw_s[2, grow, :], pw_s[3, grow, :]
        p1_re, p1_im = pw_s[4, grow, :], pw_s[5, grow, :]
        br, bi = bre_t[rows, :], bim_t[rows, :]
        bb_re = coef_re * br - coef_im * bi
        bb_im = coef_re * bi + coef_im * br
        x_re = bb_re * p_re - bb_im * p_im
        x_im = bb_re * p_im + bb_im * p_re
        kfull = (jnp.dot(x_re[:, :SSM_STATE], ctre[...], precision=HIGHEST, preferred_element_type=F32)
                 - jnp.dot(x_im[:, :SSM_STATE], ctim[...], precision=HIGHEST, preferred_element_type=F32))
        kb_ref[0, rows, :] = jnp.where(col // SSM_GROUP == g, kfull, 0.0).astype(BF16)
        a = g // OCT
        lrows = pl.ds(pl.multiple_of((g % OCT) * SSM_GROUP, SSM_GROUP), SSM_GROUP)
        m_in = col // SSM_STATE == g % OCT
        win_ref[0, a, lrows, :] = jnp.concatenate(
            [jnp.where(m_in, x_re, 0.0), jnp.where(m_in, x_im, 0.0)], axis=1).astype(BF16)
        cr, ci = cre_t[rows, :], cim_t[rows, :]
        w_re = cr * p1_re - ci * p1_im
        w_im = cr * p1_im + ci * p1_re
        woutt_ref[0, a, lrows, :] = jnp.concatenate(
            [jnp.where(m_in, w_re, 0.0), jnp.where(m_in, -w_im, 0.0)], axis=1).astype(BF16)
        return carry

    lax.fori_loop(0, SSM_GROUPS, strip, 0)


def _s5_ptab_kernel(lre, lim, ldt, ptre_ref, ptim_ref):
    lr, li, dt = _disc(lre[0], lim[0], ldt[0])
    j = lax.broadcasted_iota(jnp.int32, (S5_PT_ROWS, OCT_STATE), 0).astype(F32) * float(S5_Q)
    ptre_ref[0], ptim_ref[0] = _cpow(lr, li, dt, j)


def _s5_prep(lam_re, lam_im, b_re, b_im, c_re, c_im, log_dt):
    g, n, p = SSM_GROUPS, SSM_STATE, SSM_GROUP
    ldt_gn = jnp.broadcast_to(log_dt[:, None], (g, n))
    compact = lambda a: jnp.tile(a, (1, OCT))
    b_t = lambda b: jnp.tile(b.transpose(0, 2, 1).reshape(g * p, n), (1, OCT))
    c_t = lambda c: jnp.tile(c.reshape(g * p, n), (1, OCT))
    c_n = lambda c: c.transpose(2, 0, 1).reshape(n, g * p)
    args = (compact(lam_re), compact(lam_im), compact(ldt_gn),
            b_t(b_re), b_t(b_im), c_t(c_re), c_t(c_im), c_n(c_re), c_n(c_im))
    mat = (S5_Q, N_OCT, LANES, 2 * OCT_STATE)
    kb, win, woutt = pl.pallas_call(
        _s5_prep_kernel,
        grid=(S5_Q,),
        in_specs=[_const_spec(a.shape) for a in args],
        out_specs=[
            pl.BlockSpec((1, SSM_WIDTH, SSM_WIDTH), lambda t: (t, 0, 0)),
            pl.BlockSpec((1,) + mat[1:], lambda t: (S5_Q - 1 - t, 0, 0, 0)),
            pl.BlockSpec((1,) + mat[1:], lambda t: (t, 0, 0, 0)),
        ],
        out_shape=[
            jax.ShapeDtypeStruct((S5_Q, SSM_WIDTH, SSM_WIDTH), BF16),
            jax.ShapeDtypeStruct(mat, BF16),
            jax.ShapeDtypeStruct(mat, BF16),
        ],
        scratch_shapes=[pltpu.VMEM((6, SSM_GROUPS, SSM_WIDTH), F32)],
        compiler_params=pltpu.CompilerParams(
            dimension_semantics=("arbitrary",), vmem_limit_bytes=VMEM_LIMIT),
        name="s5_prep",
    )(*args)

    row_r = lambda a: a.reshape(N_OCT, 1, OCT_STATE)
    tab = jax.ShapeDtypeStruct((N_OCT, S5_PT_ROWS, OCT_STATE), F32)
    ptre, ptim = pl.pallas_call(
        _s5_ptab_kernel,
        grid=(N_OCT,),
        in_specs=[pl.BlockSpec((1, 1, OCT_STATE), lambda a: (a, 0, 0))] * 3,
        out_specs=[pl.BlockSpec((1, S5_PT_ROWS, OCT_STATE), lambda a: (a, 0, 0))] * 2,
        out_shape=[tab, tab],
        compiler_params=pltpu.CompilerParams(dimension_semantics=("arbitrary",)),
        name="s5_ptab",
    )(row_r(lam_re), row_r(lam_im), row_r(ldt_gn))
    return kb, win, woutt, ptre, ptim


def _gelu_tanh(x):
    c = 0.7978845608028654
    return x * (0.5 * (1.0 + jnp.tanh(c * (x + 0.044715 * (x * x * x)))))


def _s5_kernel(u0_ref, u1_ref, u2_ref, u3_ref, z_ref, kb_ref, win_ref, woutt_ref, ptre_ref, ptim_ref,
               d_ref, wglu_ref, bglu_ref, o_ref, y0_s, y1_s, y2_s, y3_s, cre_scr, cim_scr):
    u_refs = (u0_ref, u1_ref, u2_ref, u3_ref)
    y_scrs = (y0_s, y1_s, y2_s, y3_s)

    @pl.when(pl.program_id(1) == 0)
    def _():
        cre_scr[...] = jnp.zeros_like(cre_scr)
        cim_scr[...] = jnp.zeros_like(cim_scr)

    xs = [[u_refs[a][pl.ds(s, S5_NCH, stride=S5_Q), :].astype(BF16) for a in range(N_OCT)]
          for s in range(S5_Q)]
    xs_full = [jnp.concatenate(xs[s], axis=1) for s in range(S5_Q)]
    rowi = lax.broadcasted_iota(jnp.int32, (S5_NCH, OCT_STATE), 0)

    hs = []
    for a in range(N_OCT):
        acc = None
        for s in range(S5_Q):
            part = jnp.dot(xs[s][a], win_ref[s, a], preferred_element_type=F32)
            acc = part if acc is None else acc + part
        g_re = acc[:, :OCT_STATE]
        g_im = acc[:, OCT_STATE:]
        d = 1
        while d < S5_NCH:
            mr = ptre_ref[a, d:d + 1, :]
            mi = ptim_ref[a, d:d + 1, :]
            sh_re = jnp.where(rowi >= d, pltpu.roll(g_re, d, 0), 0.0)
            sh_im = jnp.where(rowi >= d, pltpu.roll(g_im, d, 0), 0.0)
            g_re, g_im = g_re + mr * sh_re - mi * sh_im, g_im + mr * sh_im + mi * sh_re
            d *= 2
        c_re = cre_scr[a:a + 1, :]
        c_im = cim_scr[a:a + 1, :]
        p_re = ptre_ref[a, 0:S5_NCH, :]
        p_im = ptim_ref[a, 0:S5_NCH, :]
        e_re = jnp.where(rowi >= 1, pltpu.roll(g_re, 1, 0), 0.0)
        e_im = jnp.where(rowi >= 1, pltpu.roll(g_im, 1, 0), 0.0)
        hs_re = e_re + p_re * c_re - p_im * c_im
        hs_im = e_im + p_re * c_im + p_im * c_re
        l_re = ptre_ref[a, S5_NCH:S5_NCH + 1, :]
        l_im = ptim_ref[a, S5_NCH:S5_NCH + 1, :]
        cre_scr[a:a + 1, :] = g_re[S5_NCH - 1:S5_NCH, :] + l_re * c_re - l_im * c_im
        cim_scr[a:a + 1, :] = g_im[S5_NCH - 1:S5_NCH, :] + l_re * c_im + l_im * c_re
        hs.append(jnp.concatenate([hs_re, hs_im], axis=1).astype(BF16))

    for t in range(S5_Q):
        acc = None
        for s in range(t + 1):
            part = jnp.dot(xs_full[s], kb_ref[t - s], preferred_element_type=F32)
            acc = part if acc is None else acc + part
        for a in range(N_OCT):
            inter = lax.dot_general(hs[a], woutt_ref[t, a], (((1,), (1,)), ((), ())),
                                    preferred_element_type=F32)
            y_scrs[a][pl.ds(t, S5_NCH, stride=S5_Q), :] = acc[:, LANES * a:LANES * (a + 1)] + inter

    u = jnp.concatenate([r[...] for r in u_refs], axis=1)
    y = jnp.concatenate([r[...] for r in y_scrs], axis=1) + d_ref[...] * u
    y = _gelu_tanh(y)
    gate = jnp.dot(y.astype(BF16), wglu_ref[...], preferred_element_type=F32) + bglu_ref[...]
    y = y * _sigmoid(gate)
    z = z_ref[...]
    o_ref[...] = y * (z * _sigmoid(z))


def _s5(proj_main, kb, win, woutt, ptre, ptim, d_row, w_glu, b_glu, batch, seqlen):
    nb = seqlen // S5_TB
    t = batch * seqlen
    u_tile = lambda a: pl.BlockSpec((S5_TB, LANES), lambda b, i, a=a: (b * nb + i, a))
    return pl.pallas_call(
        _s5_kernel,
        grid=(batch, nb),
        in_specs=[u_tile(a) for a in range(N_OCT)] + [
            pl.BlockSpec((S5_TB, SSM_WIDTH), lambda b, i: (b * nb + i, 1)),
            _const_spec(kb.shape), _const_spec(win.shape), _const_spec(woutt.shape),
            _const_spec(ptre.shape), _const_spec(ptim.shape),
            _const_spec((1, SSM_WIDTH)), _const_spec((SSM_WIDTH, SSM_WIDTH)), _const_spec((1, SSM_WIDTH)),
        ],
        out_specs=pl.BlockSpec((S5_TB, SSM_WIDTH), lambda b, i: (b * nb + i, 0)),
        out_shape=jax.ShapeDtypeStruct((t, SSM_WIDTH), F32),
        scratch_shapes=[pltpu.VMEM((S5_TB, LANES), F32)] * N_OCT + [
            pltpu.VMEM((8, OCT_STATE), F32),
            pltpu.VMEM((8, OCT_STATE), F32),
        ],
        compiler_params=pltpu.CompilerParams(
            dimension_semantics=("arbitrary", "arbitrary"), vmem_limit_bytes=VMEM_LIMIT),
        name="s5",
    )(proj_main, proj_main, proj_main, proj_main, proj_main, kb, win, woutt, ptre, ptim, d_row, w_glu, b_glu)


def _hgrn_kernel(q_ref, f_ref, i_ref, og_ref, z_ref, lbl_ref, nw_ref, o_ref,
                 q0_s, qc_s, kt_s, k0_s, k1_s, k2_s, ke_s, v_s, dec_s, oacc_s, st_s):
    @pl.when(pl.program_id(1) == 0)
    def _():
        st_s[...] = jnp.zeros_like(st_s)

    logits = lbl_ref[...]
    e = jnp.exp(logits - jnp.max(logits, axis=0, keepdims=True))
    lb = (e / jnp.sum(e, axis=0, keepdims=True))[0:1, :]

    q = q_ref[...]
    qf = q * _sigmoid(q)
    forget = lb + (1.0 - lb) * _sigmoid(f_ref[...])
    lf = jnp.log(forget)
    key = 1.0 - forget

    row = lax.broadcasted_iota(jnp.int32, (HG_TB, HGRN_WIDTH), 0)
    r_sub = row % HG_SUB
    r_ch = row % HG_CH

    def down(x, d):
        return pltpu.roll(x, d, 0)

    def up(x, d):
        return pltpu.roll(x, HG_TB - d, 0)

    a = lf
    d = 1
    while d < HG_SUB:
        a = a + jnp.where(r_sub >= d, down(a, d), 0.0)
        d *= 2
    tsub = jnp.where(r_sub == HG_SUB - 1, a, 0.0)
    d = 1
    while d < HG_SUB:
        tsub = tsub + jnp.where(r_sub + d <= HG_SUB - 1, up(tsub, d), 0.0)
        d *= 2
    n_sub = HG_CH // HG_SUB
    prev = [jnp.where(r_ch >= HG_SUB * k, down(tsub, HG_SUB * k), 0.0) for k in range(1, n_sub)]
    nxt = [jnp.where(r_ch < HG_CH - HG_SUB * k, up(tsub, HG_SUB * k), 0.0) for k in range(1, n_sub)]
    eprev = prev[0] + prev[1] + prev[2]
    enext = nxt[0] + nxt[1] + nxt[2]
    suf = tsub - a

    q0 = qf * jnp.exp(a)
    q0_s[...] = q0.astype(BF16)
    qc_s[...] = (q0 * jnp.exp(eprev)).astype(BF16)
    kt_s[...] = (key * jnp.exp(-a)).astype(BF16)
    k0 = key * jnp.exp(suf)
    k0_s[...] = k0.astype(BF16)
    k1 = k0 * jnp.exp(nxt[0])
    k1_s[...] = k1.astype(BF16)
    k2 = k1 * jnp.exp(nxt[1])
    k2_s[...] = k2.astype(BF16)
    ke_s[...] = (k2 * jnp.exp(nxt[2])).astype(BF16)
    v_s[...] = i_ref[...].astype(BF16)
    dec_s[...] = jnp.exp(eprev + tsub + enext)

    tq = lax.broadcasted_iota(jnp.int32, (HG_CH, n_sub * HG_CH), 0)
    cc = lax.broadcasted_iota(jnp.int32, (HG_CH, n_sub * HG_CH), 1)
    cls = cc // HG_CH
    ts = cc % HG_CH
    bi = tq // HG_SUB
    bj = ts // HG_SUB
    mask4 = ((cls == 0) & (bi == bj) & (ts <= tq)) | ((cls > 0) & ((bi - bj) == cls))

    def chunk_body(c, carry):
        r0 = pl.multiple_of(c * HG_CH, HG_CH)
        rows = pl.ds(r0, HG_CH)
        for h in range(HEADS):
            ls = slice(HEAD_DIM * h, HEAD_DIM * (h + 1))
            q0h = q0_s[rows, ls]
            kcat = jnp.concatenate([kt_s[rows, ls], k0_s[rows, ls], k1_s[rows, ls], k2_s[rows, ls]], axis=0)
            sc = lax.dot_general(q0h, kcat, (((1,), (1,)), ((), ())), preferred_element_type=F32)
            amat = jnp.where(mask4, sc, 0.0).astype(BF16)
            vv = v_s[rows, ls]
            vcat = jnp.concatenate([vv] * n_sub, axis=0)
            o_intra = jnp.dot(amat, vcat, preferred_element_type=F32)
            st = st_s[h]
            o_inter = lax.dot_general(qc_s[rows, ls], st.astype(BF16), (((1,), (1,)), ((), ())),
                                      preferred_element_type=F32)
            oacc_s[rows, ls] = o_intra + o_inter
            upd = lax.dot_general(vv, ke_s[rows, ls], (((0,), (0,)), ((), ())), preferred_element_type=F32)
            st_s[h] = st * dec_s[pl.ds(r0, 1), ls] + upd
        return carry

    lax.fori_loop(0, HG_TB // HG_CH, chunk_body, 0)

    o = oacc_s[...] * _sigmoid(og_ref[...])
    parts = []
    for h in range(HEADS):
        oh = o[:, HEAD_DIM * h:HEAD_DIM * (h + 1)]
        ms = jnp.mean(oh * oh, axis=-1, keepdims=True)
        parts.append(oh * lax.rsqrt(ms + NORM_EPS))
    o = jnp.concatenate(parts, axis=1) * nw_ref[...]
    z = z_ref[...]
    o_ref[...] = o * (z * _sigmoid(z))


def _hgrn(proj_main, lb_logits, norm_w, batch, seqlen):
    nb = seqlen // HG_TB
    t = batch * seqlen
    col = lambda k: pl.BlockSpec((HG_TB, HGRN_WIDTH), lambda b, i, k=k: (b * nb + i, k))
    wide = (HG_TB, HGRN_WIDTH)
    return pl.pallas_call(
        _hgrn_kernel,
        grid=(batch, nb),
        in_specs=[col(2), col(3), col(4), col(5), col(6),
                  _const_spec(lb_logits.shape), _const_spec((1, HGRN_WIDTH))],
        out_specs=pl.BlockSpec((HG_TB, HGRN_WIDTH), lambda b, i: (b * nb + i, 0)),
        out_shape=jax.ShapeDtypeStruct((t, HGRN_WIDTH), F32),
        scratch_shapes=[pltpu.VMEM(wide, BF16)] * 8 + [
            pltpu.VMEM(wide, F32), pltpu.VMEM(wide, F32),
            pltpu.VMEM((HEADS, HEAD_DIM, HEAD_DIM), F32)],
        compiler_params=pltpu.CompilerParams(
            dimension_semantics=("arbitrary", "arbitrary"), vmem_limit_bytes=VMEM_LIMIT),
        name="hgrn",
    )(proj_main, proj_main, proj_main, proj_main, proj_main, lb_logits, norm_w)


def _merge_kernel(x_ref, ya_ref, yb_ref, ga_ref, gb_ref, wpa_ref, wpb_ref, wo_ref, fnw_ref, o_ref):
    pa = jnp.dot(ya_ref[...].astype(BF16), wpa_ref[...], preferred_element_type=F32)
    pb = jnp.dot(yb_ref[...].astype(BF16), wpb_ref[...], preferred_element_type=F32)
    merged = _sigmoid(ga_ref[...]) * pa + _sigmoid(gb_ref[...]) * pb
    h = x_ref[...] + jnp.dot(merged.astype(BF16), wo_ref[...], preferred_element_type=F32)
    ms = jnp.mean(h * h, axis=-1, keepdims=True)
    o_ref[...] = h * lax.rsqrt(ms + NORM_EPS) * fnw_ref[...]


def _merge(x2, ya, yb, proj_gate, w_pa, w_pb, w_out, fnw):
    t = x2.shape[0]
    return pl.pallas_call(
        _merge_kernel,
        grid=(t // TM_OUT,),
        in_specs=[
            pl.BlockSpec((TM_OUT, D_MODEL), lambda i: (i, 0)),
            pl.BlockSpec((TM_OUT, SSM_WIDTH), lambda i: (i, 0)),
            pl.BlockSpec((TM_OUT, HGRN_WIDTH), lambda i: (i, 0)),
            pl.BlockSpec((TM_OUT, D_MODEL), lambda i: (i, 0)),
            pl.BlockSpec((TM_OUT, D_MODEL), lambda i: (i, 1)),
            _const_spec((SSM_WIDTH, D_MODEL)), _const_spec((HGRN_WIDTH, D_MODEL)),
            _const_spec((D_MODEL, D_MODEL)), _const_spec((1, D_MODEL)),
        ],
        out_specs=pl.BlockSpec((TM_OUT, D_MODEL), lambda i: (i, 0)),
        out_shape=jax.ShapeDtypeStruct((t, D_MODEL), F32),
        compiler_params=pltpu.CompilerParams(
            dimension_semantics=("arbitrary",), vmem_limit_bytes=VMEM_LIMIT),
        name="merge",
    )(x2, ya, yb, proj_gate, proj_gate, w_pa, w_pb, w_out, fnw)


def kernel(x, norm_w, w_in, ssm_lambda_re, ssm_lambda_im, ssm_b_re, ssm_b_im, ssm_c_re, ssm_c_im, ssm_d,
           ssm_log_dt, ssm_w_glu, ssm_b_glu, hgrn_lb_logits, hgrn_norm_w, w_proj_a, w_proj_b, w_out,
           final_norm_w):
    batch, seqlen, _ = x.shape
    assert norm_w.shape[0] == 1, "single-layer block"
    assert seqlen % S5_TB == 0 and seqlen % HG_TB == 0
    x2 = x.reshape(batch * seqlen, D_MODEL)
    w_in_b = w_in[0].astype(BF16)
    proj_main, proj_gate = _inproj(x2, norm_w[0][None, :], w_in_b[:, :MAIN_WIDTH], w_in_b[:, MAIN_WIDTH:])

    kb, win, woutt, ptre, ptim = _s5_prep(ssm_lambda_re[0], ssm_lambda_im[0], ssm_b_re[0], ssm_b_im[0],
                                          ssm_c_re[0], ssm_c_im[0], ssm_log_dt[0])
    ya = _s5(proj_main, kb, win, woutt, ptre, ptim, ssm_d[0].reshape(1, SSM_WIDTH),
             ssm_w_glu[0].astype(BF16), ssm_b_glu[0][None, :], batch, seqlen)
    yb = _hgrn(proj_main, hgrn_lb_logits, hgrn_norm_w[0][None, :], batch, seqlen)
    out = _merge(x2, ya, yb, proj_gate, w_proj_a[0].astype(BF16), w_proj_b[0].astype(BF16),
                 w_out[0].astype(BF16), final_norm_w[None, :])
    return out.reshape(batch, seqlen, D_MODEL)
```

```python
import jax
import jax.numpy as jnp
from jax import lax
from jax.experimental import pallas as pl
from jax.experimental.pallas import tpu as pltpu

F32 = jnp.float32
BF16 = jnp.bfloat16
HIGHEST = lax.Precision.HIGHEST

D_MODEL = 1024
SSM_WIDTH = 512
SSM_GROUP = 16
SSM_GROUPS = 32
SSM_STATE = 64
HGRN_WIDTH = 512
HEAD_DIM = 128
HEADS = 4
NORM_EPS = 1e-6
LAMBDA_RE_MAX = -1e-4

P32_WIDTH = SSM_WIDTH + HGRN_WIDTH
PBF_WIDTH = 2 * D_MODEL + SSM_WIDTH + 4 * HGRN_WIDTH
PBF_ZA, PBF_Q, PBF_I, PBF_OG, PBF_ZB = 4, 5, 6, 7, 8

LANES = 128
OCT = LANES // SSM_GROUP
N_OCT = SSM_GROUPS // OCT
OCT_STATE = OCT * SSM_STATE

TM_IN = 512
S5_Q = 8
S5_QP = S5_Q // 2
S5_TB = 2048
S5_NCH = S5_TB // S5_Q
S5_PT_ROWS = S5_NCH + 8
HG_TB = 512
HG_CH = 64
HG_SUB = 16
TM_OUT = 512

VMEM_LIMIT = 56 * 1024 * 1024


def _sigmoid(x):
    return jax.nn.sigmoid(x)


def _const_spec(shape):
    nd = len(shape)
    return pl.BlockSpec(shape, lambda *_: (0,) * nd, pipeline_mode=pl.Buffered(1))


def _inproj_kernel(x_ref, nw_ref, w32_ref, wbf_ref, o32_ref, obf_ref):
    x = x_ref[...]
    ms = jnp.mean(x * x, axis=-1, keepdims=True)
    xn = (x * lax.rsqrt(ms + NORM_EPS) * nw_ref[...]).astype(BF16)
    o32_ref[...] = jnp.dot(xn, w32_ref[...], preferred_element_type=F32)
    obf_ref[...] = jnp.dot(xn, wbf_ref[...], preferred_element_type=F32).astype(BF16)


def _inproj(x2, norm_w, w32, wbf):
    t = x2.shape[0]
    return pl.pallas_call(
        _inproj_kernel,
        grid=(t // TM_IN,),
        in_specs=[
            pl.BlockSpec((TM_IN, D_MODEL), lambda i: (i, 0)),
            _const_spec((1, D_MODEL)),
            _const_spec((D_MODEL, P32_WIDTH)),
            _const_spec((D_MODEL, PBF_WIDTH)),
        ],
        out_specs=[
            pl.BlockSpec((TM_IN, P32_WIDTH), lambda i: (i, 0)),
            pl.BlockSpec((TM_IN, PBF_WIDTH), lambda i: (i, 0)),
        ],
        out_shape=[
            jax.ShapeDtypeStruct((t, P32_WIDTH), F32),
            jax.ShapeDtypeStruct((t, PBF_WIDTH), BF16),
        ],
        compiler_params=pltpu.CompilerParams(
            dimension_semantics=("arbitrary",), vmem_limit_bytes=VMEM_LIMIT),
        name="inproj",
    )(x2, norm_w, w32, wbf)


def _disc(lam_re, lam_im, log_dt):
    return jnp.minimum(lam_re, LAMBDA_RE_MAX), lam_im, jnp.exp(log_dt)


def _cpow(lr, li, dt, k):
    mag = jnp.exp(k * (lr * dt))
    ang = k * (li * dt)
    return mag * jnp.cos(ang), mag * jnp.sin(ang)


def _s5_prep_kernel(lre, lim, ldt, bre_t, bim_t, cre_t, cim_t, ctre, ctim,
                    kb_ref, win_ref, woutt_ref, pw_s, xk_s):
    tau = pl.program_id(0).astype(F32)
    lr, li, dt = _disc(lre[...], lim[...], ldt[...])
    ab_re, ab_im = _cpow(lr, li, dt, 1.0)
    den = lr * lr + li * li
    nr = ab_re - 1.0
    pw_s[0] = (nr * lr + ab_im * li) / den
    pw_s[1] = (ab_im * lr - nr * li) / den
    pw_s[2], pw_s[3] = _cpow(lr, li, dt, tau)
    pw_s[4], pw_s[5] = _cpow(lr, li, dt, tau + 1.0)

    col = lax.broadcasted_iota(jnp.int32, (SSM_GROUP, SSM_WIDTH), 1)

    def strip(g, carry):
        rows = pl.ds(pl.multiple_of(g * SSM_GROUP, SSM_GROUP), SSM_GROUP)
        grow = pl.ds(g, 1)
        coef_re, coef_im = pw_s[0, grow, :], pw_s[1, grow, :]
        p_re, p_im = pw_s[2, grow, :], pw_s[3, grow, :]
        p1_re, p1_im = pw_s[4, grow, :], pw_s[5, grow, :]
        br, bi = bre_t[rows, :], bim_t[rows, :]
        bb_re = coef_re * br - coef_im * bi
        bb_im = coef_re * bi + coef_im * br
        x_re = bb_re * p_re - bb_im * p_im
        x_im = bb_re * p_im + bb_im * p_re
        xk_s[0, rows, :] = x_re[:, :SSM_STATE]
        xk_s[1, rows, :] = x_im[:, :SSM_STATE]
        a = g // OCT
        lrows = pl.ds(pl.multiple_of((g % OCT) * SSM_GROUP, SSM_GROUP), SSM_GROUP)
        m_in = col // SSM_STATE == g % OCT
        win_ref[0, a, lrows, :] = jnp.concatenate(
            [jnp.where(m_in, x_re, 0.0), jnp.where(m_in, x_im, 0.0)], axis=1).astype(BF16)
        cr, ci = cre_t[rows, :], cim_t[rows, :]
        w_re = cr * p1_re - ci * p1_im
        w_im = cr * p1_im + ci * p1_re
        woutt_ref[0, a, lrows, :] = jnp.concatenate(
            [jnp.where(m_in, w_re, 0.0), jnp.where(m_in, -w_im, 0.0)], axis=1).astype(BF16)
        return carry

    lax.fori_loop(0, SSM_GROUPS, strip, 0)

    kfull = (jnp.dot(xk_s[0], ctre[...], precision=HIGHEST, preferred_element_type=F32)
             - jnp.dot(xk_s[1], ctim[...], precision=HIGHEST, preferred_element_type=F32))
    rowk = lax.broadcasted_iota(jnp.int32, (SSM_WIDTH, SSM_WIDTH), 0)
    colk = lax.broadcasted_iota(jnp.int32, (SSM_WIDTH, SSM_WIDTH), 1)
    kb_ref[0] = jnp.where(rowk // SSM_GROUP == colk // SSM_GROUP, kfull, 0.0).astype(BF16)


def _s5_ptab_kernel(lre, lim, ldt, ptre_ref, ptim_ref):
    lr, li, dt = _disc(lre[0], lim[0], ldt[0])
    j = lax.broadcasted_iota(jnp.int32, (S5_PT_ROWS, OCT_STATE), 0).astype(F32) * float(S5_Q)
    ptre_ref[0], ptim_ref[0] = _cpow(lr, li, dt, j)


def _s5_prep(lam_re, lam_im, b_re, b_im, c_re, c_im, log_dt):
    g, n, p = SSM_GROUPS, SSM_STATE, SSM_GROUP
    ldt_gn = jnp.broadcast_to(log_dt[:, None], (g, n))
    compact = lambda a: jnp.tile(a, (1, OCT))
    b_t = lambda b: jnp.tile(b.transpose(0, 2, 1).reshape(g * p, n), (1, OCT))
    c_t = lambda c: jnp.tile(c.reshape(g * p, n), (1, OCT))
    c_n = lambda c: c.transpose(2, 0, 1).reshape(n, g * p)
    args = (compact(lam_re), compact(lam_im), compact(ldt_gn),
            b_t(b_re), b_t(b_im), c_t(c_re), c_t(c_im), c_n(c_re), c_n(c_im))
    mat = (S5_Q, N_OCT, LANES, 2 * OCT_STATE)
    kb, win, woutt = pl.pallas_call(
        _s5_prep_kernel,
        grid=(S5_Q,),
        in_specs=[_const_spec(a.shape) for a in args],
        out_specs=[
            pl.BlockSpec((1, SSM_WIDTH, SSM_WIDTH), lambda t: (t, 0, 0)),
            pl.BlockSpec((1,) + mat[1:], lambda t: (S5_Q - 1 - t, 0, 0, 0)),
            pl.BlockSpec((1,) + mat[1:], lambda t: (t, 0, 0, 0)),
        ],
        out_shape=[
            jax.ShapeDtypeStruct((S5_Q, SSM_WIDTH, SSM_WIDTH), BF16),
            jax.ShapeDtypeStruct(mat, BF16),
            jax.ShapeDtypeStruct(mat, BF16),
        ],
        scratch_shapes=[pltpu.VMEM((6, SSM_GROUPS, SSM_WIDTH), F32),
                        pltpu.VMEM((2, SSM_WIDTH, SSM_STATE), F32)],
        compiler_params=pltpu.CompilerParams(
            dimension_semantics=("arbitrary",), vmem_limit_bytes=VMEM_LIMIT),
        name="s5_prep",
    )(*args)

    row_r = lambda a: a.reshape(N_OCT, 1, OCT_STATE)
    tab = jax.ShapeDtypeStruct((N_OCT, S5_PT_ROWS, OCT_STATE), F32)
    ptre, ptim = pl.pallas_call(
        _s5_ptab_kernel,
        grid=(N_OCT,),
        in_specs=[pl.BlockSpec((1, 1, OCT_STATE), lambda a: (a, 0, 0))] * 3,
        out_specs=[pl.BlockSpec((1, S5_PT_ROWS, OCT_STATE), lambda a: (a, 0, 0))] * 2,
        out_shape=[tab, tab],
        compiler_params=pltpu.CompilerParams(dimension_semantics=("arbitrary",)),
        name="s5_ptab",
    )(row_r(lam_re), row_r(lam_im), row_r(ldt_gn))

    kbo = jnp.stack([kb[:, LANES * a:LANES * (a + 1), LANES * a:LANES * (a + 1)] for a in range(N_OCT)], axis=1)
    zero = jnp.zeros_like(kbo[0])

    def pair_tile(d):
        top = jnp.concatenate([kbo[2 * d], kbo[2 * d + 1]], axis=-1)
        bot = jnp.concatenate([kbo[2 * d - 1] if d > 0 else zero, kbo[2 * d]], axis=-1)
        return jnp.concatenate([top, bot], axis=-2)

    wt = jnp.stack([pair_tile(d) for d in range(S5_QP)])
    split = (S5_QP, 2, N_OCT, LANES, 2 * OCT_STATE)
    winp = win.reshape(split).transpose(0, 2, 1, 3, 4).reshape(S5_QP, N_OCT, 2 * LANES, 2 * OCT_STATE)
    woutp = woutt.reshape(split).transpose(0, 2, 4, 1, 3).reshape(S5_QP, N_OCT, 2 * OCT_STATE, 2 * LANES)
    return wt, winp, woutp, ptre, ptim


def _gelu_tanh(x):
    c = 0.7978845608028654
    return x * (0.5 * (1.0 + jnp.tanh(c * (x + 0.044715 * (x * x * x)))))


def _s5_kernel(u0_ref, u1_ref, u2_ref, u3_ref, z_ref, wt_ref, winp_ref, woutp_ref, ptre_ref, ptim_ref,
               d_ref, wglu_ref, bglu_ref, o_ref, y0_s, y1_s, y2_s, y3_s, cre_scr, cim_scr):
    u_refs = (u0_ref, u1_ref, u2_ref, u3_ref)
    y_scrs = (y0_s, y1_s, y2_s, y3_s)

    @pl.when(pl.program_id(1) == 0)
    def _():
        cre_scr[...] = jnp.zeros_like(cre_scr)
        cim_scr[...] = jnp.zeros_like(cim_scr)

    def tok(a, s):
        return u_refs[a][pl.ds(s, S5_NCH, stride=S5_Q), :]

    xp = [[jnp.concatenate([tok(a, 2 * sp), tok(a, 2 * sp + 1)], axis=1).astype(BF16) for a in range(N_OCT)]
          for sp in range(S5_QP)]
    rowi = lax.broadcasted_iota(jnp.int32, (S5_NCH, OCT_STATE), 0)

    hs = []
    for a in range(N_OCT):
        acc = None
        for sp in range(S5_QP):
            part = jnp.dot(xp[sp][a], winp_ref[sp, a], preferred_element_type=F32)
            acc = part if acc is None else acc + part
        g_re = acc[:, :OCT_STATE]
        g_im = acc[:, OCT_STATE:]
        d = 1
        while d < S5_NCH:
            mr = ptre_ref[a, d:d + 1, :]
            mi = ptim_ref[a, d:d + 1, :]
            sh_re = jnp.where(rowi >= d, pltpu.roll(g_re, d, 0), 0.0)
            sh_im = jnp.where(rowi >= d, pltpu.roll(g_im, d, 0), 0.0)
            g_re, g_im = g_re + mr * sh_re - mi * sh_im, g_im + mr * sh_im + mi * sh_re
            d *= 2
        c_re = cre_scr[a:a + 1, :]
        c_im = cim_scr[a:a + 1, :]
        p_re = ptre_ref[a, 0:S5_NCH, :]
        p_im = ptim_ref[a, 0:S5_NCH, :]
        e_re = jnp.where(rowi >= 1, pltpu.roll(g_re, 1, 0), 0.0)
        e_im = jnp.where(rowi >= 1, pltpu.roll(g_im, 1, 0), 0.0)
        hs_re = e_re + p_re * c_re - p_im * c_im
        hs_im = e_im + p_re * c_im + p_im * c_re
        l_re = ptre_ref[a, S5_NCH:S5_NCH + 1, :]
        l_im = ptim_ref[a, S5_NCH:S5_NCH + 1, :]
        cre_scr[a:a + 1, :] = g_re[S5_NCH - 1:S5_NCH, :] + l_re * c_re - l_im * c_im
        cim_scr[a:a + 1, :] = g_im[S5_NCH - 1:S5_NCH, :] + l_re * c_im + l_im * c_re
        hs.append(jnp.concatenate([hs_re, hs_im], axis=1).astype(BF16))

    for tp in range(S5_QP):
        for a in range(N_OCT):
            acc = jnp.dot(hs[a], woutp_ref[tp, a], preferred_element_type=F32)
            for sp in range(tp + 1):
                acc = acc + jnp.dot(xp[sp][a], wt_ref[tp - sp, a], preferred_element_type=F32)
            y_scrs[a][pl.ds(2 * tp, S5_NCH, stride=S5_Q), :] = acc[:, :LANES]
            y_scrs[a][pl.ds(2 * tp + 1, S5_NCH, stride=S5_Q), :] = acc[:, LANES:]

    u = jnp.concatenate([r[...] for r in u_refs], axis=1)
    y = jnp.concatenate([r[...] for r in y_scrs], axis=1) + d_ref[...] * u
    y = _gelu_tanh(y)
    gate = jnp.dot(y.astype(BF16), wglu_ref[...], preferred_element_type=F32) + bglu_ref[...]
    y = y * _sigmoid(gate)
    z = z_ref[...].astype(F32)
    o_ref[...] = (y * (z * _sigmoid(z))).astype(BF16)


def _s5(p32, pbf, wt, winp, woutp, ptre, ptim, d_row, w_glu, b_glu, batch, seqlen):
    nb = seqlen // S5_TB
    t = batch * seqlen
    u_tile = lambda a: pl.BlockSpec((S5_TB, LANES), lambda b, i, a=a: (b * nb + i, a))
    return pl.pallas_call(
        _s5_kernel,
        grid=(batch, nb),
        in_specs=[u_tile(a) for a in range(N_OCT)] + [
            pl.BlockSpec((S5_TB, SSM_WIDTH), lambda b, i: (b * nb + i, PBF_ZA)),
            _const_spec(wt.shape), _const_spec(winp.shape), _const_spec(woutp.shape),
            _const_spec(ptre.shape), _const_spec(ptim.shape),
            _const_spec((1, SSM_WIDTH)), _const_spec((SSM_WIDTH, SSM_WIDTH)), _const_spec((1, SSM_WIDTH)),
        ],
        out_specs=pl.BlockSpec((S5_TB, SSM_WIDTH), lambda b, i: (b * nb + i, 0)),
        out_shape=jax.ShapeDtypeStruct((t, SSM_WIDTH), BF16),
        scratch_shapes=[pltpu.VMEM((S5_TB, LANES), F32)] * N_OCT + [
            pltpu.VMEM((8, OCT_STATE), F32),
            pltpu.VMEM((8, OCT_STATE), F32),
        ],
        compiler_params=pltpu.CompilerParams(
            dimension_semantics=("arbitrary", "arbitrary"), vmem_limit_bytes=VMEM_LIMIT),
        name="s5",
    )(p32, p32, p32, p32, pbf, wt, winp, woutp, ptre, ptim, d_row, w_glu, b_glu)


def _hgrn_kernel(f_ref, q_ref, i_ref, og_ref, z_ref, lbl_ref, nw_ref, o_ref,
                 q0_s, qc_s, kt_s, k0_s, k1_s, k2_s, ke_s, dec_s, oacc_s, st_s, upd_s):
    @pl.when(pl.program_id(1) == 0)
    def _():
        st_s[...] = jnp.zeros_like(st_s)

    logits = lbl_ref[...]
    e = jnp.exp(logits - jnp.max(logits, axis=0, keepdims=True))
    lb = (e / jnp.sum(e, axis=0, keepdims=True))[0:1, :]

    q = q_ref[...].astype(F32)
    qf = q * _sigmoid(q)
    forget = lb + (1.0 - lb) * _sigmoid(f_ref[...])
    lf = jnp.log(forget)
    key = 1.0 - forget

    row = lax.broadcasted_iota(jnp.int32, (HG_TB, HGRN_WIDTH), 0)
    r_sub = row % HG_SUB
    r_ch = row % HG_CH

    def down(x, d):
        return pltpu.roll(x, d, 0)

    def up(x, d):
        return pltpu.roll(x, HG_TB - d, 0)

    a = lf
    d = 1
    while d < HG_SUB:
        a = a + jnp.where(r_sub >= d, down(a, d), 0.0)
        d *= 2
    a3 = a.reshape(HG_TB // HG_SUB, HG_SUB, HGRN_WIDTH)
    tsub = jnp.broadcast_to(a3[:, HG_SUB - 1:HG_SUB, :], a3.shape).reshape(HG_TB, HGRN_WIDTH)
    n_sub = HG_CH // HG_SUB
    prev = [jnp.where(r_ch >= HG_SUB * k, down(tsub, HG_SUB * k), 0.0) for k in range(1, n_sub)]
    nxt = [jnp.where(r_ch < HG_CH - HG_SUB * k, up(tsub, HG_SUB * k), 0.0) for k in range(1, n_sub)]
    eprev = prev[0] + prev[1] + prev[2]
    enext = nxt[0] + nxt[1] + nxt[2]
    suf = tsub - a

    q0 = qf * jnp.exp(a)
    q0_s[...] = q0.astype(BF16)
    qc_s[...] = (q0 * jnp.exp(eprev)).astype(BF16)
    kt_s[...] = (key * jnp.exp(-a)).astype(BF16)
    k0 = key * jnp.exp(suf)
    k0_s[...] = k0.astype(BF16)
    k1 = k0 * jnp.exp(nxt[0])
    k1_s[...] = k1.astype(BF16)
    k2 = k1 * jnp.exp(nxt[1])
    k2_s[...] = k2.astype(BF16)
    ke_s[...] = (k2 * jnp.exp(nxt[2])).astype(BF16)
    dec_s[...] = jnp.exp(eprev + tsub + enext)

    tq = lax.broadcasted_iota(jnp.int32, (HG_CH, n_sub * HG_CH), 0)
    cc = lax.broadcasted_iota(jnp.int32, (HG_CH, n_sub * HG_CH), 1)
    cls = cc // HG_CH
    ts = cc % HG_CH
    bi = tq // HG_SUB
    bj = ts // HG_SUB
    mask4 = ((cls == 0) & (bi == bj) & (ts <= tq)) | ((cls > 0) & ((bi - bj) == cls))

    n_ch = HG_TB // HG_CH
    half = n_sub * HG_CH // 2
    for c in range(n_ch):
        rows = slice(c * HG_CH, (c + 1) * HG_CH)
        for h in range(HEADS):
            ls = slice(HEAD_DIM * h, HEAD_DIM * (h + 1))
            kcat = jnp.concatenate([kt_s[rows, ls], k0_s[rows, ls], k1_s[rows, ls], k2_s[rows, ls]], axis=0)
            sc = lax.dot_general(q0_s[rows, ls], kcat, (((1,), (1,)), ((), ())), preferred_element_type=F32)
            sc = jnp.where(mask4, sc, 0.0)
            amat = (sc[:, :half] + sc[:, half:]).astype(BF16)
            vv = i_ref[rows, ls]
            oacc_s[rows, ls] = jnp.dot(amat, jnp.concatenate([vv, vv], axis=0), preferred_element_type=F32)
            upd_s[c * HEADS + h] = lax.dot_general(vv, ke_s[rows, ls], (((0,), (0,)), ((), ())),
                                                   preferred_element_type=F32)
    for h in range(HEADS):
        ls = slice(HEAD_DIM * h, HEAD_DIM * (h + 1))
        st = st_s[h]
        for c in range(n_ch):
            rows = slice(c * HG_CH, (c + 1) * HG_CH)
            oacc_s[rows, ls] += lax.dot_general(qc_s[rows, ls], st.astype(BF16), (((1,), (1,)), ((), ())),
                                                preferred_element_type=F32)
            st = st * dec_s[c * HG_CH:c * HG_CH + 1, ls] + upd_s[c * HEADS + h]
        st_s[h] = st

    o = oacc_s[...] * _sigmoid(og_ref[...].astype(F32))
    parts = []
    for h in range(HEADS):
        oh = o[:, HEAD_DIM * h:HEAD_DIM * (h + 1)]
        ms = jnp.mean(oh * oh, axis=-1, keepdims=True)
        parts.append(oh * lax.rsqrt(ms + NORM_EPS))
    o = jnp.concatenate(parts, axis=1) * nw_ref[...]
    z = z_ref[...].astype(F32)
    o_ref[...] = (o * (z * _sigmoid(z))).astype(BF16)


def _hgrn(p32, pbf, lb_logits, norm_w, batch, seqlen):
    nb = seqlen // HG_TB
    t = batch * seqlen
    col = lambda k: pl.BlockSpec((HG_TB, HGRN_WIDTH), lambda b, i, k=k: (b * nb + i, k))
    wide = (HG_TB, HGRN_WIDTH)
    return pl.pallas_call(
        _hgrn_kernel,
        grid=(batch, nb),
        in_specs=[col(1), col(PBF_Q), col(PBF_I), col(PBF_OG), col(PBF_ZB),
                  _const_spec(lb_logits.shape), _const_spec((1, HGRN_WIDTH))],
        out_specs=pl.BlockSpec((HG_TB, HGRN_WIDTH), lambda b, i: (b * nb + i, 0)),
        out_shape=jax.ShapeDtypeStruct((t, HGRN_WIDTH), BF16),
        scratch_shapes=[pltpu.VMEM(wide, BF16)] * 7 + [
            pltpu.VMEM(wide, F32), pltpu.VMEM(wide, F32),
            pltpu.VMEM((HEADS, HEAD_DIM, HEAD_DIM), F32),
            pltpu.VMEM((HG_TB // HG_CH * HEADS, HEAD_DIM, HEAD_DIM), F32)],
        compiler_params=pltpu.CompilerParams(
            dimension_semantics=("arbitrary", "arbitrary"), vmem_limit_bytes=VMEM_LIMIT),
        name="hgrn",
    )(p32, pbf, pbf, pbf, pbf, lb_logits, norm_w)


def _merge_kernel(x_ref, ya_ref, yb_ref, ga_ref, gb_ref, wpa_ref, wpb_ref, wo_ref, fnw_ref, o_ref):
    pa = jnp.dot(ya_ref[...], wpa_ref[...], preferred_element_type=F32)
    pb = jnp.dot(yb_ref[...], wpb_ref[...], preferred_element_type=F32)
    merged = _sigmoid(ga_ref[...].astype(F32)) * pa + _sigmoid(gb_ref[...].astype(F32)) * pb
    h = x_ref[...] + jnp.dot(merged.astype(BF16), wo_ref[...], preferred_element_type=F32)
    ms = jnp.mean(h * h, axis=-1, keepdims=True)
    o_ref[...] = h * lax.rsqrt(ms + NORM_EPS) * fnw_ref[...]


def _merge(x2, ya, yb, pbf, w_pa, w_pb, w_out, fnw):
    t = x2.shape[0]
    return pl.pallas_call(
        _merge_kernel,
        grid=(t // TM_OUT,),
        in_specs=[
            pl.BlockSpec((TM_OUT, D_MODEL), lambda i: (i, 0)),
            pl.BlockSpec((TM_OUT, SSM_WIDTH), lambda i: (i, 0)),
            pl.BlockSpec((TM_OUT, HGRN_WIDTH), lambda i: (i, 0)),
            pl.BlockSpec((TM_OUT, D_MODEL), lambda i: (i, 0)),
            pl.BlockSpec((TM_OUT, D_MODEL), lambda i: (i, 1)),
            _const_spec((SSM_WIDTH, D_MODEL)), _const_spec((HGRN_WIDTH, D_MODEL)),
            _const_spec((D_MODEL, D_MODEL)), _const_spec((1, D_MODEL)),
        ],
        out_specs=pl.BlockSpec((TM_OUT, D_MODEL), lambda i: (i, 0)),
        out_shape=jax.ShapeDtypeStruct((t, D_MODEL), F32),
        compiler_params=pltpu.CompilerParams(
            dimension_semantics=("arbitrary",), vmem_limit_bytes=VMEM_LIMIT),
        name="merge",
    )(x2, ya, yb, pbf, pbf, w_pa, w_pb, w_out, fnw)


def kernel(x, norm_w, w_in, ssm_lambda_re, ssm_lambda_im, ssm_b_re, ssm_b_im, ssm_c_re, ssm_c_im, ssm_d,
           ssm_log_dt, ssm_w_glu, ssm_b_glu, hgrn_lb_logits, hgrn_norm_w, w_proj_a, w_proj_b, w_out,
           final_norm_w):
    batch, seqlen, _ = x.shape
    assert norm_w.shape[0] == 1, "single-layer block"
    assert seqlen % S5_TB == 0 and seqlen % HG_TB == 0
    x2 = x.reshape(batch * seqlen, D_MODEL)
    w = w_in[0].astype(BF16)
    sw, hw = SSM_WIDTH, HGRN_WIDTH
    u_c, za_c = w[:, :sw], w[:, sw:2 * sw]
    q_c, f_c, i_c, og_c, zb_c = (w[:, 2 * sw + k * hw:2 * sw + (k + 1) * hw] for k in range(5))
    g_c = w[:, 2 * sw + 5 * hw:]
    p32, pbf = _inproj(x2, norm_w[0][None, :], jnp.concatenate([u_c, f_c], axis=1),
                       jnp.concatenate([g_c, za_c, q_c, i_c, og_c, zb_c], axis=1))

    wt, winp, woutp, ptre, ptim = _s5_prep(ssm_lambda_re[0], ssm_lambda_im[0], ssm_b_re[0], ssm_b_im[0],
                                           ssm_c_re[0], ssm_c_im[0], ssm_log_dt[0])
    ya = _s5(p32, pbf, wt, winp, woutp, ptre, ptim, ssm_d[0].reshape(1, SSM_WIDTH),
             ssm_w_glu[0].astype(BF16), ssm_b_glu[0][None, :], batch, seqlen)
    yb = _hgrn(p32, pbf, hgrn_lb_logits, hgrn_norm_w[0][None, :], batch, seqlen)
    out = _merge(x2, ya, yb, pbf, w_proj_a[0].astype(BF16), w_proj_b[0].astype(BF16),
                 w_out[0].astype(BF16), final_norm_w[None, :])
    return out.reshape(batch, seqlen, D_MODEL)
```

```python
import jax
import jax.numpy as jnp
from jax import lax
from jax.experimental import pallas as pl
from jax.experimental.pallas import tpu as pltpu

F32 = jnp.float32
BF16 = jnp.bfloat16
HIGHEST = lax.Precision.HIGHEST

D_MODEL = 1024
SSM_WIDTH = 512
SSM_GROUP = 16
SSM_GROUPS = 32
SSM_STATE = 64
HGRN_WIDTH = 512
HEAD_DIM = 128
HEADS = 4
NORM_EPS = 1e-6
LAMBDA_RE_MAX = -1e-4

P32_WIDTH = SSM_WIDTH + HGRN_WIDTH
PBF_WIDTH = 2 * D_MODEL + SSM_WIDTH + 4 * HGRN_WIDTH
PBF_ZA, PBF_Q, PBF_I, PBF_OG, PBF_ZB = 4, 5, 6, 7, 8

LANES = 128
OCT = LANES // SSM_GROUP
N_OCT = SSM_GROUPS // OCT
OCT_STATE = OCT * SSM_STATE

TM_IN = 512
S5_Q = 8
S5_QP = S5_Q // 2
S5_TB = 2048
S5_NCH = S5_TB // S5_Q
S5_SEG = 8
S5_NV = S5_NCH // S5_SEG
S5_PT_ROWS = S5_NV + 8
HG_TB = 512
HG_CH = 64
HG_SUB = 16
TM_OUT = 512

VMEM_LIMIT = 56 * 1024 * 1024


def _sigmoid(x):
    return jax.nn.sigmoid(x)


def _const_spec(shape):
    nd = len(shape)
    return pl.BlockSpec(shape, lambda *_: (0,) * nd, pipeline_mode=pl.Buffered(1))


def _inproj_kernel(x_ref, nw_ref, w32_ref, wbf_ref, o32_ref, obf_ref):
    x = x_ref[...]
    ms = jnp.mean(x * x, axis=-1, keepdims=True)
    xn = (x * lax.rsqrt(ms + NORM_EPS) * nw_ref[...]).astype(BF16)
    o32_ref[...] = jnp.dot(xn, w32_ref[...], preferred_element_type=F32)
    obf_ref[...] = jnp.dot(xn, wbf_ref[...], preferred_element_type=F32).astype(BF16)


def _inproj(x2, norm_w, w32, wbf):
    t = x2.shape[0]
    return pl.pallas_call(
        _inproj_kernel,
        grid=(t // TM_IN,),
        in_specs=[
            pl.BlockSpec((TM_IN, D_MODEL), lambda i: (i, 0)),
            _const_spec((1, D_MODEL)),
            _const_spec((D_MODEL, P32_WIDTH)),
            _const_spec((D_MODEL, PBF_WIDTH)),
        ],
        out_specs=[
            pl.BlockSpec((TM_IN, P32_WIDTH), lambda i: (i, 0)),
            pl.BlockSpec((TM_IN, PBF_WIDTH), lambda i: (i, 0)),
        ],
        out_shape=[
            jax.ShapeDtypeStruct((t, P32_WIDTH), F32),
            jax.ShapeDtypeStruct((t, PBF_WIDTH), BF16),
        ],
        compiler_params=pltpu.CompilerParams(
            dimension_semantics=("arbitrary",), vmem_limit_bytes=VMEM_LIMIT),
        name="inproj",
    )(x2, norm_w, w32, wbf)


def _disc(lam_re, lam_im, log_dt):
    return jnp.minimum(lam_re, LAMBDA_RE_MAX), lam_im, jnp.exp(log_dt)


def _cpow(lr, li, dt, k):
    mag = jnp.exp(k * (lr * dt))
    ang = k * (li * dt)
    return mag * jnp.cos(ang), mag * jnp.sin(ang)


def _s5_prep_kernel(lre, lim, ldt, bre_t, bim_t, cre_t, cim_t, ctre, ctim,
                    kb_ref, win_ref, woutt_ref, pw_s, xk_s):
    tau = pl.program_id(0).astype(F32)
    lr, li, dt = _disc(lre[...], lim[...], ldt[...])
    ab_re, ab_im = _cpow(lr, li, dt, 1.0)
    den = lr * lr + li * li
    nr = ab_re - 1.0
    pw_s[0] = (nr * lr + ab_im * li) / den
    pw_s[1] = (ab_im * lr - nr * li) / den
    pw_s[2], pw_s[3] = _cpow(lr, li, dt, tau)
    pw_s[4], pw_s[5] = _cpow(lr, li, dt, tau + 1.0)

    col = lax.broadcasted_iota(jnp.int32, (SSM_GROUP, SSM_WIDTH), 1)

    def strip(g, carry):
        rows = pl.ds(pl.multiple_of(g * SSM_GROUP, SSM_GROUP), SSM_GROUP)
        grow = pl.ds(g, 1)
        coef_re, coef_im = pw_s[0, grow, :], pw_s[1, grow, :]
        p_re, p_im = pw_s[2, grow, :], pw_s[3, grow, :]
        p1_re, p1_im = pw_s[4, grow, :], pw_s[5, grow, :]
        br, bi = bre_t[rows, :], bim_t[rows, :]
        bb_re = coef_re * br - coef_im * bi
        bb_im = coef_re * bi + coef_im * br
        x_re = bb_re * p_re - bb_im * p_im
        x_im = bb_re * p_im + bb_im * p_re
        xk_s[0, rows, :] = x_re[:, :SSM_STATE]
        xk_s[1, rows, :] = x_im[:, :SSM_STATE]
        a = g // OCT
        lrows = pl.ds(pl.multiple_of((g % OCT) * SSM_GROUP, SSM_GROUP), SSM_GROUP)
        m_in = col // SSM_STATE == g % OCT
        win_ref[0, a, 0, lrows, :] = jnp.concatenate(
            [jnp.where(m_in, x_re, 0.0), jnp.where(m_in, x_im, 0.0)], axis=1).astype(BF16)
        cr, ci = cre_t[rows, :], cim_t[rows, :]
        w_re = cr * p1_re - ci * p1_im
        w_im = cr * p1_im + ci * p1_re
        woutt_ref[0, a, 0, lrows, :] = jnp.concatenate(
            [jnp.where(m_in, w_re, 0.0), jnp.where(m_in, -w_im, 0.0)], axis=1).astype(BF16)
        return carry

    lax.fori_loop(0, SSM_GROUPS, strip, 0)

    kfull = (jnp.dot(xk_s[0], ctre[...], precision=HIGHEST, preferred_element_type=F32)
             - jnp.dot(xk_s[1], ctim[...], precision=HIGHEST, preferred_element_type=F32))
    rowk = lax.broadcasted_iota(jnp.int32, (SSM_WIDTH, SSM_WIDTH), 0)
    colk = lax.broadcasted_iota(jnp.int32, (SSM_WIDTH, SSM_WIDTH), 1)
    kb_ref[0] = jnp.where(rowk // SSM_GROUP == colk // SSM_GROUP, kfull, 0.0).astype(BF16)


def _s5_ptab_kernel(lre, lim, ldt, ptre_ref, ptim_ref):
    lr, li, dt = _disc(lre[0], lim[0], ldt[0])
    j = lax.broadcasted_iota(jnp.int32, (S5_PT_ROWS, OCT_STATE), 0).astype(F32) * float(S5_Q)
    ptre_ref[0], ptim_ref[0] = _cpow(lr, li, dt, j)


def _s5_prep(lam_re, lam_im, b_re, b_im, c_re, c_im, log_dt):
    g, n, p = SSM_GROUPS, SSM_STATE, SSM_GROUP
    ldt_gn = jnp.broadcast_to(log_dt[:, None], (g, n))
    compact = lambda a: jnp.tile(a, (1, OCT))
    b_t = lambda b: jnp.tile(b.transpose(0, 2, 1).reshape(g * p, n), (1, OCT))
    c_t = lambda c: jnp.tile(c.reshape(g * p, n), (1, OCT))
    c_n = lambda c: c.transpose(2, 0, 1).reshape(n, g * p)
    args = (compact(lam_re), compact(lam_im), compact(ldt_gn),
            b_t(b_re), b_t(b_im), c_t(c_re), c_t(c_im), c_n(c_re), c_n(c_im))
    mat = (S5_QP, N_OCT, 2, LANES, 2 * OCT_STATE)
    blk = (1, N_OCT, 1, LANES, 2 * OCT_STATE)
    kb, win, woutt = pl.pallas_call(
        _s5_prep_kernel,
        grid=(S5_Q,),
        in_specs=[_const_spec(a.shape) for a in args],
        out_specs=[
            pl.BlockSpec((1, SSM_WIDTH, SSM_WIDTH), lambda t: (t, 0, 0)),
            pl.BlockSpec(blk, lambda t: ((S5_Q - 1 - t) // 2, 0, (S5_Q - 1 - t) % 2, 0, 0)),
            pl.BlockSpec(blk, lambda t: (t // 2, 0, t % 2, 0, 0)),
        ],
        out_shape=[
            jax.ShapeDtypeStruct((S5_Q, SSM_WIDTH, SSM_WIDTH), BF16),
            jax.ShapeDtypeStruct(mat, BF16),
            jax.ShapeDtypeStruct(mat, BF16),
        ],
        scratch_shapes=[pltpu.VMEM((6, SSM_GROUPS, SSM_WIDTH), F32),
                        pltpu.VMEM((2, SSM_WIDTH, SSM_STATE), F32)],
        compiler_params=pltpu.CompilerParams(
            dimension_semantics=("arbitrary",), vmem_limit_bytes=VMEM_LIMIT),
        name="s5_prep",
    )(*args)

    row_r = lambda a: a.reshape(N_OCT, 1, OCT_STATE)
    tab = jax.ShapeDtypeStruct((N_OCT, S5_PT_ROWS, OCT_STATE), F32)
    ptre, ptim = pl.pallas_call(
        _s5_ptab_kernel,
        grid=(N_OCT,),
        in_specs=[pl.BlockSpec((1, 1, OCT_STATE), lambda a: (a, 0, 0))] * 3,
        out_specs=[pl.BlockSpec((1, S5_PT_ROWS, OCT_STATE), lambda a: (a, 0, 0))] * 2,
        out_shape=[tab, tab],
        compiler_params=pltpu.CompilerParams(dimension_semantics=("arbitrary",)),
        name="s5_ptab",
    )(row_r(lam_re), row_r(lam_im), row_r(ldt_gn))

    kbo = jnp.stack([kb[:, LANES * a:LANES * (a + 1), LANES * a:LANES * (a + 1)] for a in range(N_OCT)], axis=1)
    zero = jnp.zeros_like(kbo[0])

    def pair_tile(d):
        top = jnp.concatenate([kbo[2 * d], kbo[2 * d + 1]], axis=-1)
        bot = jnp.concatenate([kbo[2 * d - 1] if d > 0 else zero, kbo[2 * d]], axis=-1)
        return jnp.concatenate([top, bot], axis=-2)

    wt = jnp.stack([pair_tile(d) for d in range(S5_QP)])
    winp = win.reshape(S5_QP, N_OCT, 2 * LANES, 2 * OCT_STATE)
    woutp = woutt.reshape(S5_QP, N_OCT, 2 * LANES, 2 * OCT_STATE)
    return wt, winp, woutp, ptre, ptim


def _gelu_tanh(x):
    c = 0.7978845608028654
    return x * (0.5 * (1.0 + jnp.tanh(c * (x + 0.044715 * (x * x * x)))))


def _s5_kernel(u0_ref, u1_ref, u2_ref, u3_ref, z_ref, wt_ref, winp_ref, woutp_ref, ptre_ref, ptim_ref,
               d_ref, wglu_ref, bglu_ref, o_ref, y0_s, y1_s, y2_s, y3_s, cre_scr, cim_scr):
    u_refs = (u0_ref, u1_ref, u2_ref, u3_ref)
    y_scrs = (y0_s, y1_s, y2_s, y3_s)

    @pl.when(pl.program_id(1) == 0)
    def _():
        cre_scr[...] = jnp.zeros_like(cre_scr)
        cim_scr[...] = jnp.zeros_like(cim_scr)

    seg_stride = S5_Q * S5_NV

    def tok(a, s):
        return jnp.concatenate(
            [u_refs[a][pl.ds(s + S5_Q * v, S5_SEG, stride=seg_stride), :] for v in range(S5_NV)], axis=0)

    xp = [[jnp.concatenate([tok(a, 2 * sp), tok(a, 2 * sp + 1)], axis=1).astype(BF16) for a in range(N_OCT)]
          for sp in range(S5_QP)]

    def cmul_add(b_re, b_im, m_re, m_im, x_re, x_im):
        return b_re + m_re * x_re - m_im * x_im, b_im + m_re * x_im + m_im * x_re

    hs = []
    for a in range(N_OCT):
        acc = None
        for sp in range(S5_QP):
            part = jnp.dot(xp[sp][a], winp_ref[sp, a], preferred_element_type=F32)
            acc = part if acc is None else acc + part
        blk = lambda v: (acc[S5_SEG * v:S5_SEG * (v + 1), :OCT_STATE], acc[S5_SEG * v:S5_SEG * (v + 1), OCT_STATE:])
        m_re, m_im = ptre_ref[a, 1:2, :], ptim_ref[a, 1:2, :]
        loc = [blk(0)]
        for v in range(1, S5_NV):
            loc.append(cmul_add(*blk(v), m_re, m_im, *loc[-1]))
        l_re, l_im = ptre_ref[a, S5_NV:S5_NV + 1, :], ptim_ref[a, S5_NV:S5_NV + 1, :]
        c_re, c_im = cre_scr[a:a + 1, :], cim_scr[a:a + 1, :]
        carry = []
        for r in range(S5_SEG):
            carry.append((c_re, c_im))
            c_re, c_im = cmul_add(loc[-1][0][r:r + 1, :], loc[-1][1][r:r + 1, :], l_re, l_im, c_re, c_im)
        cre_scr[a:a + 1, :] = c_re
        cim_scr[a:a + 1, :] = c_im
        cs_re = jnp.concatenate([c[0] for c in carry], axis=0)
        cs_im = jnp.concatenate([c[1] for c in carry], axis=0)
        ent = [(cs_re, cs_im)]
        for v in range(S5_NV - 1):
            ent.append(cmul_add(*loc[v], ptre_ref[a, v + 1:v + 2, :], ptim_ref[a, v + 1:v + 2, :], cs_re, cs_im))
        hs.append(jnp.concatenate([jnp.concatenate([e[0] for e in ent], axis=0),
                                   jnp.concatenate([e[1] for e in ent], axis=0)], axis=1).astype(BF16))

    for tp in range(S5_QP):
        for a in range(N_OCT):
            acc = lax.dot_general(hs[a], woutp_ref[tp, a], (((1,), (1,)), ((), ())), preferred_element_type=F32)
            for sp in range(tp + 1):
                acc = acc + jnp.dot(xp[sp][a], wt_ref[tp - sp, a], preferred_element_type=F32)
            for v in range(S5_NV):
                rows = slice(S5_SEG * v, S5_SEG * (v + 1))
                y_scrs[a][pl.ds(2 * tp + S5_Q * v, S5_SEG, stride=seg_stride), :] = acc[rows, :LANES]
                y_scrs[a][pl.ds(2 * tp + 1 + S5_Q * v, S5_SEG, stride=seg_stride), :] = acc[rows, LANES:]

    u = jnp.concatenate([r[...] for r in u_refs], axis=1)
    y = jnp.concatenate([r[...] for r in y_scrs], axis=1) + d_ref[...] * u
    y = _gelu_tanh(y)
    gate = jnp.dot(y.astype(BF16), wglu_ref[...], preferred_element_type=F32) + bglu_ref[...]
    y = y * _sigmoid(gate)
    z = z_ref[...].astype(F32)
    o_ref[...] = (y * (z * _sigmoid(z))).astype(BF16)


def _s5(p32, pbf, wt, winp, woutp, ptre, ptim, d_row, w_glu, b_glu, batch, seqlen):
    nb = seqlen // S5_TB
    t = batch * seqlen
    u_tile = lambda a: pl.BlockSpec((S5_TB, LANES), lambda b, i, a=a: (b * nb + i, a))
    return pl.pallas_call(
        _s5_kernel,
        grid=(batch, nb),
        in_specs=[u_tile(a) for a in range(N_OCT)] + [
            pl.BlockSpec((S5_TB, SSM_WIDTH), lambda b, i: (b * nb + i, PBF_ZA)),
            _const_spec(wt.shape), _const_spec(winp.shape), _const_spec(woutp.shape),
            _const_spec(ptre.shape), _const_spec(ptim.shape),
            _const_spec((1, SSM_WIDTH)), _const_spec((SSM_WIDTH, SSM_WIDTH)), _const_spec((1, SSM_WIDTH)),
        ],
        out_specs=pl.BlockSpec((S5_TB, SSM_WIDTH), lambda b, i: (b * nb + i, 0)),
        out_shape=jax.ShapeDtypeStruct((t, SSM_WIDTH), BF16),
        scratch_shapes=[pltpu.VMEM((S5_TB, LANES), F32)] * N_OCT + [
            pltpu.VMEM((8, OCT_STATE), F32),
            pltpu.VMEM((8, OCT_STATE), F32),
        ],
        compiler_params=pltpu.CompilerParams(
            dimension_semantics=("arbitrary", "arbitrary"), vmem_limit_bytes=VMEM_LIMIT),
        name="s5",
    )(p32, p32, p32, p32, pbf, wt, winp, woutp, ptre, ptim, d_row, w_glu, b_glu)


def _hgrn_kernel(f_ref, q_ref, i_ref, og_ref, z_ref, lbl_ref, nw_ref, o_ref,
                 q0_s, qc_s, kt_s, k0_s, k1_s, k2_s, ke_s, dec_s, oacc_s, st_s, upd_s, sc_s, am_s):
    @pl.when(pl.program_id(1) == 0)
    def _():
        st_s[...] = jnp.zeros_like(st_s)

    logits = lbl_ref[...]
    e = jnp.exp(logits - jnp.max(logits, axis=0, keepdims=True))
    lb = (e / jnp.sum(e, axis=0, keepdims=True))[0:1, :]

    q = q_ref[...].astype(F32)
    qf = q * _sigmoid(q)
    forget = lb + (1.0 - lb) * _sigmoid(f_ref[...])
    lf = jnp.log(forget)
    key = 1.0 - forget

    row = lax.broadcasted_iota(jnp.int32, (HG_TB, HGRN_WIDTH), 0)
    r_sub = row % HG_SUB
    r_ch = row % HG_CH

    def down(x, d):
        return pltpu.roll(x, d, 0)

    def up(x, d):
        return pltpu.roll(x, HG_TB - d, 0)

    a = lf
    d = 1
    while d < HG_SUB:
        a = a + jnp.where(r_sub >= d, down(a, d), 0.0)
        d *= 2
    a3 = a.reshape(HG_TB // HG_SUB, HG_SUB, HGRN_WIDTH)
    tsub = jnp.broadcast_to(a3[:, HG_SUB - 1:HG_SUB, :], a3.shape).reshape(HG_TB, HGRN_WIDTH)
    n_sub = HG_CH // HG_SUB
    prev = [jnp.where(r_ch >= HG_SUB * k, down(tsub, HG_SUB * k), 0.0) for k in range(1, n_sub)]
    nxt = [jnp.where(r_ch < HG_CH - HG_SUB * k, up(tsub, HG_SUB * k), 0.0) for k in range(1, n_sub)]
    eprev = prev[0] + prev[1] + prev[2]
    enext = nxt[0] + nxt[1] + nxt[2]
    suf = tsub - a

    q0 = qf * jnp.exp(a)
    q0_s[...] = q0.astype(BF16)
    qc_s[...] = (q0 * jnp.exp(eprev)).astype(BF16)
    kt_s[...] = (key * jnp.exp(-a)).astype(BF16)
    k0 = key * jnp.exp(suf)
    k0_s[...] = k0.astype(BF16)
    k1 = k0 * jnp.exp(nxt[0])
    k1_s[...] = k1.astype(BF16)
    k2 = k1 * jnp.exp(nxt[1])
    k2_s[...] = k2.astype(BF16)
    ke_s[...] = (k2 * jnp.exp(nxt[2])).astype(BF16)
    dec_s[...] = jnp.exp(eprev + tsub + enext)

    tq = lax.broadcasted_iota(jnp.int32, (HG_CH, n_sub * HG_CH), 0)
    cc = lax.broadcasted_iota(jnp.int32, (HG_CH, n_sub * HG_CH), 1)
    cls = cc // HG_CH
    ts = cc % HG_CH
    bi = tq // HG_SUB
    bj = ts // HG_SUB
    mask4 = ((cls == 0) & (bi == bj) & (ts <= tq)) | ((cls > 0) & ((bi - bj) == cls))

    n_ch = HG_TB // HG_CH
    half = n_sub * HG_CH // 2
    units = [(c, h, slice(c * HG_CH, (c + 1) * HG_CH), slice(HEAD_DIM * h, HEAD_DIM * (h + 1)))
             for c in range(n_ch) for h in range(HEADS)]
    for c, h, rows, ls in units:
        kcat = jnp.concatenate([kt_s[rows, ls], k0_s[rows, ls], k1_s[rows, ls], k2_s[rows, ls]], axis=0)
        sc_s[c * HEADS + h] = lax.dot_general(q0_s[rows, ls], kcat, (((1,), (1,)), ((), ())),
                                              preferred_element_type=F32)
    for c, h, rows, ls in units:
        upd_s[c * HEADS + h] = lax.dot_general(i_ref[rows, ls], ke_s[rows, ls], (((0,), (0,)), ((), ())),
                                               preferred_element_type=F32)
    for c, h, rows, ls in units:
        sc = jnp.where(mask4, sc_s[c * HEADS + h], 0.0)
        am_s[c * HEADS + h] = (sc[:, :half] + sc[:, half:]).astype(BF16)
    for c, h, rows, ls in units:
        vv = i_ref[rows, ls]
        oacc_s[rows, ls] = jnp.dot(am_s[c * HEADS + h], jnp.concatenate([vv, vv], axis=0),
                                   preferred_element_type=F32)
    for h in range(HEADS):
        ls = slice(HEAD_DIM * h, HEAD_DIM * (h + 1))
        st = st_s[h]
        for c in range(n_ch):
            rows = slice(c * HG_CH, (c + 1) * HG_CH)
            oacc_s[rows, ls] += lax.dot_general(qc_s[rows, ls], st.astype(BF16), (((1,), (1,)), ((), ())),
                                                preferred_element_type=F32)
            st = st * dec_s[c * HG_CH:c * HG_CH + 1, ls] + upd_s[c * HEADS + h]
        st_s[h] = st

    o = oacc_s[...] * _sigmoid(og_ref[...].astype(F32))
    parts = []
    for h in range(HEADS):
        oh = o[:, HEAD_DIM * h:HEAD_DIM * (h + 1)]
        ms = jnp.mean(oh * oh, axis=-1, keepdims=True)
        parts.append(oh * lax.rsqrt(ms + NORM_EPS))
    o = jnp.concatenate(parts, axis=1) * nw_ref[...]
    z = z_ref[...].astype(F32)
    o_ref[...] = (o * (z * _sigmoid(z))).astype(BF16)


def _hgrn(p32, pbf, lb_logits, norm_w, batch, seqlen):
    nb = seqlen // HG_TB
    t = batch * seqlen
    col = lambda k: pl.BlockSpec((HG_TB, HGRN_WIDTH), lambda b, i, k=k: (b * nb + i, k))
    wide = (HG_TB, HGRN_WIDTH)
    return pl.pallas_call(
        _hgrn_kernel,
        grid=(batch, nb),
        in_specs=[col(1), col(PBF_Q), col(PBF_I), col(PBF_OG), col(PBF_ZB),
                  _const_spec(lb_logits.shape), _const_spec((1, HGRN_WIDTH))],
        out_specs=pl.BlockSpec((HG_TB, HGRN_WIDTH), lambda b, i: (b * nb + i, 0)),
        out_shape=jax.ShapeDtypeStruct((t, HGRN_WIDTH), BF16),
        scratch_shapes=[pltpu.VMEM(wide, BF16)] * 7 + [
            pltpu.VMEM(wide, F32), pltpu.VMEM(wide, F32),
            pltpu.VMEM((HEADS, HEAD_DIM, HEAD_DIM), F32),
            pltpu.VMEM((HG_TB // HG_CH * HEADS, HEAD_DIM, HEAD_DIM), F32),
            pltpu.VMEM((HG_TB // HG_CH * HEADS, HG_CH, HG_CH // HG_SUB * HG_CH), F32),
            pltpu.VMEM((HG_TB // HG_CH * HEADS, HG_CH, HG_CH // HG_SUB * HG_CH // 2), BF16)],
        compiler_params=pltpu.CompilerParams(
            dimension_semantics=("arbitrary", "arbitrary"), vmem_limit_bytes=VMEM_LIMIT),
        name="hgrn",
    )(p32, pbf, pbf, pbf, pbf, lb_logits, norm_w)


def _merge_kernel(x_ref, ya_ref, yb_ref, ga_ref, gb_ref, wpa_ref, wpb_ref, wo_ref, fnw_ref, o_ref):
    pa = jnp.dot(ya_ref[...], wpa_ref[...], preferred_element_type=F32)
    pb = jnp.dot(yb_ref[...], wpb_ref[...], preferred_element_type=F32)
    merged = _sigmoid(ga_ref[...].astype(F32)) * pa + _sigmoid(gb_ref[...].astype(F32)) * pb
    h = x_ref[...] + jnp.dot(merged.astype(BF16), wo_ref[...], preferred_element_type=F32)
    ms = jnp.mean(h * h, axis=-1, keepdims=True)
    o_ref[...] = h * lax.rsqrt(ms + NORM_EPS) * fnw_ref[...]


def _merge(x2, ya, yb, pbf, w_pa, w_pb, w_out, fnw):
    t = x2.shape[0]
    return pl.pallas_call(
        _merge_kernel,
        grid=(t // TM_OUT,),
        in_specs=[
            pl.BlockSpec((TM_OUT, D_MODEL), lambda i: (i, 0)),
            pl.BlockSpec((TM_OUT, SSM_WIDTH), lambda i: (i, 0)),
            pl.BlockSpec((TM_OUT, HGRN_WIDTH), lambda i: (i, 0)),
            pl.BlockSpec((TM_OUT, D_MODEL), lambda i: (i, 0)),
            pl.BlockSpec((TM_OUT, D_MODEL), lambda i: (i, 1)),
            _const_spec((SSM_WIDTH, D_MODEL)), _const_spec((HGRN_WIDTH, D_MODEL)),
            _const_spec((D_MODEL, D_MODEL)), _const_spec((1, D_MODEL)),
        ],
        out_specs=pl.BlockSpec((TM_OUT, D_MODEL), lambda i: (i, 0)),
        out_shape=jax.ShapeDtypeStruct((t, D_MODEL), F32),
        compiler_params=pltpu.CompilerParams(
            dimension_semantics=("arbitrary",), vmem_limit_bytes=VMEM_LIMIT),
        name="merge",
    )(x2, ya, yb, pbf, pbf, w_pa, w_pb, w_out, fnw)


def kernel(x, norm_w, w_in, ssm_lambda_re, ssm_lambda_im, ssm_b_re, ssm_b_im, ssm_c_re, ssm_c_im, ssm_d,
           ssm_log_dt, ssm_w_glu, ssm_b_glu, hgrn_lb_logits, hgrn_norm_w, w_proj_a, w_proj_b, w_out,
           final_norm_w):
    batch, seqlen, _ = x.shape
    assert norm_w.shape[0] == 1, "single-layer block"
    assert seqlen % S5_TB == 0 and seqlen % HG_TB == 0
    x2 = x.reshape(batch * seqlen, D_MODEL)
    w = w_in[0].astype(BF16)
    sw, hw = SSM_WIDTH, HGRN_WIDTH
    u_c, za_c = w[:, :sw], w[:, sw:2 * sw]
    q_c, f_c, i_c, og_c, zb_c = (w[:, 2 * sw + k * hw:2 * sw + (k + 1) * hw] for k in range(5))
    g_c = w[:, 2 * sw + 5 * hw:]
    p32, pbf = _inproj(x2, norm_w[0][None, :], jnp.concatenate([u_c, f_c], axis=1),
                       jnp.concatenate([g_c, za_c, q_c, i_c, og_c, zb_c], axis=1))

    wt, winp, woutp, ptre, ptim = _s5_prep(ssm_lambda_re[0], ssm_lambda_im[0], ssm_b_re[0], ssm_b_im[0],
                                           ssm_c_re[0], ssm_c_im[0], ssm_log_dt[0])
    ya = _s5(p32, pbf, wt, winp, woutp, ptre, ptim, ssm_d[0].reshape(1, SSM_WIDTH),
             ssm_w_glu[0].astype(BF16), ssm_b_glu[0][None, :], batch, seqlen)
    yb = _hgrn(p32, pbf, hgrn_lb_logits, hgrn_norm_w[0][None, :], batch, seqlen)
    out = _merge(x2, ya, yb, pbf, w_proj_a[0].astype(BF16), w_proj_b[0].astype(BF16),
                 w_out[0].astype(BF16), final_norm_w[None, :])
    return out.reshape(batch, seqlen, D_MODEL)
```

```python
import functools

import jax
import jax.numpy as jnp
from jax import lax
from jax.experimental import pallas as pl
from jax.experimental.pallas import tpu as pltpu

F32 = jnp.float32
BF16 = jnp.bfloat16
HIGHEST = lax.Precision.HIGHEST

D_MODEL = 1024
SSM_WIDTH = 512
SSM_GROUP = 16
SSM_GROUPS = 32
SSM_STATE = 64
HGRN_WIDTH = 512
HEAD_DIM = 128
HEADS = 4
NORM_EPS = 1e-6
LAMBDA_RE_MAX = -1e-4

PH_WIDTH = 5 * HGRN_WIDTH
PH_F, PH_Q, PH_I, PH_OG, PH_ZB = 0, 1, 2, 3, 4
PBF_WIDTH = 2 * D_MODEL + SSM_WIDTH
PBF_ZA = 4

LANES = 128
OCT = LANES // SSM_GROUP
N_OCT = SSM_GROUPS // OCT
OCT_STATE = OCT * SSM_STATE

S5_Q = 8
S5_QP = S5_Q // 2
S5_TB = 2048
S5_NCH = S5_TB // S5_Q
S5_SEG = 8
S5_NV = S5_NCH // S5_SEG
S5_PT_ROWS = S5_NV + 8
HG_TB = 512
HG_CH = 64
HG_SUB = 16
TM_OUT = 512

VMEM_LIMIT = 56 * 1024 * 1024


def _sigmoid(x):
    return jax.nn.sigmoid(x)


def _const_spec(shape):
    nd = len(shape)
    return pl.BlockSpec(shape, lambda *_: (0,) * nd, pipeline_mode=pl.Buffered(1))


def _disc(lam_re, lam_im, log_dt):
    return jnp.minimum(lam_re, LAMBDA_RE_MAX), lam_im, jnp.exp(log_dt)


def _cpow(lr, li, dt, k):
    mag = jnp.exp(k * (lr * dt))
    ang = k * (li * dt)
    return mag * jnp.cos(ang), mag * jnp.sin(ang)


def _s5_prep_kernel(lre, lim, ldt, bre_t, bim_t, cre_t, cim_t, ctre, ctim,
                    kb_ref, win_ref, woutt_ref, pw_s, xk_s):
    tau = pl.program_id(0).astype(F32)
    lr, li, dt = _disc(lre[...], lim[...], ldt[...])
    ab_re, ab_im = _cpow(lr, li, dt, 1.0)
    den = lr * lr + li * li
    nr = ab_re - 1.0
    pw_s[0] = (nr * lr + ab_im * li) / den
    pw_s[1] = (ab_im * lr - nr * li) / den
    pw_s[2], pw_s[3] = _cpow(lr, li, dt, tau)
    pw_s[4], pw_s[5] = _cpow(lr, li, dt, tau + 1.0)

    col = lax.broadcasted_iota(jnp.int32, (SSM_GROUP, SSM_WIDTH), 1)

    def strip(g, carry):
        rows = pl.ds(pl.multiple_of(g * SSM_GROUP, SSM_GROUP), SSM_GROUP)
        grow = pl.ds(g, 1)
        coef_re, coef_im = pw_s[0, grow, :], pw_s[1, grow, :]
        p_re, p_im = pw_s[2, grow, :], pw_s[3, grow, :]
        p1_re, p1_im = pw_s[4, grow, :], pw_s[5, grow, :]
        br, bi = bre_t[rows, :], bim_t[rows, :]
        bb_re = coef_re * br - coef_im * bi
        bb_im = coef_re * bi + coef_im * br
        x_re = bb_re * p_re - bb_im * p_im
        x_im = bb_re * p_im + bb_im * p_re
        xk_s[0, rows, :] = x_re[:, :SSM_STATE]
        xk_s[1, rows, :] = x_im[:, :SSM_STATE]
        a = g // OCT
        lrows = pl.ds(pl.multiple_of((g % OCT) * SSM_GROUP, SSM_GROUP), SSM_GROUP)
        m_in = col // SSM_STATE == g % OCT
        win_ref[0, a, 0, lrows, :] = jnp.concatenate(
            [jnp.where(m_in, x_re, 0.0), jnp.where(m_in, x_im, 0.0)], axis=1).astype(BF16)
        cr, ci = cre_t[rows, :], cim_t[rows, :]
        w_re = cr * p1_re - ci * p1_im
        w_im = cr * p1_im + ci * p1_re
        woutt_ref[0, a, 0, lrows, :] = jnp.concatenate(
            [jnp.where(m_in, w_re, 0.0), jnp.where(m_in, -w_im, 0.0)], axis=1).astype(BF16)
        return carry

    lax.fori_loop(0, SSM_GROUPS, strip, 0)

    kfull = (jnp.dot(xk_s[0], ctre[...], precision=HIGHEST, preferred_element_type=F32)
             - jnp.dot(xk_s[1], ctim[...], precision=HIGHEST, preferred_element_type=F32))
    rowk = lax.broadcasted_iota(jnp.int32, (SSM_WIDTH, SSM_WIDTH), 0)
    colk = lax.broadcasted_iota(jnp.int32, (SSM_WIDTH, SSM_WIDTH), 1)
    kb_ref[0] = jnp.where(rowk // SSM_GROUP == colk // SSM_GROUP, kfull, 0.0).astype(BF16)


def _s5_ptab_kernel(lre, lim, ldt, ptre_ref, ptim_ref):
    lr, li, dt = _disc(lre[0], lim[0], ldt[0])
    j = lax.broadcasted_iota(jnp.int32, (S5_PT_ROWS, OCT_STATE), 0).astype(F32) * float(S5_Q)
    ptre_ref[0], ptim_ref[0] = _cpow(lr, li, dt, j)


def _s5_prep(lam_re, lam_im, b_re, b_im, c_re, c_im, log_dt):
    g, n, p = SSM_GROUPS, SSM_STATE, SSM_GROUP
    ldt_gn = jnp.broadcast_to(log_dt[:, None], (g, n))
    compact = lambda a: jnp.tile(a, (1, OCT))
    b_t = lambda b: jnp.tile(b.transpose(0, 2, 1).reshape(g * p, n), (1, OCT))
    c_t = lambda c: jnp.tile(c.reshape(g * p, n), (1, OCT))
    c_n = lambda c: c.transpose(2, 0, 1).reshape(n, g * p)
    args = (compact(lam_re), compact(lam_im), compact(ldt_gn),
            b_t(b_re), b_t(b_im), c_t(c_re), c_t(c_im), c_n(c_re), c_n(c_im))
    mat = (S5_QP, N_OCT, 2, LANES, 2 * OCT_STATE)
    blk = (1, N_OCT, 1, LANES, 2 * OCT_STATE)
    kb, win, woutt = pl.pallas_call(
        _s5_prep_kernel,
        grid=(S5_Q,),
        in_specs=[_const_spec(a.shape) for a in args],
        out_specs=[
            pl.BlockSpec((1, SSM_WIDTH, SSM_WIDTH), lambda t: (t, 0, 0)),
            pl.BlockSpec(blk, lambda t: ((S5_Q - 1 - t) // 2, 0, (S5_Q - 1 - t) % 2, 0, 0)),
            pl.BlockSpec(blk, lambda t: (t // 2, 0, t % 2, 0, 0)),
        ],
        out_shape=[
            jax.ShapeDtypeStruct((S5_Q, SSM_WIDTH, SSM_WIDTH), BF16),
            jax.ShapeDtypeStruct(mat, BF16),
            jax.ShapeDtypeStruct(mat, BF16),
        ],
        scratch_shapes=[pltpu.VMEM((6, SSM_GROUPS, SSM_WIDTH), F32),
                        pltpu.VMEM((2, SSM_WIDTH, SSM_STATE), F32)],
        compiler_params=pltpu.CompilerParams(
            dimension_semantics=("arbitrary",), vmem_limit_bytes=VMEM_LIMIT),
        name="s5_prep",
    )(*args)

    row_r = lambda a: a.reshape(N_OCT, 1, OCT_STATE)
    tab = jax.ShapeDtypeStruct((N_OCT, S5_PT_ROWS, OCT_STATE), F32)
    ptre, ptim = pl.pallas_call(
        _s5_ptab_kernel,
        grid=(N_OCT,),
        in_specs=[pl.BlockSpec((1, 1, OCT_STATE), lambda a: (a, 0, 0))] * 3,
        out_specs=[pl.BlockSpec((1, S5_PT_ROWS, OCT_STATE), lambda a: (a, 0, 0))] * 2,
        out_shape=[tab, tab],
        compiler_params=pltpu.CompilerParams(dimension_semantics=("arbitrary",)),
        name="s5_ptab",
    )(row_r(lam_re), row_r(lam_im), row_r(ldt_gn))

    kbo = jnp.stack([kb[:, LANES * a:LANES * (a + 1), LANES * a:LANES * (a + 1)] for a in range(N_OCT)], axis=1)
    zero = jnp.zeros_like(kbo[0])

    def pair_tile(d):
        top = jnp.concatenate([kbo[2 * d], kbo[2 * d + 1]], axis=-1)
        bot = jnp.concatenate([kbo[2 * d - 1] if d > 0 else zero, kbo[2 * d]], axis=-1)
        return jnp.concatenate([top, bot], axis=-2)

    wt = jnp.stack([pair_tile(d) for d in range(S5_QP)])
    winp = win.reshape(S5_QP, N_OCT, 2 * LANES, 2 * OCT_STATE)
    woutp = woutt.reshape(S5_QP, N_OCT, 2 * LANES, 2 * OCT_STATE)
    return wt, winp, woutp, ptre, ptim


def _gelu_tanh(x):
    c = 0.7978845608028654
    return x * (0.5 * (1.0 + jnp.tanh(c * (x + 0.044715 * (x * x * x)))))


def _s5_kernel(u0_ref, u1_ref, u2_ref, u3_ref, z_ref, wt_ref, winp_ref, woutp_ref, ptre_ref, ptim_ref,
               d_ref, wglu_ref, bglu_ref, o_ref, y0_s, y1_s, y2_s, y3_s, cre_scr, cim_scr):
    u_refs = (u0_ref, u1_ref, u2_ref, u3_ref)
    y_scrs = (y0_s, y1_s, y2_s, y3_s)

    @pl.when(pl.program_id(1) == 0)
    def _():
        cre_scr[...] = jnp.zeros_like(cre_scr)
        cim_scr[...] = jnp.zeros_like(cim_scr)

    seg_stride = S5_Q * S5_NV

    def tok(a, s):
        return jnp.concatenate(
            [u_refs[a][pl.ds(s + S5_Q * v, S5_SEG, stride=seg_stride), :] for v in range(S5_NV)], axis=0)

    xp = [[jnp.concatenate([tok(a, 2 * sp), tok(a, 2 * sp + 1)], axis=1).astype(BF16) for a in range(N_OCT)]
          for sp in range(S5_QP)]

    def cmul_add(b_re, b_im, m_re, m_im, x_re, x_im):
        return b_re + m_re * x_re - m_im * x_im, b_im + m_re * x_im + m_im * x_re

    hs = []
    for a in range(N_OCT):
        acc = None
        for sp in range(S5_QP):
            part = jnp.dot(xp[sp][a], winp_ref[sp, a], preferred_element_type=F32)
            acc = part if acc is None else acc + part
        blk = lambda v: (acc[S5_SEG * v:S5_SEG * (v + 1), :OCT_STATE], acc[S5_SEG * v:S5_SEG * (v + 1), OCT_STATE:])
        m_re, m_im = ptre_ref[a, 1:2, :], ptim_ref[a, 1:2, :]
        loc = [blk(0)]
        for v in range(1, S5_NV):
            loc.append(cmul_add(*blk(v), m_re, m_im, *loc[-1]))
        l_re, l_im = ptre_ref[a, S5_NV:S5_NV + 1, :], ptim_ref[a, S5_NV:S5_NV + 1, :]
        c_re, c_im = cre_scr[a:a + 1, :], cim_scr[a:a + 1, :]
        carry = []
        for r in range(S5_SEG):
            carry.append((c_re, c_im))
            c_re, c_im = cmul_add(loc[-1][0][r:r + 1, :], loc[-1][1][r:r + 1, :], l_re, l_im, c_re, c_im)
        cre_scr[a:a + 1, :] = c_re
        cim_scr[a:a + 1, :] = c_im
        cs_re = jnp.concatenate([c[0] for c in carry], axis=0)
        cs_im = jnp.concatenate([c[1] for c in carry], axis=0)
        ent = [(cs_re, cs_im)]
        for v in range(S5_NV - 1):
            ent.append(cmul_add(*loc[v], ptre_ref[a, v + 1:v + 2, :], ptim_ref[a, v + 1:v + 2, :], cs_re, cs_im))
        hs.append(jnp.concatenate([jnp.concatenate([e[0] for e in ent], axis=0),
                                   jnp.concatenate([e[1] for e in ent], axis=0)], axis=1).astype(BF16))

    for tp in range(S5_QP):
        for a in range(N_OCT):
            acc = lax.dot_general(hs[a], woutp_ref[tp, a], (((1,), (1,)), ((), ())), preferred_element_type=F32)
            for sp in range(tp + 1):
                acc = acc + jnp.dot(xp[sp][a], wt_ref[tp - sp, a], preferred_element_type=F32)
            for v in range(S5_NV):
                rows = slice(S5_SEG * v, S5_SEG * (v + 1))
                y_scrs[a][pl.ds(2 * tp + S5_Q * v, S5_SEG, stride=seg_stride), :] = acc[rows, :LANES]
                y_scrs[a][pl.ds(2 * tp + 1 + S5_Q * v, S5_SEG, stride=seg_stride), :] = acc[rows, LANES:]

    u = jnp.concatenate([r[...] for r in u_refs], axis=1)
    y = jnp.concatenate([r[...] for r in y_scrs], axis=1) + d_ref[...] * u
    y = _gelu_tanh(y)
    gate = jnp.dot(y.astype(BF16), wglu_ref[...], preferred_element_type=F32) + bglu_ref[...]
    y = y * _sigmoid(gate)
    z = z_ref[...].astype(F32)
    o_ref[...] = (y * (z * _sigmoid(z))).astype(BF16)


def _s5(u32, pbf, wt, winp, woutp, ptre, ptim, d_row, w_glu, b_glu, batch, seqlen):
    nb = seqlen // S5_TB
    t = batch * seqlen
    u_tile = lambda a: pl.BlockSpec((S5_TB, LANES), lambda b, i, a=a: (b * nb + i, a))
    return pl.pallas_call(
        _s5_kernel,
        grid=(batch, nb),
        in_specs=[u_tile(a) for a in range(N_OCT)] + [
            pl.BlockSpec((S5_TB, SSM_WIDTH), lambda b, i: (b * nb + i, PBF_ZA)),
            _const_spec(wt.shape), _const_spec(winp.shape), _const_spec(woutp.shape),
            _const_spec(ptre.shape), _const_spec(ptim.shape),
            _const_spec((1, SSM_WIDTH)), _const_spec((SSM_WIDTH, SSM_WIDTH)), _const_spec((1, SSM_WIDTH)),
        ],
        out_specs=pl.BlockSpec((S5_TB, SSM_WIDTH), lambda b, i: (b * nb + i, 0)),
        out_shape=jax.ShapeDtypeStruct((t, SSM_WIDTH), BF16),
        scratch_shapes=[pltpu.VMEM((S5_TB, LANES), F32)] * N_OCT + [
            pltpu.VMEM((8, OCT_STATE), F32),
            pltpu.VMEM((8, OCT_STATE), F32),
        ],
        compiler_params=pltpu.CompilerParams(
            dimension_semantics=("arbitrary", "arbitrary"), vmem_limit_bytes=VMEM_LIMIT),
        name="s5",
    )(u32, u32, u32, u32, pbf, wt, winp, woutp, ptre, ptim, d_row, w_glu, b_glu)


def _proj_hgrn_kernel(x_ref, nw_ref, wh_ref, wu_ref, wr_ref, lbl_ref, hnw_ref, ou_ref, or_ref, o_ref,
                      ph_s, q0_s, qc_s, kt_s, k0_s, k1_s, k2_s, ke_s, v_s, dec_s, oacc_s, st_s, upd_s, sc_s, am_s,
                      *, blocks_per_seq):
    @pl.when(pl.program_id(0) % blocks_per_seq == 0)
    def _():
        st_s[...] = jnp.zeros_like(st_s)

    x = x_ref[...]
    ms = jnp.mean(x * x, axis=-1, keepdims=True)
    xn = (x * lax.rsqrt(ms + NORM_EPS) * nw_ref[...]).astype(BF16)
    ph_s[...] = jnp.dot(xn, wh_ref[...], preferred_element_type=F32)
    ou_ref[...] = jnp.dot(xn, wu_ref[...], preferred_element_type=F32)
    or_ref[...] = jnp.dot(xn, wr_ref[...], preferred_element_type=F32).astype(BF16)
    col = lambda k: ph_s[:, HGRN_WIDTH * k:HGRN_WIDTH * (k + 1)]

    logits = lbl_ref[...]
    e = jnp.exp(logits - jnp.max(logits, axis=0, keepdims=True))
    lb = (e / jnp.sum(e, axis=0, keepdims=True))[0:1, :]

    q = col(PH_Q)
    qf = q * _sigmoid(q)
    forget = lb + (1.0 - lb) * _sigmoid(col(PH_F))
    lf = jnp.log(forget)
    key = 1.0 - forget

    row = lax.broadcasted_iota(jnp.int32, (HG_TB, HGRN_WIDTH), 0)
    r_sub = row % HG_SUB
    r_ch = row % HG_CH

    def down(x, d):
        return pltpu.roll(x, d, 0)

    def up(x, d):
        return pltpu.roll(x, HG_TB - d, 0)

    a = lf
    d = 1
    while d < HG_SUB:
        a = a + jnp.where(r_sub >= d, down(a, d), 0.0)
        d *= 2
    a3 = a.reshape(HG_TB // HG_SUB, HG_SUB, HGRN_WIDTH)
    tsub = jnp.broadcast_to(a3[:, HG_SUB - 1:HG_SUB, :], a3.shape).reshape(HG_TB, HGRN_WIDTH)
    n_sub = HG_CH // HG_SUB
    prev = [jnp.where(r_ch >= HG_SUB * k, down(tsub, HG_SUB * k), 0.0) for k in range(1, n_sub)]
    nxt = [jnp.where(r_ch < HG_CH - HG_SUB * k, up(tsub, HG_SUB * k), 0.0) for k in range(1, n_sub)]
    eprev = prev[0] + prev[1] + prev[2]
    enext = nxt[0] + nxt[1] + nxt[2]
    suf = tsub - a

    q0 = qf * jnp.exp(a)
    q0_s[...] = q0.astype(BF16)
    qc_s[...] = (q0 * jnp.exp(eprev)).astype(BF16)
    kt_s[...] = (key * jnp.exp(-a)).astype(BF16)
    k0 = key * jnp.exp(suf)
    k0_s[...] = k0.astype(BF16)
    k1 = k0 * jnp.exp(nxt[0])
    k1_s[...] = k1.astype(BF16)
    k2 = k1 * jnp.exp(nxt[1])
    k2_s[...] = k2.astype(BF16)
    ke_s[...] = (k2 * jnp.exp(nxt[2])).astype(BF16)
    dec_s[...] = jnp.exp(eprev + tsub + enext)
    v_s[...] = col(PH_I).astype(BF16)

    tq = lax.broadcasted_iota(jnp.int32, (HG_CH, n_sub * HG_CH), 0)
    cc = lax.broadcasted_iota(jnp.int32, (HG_CH, n_sub * HG_CH), 1)
    cls = cc // HG_CH
    ts = cc % HG_CH
    bi = tq // HG_SUB
    bj = ts // HG_SUB
    mask4 = ((cls == 0) & (bi == bj) & (ts <= tq)) | ((cls > 0) & ((bi - bj) == cls))

    n_ch = HG_TB // HG_CH
    half = n_sub * HG_CH // 2
    units = [(c, h, slice(c * HG_CH, (c + 1) * HG_CH), slice(HEAD_DIM * h, HEAD_DIM * (h + 1)))
             for c in range(n_ch) for h in range(HEADS)]
    for c, h, rows, ls in units:
        kcat = jnp.concatenate([kt_s[rows, ls], k0_s[rows, ls], k1_s[rows, ls], k2_s[rows, ls]], axis=0)
        sc_s[c * HEADS + h] = lax.dot_general(q0_s[rows, ls], kcat, (((1,), (1,)), ((), ())),
                                              preferred_element_type=F32)
    for c, h, rows, ls in units:
        upd_s[c * HEADS + h] = lax.dot_general(v_s[rows, ls], ke_s[rows, ls], (((0,), (0,)), ((), ())),
                                               preferred_element_type=F32)
    for c, h, rows, ls in units:
        sc = jnp.where(mask4, sc_s[c * HEADS + h], 0.0)
        am_s[c * HEADS + h] = (sc[:, :half] + sc[:, half:]).astype(BF16)
    for c, h, rows, ls in units:
        vv = v_s[rows, ls]
        oacc_s[rows, ls] = jnp.dot(am_s[c * HEADS + h], jnp.concatenate([vv, vv], axis=0),
                                   preferred_element_type=F32)
    for h in range(HEADS):
        ls = slice(HEAD_DIM * h, HEAD_DIM * (h + 1))
        st = st_s[h]
        for c in range(n_ch):
            rows = slice(c * HG_CH, (c + 1) * HG_CH)
            oacc_s[rows, ls] += lax.dot_general(qc_s[rows, ls], st.astype(BF16), (((1,), (1,)), ((), ())),
                                                preferred_element_type=F32)
            st = st * dec_s[c * HG_CH:c * HG_CH + 1, ls] + upd_s[c * HEADS + h]
        st_s[h] = st

    o = oacc_s[...] * _sigmoid(col(PH_OG))
    parts = []
    for h in range(HEADS):
        oh = o[:, HEAD_DIM * h:HEAD_DIM * (h + 1)]
        ms = jnp.mean(oh * oh, axis=-1, keepdims=True)
        parts.append(oh * lax.rsqrt(ms + NORM_EPS))
    o = jnp.concatenate(parts, axis=1) * hnw_ref[...]
    z = col(PH_ZB)
    o_ref[...] = (o * (z * _sigmoid(z))).astype(BF16)


def _proj_hgrn(x2, norm_w, w_h, w_u, w_r, lb_logits, hgrn_norm_w, seqlen):
    t = x2.shape[0]
    wide = (HG_TB, HGRN_WIDTH)
    units = HG_TB // HG_CH * HEADS
    n_cls = HG_CH // HG_SUB
    row_blk = lambda w: pl.BlockSpec((HG_TB, w), lambda i: (i, 0))
    return pl.pallas_call(
        functools.partial(_proj_hgrn_kernel, blocks_per_seq=seqlen // HG_TB),
        grid=(t // HG_TB,),
        in_specs=[
            row_blk(D_MODEL), _const_spec((1, D_MODEL)),
            _const_spec((D_MODEL, PH_WIDTH)), _const_spec((D_MODEL, SSM_WIDTH)), _const_spec((D_MODEL, PBF_WIDTH)),
            _const_spec(lb_logits.shape), _const_spec((1, HGRN_WIDTH)),
        ],
        out_specs=[row_blk(SSM_WIDTH), row_blk(PBF_WIDTH), row_blk(HGRN_WIDTH)],
        out_shape=[
            jax.ShapeDtypeStruct((t, SSM_WIDTH), F32),
            jax.ShapeDtypeStruct((t, PBF_WIDTH), BF16),
            jax.ShapeDtypeStruct((t, HGRN_WIDTH), BF16),
        ],
        scratch_shapes=[pltpu.VMEM((HG_TB, PH_WIDTH), F32)] + [pltpu.VMEM(wide, BF16)] * 8 + [
            pltpu.VMEM(wide, F32), pltpu.VMEM(wide, F32),
            pltpu.VMEM((HEADS, HEAD_DIM, HEAD_DIM), F32),
            pltpu.VMEM((units, HEAD_DIM, HEAD_DIM), F32),
            pltpu.VMEM((units, HG_CH, n_cls * HG_CH), F32),
            pltpu.VMEM((units, HG_CH, n_cls * HG_CH // 2), BF16)],
        compiler_params=pltpu.CompilerParams(
            dimension_semantics=("arbitrary",), vmem_limit_bytes=VMEM_LIMIT),
        name="proj_hgrn",
    )(x2, norm_w, w_h, w_u, w_r, lb_logits, hgrn_norm_w)


def _merge_kernel(x_ref, ya_ref, yb_ref, ga_ref, gb_ref, wpa_ref, wpb_ref, wo_ref, fnw_ref, o_ref):
    pa = jnp.dot(ya_ref[...], wpa_ref[...], preferred_element_type=F32)
    pb = jnp.dot(yb_ref[...], wpb_ref[...], preferred_element_type=F32)
    merged = _sigmoid(ga_ref[...].astype(F32)) * pa + _sigmoid(gb_ref[...].astype(F32)) * pb
    h = x_ref[...] + jnp.dot(merged.astype(BF16), wo_ref[...], preferred_element_type=F32)
    ms = jnp.mean(h * h, axis=-1, keepdims=True)
    o_ref[...] = h * lax.rsqrt(ms + NORM_EPS) * fnw_ref[...]


def _merge(x2, ya, yb, pbf, w_pa, w_pb, w_out, fnw):
    t = x2.shape[0]
    return pl.pallas_call(
        _merge_kernel,
        grid=(t // TM_OUT,),
        in_specs=[
            pl.BlockSpec((TM_OUT, D_MODEL), lambda i: (i, 0)),
            pl.BlockSpec((TM_OUT, SSM_WIDTH), lambda i: (i, 0)),
            pl.BlockSpec((TM_OUT, HGRN_WIDTH), lambda i: (i, 0)),
            pl.BlockSpec((TM_OUT, D_MODEL), lambda i: (i, 0)),
            pl.BlockSpec((TM_OUT, D_MODEL), lambda i: (i, 1)),
            _const_spec((SSM_WIDTH, D_MODEL)), _const_spec((HGRN_WIDTH, D_MODEL)),
            _const_spec((D_MODEL, D_MODEL)), _const_spec((1, D_MODEL)),
        ],
        out_specs=pl.BlockSpec((TM_OUT, D_MODEL), lambda i: (i, 0)),
        out_shape=jax.ShapeDtypeStruct((t, D_MODEL), F32),
        compiler_params=pltpu.CompilerParams(
            dimension_semantics=("arbitrary",), vmem_limit_bytes=VMEM_LIMIT),
        name="merge",
    )(x2, ya, yb, pbf, pbf, w_pa, w_pb, w_out, fnw)


def kernel(x, norm_w, w_in, ssm_lambda_re, ssm_lambda_im, ssm_b_re, ssm_b_im, ssm_c_re, ssm_c_im, ssm_d,
           ssm_log_dt, ssm_w_glu, ssm_b_glu, hgrn_lb_logits, hgrn_norm_w, w_proj_a, w_proj_b, w_out,
           final_norm_w):
    batch, seqlen, _ = x.shape
    assert norm_w.shape[0] == 1, "single-layer block"
    assert seqlen % S5_TB == 0 and seqlen % HG_TB == 0
    x2 = x.reshape(batch * seqlen, D_MODEL)
    w = w_in[0].astype(BF16)
    sw, hw = SSM_WIDTH, HGRN_WIDTH
    u_c, za_c = w[:, :sw], w[:, sw:2 * sw]
    q_c, f_c, i_c, og_c, zb_c = (w[:, 2 * sw + k * hw:2 * sw + (k + 1) * hw] for k in range(5))
    g_c = w[:, 2 * sw + 5 * hw:]
    u32, pbf, yb = _proj_hgrn(x2, norm_w[0][None, :], jnp.concatenate([f_c, q_c, i_c, og_c, zb_c], axis=1), u_c,
                              jnp.concatenate([g_c, za_c], axis=1), hgrn_lb_logits, hgrn_norm_w[0][None, :], seqlen)

    wt, winp, woutp, ptre, ptim = _s5_prep(ssm_lambda_re[0], ssm_lambda_im[0], ssm_b_re[0], ssm_b_im[0],
                                           ssm_c_re[0], ssm_c_im[0], ssm_log_dt[0])
    ya = _s5(u32, pbf, wt, winp, woutp, ptre, ptim, ssm_d[0].reshape(1, SSM_WIDTH),
             ssm_w_glu[0].astype(BF16), ssm_b_glu[0][None, :], batch, seqlen)
    out = _merge(x2, ya, yb, pbf, w_proj_a[0].astype(BF16), w_proj_b[0].astype(BF16),
                 w_out[0].astype(BF16), final_norm_w[None, :])
    return out.reshape(batch, seqlen, D_MODEL)
```

```python
import functools

import jax
import jax.numpy as jnp
from jax import lax
from jax.experimental import pallas as pl
from jax.experimental.pallas import tpu as pltpu

F32 = jnp.float32
BF16 = jnp.bfloat16
HIGHEST = lax.Precision.HIGHEST

D_MODEL = 1024
SSM_WIDTH = 512
SSM_GROUP = 16
SSM_GROUPS = 32
SSM_STATE = 64
HGRN_WIDTH = 512
HEAD_DIM = 128
HEADS = 4
NORM_EPS = 1e-6
LAMBDA_RE_MAX = -1e-4

PH_WIDTH = 5 * HGRN_WIDTH
PH_F, PH_Q, PH_I, PH_OG, PH_ZB = 0, 1, 2, 3, 4
PBF_WIDTH = 2 * D_MODEL + SSM_WIDTH
PBF_ZA = 4

LANES = 128
OCT = LANES // SSM_GROUP
N_OCT = SSM_GROUPS // OCT
OCT_STATE = OCT * SSM_STATE

S5_Q = 8
S5_QP = S5_Q // 2
S5_TB = 2048
S5_NCH = S5_TB // S5_Q
S5_SEG = 8
S5_NV = S5_NCH // S5_SEG
S5_PT_ROWS = S5_NV + 8
S5_SEG_ROWS = S5_Q * S5_NV
S5_SEG_PITCH = S5_SEG_ROWS + 8
HG_TB = 512
HG_CH = 64
HG_SUB = 16
TM_OUT = 512

VMEM_LIMIT = 56 * 1024 * 1024


def _sigmoid(x):
    return jax.nn.sigmoid(x)


def _const_spec(shape):
    nd = len(shape)
    return pl.BlockSpec(shape, lambda *_: (0,) * nd, pipeline_mode=pl.Buffered(1))


def _disc(lam_re, lam_im, log_dt):
    return jnp.minimum(lam_re, LAMBDA_RE_MAX), lam_im, jnp.exp(log_dt)


def _cpow(lr, li, dt, k):
    mag = jnp.exp(k * (lr * dt))
    ang = k * (li * dt)
    return mag * jnp.cos(ang), mag * jnp.sin(ang)


def _s5_prep_kernel(lre, lim, ldt, bre_t, bim_t, ctre, ctim, lre_n, lim_n, ldt_n,
                    kb_ref, win_ref, wout_ref, pw_s, xk_s):
    tau = pl.program_id(0).astype(F32)
    lr, li, dt = _disc(lre[...], lim[...], ldt[...])
    ab_re, ab_im = _cpow(lr, li, dt, 1.0)
    den = lr * lr + li * li
    nr = ab_re - 1.0
    pw_s[0] = (nr * lr + ab_im * li) / den
    pw_s[1] = (ab_im * lr - nr * li) / den
    pw_s[2], pw_s[3] = _cpow(lr, li, dt, tau)

    col = lax.broadcasted_iota(jnp.int32, (SSM_GROUP, SSM_WIDTH), 1)

    def strip(g, carry):
        rows = pl.ds(pl.multiple_of(g * SSM_GROUP, SSM_GROUP), SSM_GROUP)
        grow = pl.ds(g, 1)
        coef_re, coef_im = pw_s[0, grow, :], pw_s[1, grow, :]
        p_re, p_im = pw_s[2, grow, :], pw_s[3, grow, :]
        br, bi = bre_t[rows, :], bim_t[rows, :]
        bb_re = coef_re * br - coef_im * bi
        bb_im = coef_re * bi + coef_im * br
        x_re = bb_re * p_re - bb_im * p_im
        x_im = bb_re * p_im + bb_im * p_re
        xk_s[0, rows, :] = x_re[:, :SSM_STATE]
        xk_s[1, rows, :] = x_im[:, :SSM_STATE]
        a = g // OCT
        lrows = pl.ds(pl.multiple_of((g % OCT) * SSM_GROUP, SSM_GROUP), SSM_GROUP)
        m_in = col // SSM_STATE == g % OCT
        win_ref[0, a, 0, lrows, :] = jnp.concatenate(
            [jnp.where(m_in, x_re, 0.0), jnp.where(m_in, x_im, 0.0)], axis=1).astype(BF16)
        return carry

    lax.fori_loop(0, SSM_GROUPS, strip, 0, unroll=4)

    p1_re, p1_im = _cpow(*_disc(lre_n[...], lim_n[...], ldt_n[...]), tau + 1.0)
    cr, ci = ctre[...], ctim[...]
    w_re = cr * p1_re - ci * p1_im
    w_im = cr * p1_im + ci * p1_re
    rowo = lax.broadcasted_iota(jnp.int32, (OCT_STATE, LANES), 0)
    colo = lax.broadcasted_iota(jnp.int32, (OCT_STATE, LANES), 1)
    m_out = rowo // SSM_STATE == colo // SSM_GROUP
    for a in range(N_OCT):
        ls = slice(LANES * a, LANES * (a + 1))
        wout_ref[0, a, :OCT_STATE, :] = jnp.where(m_out, jnp.tile(w_re[:, ls], (OCT, 1)), 0.0).astype(BF16)
        wout_ref[0, a, OCT_STATE:, :] = jnp.where(m_out, jnp.tile(-w_im[:, ls], (OCT, 1)), 0.0).astype(BF16)

    kfull = (jnp.dot(xk_s[0], ctre[...], precision=HIGHEST, preferred_element_type=F32)
             - jnp.dot(xk_s[1], ctim[...], precision=HIGHEST, preferred_element_type=F32))
    rowk = lax.broadcasted_iota(jnp.int32, (SSM_WIDTH, SSM_WIDTH), 0)
    colk = lax.broadcasted_iota(jnp.int32, (SSM_WIDTH, SSM_WIDTH), 1)
    kb_ref[0] = jnp.where(rowk // SSM_GROUP == colk // SSM_GROUP, kfull, 0.0).astype(BF16)


def _s5_ptab_kernel(lre, lim, ldt, ptre_ref, ptim_ref):
    lr, li, dt = _disc(lre[0], lim[0], ldt[0])
    j = lax.broadcasted_iota(jnp.int32, (S5_PT_ROWS, OCT_STATE), 0).astype(F32) * float(S5_Q)
    ptre_ref[0], ptim_ref[0] = _cpow(lr, li, dt, j)


def _s5_prep(lam_re, lam_im, b_re, b_im, c_re, c_im, log_dt):
    g, n, p = SSM_GROUPS, SSM_STATE, SSM_GROUP
    ldt_gn = jnp.broadcast_to(log_dt[:, None], (g, n))
    compact = lambda a: jnp.tile(a, (1, OCT))
    b_t = lambda b: jnp.tile(b.transpose(0, 2, 1).reshape(g * p, n), (1, OCT))
    c_n = lambda c: c.transpose(2, 0, 1).reshape(n, g * p)
    mode_rows = lambda a: jnp.repeat(a.T, p, axis=1)
    args = (compact(lam_re), compact(lam_im), compact(ldt_gn), b_t(b_re), b_t(b_im), c_n(c_re), c_n(c_im),
            mode_rows(lam_re), mode_rows(lam_im), mode_rows(ldt_gn))
    mat = (S5_QP, N_OCT, 2, LANES, 2 * OCT_STATE)
    blk = (1, N_OCT, 1, LANES, 2 * OCT_STATE)
    kb, win, woutp = pl.pallas_call(
        _s5_prep_kernel,
        grid=(S5_Q,),
        in_specs=[_const_spec(a.shape) for a in args],
        out_specs=[
            pl.BlockSpec((1, SSM_WIDTH, SSM_WIDTH), lambda t: (t, 0, 0)),
            pl.BlockSpec(blk, lambda t: ((S5_Q - 1 - t) // 2, 0, (S5_Q - 1 - t) % 2, 0, 0)),
            pl.BlockSpec((1, N_OCT, 2 * OCT_STATE, LANES), lambda t: (t // 2, 0, 0, t % 2)),
        ],
        out_shape=[
            jax.ShapeDtypeStruct((S5_Q, SSM_WIDTH, SSM_WIDTH), BF16),
            jax.ShapeDtypeStruct(mat, BF16),
            jax.ShapeDtypeStruct((S5_QP, N_OCT, 2 * OCT_STATE, 2 * LANES), BF16),
        ],
        scratch_shapes=[pltpu.VMEM((4, SSM_GROUPS, SSM_WIDTH), F32),
                        pltpu.VMEM((2, SSM_WIDTH, SSM_STATE), F32)],
        compiler_params=pltpu.CompilerParams(
            dimension_semantics=("arbitrary",), vmem_limit_bytes=VMEM_LIMIT),
        name="s5_prep",
    )(*args)

    row_r = lambda a: a.reshape(N_OCT, 1, OCT_STATE)
    tab = jax.ShapeDtypeStruct((N_OCT, S5_PT_ROWS, OCT_STATE), F32)
    ptre, ptim = pl.pallas_call(
        _s5_ptab_kernel,
        grid=(N_OCT,),
        in_specs=[pl.BlockSpec((1, 1, OCT_STATE), lambda a: (a, 0, 0))] * 3,
        out_specs=[pl.BlockSpec((1, S5_PT_ROWS, OCT_STATE), lambda a: (a, 0, 0))] * 2,
        out_shape=[tab, tab],
        compiler_params=pltpu.CompilerParams(dimension_semantics=("arbitrary",)),
        name="s5_ptab",
    )(row_r(lam_re), row_r(lam_im), row_r(ldt_gn))

    kbo = jnp.stack([kb[:, LANES * a:LANES * (a + 1), LANES * a:LANES * (a + 1)] for a in range(N_OCT)], axis=1)
    zero = jnp.zeros_like(kbo[0])

    def pair_tile(d):
        top = jnp.concatenate([kbo[2 * d], kbo[2 * d + 1]], axis=-1)
        bot = jnp.concatenate([kbo[2 * d - 1] if d > 0 else zero, kbo[2 * d]], axis=-1)
        return jnp.concatenate([top, bot], axis=-2)

    wt = jnp.stack([pair_tile(d) for d in range(S5_QP)])
    winp = win.reshape(S5_QP, N_OCT, 2 * LANES, 2 * OCT_STATE)
    return wt, winp, woutp, ptre, ptim


def _gelu_tanh(x):
    c = 0.7978845608028654
    return x * (0.5 * (1.0 + jnp.tanh(c * (x + 0.044715 * (x * x * x)))))


def _s5_kernel(u0_ref, u1_ref, u2_ref, u3_ref, z_ref, wt_ref, winp_ref, woutp_ref, ptre_ref, ptim_ref,
               d_ref, wglu_ref, bglu_ref, o_ref, up0_s, up1_s, up2_s, up3_s, y0_s, y1_s, y2_s, y3_s,
               cre_scr, cim_scr):
    u_refs = (u0_ref, u1_ref, u2_ref, u3_ref)
    up_scrs = (up0_s, up1_s, up2_s, up3_s)
    y_scrs = (y0_s, y1_s, y2_s, y3_s)

    @pl.when(pl.program_id(1) == 0)
    def _():
        cre_scr[...] = jnp.zeros_like(cre_scr)
        cim_scr[...] = jnp.zeros_like(cim_scr)

    for a in range(N_OCT):
        for r in range(S5_SEG):
            up_scrs[a][S5_SEG_PITCH * r:S5_SEG_PITCH * r + S5_SEG_ROWS, :] = (
                u_refs[a][S5_SEG_ROWS * r:S5_SEG_ROWS * (r + 1), :])

    def tok(a, s):
        return jnp.concatenate(
            [up_scrs[a][pl.ds(s + S5_Q * v, S5_SEG, stride=S5_SEG_PITCH), :] for v in range(S5_NV)], axis=0)

    xp = [[jnp.concatenate([tok(a, 2 * sp), tok(a, 2 * sp + 1)], axis=1).astype(BF16) for a in range(N_OCT)]
          for sp in range(S5_QP)]

    def cmul_add(b_re, b_im, m_re, m_im, x_re, x_im):
        return b_re + m_re * x_re - m_im * x_im, b_im + m_re * x_im + m_im * x_re

    hs = []
    for a in range(N_OCT):
        acc = None
        for sp in range(S5_QP):
            part = jnp.dot(xp[sp][a], winp_ref[sp, a], preferred_element_type=F32)
            acc = part if acc is None else acc + part
        blk = lambda v: (acc[S5_SEG * v:S5_SEG * (v + 1), :OCT_STATE], acc[S5_SEG * v:S5_SEG * (v + 1), OCT_STATE:])
        m_re, m_im = ptre_ref[a, 1:2, :], ptim_ref[a, 1:2, :]
        loc = [blk(0)]
        for v in range(1, S5_NV):
            loc.append(cmul_add(*blk(v), m_re, m_im, *loc[-1]))
        l_re, l_im = ptre_ref[a, S5_NV:S5_NV + 1, :], ptim_ref[a, S5_NV:S5_NV + 1, :]
        c_re, c_im = cre_scr[a:a + 1, :], cim_scr[a:a + 1, :]
        carry = []
        for r in range(S5_SEG):
            carry.append((c_re, c_im))
            c_re, c_im = cmul_add(loc[-1][0][r:r + 1, :], loc[-1][1][r:r + 1, :], l_re, l_im, c_re, c_im)
        cre_scr[a:a + 1, :] = c_re
        cim_scr[a:a + 1, :] = c_im
        cs_re = jnp.concatenate([c[0] for c in carry], axis=0)
        cs_im = jnp.concatenate([c[1] for c in carry], axis=0)
        ent = [(cs_re, cs_im)]
        for v in range(S5_NV - 1):
            ent.append(cmul_add(*loc[v], ptre_ref[a, v + 1:v + 2, :], ptim_ref[a, v + 1:v + 2, :], cs_re, cs_im))
        hs.append(jnp.concatenate([jnp.concatenate([e[0] for e in ent], axis=0),
                                   jnp.concatenate([e[1] for e in ent], axis=0)], axis=1).astype(BF16))

    for tp in range(S5_QP):
        for a in range(N_OCT):
            acc = jnp.dot(hs[a], woutp_ref[tp, a], preferred_element_type=F32)
            for sp in range(tp + 1):
                acc = acc + jnp.dot(xp[sp][a], wt_ref[tp - sp, a], preferred_element_type=F32)
            for v in range(S5_NV):
                rows = slice(S5_SEG * v, S5_SEG * (v + 1))
                y_scrs[a][pl.ds(2 * tp + S5_Q * v, S5_SEG, stride=S5_SEG_PITCH), :] = acc[rows, :LANES]
                y_scrs[a][pl.ds(2 * tp + 1 + S5_Q * v, S5_SEG, stride=S5_SEG_PITCH), :] = acc[rows, LANES:]

    u = jnp.concatenate([r[...] for r in u_refs], axis=1)
    unpad = lambda ref: jnp.concatenate(
        [ref[S5_SEG_PITCH * r:S5_SEG_PITCH * r + S5_SEG_ROWS, :] for r in range(S5_SEG)], axis=0)
    y = jnp.concatenate([unpad(r) for r in y_scrs], axis=1) + d_ref[...] * u
    y = _gelu_tanh(y)
    gate = jnp.dot(y.astype(BF16), wglu_ref[...], preferred_element_type=F32) + bglu_ref[...]
    y = y * _sigmoid(gate)
    z = z_ref[...].astype(F32)
    o_ref[...] = (y * (z * _sigmoid(z))).astype(BF16)


def _s5(u32, pbf, wt, winp, woutp, ptre, ptim, d_row, w_glu, b_glu, batch, seqlen):
    nb = seqlen // S5_TB
    t = batch * seqlen
    u_tile = lambda a: pl.BlockSpec((S5_TB, LANES), lambda b, i, a=a: (b * nb + i, a))
    return pl.pallas_call(
        _s5_kernel,
        grid=(batch, nb),
        in_specs=[u_tile(a) for a in range(N_OCT)] + [
            pl.BlockSpec((S5_TB, SSM_WIDTH), lambda b, i: (b * nb + i, PBF_ZA)),
            _const_spec(wt.shape), _const_spec(winp.shape), _const_spec(woutp.shape),
            _const_spec(ptre.shape), _const_spec(ptim.shape),
            _const_spec((1, SSM_WIDTH)), _const_spec((SSM_WIDTH, SSM_WIDTH)), _const_spec((1, SSM_WIDTH)),
        ],
        out_specs=pl.BlockSpec((S5_TB, SSM_WIDTH), lambda b, i: (b * nb + i, 0)),
        out_shape=jax.ShapeDtypeStruct((t, SSM_WIDTH), BF16),
        scratch_shapes=[pltpu.VMEM((S5_SEG * S5_SEG_PITCH, LANES), F32)] * (2 * N_OCT) + [
            pltpu.VMEM((8, OCT_STATE), F32),
            pltpu.VMEM((8, OCT_STATE), F32),
        ],
        compiler_params=pltpu.CompilerParams(
            dimension_semantics=("arbitrary", "arbitrary"), vmem_limit_bytes=VMEM_LIMIT),
        name="s5",
    )(u32, u32, u32, u32, pbf, wt, winp, woutp, ptre, ptim, d_row, w_glu, b_glu)


def _proj_hgrn_kernel(x_ref, nw_ref, wh_ref, wu_ref, wr_ref, lbl_ref, hnw_ref, ou_ref, or_ref, o_ref,
                      ph_s, q0_s, qc_s, kt_s, k0_s, k1_s, k2_s, ke_s, v_s, dec_s, oacc_s, st_s, upd_s, sc_s, am_s,
                      *, blocks_per_seq):
    @pl.when(pl.program_id(0) % blocks_per_seq == 0)
    def _():
        st_s[...] = jnp.zeros_like(st_s)

    x = x_ref[...]
    ms = jnp.mean(x * x, axis=-1, keepdims=True)
    xn = (x * lax.rsqrt(ms + NORM_EPS) * nw_ref[...]).astype(BF16)
    ph_s[...] = jnp.dot(xn, wh_ref[...], preferred_element_type=F32)
    ou_ref[...] = jnp.dot(xn, wu_ref[...], preferred_element_type=F32)
    or_ref[...] = jnp.dot(xn, wr_ref[...], preferred_element_type=F32).astype(BF16)
    col = lambda k: ph_s[:, HGRN_WIDTH * k:HGRN_WIDTH * (k + 1)]

    logits = lbl_ref[...]
    e = jnp.exp(logits - jnp.max(logits, axis=0, keepdims=True))
    lb = (e / jnp.sum(e, axis=0, keepdims=True))[0:1, :]

    q = col(PH_Q)
    qf = q * _sigmoid(q)
    forget = lb + (1.0 - lb) * _sigmoid(col(PH_F))
    lf = jnp.log(forget)
    key = 1.0 - forget

    row = lax.broadcasted_iota(jnp.int32, (HG_TB, HGRN_WIDTH), 0)
    r_sub = row % HG_SUB
    r_ch = row % HG_CH

    def down(x, d):
        return pltpu.roll(x, d, 0)

    def up(x, d):
        return pltpu.roll(x, HG_TB - d, 0)

    a = lf
    d = 1
    while d < HG_SUB:
        a = a + jnp.where(r_sub >= d, down(a, d), 0.0)
        d *= 2
    a3 = a.reshape(HG_TB // HG_SUB, HG_SUB, HGRN_WIDTH)
    tsub = jnp.broadcast_to(a3[:, HG_SUB - 1:HG_SUB, :], a3.shape).reshape(HG_TB, HGRN_WIDTH)
    n_sub = HG_CH // HG_SUB
    prev = [jnp.where(r_ch >= HG_SUB * k, down(tsub, HG_SUB * k), 0.0) for k in range(1, n_sub)]
    nxt = [jnp.where(r_ch < HG_CH - HG_SUB * k, up(tsub, HG_SUB * k), 0.0) for k in range(1, n_sub)]
    eprev = prev[0] + prev[1] + prev[2]
    enext = nxt[0] + nxt[1] + nxt[2]
    suf = tsub - a

    q0 = qf * jnp.exp(a)
    q0_s[...] = q0.astype(BF16)
    qc_s[...] = (q0 * jnp.exp(eprev)).astype(BF16)
    kt_s[...] = (key * jnp.exp(-a)).astype(BF16)
    k0 = key * jnp.exp(suf)
    k0_s[...] = k0.astype(BF16)
    k1 = k0 * jnp.exp(nxt[0])
    k1_s[...] = k1.astype(BF16)
    k2 = k1 * jnp.exp(nxt[1])
    k2_s[...] = k2.astype(BF16)
    ke_s[...] = (k2 * jnp.exp(nxt[2])).astype(BF16)
    dec_s[...] = jnp.exp(eprev + tsub + enext)
    v_s[...] = col(PH_I).astype(BF16)

    tq = lax.broadcasted_iota(jnp.int32, (HG_CH, n_sub * HG_CH), 0)
    cc = lax.broadcasted_iota(jnp.int32, (HG_CH, n_sub * HG_CH), 1)
    cls = cc // HG_CH
    ts = cc % HG_CH
    bi = tq // HG_SUB
    bj = ts // HG_SUB
    mask4 = ((cls == 0) & (bi == bj) & (ts <= tq)) | ((cls > 0) & ((bi - bj) == cls))

    n_ch = HG_TB // HG_CH
    half = n_sub * HG_CH // 2
    units = [(c, h, slice(c * HG_CH, (c + 1) * HG_CH), slice(HEAD_DIM * h, HEAD_DIM * (h + 1)))
             for c in range(n_ch) for h in range(HEADS)]
    for c, h, rows, ls in units:
        kcat = jnp.concatenate([kt_s[rows, ls], k0_s[rows, ls], k1_s[rows, ls], k2_s[rows, ls]], axis=0)
        sc_s[c * HEADS + h] = lax.dot_general(q0_s[rows, ls], kcat, (((1,), (1,)), ((), ())),
                                              preferred_element_type=F32)
    for c, h, rows, ls in units:
        upd_s[c * HEADS + h] = lax.dot_general(v_s[rows, ls], ke_s[rows, ls], (((0,), (0,)), ((), ())),
                                               preferred_element_type=F32)
    for c, h, rows, ls in units:
        sc = jnp.where(mask4, sc_s[c * HEADS + h], 0.0)
        am_s[c * HEADS + h] = (sc[:, :half] + sc[:, half:]).astype(BF16)
    for c, h, rows, ls in units:
        vv = v_s[rows, ls]
        oacc_s[rows, ls] = jnp.dot(am_s[c * HEADS + h], jnp.concatenate([vv, vv], axis=0),
                                   preferred_element_type=F32)
    for h in range(HEADS):
        ls = slice(HEAD_DIM * h, HEAD_DIM * (h + 1))
        st = st_s[h]
        for c in range(n_ch):
            rows = slice(c * HG_CH, (c + 1) * HG_CH)
            oacc_s[rows, ls] += lax.dot_general(qc_s[rows, ls], st.astype(BF16), (((1,), (1,)), ((), ())),
                                                preferred_element_type=F32)
            st = st * dec_s[c * HG_CH:c * HG_CH + 1, ls] + upd_s[c * HEADS + h]
        st_s[h] = st

    o = oacc_s[...] * _sigmoid(col(PH_OG))
    parts = []
    for h in range(HEADS):
        oh = o[:, HEAD_DIM * h:HEAD_DIM * (h + 1)]
        ms = jnp.mean(oh * oh, axis=-1, keepdims=True)
        parts.append(oh * lax.rsqrt(ms + NORM_EPS))
    o = jnp.concatenate(parts, axis=1) * hnw_ref[...]
    z = col(PH_ZB)
    o_ref[...] = (o * (z * _sigmoid(z))).astype(BF16)


def _proj_hgrn(x2, norm_w, w_h, w_u, w_r, lb_logits, hgrn_norm_w, seqlen):
    t = x2.shape[0]
    wide = (HG_TB, HGRN_WIDTH)
    units = HG_TB // HG_CH * HEADS
    n_cls = HG_CH // HG_SUB
    row_blk = lambda w: pl.BlockSpec((HG_TB, w), lambda i: (i, 0))
    return pl.pallas_call(
        functools.partial(_proj_hgrn_kernel, blocks_per_seq=seqlen // HG_TB),
        grid=(t // HG_TB,),
        in_specs=[
            row_blk(D_MODEL), _const_spec((1, D_MODEL)),
            _const_spec((D_MODEL, PH_WIDTH)), _const_spec((D_MODEL, SSM_WIDTH)), _const_spec((D_MODEL, PBF_WIDTH)),
            _const_spec(lb_logits.shape), _const_spec((1, HGRN_WIDTH)),
        ],
        out_specs=[row_blk(SSM_WIDTH), row_blk(PBF_WIDTH), row_blk(HGRN_WIDTH)],
        out_shape=[
            jax.ShapeDtypeStruct((t, SSM_WIDTH), F32),
            jax.ShapeDtypeStruct((t, PBF_WIDTH), BF16),
            jax.ShapeDtypeStruct((t, HGRN_WIDTH), BF16),
        ],
        scratch_shapes=[pltpu.VMEM((HG_TB, PH_WIDTH), F32)] + [pltpu.VMEM(wide, BF16)] * 8 + [
            pltpu.VMEM(wide, F32), pltpu.VMEM(wide, F32),
            pltpu.VMEM((HEADS, HEAD_DIM, HEAD_DIM), F32),
            pltpu.VMEM((units, HEAD_DIM, HEAD_DIM), F32),
            pltpu.VMEM((units, HG_CH, n_cls * HG_CH), F32),
            pltpu.VMEM((units, HG_CH, n_cls * HG_CH // 2), BF16)],
        compiler_params=pltpu.CompilerParams(
            dimension_semantics=("arbitrary",), vmem_limit_bytes=VMEM_LIMIT),
        name="proj_hgrn",
    )(x2, norm_w, w_h, w_u, w_r, lb_logits, hgrn_norm_w)


def _merge_kernel(x_ref, ya_ref, yb_ref, ga_ref, gb_ref, wpa_ref, wpb_ref, wo_ref, fnw_ref, o_ref):
    pa = jnp.dot(ya_ref[...], wpa_ref[...], preferred_element_type=F32)
    pb = jnp.dot(yb_ref[...], wpb_ref[...], preferred_element_type=F32)
    merged = _sigmoid(ga_ref[...].astype(F32)) * pa + _sigmoid(gb_ref[...].astype(F32)) * pb
    h = x_ref[...] + jnp.dot(merged.astype(BF16), wo_ref[...], preferred_element_type=F32)
    ms = jnp.mean(h * h, axis=-1, keepdims=True)
    o_ref[...] = h * lax.rsqrt(ms + NORM_EPS) * fnw_ref[...]


def _merge(x2, ya, yb, pbf, w_pa, w_pb, w_out, fnw):
    t = x2.shape[0]
    return pl.pallas_call(
        _merge_kernel,
        grid=(t // TM_OUT,),
        in_specs=[
            pl.BlockSpec((TM_OUT, D_MODEL), lambda i: (i, 0)),
            pl.BlockSpec((TM_OUT, SSM_WIDTH), lambda i: (i, 0)),
            pl.BlockSpec((TM_OUT, HGRN_WIDTH), lambda i: (i, 0)),
            pl.BlockSpec((TM_OUT, D_MODEL), lambda i: (i, 0)),
            pl.BlockSpec((TM_OUT, D_MODEL), lambda i: (i, 1)),
            _const_spec((SSM_WIDTH, D_MODEL)), _const_spec((HGRN_WIDTH, D_MODEL)),
            _const_spec((D_MODEL, D_MODEL)), _const_spec((1, D_MODEL)),
        ],
        out_specs=pl.BlockSpec((TM_OUT, D_MODEL), lambda i: (i, 0)),
        out_shape=jax.ShapeDtypeStruct((t, D_MODEL), F32),
        compiler_params=pltpu.CompilerParams(
            dimension_semantics=("arbitrary",), vmem_limit_bytes=VMEM_LIMIT),
        name="merge",
    )(x2, ya, yb, pbf, pbf, w_pa, w_pb, w_out, fnw)


def kernel(x, norm_w, w_in, ssm_lambda_re, ssm_lambda_im, ssm_b_re, ssm_b_im, ssm_c_re, ssm_c_im, ssm_d,
           ssm_log_dt, ssm_w_glu, ssm_b_glu, hgrn_lb_logits, hgrn_norm_w, w_proj_a, w_proj_b, w_out,
           final_norm_w):
    batch, seqlen, _ = x.shape
    assert norm_w.shape[0] == 1, "single-layer block"
    assert seqlen % S5_TB == 0 and seqlen % HG_TB == 0
    x2 = x.reshape(batch * seqlen, D_MODEL)
    w = w_in[0].astype(BF16)
    sw, hw = SSM_WIDTH, HGRN_WIDTH
    u_c, za_c = w[:, :sw], w[:, sw:2 * sw]
    q_c, f_c, i_c, og_c, zb_c = (w[:, 2 * sw + k * hw:2 * sw + (k + 1) * hw] for k in range(5))
    g_c = w[:, 2 * sw + 5 * hw:]
    u32, pbf, yb = _proj_hgrn(x2, norm_w[0][None, :], jnp.concatenate([f_c, q_c, i_c, og_c, zb_c], axis=1), u_c,
                              jnp.concatenate([g_c, za_c], axis=1), hgrn_lb_logits, hgrn_norm_w[0][None, :], seqlen)

    wt, winp, woutp, ptre, ptim = _s5_prep(ssm_lambda_re[0], ssm_lambda_im[0], ssm_b_re[0], ssm_b_im[0],
                                           ssm_c_re[0], ssm_c_im[0], ssm_log_dt[0])
    ya = _s5(u32, pbf, wt, winp, woutp, ptre, ptim, ssm_d[0].reshape(1, SSM_WIDTH),
             ssm_w_glu[0].astype(BF16), ssm_b_glu[0][None, :], batch, seqlen)
    out = _merge(x2, ya, yb, pbf, w_proj_a[0].astype(BF16), w_proj_b[0].astype(BF16),
                 w_out[0].astype(BF16), final_norm_w[None, :])
    return out.reshape(batch, seqlen, D_MODEL)
```

```python
import functools

import jax
import jax.numpy as jnp
from jax import lax
from jax.experimental import pallas as pl
from jax.experimental.pallas import tpu as pltpu

F32 = jnp.float32
BF16 = jnp.bfloat16

D_MODEL = 1024
SSM_WIDTH = 512
SSM_GROUP = 16
SSM_GROUPS = 32
SSM_STATE = 64
HGRN_WIDTH = 512
HEAD_DIM = 128
HEADS = 4
NORM_EPS = 1e-6
LAMBDA_RE_MAX = -1e-4

IN_WIDTH = 2 * SSM_WIDTH + 5 * HGRN_WIDTH + 2 * D_MODEL
COL_ZA, COL_H, COL_G = SSM_WIDTH, 2 * SSM_WIDTH, 2 * SSM_WIDTH + 5 * HGRN_WIDTH
PH_WIDTH = 5 * HGRN_WIDTH
PH_Q, PH_F, PH_I, PH_OG, PH_ZB = 0, 1, 2, 3, 4
PBF_WIDTH = 2 * D_MODEL + SSM_WIDTH
PBF_ZA = 4

LANES = 128
OCT = LANES // SSM_GROUP
N_OCT = SSM_GROUPS // OCT
OCT_STATE = OCT * SSM_STATE

S5_Q = 8
S5_QP = S5_Q // 2
S5_TB = 2048
S5_NCH = S5_TB // S5_Q
S5_SEG = 8
S5_NV = S5_NCH // S5_SEG
S5_PT_ROWS = S5_NV + 8
S5_SEG_ROWS = S5_Q * S5_NV
S5_SEG_PITCH = S5_SEG_ROWS + 8
HG_TB = 512
HG_CH = 64
HG_SUB = 16
TM_OUT = 512

VMEM_LIMIT = 56 * 1024 * 1024


def _sigmoid(x):
    return jax.nn.sigmoid(x)


def _const_spec(shape):
    nd = len(shape)
    return pl.BlockSpec(shape, lambda *_: (0,) * nd, pipeline_mode=pl.Buffered(1))


def _disc(lam_re, lam_im, log_dt):
    return jnp.minimum(lam_re, LAMBDA_RE_MAX), lam_im, jnp.exp(log_dt)


def _cpow(lr, li, dt, k):
    mag = jnp.exp(k * (lr * dt))
    ang = k * (li * dt)
    return mag * jnp.cos(ang), mag * jnp.sin(ang)


def _split_bf16(x):
    hi = x.astype(BF16)
    return hi, (x - hi.astype(F32)).astype(BF16)


def _dot3(a, b_hi, b_lo):
    a_hi, a_lo = _split_bf16(a)
    d = lambda p, q: jnp.dot(p, q, preferred_element_type=F32)
    return d(a_hi, b_hi) + (d(a_hi, b_lo) + d(a_lo, b_hi))


def _cmul(a_re, a_im, b_re, b_im):
    return a_re * b_re - a_im * b_im, a_re * b_im + a_im * b_re


def _s5_prep_kernel(lre, lim, ldt, bre_t, bim_t, ctre, ctim, lre_n, lim_n, ldt_n, lre_r, lim_r, ldt_r,
                    kb_ref, win_ref, wout_ref, ptre_ref, ptim_ref, pw_s, pm_s, xk_s):
    @pl.when(pl.program_id(0) == 0)
    def _():
        lr, li, dt = _disc(lre[...], lim[...], ldt[...])
        ab_re, ab_im = _cpow(lr, li, dt, 1.0)
        den = lr * lr + li * li
        nr = ab_re - 1.0
        pw_s[0] = (nr * lr + ab_im * li) / den
        pw_s[1] = (ab_im * lr - nr * li) / den
        pw_s[2] = jnp.ones_like(ab_re)
        pw_s[3] = jnp.zeros_like(ab_re)
        pw_s[4] = ab_re
        pw_s[5] = ab_im
        m_re, m_im = _cpow(*_disc(lre_n[...], lim_n[...], ldt_n[...]), 1.0)
        pm_s[0] = m_re
        pm_s[1] = m_im
        pm_s[2] = m_re
        pm_s[3] = m_im
        j = lax.broadcasted_iota(jnp.int32, (S5_PT_ROWS, OCT_STATE), 0).astype(F32) * float(S5_Q)
        for a in range(N_OCT):
            t_re, t_im = _cpow(*_disc(lre_r[a:a + 1, :], lim_r[a:a + 1, :], ldt_r[a:a + 1, :]), j)
            ptre_ref[a] = t_re
            ptim_ref[a] = t_im

    col = lax.broadcasted_iota(jnp.int32, (SSM_GROUP, SSM_WIDTH), 1)

    def strip(g, carry):
        rows = pl.ds(pl.multiple_of(g * SSM_GROUP, SSM_GROUP), SSM_GROUP)
        grow = pl.ds(g, 1)
        coef_re, coef_im = pw_s[0, grow, :], pw_s[1, grow, :]
        p_re, p_im = pw_s[2, grow, :], pw_s[3, grow, :]
        bb_re, bb_im = _cmul(coef_re, coef_im, bre_t[rows, :], bim_t[rows, :])
        x_re, x_im = _cmul(bb_re, bb_im, p_re, p_im)
        xk_s[0, rows, :] = x_re[:, :SSM_STATE]
        xk_s[1, rows, :] = x_im[:, :SSM_STATE]
        a = g // OCT
        lrows = pl.ds(pl.multiple_of((g % OCT) * SSM_GROUP, SSM_GROUP), SSM_GROUP)
        m_in = col // SSM_STATE == g % OCT
        win_ref[0, a, 0, lrows, :] = jnp.concatenate(
            [jnp.where(m_in, x_re, 0.0), jnp.where(m_in, x_im, 0.0)], axis=1).astype(BF16)
        return carry

    lax.fori_loop(0, SSM_GROUPS, strip, 0, unroll=4)

    cr, ci = ctre[...], ctim[...]
    kfull = _dot3(xk_s[0], *_split_bf16(cr)) - _dot3(xk_s[1], *_split_bf16(ci))
    rowk = lax.broadcasted_iota(jnp.int32, (SSM_WIDTH, SSM_WIDTH), 0)
    colk = lax.broadcasted_iota(jnp.int32, (SSM_WIDTH, SSM_WIDTH), 1)
    kb_ref[0] = jnp.where(rowk // SSM_GROUP == colk // SSM_GROUP, kfull, 0.0).astype(BF16)

    p1_re, p1_im = pm_s[0], pm_s[1]
    w_re, w_im = _cmul(cr, ci, p1_re, p1_im)
    rowo = lax.broadcasted_iota(jnp.int32, (OCT_STATE, LANES), 0)
    colo = lax.broadcasted_iota(jnp.int32, (OCT_STATE, LANES), 1)
    m_out = rowo // SSM_STATE == colo // SSM_GROUP
    for a in range(N_OCT):
        ls = slice(LANES * a, LANES * (a + 1))
        wout_ref[0, a, :OCT_STATE, :] = jnp.where(m_out, jnp.tile(w_re[:, ls], (OCT, 1)), 0.0).astype(BF16)
        wout_ref[0, a, OCT_STATE:, :] = jnp.where(m_out, jnp.tile(-w_im[:, ls], (OCT, 1)), 0.0).astype(BF16)

    pw_s[2], pw_s[3] = _cmul(pw_s[2], pw_s[3], pw_s[4], pw_s[5])
    pm_s[0], pm_s[1] = _cmul(p1_re, p1_im, pm_s[2], pm_s[3])


def _s5_prep(lam_re, lam_im, b_re, b_im, c_re, c_im, log_dt):
    g, n, p = SSM_GROUPS, SSM_STATE, SSM_GROUP
    ldt_gn = jnp.broadcast_to(log_dt[:, None], (g, n))
    compact = lambda a: jnp.tile(a, (1, OCT))
    b_t = lambda b: jnp.tile(b.transpose(0, 2, 1).reshape(g * p, n), (1, OCT))
    c_n = lambda c: c.transpose(2, 0, 1).reshape(n, g * p)
    mode_rows = lambda a: jnp.repeat(a.T, p, axis=1)
    tile_rows = lambda a: a.reshape(N_OCT, OCT_STATE)
    args = (compact(lam_re), compact(lam_im), compact(ldt_gn), b_t(b_re), b_t(b_im), c_n(c_re), c_n(c_im),
            mode_rows(lam_re), mode_rows(lam_im), mode_rows(ldt_gn),
            tile_rows(lam_re), tile_rows(lam_im), tile_rows(ldt_gn))
    mat = (S5_QP, N_OCT, 2, LANES, 2 * OCT_STATE)
    blk = (1, N_OCT, 1, LANES, 2 * OCT_STATE)
    tab = (N_OCT, S5_PT_ROWS, OCT_STATE)
    kb, win, woutp, ptre, ptim = pl.pallas_call(
        _s5_prep_kernel,
        grid=(S5_Q,),
        in_specs=[_const_spec(a.shape) for a in args],
        out_specs=[
            pl.BlockSpec((1, SSM_WIDTH, SSM_WIDTH), lambda t: (t, 0, 0)),
            pl.BlockSpec(blk, lambda t: ((S5_Q - 1 - t) // 2, 0, (S5_Q - 1 - t) % 2, 0, 0)),
            pl.BlockSpec((1, N_OCT, 2 * OCT_STATE, LANES), lambda t: (t // 2, 0, 0, t % 2)),
            pl.BlockSpec(tab, lambda t: (0, 0, 0)),
            pl.BlockSpec(tab, lambda t: (0, 0, 0)),
        ],
        out_shape=[
            jax.ShapeDtypeStruct((S5_Q, SSM_WIDTH, SSM_WIDTH), BF16),
            jax.ShapeDtypeStruct(mat, BF16),
            jax.ShapeDtypeStruct((S5_QP, N_OCT, 2 * OCT_STATE, 2 * LANES), BF16),
            jax.ShapeDtypeStruct(tab, F32),
            jax.ShapeDtypeStruct(tab, F32),
        ],
        scratch_shapes=[pltpu.VMEM((6, SSM_GROUPS, SSM_WIDTH), F32),
                        pltpu.VMEM((4, SSM_STATE, SSM_WIDTH), F32),
                        pltpu.VMEM((2, SSM_WIDTH, SSM_STATE), F32)],
        compiler_params=pltpu.CompilerParams(
            dimension_semantics=("arbitrary",), vmem_limit_bytes=VMEM_LIMIT),
        name="s5_prep",
    )(*args)

    kbo = jnp.stack([kb[:, LANES * a:LANES * (a + 1), LANES * a:LANES * (a + 1)] for a in range(N_OCT)], axis=1)
    zero = jnp.zeros_like(kbo[0])

    def pair_tile(d):
        top = jnp.concatenate([kbo[2 * d], kbo[2 * d + 1]], axis=-1)
        bot = jnp.concatenate([kbo[2 * d - 1] if d > 0 else zero, kbo[2 * d]], axis=-1)
        return jnp.concatenate([top, bot], axis=-2)

    wt = jnp.stack([pair_tile(d) for d in range(S5_QP)])
    winp = win.reshape(S5_QP, N_OCT, 2 * LANES, 2 * OCT_STATE)
    return wt, winp, woutp, ptre, ptim


def _gelu_tanh(x):
    c = 0.7978845608028654
    return x * (0.5 * (1.0 + jnp.tanh(c * (x + 0.044715 * (x * x * x)))))


def _s5_kernel(u0_ref, u1_ref, u2_ref, u3_ref, z_ref, wt_ref, winp_ref, woutp_ref, ptre_ref, ptim_ref,
               d_ref, wglu_ref, bglu_ref, o_ref, up0_s, up1_s, up2_s, up3_s, y0_s, y1_s, y2_s, y3_s,
               cre_scr, cim_scr):
    u_refs = (u0_ref, u1_ref, u2_ref, u3_ref)
    up_scrs = (up0_s, up1_s, up2_s, up3_s)
    y_scrs = (y0_s, y1_s, y2_s, y3_s)

    @pl.when(pl.program_id(1) == 0)
    def _():
        cre_scr[...] = jnp.zeros_like(cre_scr)
        cim_scr[...] = jnp.zeros_like(cim_scr)

    for a in range(N_OCT):
        for r in range(S5_SEG):
            up_scrs[a][S5_SEG_PITCH * r:S5_SEG_PITCH * r + S5_SEG_ROWS, :] = (
                u_refs[a][S5_SEG_ROWS * r:S5_SEG_ROWS * (r + 1), :])

    def tok(a, s):
        return jnp.concatenate(
            [up_scrs[a][pl.ds(s + S5_Q * v, S5_SEG, stride=S5_SEG_PITCH), :] for v in range(S5_NV)], axis=0)

    xp = [[jnp.concatenate([tok(a, 2 * sp), tok(a, 2 * sp + 1)], axis=1).astype(BF16) for a in range(N_OCT)]
          for sp in range(S5_QP)]

    def cmul_add(b_re, b_im, m_re, m_im, x_re, x_im):
        return b_re + m_re * x_re - m_im * x_im, b_im + m_re * x_im + m_im * x_re

    hs = []
    for a in range(N_OCT):
        acc = None
        for sp in range(S5_QP):
            part = jnp.dot(xp[sp][a], winp_ref[sp, a], preferred_element_type=F32)
            acc = part if acc is None else acc + part
        blk = lambda v: (acc[S5_SEG * v:S5_SEG * (v + 1), :OCT_STATE], acc[S5_SEG * v:S5_SEG * (v + 1), OCT_STATE:])
        m_re, m_im = ptre_ref[a, 1:2, :], ptim_ref[a, 1:2, :]
        loc = [blk(0)]
        for v in range(1, S5_NV):
            loc.append(cmul_add(*blk(v), m_re, m_im, *loc[-1]))
        l_re, l_im = ptre_ref[a, S5_NV:S5_NV + 1, :], ptim_ref[a, S5_NV:S5_NV + 1, :]
        c_re, c_im = cre_scr[a:a + 1, :], cim_scr[a:a + 1, :]
        carry = []
        for r in range(S5_SEG):
            carry.append((c_re, c_im))
            c_re, c_im = cmul_add(loc[-1][0][r:r + 1, :], loc[-1][1][r:r + 1, :], l_re, l_im, c_re, c_im)
        cre_scr[a:a + 1, :] = c_re
        cim_scr[a:a + 1, :] = c_im
        cs_re = jnp.concatenate([c[0] for c in carry], axis=0)
        cs_im = jnp.concatenate([c[1] for c in carry], axis=0)
        ent = [(cs_re, cs_im)]
        for v in range(S5_NV - 1):
            ent.append(cmul_add(*loc[v], ptre_ref[a, v + 1:v + 2, :], ptim_ref[a, v + 1:v + 2, :], cs_re, cs_im))
        hs.append(jnp.concatenate([jnp.concatenate([e[0] for e in ent], axis=0),
                                   jnp.concatenate([e[1] for e in ent], axis=0)], axis=1).astype(BF16))

    for tp in range(S5_QP):
        for a in range(N_OCT):
            acc = jnp.dot(hs[a], woutp_ref[tp, a], preferred_element_type=F32)
            for sp in range(tp + 1):
                acc = acc + jnp.dot(xp[sp][a], wt_ref[tp - sp, a], preferred_element_type=F32)
            for v in range(S5_NV):
                rows = slice(S5_SEG * v, S5_SEG * (v + 1))
                y_scrs[a][pl.ds(2 * tp + S5_Q * v, S5_SEG, stride=S5_SEG_PITCH), :] = acc[rows, :LANES]
                y_scrs[a][pl.ds(2 * tp + 1 + S5_Q * v, S5_SEG, stride=S5_SEG_PITCH), :] = acc[rows, LANES:]

    u = jnp.concatenate([r[...] for r in u_refs], axis=1)
    unpad = lambda ref: jnp.concatenate(
        [ref[S5_SEG_PITCH * r:S5_SEG_PITCH * r + S5_SEG_ROWS, :] for r in range(S5_SEG)], axis=0)
    y = jnp.concatenate([unpad(r) for r in y_scrs], axis=1) + d_ref[...] * u
    y = _gelu_tanh(y)
    gate = jnp.dot(y.astype(BF16), wglu_ref[...], preferred_element_type=F32) + bglu_ref[...]
    y = y * _sigmoid(gate)
    z = z_ref[...].astype(F32)
    o_ref[...] = (y * (z * _sigmoid(z))).astype(BF16)


def _s5(u32, pbf, wt, winp, woutp, ptre, ptim, d_row, w_glu, b_glu, batch, seqlen):
    nb = seqlen // S5_TB
    t = batch * seqlen
    u_tile = lambda a: pl.BlockSpec((S5_TB, LANES), lambda b, i, a=a: (b * nb + i, a))
    return pl.pallas_call(
        _s5_kernel,
        grid=(batch, nb),
        in_specs=[u_tile(a) for a in range(N_OCT)] + [
            pl.BlockSpec((S5_TB, SSM_WIDTH), lambda b, i: (b * nb + i, PBF_ZA)),
            _const_spec(wt.shape), _const_spec(winp.shape), _const_spec(woutp.shape),
            _const_spec(ptre.shape), _const_spec(ptim.shape),
            _const_spec((1, SSM_WIDTH)), _const_spec((SSM_WIDTH, SSM_WIDTH)), _const_spec((1, SSM_WIDTH)),
        ],
        out_specs=pl.BlockSpec((S5_TB, SSM_WIDTH), lambda b, i: (b * nb + i, 0)),
        out_shape=jax.ShapeDtypeStruct((t, SSM_WIDTH), BF16),
        scratch_shapes=[pltpu.VMEM((S5_SEG * S5_SEG_PITCH, LANES), F32)] * (2 * N_OCT) + [
            pltpu.VMEM((8, OCT_STATE), F32),
            pltpu.VMEM((8, OCT_STATE), F32),
        ],
        compiler_params=pltpu.CompilerParams(
            dimension_semantics=("arbitrary", "arbitrary"), vmem_limit_bytes=VMEM_LIMIT),
        name="s5",
    )(u32, u32, u32, u32, pbf, wt, winp, woutp, ptre, ptim, d_row, w_glu, b_glu)


def _proj_hgrn_kernel(x_ref, nw_ref, w_ref, lbl_ref, hnw_ref, ou_ref, or_ref, o_ref,
                      ph_s, q0_s, qc_s, kt_s, k0_s, k1_s, k2_s, ke_s, v_s, dec_s, oacc_s, st_s, upd_s, sc_s, am_s,
                      *, blocks_per_seq):
    @pl.when(pl.program_id(0) % blocks_per_seq == 0)
    def _():
        st_s[...] = jnp.zeros_like(st_s)

    x = x_ref[...]
    ms = jnp.mean(x * x, axis=-1, keepdims=True)
    xn = (x * lax.rsqrt(ms + NORM_EPS) * nw_ref[...]).astype(BF16)
    ph_s[...] = jnp.dot(xn, w_ref[:, COL_H:COL_G], preferred_element_type=F32)
    ou_ref[...] = jnp.dot(xn, w_ref[:, :COL_ZA], preferred_element_type=F32)
    or_ref[:, :2 * D_MODEL] = jnp.dot(xn, w_ref[:, COL_G:], preferred_element_type=F32).astype(BF16)
    or_ref[:, 2 * D_MODEL:] = jnp.dot(xn, w_ref[:, COL_ZA:COL_H], preferred_element_type=F32).astype(BF16)
    col = lambda k: ph_s[:, HGRN_WIDTH * k:HGRN_WIDTH * (k + 1)]

    logits = lbl_ref[...]
    e = jnp.exp(logits - jnp.max(logits, axis=0, keepdims=True))
    lb = (e / jnp.sum(e, axis=0, keepdims=True))[0:1, :]

    q = col(PH_Q)
    qf = q * _sigmoid(q)
    forget = lb + (1.0 - lb) * _sigmoid(col(PH_F))
    lf = jnp.log(forget)
    key = 1.0 - forget

    row = lax.broadcasted_iota(jnp.int32, (HG_TB, HGRN_WIDTH), 0)
    r_sub = row % HG_SUB
    r_ch = row % HG_CH

    def down(x, d):
        return pltpu.roll(x, d, 0)

    def up(x, d):
        return pltpu.roll(x, HG_TB - d, 0)

    a = lf
    d = 1
    while d < HG_SUB:
        a = a + jnp.where(r_sub >= d, down(a, d), 0.0)
        d *= 2
    a3 = a.reshape(HG_TB // HG_SUB, HG_SUB, HGRN_WIDTH)
    tsub = jnp.broadcast_to(a3[:, HG_SUB - 1:HG_SUB, :], a3.shape).reshape(HG_TB, HGRN_WIDTH)
    n_sub = HG_CH // HG_SUB
    prev = [jnp.where(r_ch >= HG_SUB * k, down(tsub, HG_SUB * k), 0.0) for k in range(1, n_sub)]
    nxt = [jnp.where(r_ch < HG_CH - HG_SUB * k, up(tsub, HG_SUB * k), 0.0) for k in range(1, n_sub)]
    eprev = prev[0] + prev[1] + prev[2]
    enext = nxt[0] + nxt[1] + nxt[2]
    suf = tsub - a

    q0 = qf * jnp.exp(a)
    q0_s[...] = q0.astype(BF16)
    qc_s[...] = (q0 * jnp.exp(eprev)).astype(BF16)
    kt_s[...] = (key * jnp.exp(-a)).astype(BF16)
    k0 = key * jnp.exp(suf)
    k0_s[...] = k0.astype(BF16)
    k1 = k0 * jnp.exp(nxt[0])
    k1_s[...] = k1.astype(BF16)
    k2 = k1 * jnp.exp(nxt[1])
    k2_s[...] = k2.astype(BF16)
    ke_s[...] = (k2 * jnp.exp(nxt[2])).astype(BF16)
    dec_s[...] = jnp.exp(eprev + tsub + enext)
    v_s[...] = col(PH_I).astype(BF16)

    tq = lax.broadcasted_iota(jnp.int32, (HG_CH, n_sub * HG_CH), 0)
    cc = lax.broadcasted_iota(jnp.int32, (HG_CH, n_sub * HG_CH), 1)
    cls = cc // HG_CH
    ts = cc % HG_CH
    bi = tq // HG_SUB
    bj = ts // HG_SUB
    mask4 = ((cls == 0) & (bi == bj) & (ts <= tq)) | ((cls > 0) & ((bi - bj) == cls))

    n_ch = HG_TB // HG_CH
    half = n_sub * HG_CH // 2
    units = [(c, h, slice(c * HG_CH, (c + 1) * HG_CH), slice(HEAD_DIM * h, HEAD_DIM * (h + 1)))
             for c in range(n_ch) for h in range(HEADS)]
    for c, h, rows, ls in units:
        kcat = jnp.concatenate([kt_s[rows, ls], k0_s[rows, ls], k1_s[rows, ls], k2_s[rows, ls]], axis=0)
        sc_s[c * HEADS + h] = lax.dot_general(q0_s[rows, ls], kcat, (((1,), (1,)), ((), ())),
                                              preferred_element_type=F32)
    for c, h, rows, ls in units:
        upd_s[c * HEADS + h] = lax.dot_general(v_s[rows, ls], ke_s[rows, ls], (((0,), (0,)), ((), ())),
                                               preferred_element_type=F32)
    for c, h, rows, ls in units:
        sc = jnp.where(mask4, sc_s[c * HEADS + h], 0.0)
        am_s[c * HEADS + h] = (sc[:, :half] + sc[:, half:]).astype(BF16)
    for c, h, rows, ls in units:
        vv = v_s[rows, ls]
        oacc_s[rows, ls] = jnp.dot(am_s[c * HEADS + h], jnp.concatenate([vv, vv], axis=0),
                                   preferred_element_type=F32)
    for h in range(HEADS):
        ls = slice(HEAD_DIM * h, HEAD_DIM * (h + 1))
        st = st_s[h]
        for c in range(n_ch):
            rows = slice(c * HG_CH, (c + 1) * HG_CH)
            oacc_s[rows, ls] += lax.dot_general(qc_s[rows, ls], st.astype(BF16), (((1,), (1,)), ((), ())),
                                                preferred_element_type=F32)
            st = st * dec_s[c * HG_CH:c * HG_CH + 1, ls] + upd_s[c * HEADS + h]
        st_s[h] = st

    o = oacc_s[...] * _sigmoid(col(PH_OG))
    parts = []
    for h in range(HEADS):
        oh = o[:, HEAD_DIM * h:HEAD_DIM * (h + 1)]
        ms = jnp.mean(oh * oh, axis=-1, keepdims=True)
        parts.append(oh * lax.rsqrt(ms + NORM_EPS))
    o = jnp.concatenate(parts, axis=1) * hnw_ref[...]
    z = col(PH_ZB)
    o_ref[...] = (o * (z * _sigmoid(z))).astype(BF16)


def _proj_hgrn(x2, norm_w, w, lb_logits, hgrn_norm_w, seqlen):
    t = x2.shape[0]
    wide = (HG_TB, HGRN_WIDTH)
    units = HG_TB // HG_CH * HEADS
    n_cls = HG_CH // HG_SUB
    row_blk = lambda w: pl.BlockSpec((HG_TB, w), lambda i: (i, 0))
    return pl.pallas_call(
        functools.partial(_proj_hgrn_kernel, blocks_per_seq=seqlen // HG_TB),
        grid=(t // HG_TB,),
        in_specs=[
            row_blk(D_MODEL), _const_spec((1, D_MODEL)),
            _const_spec((D_MODEL, IN_WIDTH)),
            _const_spec(lb_logits.shape), _const_spec((1, HGRN_WIDTH)),
        ],
        out_specs=[row_blk(SSM_WIDTH), row_blk(PBF_WIDTH), row_blk(HGRN_WIDTH)],
        out_shape=[
            jax.ShapeDtypeStruct((t, SSM_WIDTH), F32),
            jax.ShapeDtypeStruct((t, PBF_WIDTH), BF16),
            jax.ShapeDtypeStruct((t, HGRN_WIDTH), BF16),
        ],
        scratch_shapes=[pltpu.VMEM((HG_TB, PH_WIDTH), F32)] + [pltpu.VMEM(wide, BF16)] * 8 + [
            pltpu.VMEM(wide, F32), pltpu.VMEM(wide, F32),
            pltpu.VMEM((HEADS, HEAD_DIM, HEAD_DIM), F32),
            pltpu.VMEM((units, HEAD_DIM, HEAD_DIM), F32),
            pltpu.VMEM((units, HG_CH, n_cls * HG_CH), F32),
            pltpu.VMEM((units, HG_CH, n_cls * HG_CH // 2), BF16)],
        compiler_params=pltpu.CompilerParams(
            dimension_semantics=("arbitrary",), vmem_limit_bytes=VMEM_LIMIT),
        name="proj_hgrn",
    )(x2, norm_w, w, lb_logits, hgrn_norm_w)


def _merge_kernel(x_ref, ya_ref, yb_ref, ga_ref, gb_ref, wpa_ref, wpb_ref, wo_ref, fnw_ref, o_ref):
    pa = jnp.dot(ya_ref[...], wpa_ref[...], preferred_element_type=F32)
    pb = jnp.dot(yb_ref[...], wpb_ref[...], preferred_element_type=F32)
    merged = _sigmoid(ga_ref[...].astype(F32)) * pa + _sigmoid(gb_ref[...].astype(F32)) * pb
    h = x_ref[...] + jnp.dot(merged.astype(BF16), wo_ref[...], preferred_element_type=F32)
    ms = jnp.mean(h * h, axis=-1, keepdims=True)
    o_ref[...] = h * lax.rsqrt(ms + NORM_EPS) * fnw_ref[...]


def _merge(x2, ya, yb, pbf, w_pa, w_pb, w_out, fnw):
    t = x2.shape[0]
    return pl.pallas_call(
        _merge_kernel,
        grid=(t // TM_OUT,),
        in_specs=[
            pl.BlockSpec((TM_OUT, D_MODEL), lambda i: (i, 0)),
            pl.BlockSpec((TM_OUT, SSM_WIDTH), lambda i: (i, 0)),
            pl.BlockSpec((TM_OUT, HGRN_WIDTH), lambda i: (i, 0)),
            pl.BlockSpec((TM_OUT, D_MODEL), lambda i: (i, 0)),
            pl.BlockSpec((TM_OUT, D_MODEL), lambda i: (i, 1)),
            _const_spec((SSM_WIDTH, D_MODEL)), _const_spec((HGRN_WIDTH, D_MODEL)),
            _const_spec((D_MODEL, D_MODEL)), _const_spec((1, D_MODEL)),
        ],
        out_specs=pl.BlockSpec((TM_OUT, D_MODEL), lambda i: (i, 0)),
        out_shape=jax.ShapeDtypeStruct((t, D_MODEL), F32),
        compiler_params=pltpu.CompilerParams(
            dimension_semantics=("arbitrary",), vmem_limit_bytes=VMEM_LIMIT),
        name="merge",
    )(x2, ya, yb, pbf, pbf, w_pa, w_pb, w_out, fnw)


def kernel(x, norm_w, w_in, ssm_lambda_re, ssm_lambda_im, ssm_b_re, ssm_b_im, ssm_c_re, ssm_c_im, ssm_d,
           ssm_log_dt, ssm_w_glu, ssm_b_glu, hgrn_lb_logits, hgrn_norm_w, w_proj_a, w_proj_b, w_out,
           final_norm_w):
    batch, seqlen, _ = x.shape
    assert norm_w.shape[0] == 1, "single-layer block"
    assert seqlen % S5_TB == 0 and seqlen % HG_TB == 0
    x2 = x.reshape(batch * seqlen, D_MODEL)
    u32, pbf, yb = _proj_hgrn(x2, norm_w[0][None, :], w_in[0].astype(BF16), hgrn_lb_logits,
                              hgrn_norm_w[0][None, :], seqlen)

    wt, winp, woutp, ptre, ptim = _s5_prep(ssm_lambda_re[0], ssm_lambda_im[0], ssm_b_re[0], ssm_b_im[0],
                                           ssm_c_re[0], ssm_c_im[0], ssm_log_dt[0])
    ya = _s5(u32, pbf, wt, winp, woutp, ptre, ptim, ssm_d[0].reshape(1, SSM_WIDTH),
             ssm_w_glu[0].astype(BF16), ssm_b_glu[0][None, :], batch, seqlen)
    out = _merge(x2, ya, yb, pbf, w_proj_a[0].astype(BF16), w_proj_b[0].astype(BF16),
                 w_out[0].astype(BF16), final_norm_w[None, :])
    return out.reshape(batch, seqlen, D_MODEL)
```

```python
import functools

import jax
import jax.numpy as jnp
from jax import lax
from jax.experimental import pallas as pl
from jax.experimental.pallas import tpu as pltpu

F32 = jnp.float32
BF16 = jnp.bfloat16

D_MODEL = 1024
SSM_WIDTH = 512
SSM_GROUP = 16
SSM_GROUPS = 32
SSM_STATE = 64
HGRN_WIDTH = 512
HEAD_DIM = 128
HEADS = 4
NORM_EPS = 1e-6
LAMBDA_RE_MAX = -1e-4

IN_WIDTH = 2 * SSM_WIDTH + 5 * HGRN_WIDTH + 2 * D_MODEL
COL_ZA, COL_H, COL_G = SSM_WIDTH, 2 * SSM_WIDTH, 2 * SSM_WIDTH + 5 * HGRN_WIDTH
PH_WIDTH = 5 * HGRN_WIDTH
PH_Q, PH_F, PH_I, PH_OG, PH_ZB = 0, 1, 2, 3, 4
PBF_WIDTH = 2 * D_MODEL + SSM_WIDTH
PBF_ZA = 4

LANES = 128
OCT = LANES // SSM_GROUP
N_OCT = SSM_GROUPS // OCT
OCT_STATE = OCT * SSM_STATE

S5_Q = 8
S5_QP = S5_Q // 2
S5_TB = 2048
S5_NCH = S5_TB // S5_Q
S5_SEG = 8
S5_NV = S5_NCH // S5_SEG
S5_PT_ROWS = S5_NV + 8
S5_SEG_ROWS = S5_Q * S5_NV
S5_SEG_PITCH = S5_SEG_ROWS + 8
HG_TB = 512
HG_CH = 64
HG_SUB = 16
TM_OUT = 512

VMEM_LIMIT = 56 * 1024 * 1024


def _sigmoid(x):
    return 0.5 * jnp.tanh(0.5 * x) + 0.5


def _const_spec(shape):
    nd = len(shape)
    return pl.BlockSpec(shape, lambda *_: (0,) * nd, pipeline_mode=pl.Buffered(1))


def _disc(lam_re, lam_im, log_dt):
    return jnp.minimum(lam_re, LAMBDA_RE_MAX), lam_im, jnp.exp(log_dt)


def _cpow(lr, li, dt, k):
    mag = jnp.exp(k * (lr * dt))
    ang = k * (li * dt)
    return mag * jnp.cos(ang), mag * jnp.sin(ang)


def _split_bf16(x):
    hi = x.astype(BF16)
    return hi, (x - hi.astype(F32)).astype(BF16)


def _dot3(a, b_hi, b_lo):
    a_hi, a_lo = _split_bf16(a)
    d = lambda p, q: jnp.dot(p, q, preferred_element_type=F32)
    return d(a_hi, b_hi) + (d(a_hi, b_lo) + d(a_lo, b_hi))


def _cmul(a_re, a_im, b_re, b_im):
    return a_re * b_re - a_im * b_im, a_re * b_im + a_im * b_re


def _s5_prep_kernel(lam_c, b_t, c_n, lam_n, lam_r,
                    kb_ref, win_ref, wout_ref, ptre_ref, ptim_ref, pw_s, pm_s, xk_s):
    @pl.when(pl.program_id(0) == 0)
    def _():
        lr, li, dt = _disc(lam_c[0], lam_c[1], lam_c[2])
        ab_re, ab_im = _cpow(lr, li, dt, 1.0)
        den = lr * lr + li * li
        nr = ab_re - 1.0
        pw_s[0] = (nr * lr + ab_im * li) / den
        pw_s[1] = (ab_im * lr - nr * li) / den
        pw_s[2] = jnp.ones_like(ab_re)
        pw_s[3] = jnp.zeros_like(ab_re)
        pw_s[4] = ab_re
        pw_s[5] = ab_im
        m_re, m_im = _cpow(*_disc(lam_n[0], lam_n[1], lam_n[2]), 1.0)
        pm_s[0] = m_re
        pm_s[1] = m_im
        pm_s[2] = m_re
        pm_s[3] = m_im
        j = lax.broadcasted_iota(jnp.int32, (S5_PT_ROWS, OCT_STATE), 0).astype(F32) * float(S5_Q)
        for a in range(N_OCT):
            t_re, t_im = _cpow(*_disc(lam_r[0, a:a + 1, :], lam_r[1, a:a + 1, :], lam_r[2, a:a + 1, :]), j)
            ptre_ref[a] = t_re
            ptim_ref[a] = t_im

    col = lax.broadcasted_iota(jnp.int32, (SSM_GROUP, SSM_WIDTH), 1)

    def strip(g, carry):
        rows = pl.ds(pl.multiple_of(g * SSM_GROUP, SSM_GROUP), SSM_GROUP)
        grow = pl.ds(g, 1)
        coef_re, coef_im = pw_s[0, grow, :], pw_s[1, grow, :]
        p_re, p_im = pw_s[2, grow, :], pw_s[3, grow, :]
        bb_re, bb_im = _cmul(coef_re, coef_im, b_t[0, rows, :], b_t[1, rows, :])
        x_re, x_im = _cmul(bb_re, bb_im, p_re, p_im)
        xk_s[0, rows, :] = x_re[:, :SSM_STATE]
        xk_s[1, rows, :] = x_im[:, :SSM_STATE]
        a = g // OCT
        lrows = pl.ds(pl.multiple_of((g % OCT) * SSM_GROUP, SSM_GROUP), SSM_GROUP)
        m_in = col // SSM_STATE == g % OCT
        win_ref[0, a, 0, lrows, :] = jnp.concatenate(
            [jnp.where(m_in, x_re, 0.0), jnp.where(m_in, x_im, 0.0)], axis=1).astype(BF16)
        return carry

    lax.fori_loop(0, SSM_GROUPS, strip, 0, unroll=4)

    cr, ci = c_n[0], c_n[1]
    kfull = _dot3(xk_s[0], *_split_bf16(cr)) - _dot3(xk_s[1], *_split_bf16(ci))
    rowk = lax.broadcasted_iota(jnp.int32, (SSM_WIDTH, SSM_WIDTH), 0)
    colk = lax.broadcasted_iota(jnp.int32, (SSM_WIDTH, SSM_WIDTH), 1)
    kb_ref[0] = jnp.where(rowk // SSM_GROUP == colk // SSM_GROUP, kfull, 0.0).astype(BF16)

    p1_re, p1_im = pm_s[0], pm_s[1]
    w_re, w_im = _cmul(cr, ci, p1_re, p1_im)
    rowo = lax.broadcasted_iota(jnp.int32, (OCT_STATE, LANES), 0)
    colo = lax.broadcasted_iota(jnp.int32, (OCT_STATE, LANES), 1)
    m_out = rowo // SSM_STATE == colo // SSM_GROUP
    for a in range(N_OCT):
        ls = slice(LANES * a, LANES * (a + 1))
        wout_ref[0, a, :OCT_STATE, :] = jnp.where(m_out, jnp.tile(w_re[:, ls], (OCT, 1)), 0.0).astype(BF16)
        wout_ref[0, a, OCT_STATE:, :] = jnp.where(m_out, jnp.tile(-w_im[:, ls], (OCT, 1)), 0.0).astype(BF16)

    pw_s[2], pw_s[3] = _cmul(pw_s[2], pw_s[3], pw_s[4], pw_s[5])
    pm_s[0], pm_s[1] = _cmul(p1_re, p1_im, pm_s[2], pm_s[3])


def _s5_prep(lam_re, lam_im, b_re, b_im, c_re, c_im, log_dt):
    g, n, p = SSM_GROUPS, SSM_STATE, SSM_GROUP
    lam = jnp.stack([lam_re, lam_im, jnp.broadcast_to(log_dt[:, None], (g, n))])
    b = jnp.stack([b_re, b_im])
    c = jnp.stack([c_re, c_im])
    args = (jnp.tile(lam, (1, 1, OCT)),
            jnp.tile(b.transpose(0, 1, 3, 2).reshape(2, g * p, n), (1, 1, OCT)),
            c.transpose(0, 3, 1, 2).reshape(2, n, g * p),
            jnp.repeat(lam.transpose(0, 2, 1), p, axis=2),
            lam.reshape(3, N_OCT, OCT_STATE))
    mat = (S5_QP, N_OCT, 2, LANES, 2 * OCT_STATE)
    blk = (1, N_OCT, 1, LANES, 2 * OCT_STATE)
    tab = (N_OCT, S5_PT_ROWS, OCT_STATE)
    kb, win, woutp, ptre, ptim = pl.pallas_call(
        _s5_prep_kernel,
        grid=(S5_Q,),
        in_specs=[_const_spec(a.shape) for a in args],
        out_specs=[
            pl.BlockSpec((1, SSM_WIDTH, SSM_WIDTH), lambda t: (t, 0, 0)),
            pl.BlockSpec(blk, lambda t: ((S5_Q - 1 - t) // 2, 0, (S5_Q - 1 - t) % 2, 0, 0)),
            pl.BlockSpec((1, N_OCT, 2 * OCT_STATE, LANES), lambda t: (t // 2, 0, 0, t % 2)),
            pl.BlockSpec(tab, lambda t: (0, 0, 0)),
            pl.BlockSpec(tab, lambda t: (0, 0, 0)),
        ],
        out_shape=[
            jax.ShapeDtypeStruct((S5_Q, SSM_WIDTH, SSM_WIDTH), BF16),
            jax.ShapeDtypeStruct(mat, BF16),
            jax.ShapeDtypeStruct((S5_QP, N_OCT, 2 * OCT_STATE, 2 * LANES), BF16),
            jax.ShapeDtypeStruct(tab, F32),
            jax.ShapeDtypeStruct(tab, F32),
        ],
        scratch_shapes=[pltpu.VMEM((6, SSM_GROUPS, SSM_WIDTH), F32),
                        pltpu.VMEM((4, SSM_STATE, SSM_WIDTH), F32),
                        pltpu.VMEM((2, SSM_WIDTH, SSM_STATE), F32)],
        compiler_params=pltpu.CompilerParams(
            dimension_semantics=("arbitrary",), vmem_limit_bytes=VMEM_LIMIT),
        name="s5_prep",
    )(*args)

    kbo = jnp.stack([kb[:, LANES * a:LANES * (a + 1), LANES * a:LANES * (a + 1)] for a in range(N_OCT)], axis=1)
    zero = jnp.zeros_like(kbo[0])

    def pair_tile(d):
        top = jnp.concatenate([kbo[2 * d], kbo[2 * d + 1]], axis=-1)
        bot = jnp.concatenate([kbo[2 * d - 1] if d > 0 else zero, kbo[2 * d]], axis=-1)
        return jnp.concatenate([top, bot], axis=-2)

    wt = jnp.stack([pair_tile(d) for d in range(S5_QP)])
    winp = win.reshape(S5_QP, N_OCT, 2 * LANES, 2 * OCT_STATE)
    return wt, winp, woutp, ptre, ptim


def _gelu_tanh(x):
    c = 0.7978845608028654
    return x * (0.5 * (1.0 + jnp.tanh(c * (x + 0.044715 * (x * x * x)))))


def _s5_kernel(u0_ref, u1_ref, u2_ref, u3_ref, z_ref, wt_ref, winp_ref, woutp_ref, ptre_ref, ptim_ref,
               d_ref, wglu_ref, bglu_ref, o_ref, up0_s, up1_s, up2_s, up3_s, y0_s, y1_s, y2_s, y3_s,
               cre_scr, cim_scr):
    u_refs = (u0_ref, u1_ref, u2_ref, u3_ref)
    up_scrs = (up0_s, up1_s, up2_s, up3_s)
    y_scrs = (y0_s, y1_s, y2_s, y3_s)

    @pl.when(pl.program_id(1) == 0)
    def _():
        cre_scr[...] = jnp.zeros_like(cre_scr)
        cim_scr[...] = jnp.zeros_like(cim_scr)

    for a in range(N_OCT):
        for r in range(S5_SEG):
            up_scrs[a][S5_SEG_PITCH * r:S5_SEG_PITCH * r + S5_SEG_ROWS, :] = (
                u_refs[a][S5_SEG_ROWS * r:S5_SEG_ROWS * (r + 1), :])

    def tok(a, s):
        return jnp.concatenate(
            [up_scrs[a][pl.ds(s + S5_Q * v, S5_SEG, stride=S5_SEG_PITCH), :] for v in range(S5_NV)], axis=0)

    xp = [[jnp.concatenate([tok(a, 2 * sp), tok(a, 2 * sp + 1)], axis=1).astype(BF16) for a in range(N_OCT)]
          for sp in range(S5_QP)]

    def cmul_add(b_re, b_im, m_re, m_im, x_re, x_im):
        return b_re + m_re * x_re - m_im * x_im, b_im + m_re * x_im + m_im * x_re

    hs = []
    for a in range(N_OCT):
        acc = None
        for sp in range(S5_QP):
            part = jnp.dot(xp[sp][a], winp_ref[sp, a], preferred_element_type=F32)
            acc = part if acc is None else acc + part
        blk = lambda v: (acc[S5_SEG * v:S5_SEG * (v + 1), :OCT_STATE], acc[S5_SEG * v:S5_SEG * (v + 1), OCT_STATE:])
        m_re, m_im = ptre_ref[a, 1:2, :], ptim_ref[a, 1:2, :]
        loc = [blk(0)]
        for v in range(1, S5_NV):
            loc.append(cmul_add(*blk(v), m_re, m_im, *loc[-1]))
        l_re, l_im = ptre_ref[a, S5_NV:S5_NV + 1, :], ptim_ref[a, S5_NV:S5_NV + 1, :]
        c_re, c_im = cre_scr[a:a + 1, :], cim_scr[a:a + 1, :]
        carry = []
        for r in range(S5_SEG):
            carry.append((c_re, c_im))
            c_re, c_im = cmul_add(loc[-1][0][r:r + 1, :], loc[-1][1][r:r + 1, :], l_re, l_im, c_re, c_im)
        cre_scr[a:a + 1, :] = c_re
        cim_scr[a:a + 1, :] = c_im
        cs_re = jnp.concatenate([c[0] for c in carry], axis=0)
        cs_im = jnp.concatenate([c[1] for c in carry], axis=0)
        ent = [(cs_re, cs_im)]
        for v in range(S5_NV - 1):
            ent.append(cmul_add(*loc[v], ptre_ref[a, v + 1:v + 2, :], ptim_ref[a, v + 1:v + 2, :], cs_re, cs_im))
        hs.append(jnp.concatenate([jnp.concatenate([e[0] for e in ent], axis=0),
                                   jnp.concatenate([e[1] for e in ent], axis=0)], axis=1).astype(BF16))

    for a in range(N_OCT):
        for tp in range(S5_QP):
            acc = jnp.dot(hs[a], woutp_ref[tp, a], preferred_element_type=F32)
            for sp in range(tp + 1):
                acc = acc + jnp.dot(xp[sp][a], wt_ref[tp - sp, a], preferred_element_type=F32)
            for v in range(S5_NV):
                rows = slice(S5_SEG * v, S5_SEG * (v + 1))
                y_scrs[a][pl.ds(2 * tp + S5_Q * v, S5_SEG, stride=S5_SEG_PITCH), :] = acc[rows, :LANES]
                y_scrs[a][pl.ds(2 * tp + 1 + S5_Q * v, S5_SEG, stride=S5_SEG_PITCH), :] = acc[rows, LANES:]
        d_a = d_ref[:, LANES * a:LANES * (a + 1)]
        for r in range(S5_SEG):
            prow = slice(S5_SEG_PITCH * r, S5_SEG_PITCH * r + S5_SEG_ROWS)
            urow = slice(S5_SEG_ROWS * r, S5_SEG_ROWS * (r + 1))
            y_scrs[a][prow, :] = _gelu_tanh(y_scrs[a][prow, :] + d_a * u_refs[a][urow, :])

    unpad = lambda ref: jnp.concatenate(
        [ref[S5_SEG_PITCH * r:S5_SEG_PITCH * r + S5_SEG_ROWS, :] for r in range(S5_SEG)], axis=0)
    y = jnp.concatenate([unpad(r) for r in y_scrs], axis=1)
    gate = jnp.dot(y.astype(BF16), wglu_ref[...], preferred_element_type=F32) + bglu_ref[...]
    y = y * _sigmoid(gate)
    z = z_ref[...].astype(F32)
    o_ref[...] = (y * (z * _sigmoid(z))).astype(BF16)


def _s5(u32, pbf, wt, winp, woutp, ptre, ptim, d_row, w_glu, b_glu, batch, seqlen):
    nb = seqlen // S5_TB
    t = batch * seqlen
    u_tile = lambda a: pl.BlockSpec((S5_TB, LANES), lambda b, i, a=a: (b * nb + i, a))
    return pl.pallas_call(
        _s5_kernel,
        grid=(batch, nb),
        in_specs=[u_tile(a) for a in range(N_OCT)] + [
            pl.BlockSpec((S5_TB, SSM_WIDTH), lambda b, i: (b * nb + i, PBF_ZA)),
            _const_spec(wt.shape), _const_spec(winp.shape), _const_spec(woutp.shape),
            _const_spec(ptre.shape), _const_spec(ptim.shape),
            _const_spec((1, SSM_WIDTH)), _const_spec((SSM_WIDTH, SSM_WIDTH)), _const_spec((1, SSM_WIDTH)),
        ],
        out_specs=pl.BlockSpec((S5_TB, SSM_WIDTH), lambda b, i: (b * nb + i, 0)),
        out_shape=jax.ShapeDtypeStruct((t, SSM_WIDTH), BF16),
        scratch_shapes=[pltpu.VMEM((S5_SEG * S5_SEG_PITCH, LANES), F32)] * (2 * N_OCT) + [
            pltpu.VMEM((8, OCT_STATE), F32),
            pltpu.VMEM((8, OCT_STATE), F32),
        ],
        compiler_params=pltpu.CompilerParams(
            dimension_semantics=("arbitrary", "arbitrary"), vmem_limit_bytes=VMEM_LIMIT),
        name="s5",
    )(u32, u32, u32, u32, pbf, wt, winp, woutp, ptre, ptim, d_row, w_glu, b_glu)


def _proj_hgrn_kernel(x_ref, nw_ref, w_ref, lbl_ref, hnw_ref, ou_ref, or_ref, o_ref,
                      ph_s, q0_s, qc_s, kt_s, k0_s, k1_s, k2_s, ke_s, v_s, dec_s, oacc_s, st_s, upd_s, sc_s, am_s,
                      *, blocks_per_seq):
    @pl.when(pl.program_id(0) % blocks_per_seq == 0)
    def _():
        st_s[...] = jnp.zeros_like(st_s)

    x = x_ref[...]
    ms = jnp.mean(x * x, axis=-1, keepdims=True)
    xn = (x * lax.rsqrt(ms + NORM_EPS) * nw_ref[...]).astype(BF16)
    ph_s[...] = jnp.dot(xn, w_ref[:, COL_H:COL_G], preferred_element_type=F32)
    ou_ref[...] = jnp.dot(xn, w_ref[:, :COL_ZA], preferred_element_type=F32)
    or_ref[:, :2 * D_MODEL] = jnp.dot(xn, w_ref[:, COL_G:], preferred_element_type=F32).astype(BF16)
    or_ref[:, 2 * D_MODEL:] = jnp.dot(xn, w_ref[:, COL_ZA:COL_H], preferred_element_type=F32).astype(BF16)
    col = lambda k: ph_s[:, HGRN_WIDTH * k:HGRN_WIDTH * (k + 1)]

    logits = lbl_ref[...]
    e = jnp.exp(logits - jnp.max(logits, axis=0, keepdims=True))
    lb = (e / jnp.sum(e, axis=0, keepdims=True))[0:1, :]

    q = col(PH_Q)
    qf = q * _sigmoid(q)
    forget = lb + (1.0 - lb) * _sigmoid(col(PH_F))
    lf = jnp.log(forget)
    key = 1.0 - forget

    row = lax.broadcasted_iota(jnp.int32, (HG_TB, HGRN_WIDTH), 0)
    r_sub = row % HG_SUB
    r_ch = row % HG_CH

    def down(x, d):
        return pltpu.roll(x, d, 0)

    def up(x, d):
        return pltpu.roll(x, HG_TB - d, 0)

    a = lf
    d = 1
    while d < HG_SUB:
        a = a + jnp.where(r_sub >= d, down(a, d), 0.0)
        d *= 2
    a3 = a.reshape(HG_TB // HG_SUB, HG_SUB, HGRN_WIDTH)
    tsub = jnp.broadcast_to(a3[:, HG_SUB - 1:HG_SUB, :], a3.shape).reshape(HG_TB, HGRN_WIDTH)
    n_sub = HG_CH // HG_SUB
    prev = [jnp.where(r_ch >= HG_SUB * k, down(tsub, HG_SUB * k), 0.0) for k in range(1, n_sub)]
    nxt = [jnp.where(r_ch < HG_CH - HG_SUB * k, up(tsub, HG_SUB * k), 0.0) for k in range(1, n_sub)]
    eprev = prev[0] + prev[1] + prev[2]
    enext = nxt[0] + nxt[1] + nxt[2]
    suf = tsub - a

    q0 = qf * jnp.exp(a)
    q0_s[...] = q0.astype(BF16)
    qc_s[...] = (q0 * jnp.exp(eprev)).astype(BF16)
    kt_s[...] = (key * jnp.exp(-a)).astype(BF16)
    k0 = key * jnp.exp(suf)
    k0_s[...] = k0.astype(BF16)
    k1 = k0 * jnp.exp(nxt[0])
    k1_s[...] = k1.astype(BF16)
    k2 = k1 * jnp.exp(nxt[1])
    k2_s[...] = k2.astype(BF16)
    ke_s[...] = (k2 * jnp.exp(nxt[2])).astype(BF16)
    dec_s[...] = jnp.exp(eprev + tsub + enext)
    v_s[...] = col(PH_I).astype(BF16)

    tq = lax.broadcasted_iota(jnp.int32, (HG_CH, n_sub * HG_CH), 0)
    cc = lax.broadcasted_iota(jnp.int32, (HG_CH, n_sub * HG_CH), 1)
    cls = cc // HG_CH
    ts = cc % HG_CH
    bi = tq // HG_SUB
    bj = ts // HG_SUB
    mask4 = ((cls == 0) & (bi == bj) & (ts <= tq)) | ((cls > 0) & ((bi - bj) == cls))

    n_ch = HG_TB // HG_CH
    half = n_sub * HG_CH // 2
    units = [(c, h, slice(c * HG_CH, (c + 1) * HG_CH), slice(HEAD_DIM * h, HEAD_DIM * (h + 1)))
             for c in range(n_ch) for h in range(HEADS)]
    for c, h, rows, ls in units:
        kcat = jnp.concatenate([kt_s[rows, ls], k0_s[rows, ls], k1_s[rows, ls], k2_s[rows, ls]], axis=0)
        sc_s[c * HEADS + h] = lax.dot_general(q0_s[rows, ls], kcat, (((1,), (1,)), ((), ())),
                                              preferred_element_type=F32)
    for c, h, rows, ls in units:
        upd_s[c * HEADS + h] = lax.dot_general(v_s[rows, ls], ke_s[rows, ls], (((0,), (0,)), ((), ())),
                                               preferred_element_type=F32)
    for c, h, rows, ls in units:
        sc = jnp.where(mask4, sc_s[c * HEADS + h], 0.0)
        am_s[c * HEADS + h] = (sc[:, :half] + sc[:, half:]).astype(BF16)
    for c, h, rows, ls in units:
        vv = v_s[rows, ls]
        oacc_s[rows, ls] = jnp.dot(am_s[c * HEADS + h], jnp.concatenate([vv, vv], axis=0),
                                   preferred_element_type=F32)
    for h in range(HEADS):
        ls = slice(HEAD_DIM * h, HEAD_DIM * (h + 1))
        st = st_s[h]
        for c in range(n_ch):
            rows = slice(c * HG_CH, (c + 1) * HG_CH)
            oacc_s[rows, ls] += lax.dot_general(qc_s[rows, ls], st.astype(BF16), (((1,), (1,)), ((), ())),
                                                preferred_element_type=F32)
            st = st * dec_s[c * HG_CH:c * HG_CH + 1, ls] + upd_s[c * HEADS + h]
        st_s[h] = st

    o = oacc_s[...] * _sigmoid(col(PH_OG))
    parts = []
    for h in range(HEADS):
        oh = o[:, HEAD_DIM * h:HEAD_DIM * (h + 1)]
        ms = jnp.mean(oh * oh, axis=-1, keepdims=True)
        parts.append(oh * lax.rsqrt(ms + NORM_EPS))
    o = jnp.concatenate(parts, axis=1) * hnw_ref[...]
    z = col(PH_ZB)
    o_ref[...] = (o * (z * _sigmoid(z))).astype(BF16)


def _proj_hgrn(x2, norm_w, w, lb_logits, hgrn_norm_w, seqlen):
    t = x2.shape[0]
    wide = (HG_TB, HGRN_WIDTH)
    units = HG_TB // HG_CH * HEADS
    n_cls = HG_CH // HG_SUB
    row_blk = lambda w: pl.BlockSpec((HG_TB, w), lambda i: (i, 0))
    return pl.pallas_call(
        functools.partial(_proj_hgrn_kernel, blocks_per_seq=seqlen // HG_TB),
        grid=(t // HG_TB,),
        in_specs=[
            row_blk(D_MODEL), _const_spec((1, D_MODEL)),
            _const_spec((D_MODEL, IN_WIDTH)),
            _const_spec(lb_logits.shape), _const_spec((1, HGRN_WIDTH)),
        ],
        out_specs=[row_blk(SSM_WIDTH), row_blk(PBF_WIDTH), row_blk(HGRN_WIDTH)],
        out_shape=[
            jax.ShapeDtypeStruct((t, SSM_WIDTH), F32),
            jax.ShapeDtypeStruct((t, PBF_WIDTH), BF16),
            jax.ShapeDtypeStruct((t, HGRN_WIDTH), BF16),
        ],
        scratch_shapes=[pltpu.VMEM((HG_TB, PH_WIDTH), F32)] + [pltpu.VMEM(wide, BF16)] * 8 + [
            pltpu.VMEM(wide, F32), pltpu.VMEM(wide, F32),
            pltpu.VMEM((HEADS, HEAD_DIM, HEAD_DIM), F32),
            pltpu.VMEM((units, HEAD_DIM, HEAD_DIM), F32),
            pltpu.VMEM((units, HG_CH, n_cls * HG_CH), F32),
            pltpu.VMEM((units, HG_CH, n_cls * HG_CH // 2), BF16)],
        compiler_params=pltpu.CompilerParams(
            dimension_semantics=("arbitrary",), vmem_limit_bytes=VMEM_LIMIT),
        name="proj_hgrn",
    )(x2, norm_w, w, lb_logits, hgrn_norm_w)


def _merge_kernel(x_ref, ya_ref, yb_ref, ga_ref, gb_ref, wpa_ref, wpb_ref, wo_ref, fnw_ref, o_ref):
    pa = jnp.dot(ya_ref[...], wpa_ref[...], preferred_element_type=F32)
    pb = jnp.dot(yb_ref[...], wpb_ref[...], preferred_element_type=F32)
    merged = _sigmoid(ga_ref[...].astype(F32)) * pa + _sigmoid(gb_ref[...].astype(F32)) * pb
    h = x_ref[...] + jnp.dot(merged.astype(BF16), wo_ref[...], preferred_element_type=F32)
    ms = jnp.mean(h * h, axis=-1, keepdims=True)
    o_ref[...] = h * lax.rsqrt(ms + NORM_EPS) * fnw_ref[...]


def _merge(x2, ya, yb, pbf, w_pa, w_pb, w_out, fnw):
    t = x2.shape[0]
    return pl.pallas_call(
        _merge_kernel,
        grid=(t // TM_OUT,),
        in_specs=[
            pl.BlockSpec((TM_OUT, D_MODEL), lambda i: (i, 0)),
            pl.BlockSpec((TM_OUT, SSM_WIDTH), lambda i: (i, 0)),
            pl.BlockSpec((TM_OUT, HGRN_WIDTH), lambda i: (i, 0)),
            pl.BlockSpec((TM_OUT, D_MODEL), lambda i: (i, 0)),
            pl.BlockSpec((TM_OUT, D_MODEL), lambda i: (i, 1)),
            _const_spec((SSM_WIDTH, D_MODEL)), _const_spec((HGRN_WIDTH, D_MODEL)),
            _const_spec((D_MODEL, D_MODEL)), _const_spec((1, D_MODEL)),
        ],
        out_specs=pl.BlockSpec((TM_OUT, D_MODEL), lambda i: (i, 0)),
        out_shape=jax.ShapeDtypeStruct((t, D_MODEL), F32),
        compiler_params=pltpu.CompilerParams(
            dimension_semantics=("arbitrary",), vmem_limit_bytes=VMEM_LIMIT),
        name="merge",
    )(x2, ya, yb, pbf, pbf, w_pa, w_pb, w_out, fnw)


def kernel(x, norm_w, w_in, ssm_lambda_re, ssm_lambda_im, ssm_b_re, ssm_b_im, ssm_c_re, ssm_c_im, ssm_d,
           ssm_log_dt, ssm_w_glu, ssm_b_glu, hgrn_lb_logits, hgrn_norm_w, w_proj_a, w_proj_b, w_out,
           final_norm_w):
    batch, seqlen, _ = x.shape
    assert norm_w.shape[0] == 1, "single-layer block"
    assert seqlen % S5_TB == 0 and seqlen % HG_TB == 0
    x2 = x.reshape(batch * seqlen, D_MODEL)
    u32, pbf, yb = _proj_hgrn(x2, norm_w[0][None, :], w_in[0].astype(BF16), hgrn_lb_logits,
                              hgrn_norm_w[0][None, :], seqlen)

    wt, winp, woutp, ptre, ptim = _s5_prep(ssm_lambda_re[0], ssm_lambda_im[0], ssm_b_re[0], ssm_b_im[0],
                                           ssm_c_re[0], ssm_c_im[0], ssm_log_dt[0])
    ya = _s5(u32, pbf, wt, winp, woutp, ptre, ptim, ssm_d[0].reshape(1, SSM_WIDTH),
             ssm_w_glu[0].astype(BF16), ssm_b_glu[0][None, :], batch, seqlen)
    out = _merge(x2, ya, yb, pbf, w_proj_a[0].astype(BF16), w_proj_b[0].astype(BF16),
                 w_out[0].astype(BF16), final_norm_w[None, :])
    return out.reshape(batch, seqlen, D_MODEL)
```

```python
import functools

import jax
import jax.numpy as jnp
from jax import lax
from jax.experimental import pallas as pl
from jax.experimental.pallas import tpu as pltpu

F32 = jnp.float32
BF16 = jnp.bfloat16

D_MODEL = 1024
SSM_WIDTH = 512
SSM_GROUP = 16
SSM_GROUPS = 32
SSM_STATE = 64
HGRN_WIDTH = 512
HEAD_DIM = 128
HEADS = 4
NORM_EPS = 1e-6
LAMBDA_RE_MAX = -1e-4

IN_WIDTH = 2 * SSM_WIDTH + 5 * HGRN_WIDTH + 2 * D_MODEL
COL_ZA, COL_H, COL_G = SSM_WIDTH, 2 * SSM_WIDTH, 2 * SSM_WIDTH + 5 * HGRN_WIDTH
PH_WIDTH = 5 * HGRN_WIDTH
PH_Q, PH_F, PH_I, PH_OG, PH_ZB = 0, 1, 2, 3, 4
PBF_WIDTH = 2 * D_MODEL + SSM_WIDTH
PBF_ZA = 4

LANES = 128
OCT = LANES // SSM_GROUP
N_OCT = SSM_GROUPS // OCT
OCT_STATE = OCT * SSM_STATE

S5_Q = 8
S5_QP = S5_Q // 2
S5_TB = 2048
S5_NCH = S5_TB // S5_Q
S5_SEG = 8
S5_NV = S5_NCH // S5_SEG
S5_PT_ROWS = S5_NV + 8
S5_SEG_ROWS = S5_Q * S5_NV
S5_SEG_PITCH = S5_SEG_ROWS + 8
HG_TB = 512
HG_CH = 64
HG_SUB = 16
TM_OUT = 512

VMEM_LIMIT = 56 * 1024 * 1024


def _sigmoid(x):
    return 0.5 * jnp.tanh(0.5 * x) + 0.5


def _const_spec(shape):
    nd = len(shape)
    return pl.BlockSpec(shape, lambda *_: (0,) * nd, pipeline_mode=pl.Buffered(1))


def _disc(lam_re, lam_im, log_dt):
    return jnp.minimum(lam_re, LAMBDA_RE_MAX), lam_im, jnp.exp(log_dt)


def _cpow(lr, li, dt, k):
    mag = jnp.exp(k * (lr * dt))
    ang = k * (li * dt)
    return mag * jnp.cos(ang), mag * jnp.sin(ang)


def _split_bf16(x):
    hi = x.astype(BF16)
    return hi, (x - hi.astype(F32)).astype(BF16)


def _dot3(a, b_hi, b_lo):
    a_hi, a_lo = _split_bf16(a)
    d = lambda p, q: jnp.dot(p, q, preferred_element_type=F32)
    return d(a_hi, b_hi) + (d(a_hi, b_lo) + d(a_lo, b_hi))


def _cmul(a_re, a_im, b_re, b_im):
    return a_re * b_re - a_im * b_im, a_re * b_im + a_im * b_re


def _s5_prep_kernel(lam_c, b_t, c_n, lam_n, lam_r,
                    kb_ref, win_ref, wout_ref, ptre_ref, ptim_ref, pw_s, pm_s, xk_s):
    @pl.when(pl.program_id(0) == 0)
    def _():
        lr, li, dt = _disc(lam_c[0], lam_c[1], lam_c[2])
        ab_re, ab_im = _cpow(lr, li, dt, 1.0)
        den = lr * lr + li * li
        nr = ab_re - 1.0
        pw_s[0] = (nr * lr + ab_im * li) / den
        pw_s[1] = (ab_im * lr - nr * li) / den
        pw_s[2] = jnp.ones_like(ab_re)
        pw_s[3] = jnp.zeros_like(ab_re)
        pw_s[4] = ab_re
        pw_s[5] = ab_im
        m_re, m_im = _cpow(*_disc(lam_n[0], lam_n[1], lam_n[2]), 1.0)
        pm_s[0] = m_re
        pm_s[1] = m_im
        pm_s[2] = m_re
        pm_s[3] = m_im
        j = lax.broadcasted_iota(jnp.int32, (S5_PT_ROWS, OCT_STATE), 0).astype(F32) * float(S5_Q)
        for a in range(N_OCT):
            t_re, t_im = _cpow(*_disc(lam_r[0, a:a + 1, :], lam_r[1, a:a + 1, :], lam_r[2, a:a + 1, :]), j)
            ptre_ref[a] = t_re
            ptim_ref[a] = t_im

    col = lax.broadcasted_iota(jnp.int32, (SSM_GROUP, SSM_WIDTH), 1)

    def strip(g, carry):
        rows = pl.ds(pl.multiple_of(g * SSM_GROUP, SSM_GROUP), SSM_GROUP)
        grow = pl.ds(g, 1)
        coef_re, coef_im = pw_s[0, grow, :], pw_s[1, grow, :]
        p_re, p_im = pw_s[2, grow, :], pw_s[3, grow, :]
        bb_re, bb_im = _cmul(coef_re, coef_im, b_t[0, rows, :], b_t[1, rows, :])
        x_re, x_im = _cmul(bb_re, bb_im, p_re, p_im)
        xk_s[0, rows, :] = x_re[:, :SSM_STATE]
        xk_s[1, rows, :] = x_im[:, :SSM_STATE]
        a = g // OCT
        lrows = pl.ds(pl.multiple_of((g % OCT) * SSM_GROUP, SSM_GROUP), SSM_GROUP)
        m_in = col // SSM_STATE == g % OCT
        win_ref[0, a, 0, lrows, :] = jnp.concatenate(
            [jnp.where(m_in, x_re, 0.0), jnp.where(m_in, x_im, 0.0)], axis=1).astype(BF16)
        return carry

    lax.fori_loop(0, SSM_GROUPS, strip, 0, unroll=4)

    cr, ci = c_n[0], c_n[1]
    kfull = _dot3(xk_s[0], *_split_bf16(cr)) - _dot3(xk_s[1], *_split_bf16(ci))
    rowk = lax.broadcasted_iota(jnp.int32, (SSM_WIDTH, SSM_WIDTH), 0)
    colk = lax.broadcasted_iota(jnp.int32, (SSM_WIDTH, SSM_WIDTH), 1)
    kb_ref[0] = jnp.where(rowk // SSM_GROUP == colk // SSM_GROUP, kfull, 0.0).astype(BF16)

    p1_re, p1_im = pm_s[0], pm_s[1]
    w_re, w_im = _cmul(cr, ci, p1_re, p1_im)
    rowo = lax.broadcasted_iota(jnp.int32, (OCT_STATE, LANES), 0)
    colo = lax.broadcasted_iota(jnp.int32, (OCT_STATE, LANES), 1)
    m_out = rowo // SSM_STATE == colo // SSM_GROUP
    for a in range(N_OCT):
        ls = slice(LANES * a, LANES * (a + 1))
        wout_ref[0, a, :OCT_STATE, :] = jnp.where(m_out, jnp.tile(w_re[:, ls], (OCT, 1)), 0.0).astype(BF16)
        wout_ref[0, a, OCT_STATE:, :] = jnp.where(m_out, jnp.tile(-w_im[:, ls], (OCT, 1)), 0.0).astype(BF16)

    pw_s[2], pw_s[3] = _cmul(pw_s[2], pw_s[3], pw_s[4], pw_s[5])
    pm_s[0], pm_s[1] = _cmul(p1_re, p1_im, pm_s[2], pm_s[3])


def _s5_prep(lam_re, lam_im, b_re, b_im, c_re, c_im, log_dt):
    g, n, p = SSM_GROUPS, SSM_STATE, SSM_GROUP
    lam = jnp.stack([lam_re, lam_im, jnp.broadcast_to(log_dt[:, None], (g, n))])
    b = jnp.stack([b_re, b_im])
    c = jnp.stack([c_re, c_im])
    args = (jnp.tile(lam, (1, 1, OCT)),
            jnp.tile(b.transpose(0, 1, 3, 2).reshape(2, g * p, n), (1, 1, OCT)),
            c.transpose(0, 3, 1, 2).reshape(2, n, g * p),
            jnp.repeat(lam.transpose(0, 2, 1), p, axis=2),
            lam.reshape(3, N_OCT, OCT_STATE))
    mat = (S5_QP, N_OCT, 2, LANES, 2 * OCT_STATE)
    blk = (1, N_OCT, 1, LANES, 2 * OCT_STATE)
    tab = (N_OCT, S5_PT_ROWS, OCT_STATE)
    kb, win, woutp, ptre, ptim = pl.pallas_call(
        _s5_prep_kernel,
        grid=(S5_Q,),
        in_specs=[_const_spec(a.shape) for a in args],
        out_specs=[
            pl.BlockSpec((1, SSM_WIDTH, SSM_WIDTH), lambda t: (t, 0, 0)),
            pl.BlockSpec(blk, lambda t: ((S5_Q - 1 - t) // 2, 0, (S5_Q - 1 - t) % 2, 0, 0)),
            pl.BlockSpec((1, N_OCT, 2 * OCT_STATE, LANES), lambda t: (t // 2, 0, 0, t % 2)),
            pl.BlockSpec(tab, lambda t: (0, 0, 0)),
            pl.BlockSpec(tab, lambda t: (0, 0, 0)),
        ],
        out_shape=[
            jax.ShapeDtypeStruct((S5_Q, SSM_WIDTH, SSM_WIDTH), BF16),
            jax.ShapeDtypeStruct(mat, BF16),
            jax.ShapeDtypeStruct((S5_QP, N_OCT, 2 * OCT_STATE, 2 * LANES), BF16),
            jax.ShapeDtypeStruct(tab, F32),
            jax.ShapeDtypeStruct(tab, F32),
        ],
        scratch_shapes=[pltpu.VMEM((6, SSM_GROUPS, SSM_WIDTH), F32),
                        pltpu.VMEM((4, SSM_STATE, SSM_WIDTH), F32),
                        pltpu.VMEM((2, SSM_WIDTH, SSM_STATE), F32)],
        compiler_params=pltpu.CompilerParams(
            dimension_semantics=("arbitrary",), vmem_limit_bytes=VMEM_LIMIT),
        name="s5_prep",
    )(*args)

    kbo = jnp.stack([kb[:, LANES * a:LANES * (a + 1), LANES * a:LANES * (a + 1)] for a in range(N_OCT)], axis=1)
    zero = jnp.zeros_like(kbo[0])

    def pair_tile(d):
        top = jnp.concatenate([kbo[2 * d], kbo[2 * d + 1]], axis=-1)
        bot = jnp.concatenate([kbo[2 * d - 1] if d > 0 else zero, kbo[2 * d]], axis=-1)
        return jnp.concatenate([top, bot], axis=-2)

    wt = jnp.stack([pair_tile(d) for d in range(S5_QP)])
    winp = win.reshape(S5_QP, N_OCT, 2 * LANES, 2 * OCT_STATE)
    return wt, winp, woutp, ptre, ptim


def _gelu_tanh(x):
    c = 0.7978845608028654
    return x * (0.5 * (1.0 + jnp.tanh(c * (x + 0.044715 * (x * x * x)))))


def _s5_kernel(u0_ref, u1_ref, u2_ref, u3_ref, z_ref, wt_ref, winp_ref, woutp_ref, ptre_ref, ptim_ref,
               d_ref, wglu_ref, bglu_ref, o_ref, up0_s, up1_s, up2_s, up3_s, y0_s, y1_s, y2_s, y3_s,
               cre_scr, cim_scr):
    u_refs = (u0_ref, u1_ref, u2_ref, u3_ref)
    up_scrs = (up0_s, up1_s, up2_s, up3_s)
    y_scrs = (y0_s, y1_s, y2_s, y3_s)

    @pl.when(pl.program_id(1) == 0)
    def _():
        cre_scr[...] = jnp.zeros_like(cre_scr)
        cim_scr[...] = jnp.zeros_like(cim_scr)

    for a in range(N_OCT):
        for r in range(S5_SEG):
            up_scrs[a][S5_SEG_PITCH * r:S5_SEG_PITCH * r + S5_SEG_ROWS, :] = (
                u_refs[a][S5_SEG_ROWS * r:S5_SEG_ROWS * (r + 1), :])

    def tok(a, s):
        return jnp.concatenate(
            [up_scrs[a][pl.ds(s + S5_Q * v, S5_SEG, stride=S5_SEG_PITCH), :] for v in range(S5_NV)], axis=0)

    xp = [[jnp.concatenate([tok(a, 2 * sp), tok(a, 2 * sp + 1)], axis=1).astype(BF16) for a in range(N_OCT)]
          for sp in range(S5_QP)]

    def cmul_add(b_re, b_im, m_re, m_im, x_re, x_im):
        return b_re + m_re * x_re - m_im * x_im, b_im + m_re * x_im + m_im * x_re

    hs = []
    for a in range(N_OCT):
        acc = None
        for sp in range(S5_QP):
            part = jnp.dot(xp[sp][a], winp_ref[sp, a], preferred_element_type=F32)
            acc = part if acc is None else acc + part
        blk = lambda v: (acc[S5_SEG * v:S5_SEG * (v + 1), :OCT_STATE], acc[S5_SEG * v:S5_SEG * (v + 1), OCT_STATE:])
        m_re, m_im = ptre_ref[a, 1:2, :], ptim_ref[a, 1:2, :]
        loc = [blk(0)]
        for v in range(1, S5_NV):
            loc.append(cmul_add(*blk(v), m_re, m_im, *loc[-1]))
        l_re, l_im = ptre_ref[a, S5_NV:S5_NV + 1, :], ptim_ref[a, S5_NV:S5_NV + 1, :]
        c_re, c_im = cre_scr[a:a + 1, :], cim_scr[a:a + 1, :]
        carry = []
        for r in range(S5_SEG):
            carry.append((c_re, c_im))
            c_re, c_im = cmul_add(loc[-1][0][r:r + 1, :], loc[-1][1][r:r + 1, :], l_re, l_im, c_re, c_im)
        cre_scr[a:a + 1, :] = c_re
        cim_scr[a:a + 1, :] = c_im
        cs_re = jnp.concatenate([c[0] for c in carry], axis=0)
        cs_im = jnp.concatenate([c[1] for c in carry], axis=0)
        ent = [(cs_re, cs_im)]
        for v in range(S5_NV - 1):
            ent.append(cmul_add(*loc[v], ptre_ref[a, v + 1:v + 2, :], ptim_ref[a, v + 1:v + 2, :], cs_re, cs_im))
        hs.append(jnp.concatenate([jnp.concatenate([e[0] for e in ent], axis=0),
                                   jnp.concatenate([e[1] for e in ent], axis=0)], axis=1).astype(BF16))

    for a in range(N_OCT):
        for tp in range(S5_QP):
            acc = jnp.dot(hs[a], woutp_ref[tp, a], preferred_element_type=F32)
            for sp in range(tp + 1):
                acc = acc + jnp.dot(xp[sp][a], wt_ref[tp - sp, a], preferred_element_type=F32)
            for v in range(S5_NV):
                rows = slice(S5_SEG * v, S5_SEG * (v + 1))
                y_scrs[a][pl.ds(2 * tp + S5_Q * v, S5_SEG, stride=S5_SEG_PITCH), :] = acc[rows, :LANES]
                y_scrs[a][pl.ds(2 * tp + 1 + S5_Q * v, S5_SEG, stride=S5_SEG_PITCH), :] = acc[rows, LANES:]
        d_a = d_ref[:, LANES * a:LANES * (a + 1)]
        for r in range(S5_SEG):
            prow = slice(S5_SEG_PITCH * r, S5_SEG_PITCH * r + S5_SEG_ROWS)
            urow = slice(S5_SEG_ROWS * r, S5_SEG_ROWS * (r + 1))
            y_scrs[a][prow, :] = _gelu_tanh(y_scrs[a][prow, :] + d_a * u_refs[a][urow, :])

    unpad = lambda ref: jnp.concatenate(
        [ref[S5_SEG_PITCH * r:S5_SEG_PITCH * r + S5_SEG_ROWS, :] for r in range(S5_SEG)], axis=0)
    y = jnp.concatenate([unpad(r) for r in y_scrs], axis=1)
    gate = jnp.dot(y.astype(BF16), wglu_ref[...].astype(BF16), preferred_element_type=F32) + bglu_ref[...]
    y = y * _sigmoid(gate)
    z = z_ref[...].astype(F32)
    o_ref[...] = (y * (z * _sigmoid(z))).astype(BF16)


def _s5(u32, pbf, wt, winp, woutp, ptre, ptim, d_row, w_glu, b_glu, batch, seqlen):
    nb = seqlen // S5_TB
    t = batch * seqlen
    u_tile = lambda a: pl.BlockSpec((S5_TB, LANES), lambda b, i, a=a: (b * nb + i, a))
    return pl.pallas_call(
        _s5_kernel,
        grid=(batch, nb),
        in_specs=[u_tile(a) for a in range(N_OCT)] + [
            pl.BlockSpec((S5_TB, SSM_WIDTH), lambda b, i: (b * nb + i, PBF_ZA)),
            _const_spec(wt.shape), _const_spec(winp.shape), _const_spec(woutp.shape),
            _const_spec(ptre.shape), _const_spec(ptim.shape),
            _const_spec((1, SSM_WIDTH)), _const_spec((SSM_WIDTH, SSM_WIDTH)), _const_spec((1, SSM_WIDTH)),
        ],
        out_specs=pl.BlockSpec((S5_TB, SSM_WIDTH), lambda b, i: (b * nb + i, 0)),
        out_shape=jax.ShapeDtypeStruct((t, SSM_WIDTH), BF16),
        scratch_shapes=[pltpu.VMEM((S5_SEG * S5_SEG_PITCH, LANES), F32)] * (2 * N_OCT) + [
            pltpu.VMEM((8, OCT_STATE), F32),
            pltpu.VMEM((8, OCT_STATE), F32),
        ],
        compiler_params=pltpu.CompilerParams(
            dimension_semantics=("arbitrary", "arbitrary"), vmem_limit_bytes=VMEM_LIMIT),
        name="s5",
    )(u32, u32, u32, u32, pbf, wt, winp, woutp, ptre, ptim, d_row, w_glu, b_glu)


def _proj_hgrn_kernel(x_ref, nw_ref, w_ref, lbl_ref, hnw_ref, ou_ref, or_ref, o_ref,
                      ph_s, q0_s, qc_s, kt_s, k0_s, k1_s, k2_s, ke_s, v_s, dec_s, oacc_s, st_s, upd_s, sc_s, am_s,
                      *, blocks_per_seq):
    @pl.when(pl.program_id(0) % blocks_per_seq == 0)
    def _():
        st_s[...] = jnp.zeros_like(st_s)

    x = x_ref[...]
    ms = jnp.mean(x * x, axis=-1, keepdims=True)
    xn = (x * lax.rsqrt(ms + NORM_EPS) * nw_ref[...]).astype(BF16)
    proj = lambda lo, hi: jnp.dot(xn, w_ref[:, lo:hi].astype(BF16), preferred_element_type=F32)
    ph_s[...] = proj(COL_H, COL_G)
    ou_ref[...] = proj(0, COL_ZA)
    or_ref[:, :2 * D_MODEL] = proj(COL_G, IN_WIDTH).astype(BF16)
    or_ref[:, 2 * D_MODEL:] = proj(COL_ZA, COL_H).astype(BF16)
    col = lambda k: ph_s[:, HGRN_WIDTH * k:HGRN_WIDTH * (k + 1)]

    logits = lbl_ref[...]
    e = jnp.exp(logits - jnp.max(logits, axis=0, keepdims=True))
    lb = (e / jnp.sum(e, axis=0, keepdims=True))[0:1, :]

    q = col(PH_Q)
    qf = q * _sigmoid(q)
    forget = lb + (1.0 - lb) * _sigmoid(col(PH_F))
    lf = jnp.log(forget)
    key = 1.0 - forget

    row = lax.broadcasted_iota(jnp.int32, (HG_TB, HGRN_WIDTH), 0)
    r_sub = row % HG_SUB
    r_ch = row % HG_CH

    def down(x, d):
        return pltpu.roll(x, d, 0)

    def up(x, d):
        return pltpu.roll(x, HG_TB - d, 0)

    a = lf
    d = 1
    while d < HG_SUB:
        a = a + jnp.where(r_sub >= d, down(a, d), 0.0)
        d *= 2
    a3 = a.reshape(HG_TB // HG_SUB, HG_SUB, HGRN_WIDTH)
    tsub = jnp.broadcast_to(a3[:, HG_SUB - 1:HG_SUB, :], a3.shape).reshape(HG_TB, HGRN_WIDTH)
    n_sub = HG_CH // HG_SUB
    prev = [jnp.where(r_ch >= HG_SUB * k, down(tsub, HG_SUB * k), 0.0) for k in range(1, n_sub)]
    nxt = [jnp.where(r_ch < HG_CH - HG_SUB * k, up(tsub, HG_SUB * k), 0.0) for k in range(1, n_sub)]
    eprev = prev[0] + prev[1] + prev[2]
    enext = nxt[0] + nxt[1] + nxt[2]
    suf = tsub - a

    q0 = qf * jnp.exp(a)
    q0_s[...] = q0.astype(BF16)
    qc_s[...] = (q0 * jnp.exp(eprev)).astype(BF16)
    kt_s[...] = (key * jnp.exp(-a)).astype(BF16)
    k0 = key * jnp.exp(suf)
    k0_s[...] = k0.astype(BF16)
    k1 = k0 * jnp.exp(nxt[0])
    k1_s[...] = k1.astype(BF16)
    k2 = k1 * jnp.exp(nxt[1])
    k2_s[...] = k2.astype(BF16)
    ke_s[...] = (k2 * jnp.exp(nxt[2])).astype(BF16)
    dec_s[...] = jnp.exp(eprev + tsub + enext)
    v_s[...] = col(PH_I).astype(BF16)

    tq = lax.broadcasted_iota(jnp.int32, (HG_CH, n_sub * HG_CH), 0)
    cc = lax.broadcasted_iota(jnp.int32, (HG_CH, n_sub * HG_CH), 1)
    cls = cc // HG_CH
    ts = cc % HG_CH
    bi = tq // HG_SUB
    bj = ts // HG_SUB
    mask4 = ((cls == 0) & (bi == bj) & (ts <= tq)) | ((cls > 0) & ((bi - bj) == cls))

    n_ch = HG_TB // HG_CH
    half = n_sub * HG_CH // 2
    units = [(c, h, slice(c * HG_CH, (c + 1) * HG_CH), slice(HEAD_DIM * h, HEAD_DIM * (h + 1)))
             for c in range(n_ch) for h in range(HEADS)]
    for c, h, rows, ls in units:
        kcat = jnp.concatenate([kt_s[rows, ls], k0_s[rows, ls], k1_s[rows, ls], k2_s[rows, ls]], axis=0)
        sc_s[c * HEADS + h] = lax.dot_general(q0_s[rows, ls], kcat, (((1,), (1,)), ((), ())),
                                              preferred_element_type=F32)
    for c, h, rows, ls in units:
        upd_s[c * HEADS + h] = lax.dot_general(v_s[rows, ls], ke_s[rows, ls], (((0,), (0,)), ((), ())),
                                               preferred_element_type=F32)
    for c, h, rows, ls in units:
        sc = jnp.where(mask4, sc_s[c * HEADS + h], 0.0)
        am_s[c * HEADS + h] = (sc[:, :half] + sc[:, half:]).astype(BF16)
    for c, h, rows, ls in units:
        vv = v_s[rows, ls]
        oacc_s[rows, ls] = jnp.dot(am_s[c * HEADS + h], jnp.concatenate([vv, vv], axis=0),
                                   preferred_element_type=F32)
    for h in range(HEADS):
        ls = slice(HEAD_DIM * h, HEAD_DIM * (h + 1))
        st = st_s[h]
        for c in range(n_ch):
            rows = slice(c * HG_CH, (c + 1) * HG_CH)
            oacc_s[rows, ls] += lax.dot_general(qc_s[rows, ls], st.astype(BF16), (((1,), (1,)), ((), ())),
                                                preferred_element_type=F32)
            st = st * dec_s[c * HG_CH:c * HG_CH + 1, ls] + upd_s[c * HEADS + h]
        st_s[h] = st

    o = oacc_s[...] * _sigmoid(col(PH_OG))
    parts = []
    for h in range(HEADS):
        oh = o[:, HEAD_DIM * h:HEAD_DIM * (h + 1)]
        ms = jnp.mean(oh * oh, axis=-1, keepdims=True)
        parts.append(oh * lax.rsqrt(ms + NORM_EPS))
    o = jnp.concatenate(parts, axis=1) * hnw_ref[...]
    z = col(PH_ZB)
    o_ref[...] = (o * (z * _sigmoid(z))).astype(BF16)


def _proj_hgrn(x2, norm_w, w, lb_logits, hgrn_norm_w, seqlen):
    t = x2.shape[0]
    wide = (HG_TB, HGRN_WIDTH)
    units = HG_TB // HG_CH * HEADS
    n_cls = HG_CH // HG_SUB
    row_blk = lambda w: pl.BlockSpec((HG_TB, w), lambda i: (i, 0))
    return pl.pallas_call(
        functools.partial(_proj_hgrn_kernel, blocks_per_seq=seqlen // HG_TB),
        grid=(t // HG_TB,),
        in_specs=[
            row_blk(D_MODEL), _const_spec((1, D_MODEL)),
            _const_spec((D_MODEL, IN_WIDTH)),
            _const_spec(lb_logits.shape), _const_spec((1, HGRN_WIDTH)),
        ],
        out_specs=[row_blk(SSM_WIDTH), row_blk(PBF_WIDTH), row_blk(HGRN_WIDTH)],
        out_shape=[
            jax.ShapeDtypeStruct((t, SSM_WIDTH), F32),
            jax.ShapeDtypeStruct((t, PBF_WIDTH), BF16),
            jax.ShapeDtypeStruct((t, HGRN_WIDTH), BF16),
        ],
        scratch_shapes=[pltpu.VMEM((HG_TB, PH_WIDTH), F32)] + [pltpu.VMEM(wide, BF16)] * 8 + [
            pltpu.VMEM(wide, F32), pltpu.VMEM(wide, F32),
            pltpu.VMEM((HEADS, HEAD_DIM, HEAD_DIM), F32),
            pltpu.VMEM((units, HEAD_DIM, HEAD_DIM), F32),
            pltpu.VMEM((units, HG_CH, n_cls * HG_CH), F32),
            pltpu.VMEM((units, HG_CH, n_cls * HG_CH // 2), BF16)],
        compiler_params=pltpu.CompilerParams(
            dimension_semantics=("arbitrary",), vmem_limit_bytes=VMEM_LIMIT),
        name="proj_hgrn",
    )(x2, norm_w, w, lb_logits, hgrn_norm_w)


def _merge_kernel(x_ref, ya_ref, yb_ref, ga_ref, gb_ref, wpa_ref, wpb_ref, wo_ref, fnw_ref, o_ref):
    pa = jnp.dot(ya_ref[...], wpa_ref[...].astype(BF16), preferred_element_type=F32)
    pb = jnp.dot(yb_ref[...], wpb_ref[...].astype(BF16), preferred_element_type=F32)
    merged = _sigmoid(ga_ref[...].astype(F32)) * pa + _sigmoid(gb_ref[...].astype(F32)) * pb
    h = x_ref[...] + jnp.dot(merged.astype(BF16), wo_ref[...].astype(BF16), preferred_element_type=F32)
    ms = jnp.mean(h * h, axis=-1, keepdims=True)
    o_ref[...] = h * lax.rsqrt(ms + NORM_EPS) * fnw_ref[...]


def _merge(x2, ya, yb, pbf, w_pa, w_pb, w_out, fnw):
    t = x2.shape[0]
    return pl.pallas_call(
        _merge_kernel,
        grid=(t // TM_OUT,),
        in_specs=[
            pl.BlockSpec((TM_OUT, D_MODEL), lambda i: (i, 0)),
            pl.BlockSpec((TM_OUT, SSM_WIDTH), lambda i: (i, 0)),
            pl.BlockSpec((TM_OUT, HGRN_WIDTH), lambda i: (i, 0)),
            pl.BlockSpec((TM_OUT, D_MODEL), lambda i: (i, 0)),
            pl.BlockSpec((TM_OUT, D_MODEL), lambda i: (i, 1)),
            _const_spec((SSM_WIDTH, D_MODEL)), _const_spec((HGRN_WIDTH, D_MODEL)),
            _const_spec((D_MODEL, D_MODEL)), _const_spec((1, D_MODEL)),
        ],
        out_specs=pl.BlockSpec((TM_OUT, D_MODEL), lambda i: (i, 0)),
        out_shape=jax.ShapeDtypeStruct((t, D_MODEL), F32),
        compiler_params=pltpu.CompilerParams(
            dimension_semantics=("arbitrary",), vmem_limit_bytes=VMEM_LIMIT),
        name="merge",
    )(x2, ya, yb, pbf, pbf, w_pa, w_pb, w_out, fnw)


def kernel(x, norm_w, w_in, ssm_lambda_re, ssm_lambda_im, ssm_b_re, ssm_b_im, ssm_c_re, ssm_c_im, ssm_d,
           ssm_log_dt, ssm_w_glu, ssm_b_glu, hgrn_lb_logits, hgrn_norm_w, w_proj_a, w_proj_b, w_out,
           final_norm_w):
    batch, seqlen, _ = x.shape
    assert norm_w.shape[0] == 1, "single-layer block"
    assert seqlen % S5_TB == 0 and seqlen % HG_TB == 0
    x2 = x.reshape(batch * seqlen, D_MODEL)
    u32, pbf, yb = _proj_hgrn(x2, norm_w[0][None, :], w_in[0], hgrn_lb_logits,
                              hgrn_norm_w[0][None, :], seqlen)

    wt, winp, woutp, ptre, ptim = _s5_prep(ssm_lambda_re[0], ssm_lambda_im[0], ssm_b_re[0], ssm_b_im[0],
                                           ssm_c_re[0], ssm_c_im[0], ssm_log_dt[0])
    ya = _s5(u32, pbf, wt, winp, woutp, ptre, ptim, ssm_d[0].reshape(1, SSM_WIDTH),
             ssm_w_glu[0], ssm_b_glu[0][None, :], batch, seqlen)
    out = _merge(x2, ya, yb, pbf, w_proj_a[0], w_proj_b[0], w_out[0], final_norm_w[None, :])
    return out.reshape(batch, seqlen, D_MODEL)
```

```python
import functools

import jax
import jax.numpy as jnp
from jax import lax
from jax.experimental import pallas as pl
from jax.experimental.pallas import tpu as pltpu

F32 = jnp.float32
BF16 = jnp.bfloat16

D_MODEL = 1024
SSM_WIDTH = 512
SSM_GROUP = 16
SSM_GROUPS = 32
SSM_STATE = 64
HGRN_WIDTH = 512
HEAD_DIM = 128
HEADS = 4
NORM_EPS = 1e-6
LAMBDA_RE_MAX = -1e-4

IN_WIDTH = 2 * SSM_WIDTH + 5 * HGRN_WIDTH + 2 * D_MODEL
COL_ZA, COL_H, COL_G = SSM_WIDTH, 2 * SSM_WIDTH, 2 * SSM_WIDTH + 5 * HGRN_WIDTH
PH_WIDTH = 5 * HGRN_WIDTH
PH_Q, PH_F, PH_I, PH_OG, PH_ZB = 0, 1, 2, 3, 4
PBF_WIDTH = 2 * D_MODEL + SSM_WIDTH
PBF_ZA = 4

LANES = 128
OCT = LANES // SSM_GROUP
N_OCT = SSM_GROUPS // OCT
OCT_STATE = OCT * SSM_STATE

S5_Q = 8
S5_QP = S5_Q // 2
S5_TB = 2048
S5_NCH = S5_TB // S5_Q
S5_SEG = 8
S5_NV = S5_NCH // S5_SEG
S5_PT_ROWS = S5_NV + 8
S5_SEG_ROWS = S5_Q * S5_NV
S5_SEG_PITCH = S5_SEG_ROWS + 8
HG_TB = 512
HG_CH = 64
HG_SUB = 16
TM_OUT = 1024

VMEM_LIMIT = 56 * 1024 * 1024


def _sigmoid(x):
    return 0.5 * jnp.tanh(0.5 * x) + 0.5


def _const_spec(shape):
    nd = len(shape)
    return pl.BlockSpec(shape, lambda *_: (0,) * nd, pipeline_mode=pl.Buffered(1))


def _disc(lam_re, lam_im, log_dt):
    return jnp.minimum(lam_re, LAMBDA_RE_MAX), lam_im, jnp.exp(log_dt)


def _cpow(lr, li, dt, k):
    mag = jnp.exp(k * (lr * dt))
    ang = k * (li * dt)
    return mag * jnp.cos(ang), mag * jnp.sin(ang)


def _split_bf16(x):
    hi = x.astype(BF16)
    return hi, (x - hi.astype(F32)).astype(BF16)


def _dot3(a, b_hi, b_lo):
    a_hi, a_lo = _split_bf16(a)
    d = lambda p, q: jnp.dot(p, q, preferred_element_type=F32)
    return d(a_hi, b_hi) + (d(a_hi, b_lo) + d(a_lo, b_hi))


def _cmul(a_re, a_im, b_re, b_im):
    return a_re * b_re - a_im * b_im, a_re * b_im + a_im * b_re


def _s5_prep_kernel(lam_c, b_t, c_n, lam_n, lam_r,
                    kb_ref, win_ref, wout_ref, ptre_ref, ptim_ref, pw_s, pm_s, xk_s):
    @pl.when(pl.program_id(0) == 0)
    def _():
        lr, li, dt = _disc(lam_c[0], lam_c[1], lam_c[2])
        ab_re, ab_im = _cpow(lr, li, dt, 1.0)
        den = lr * lr + li * li
        nr = ab_re - 1.0
        pw_s[0] = (nr * lr + ab_im * li) / den
        pw_s[1] = (ab_im * lr - nr * li) / den
        pw_s[2] = jnp.ones_like(ab_re)
        pw_s[3] = jnp.zeros_like(ab_re)
        pw_s[4] = ab_re
        pw_s[5] = ab_im
        m_re, m_im = _cpow(*_disc(lam_n[0], lam_n[1], lam_n[2]), 1.0)
        pm_s[0] = m_re
        pm_s[1] = m_im
        pm_s[2] = m_re
        pm_s[3] = m_im
        j = lax.broadcasted_iota(jnp.int32, (S5_PT_ROWS, OCT_STATE), 0).astype(F32) * float(S5_Q)
        for a in range(N_OCT):
            t_re, t_im = _cpow(*_disc(lam_r[0, a:a + 1, :], lam_r[1, a:a + 1, :], lam_r[2, a:a + 1, :]), j)
            ptre_ref[a] = t_re
            ptim_ref[a] = t_im

    col = lax.broadcasted_iota(jnp.int32, (SSM_GROUP, SSM_WIDTH), 1)

    def strip(g, carry):
        rows = pl.ds(pl.multiple_of(g * SSM_GROUP, SSM_GROUP), SSM_GROUP)
        grow = pl.ds(g, 1)
        coef_re, coef_im = pw_s[0, grow, :], pw_s[1, grow, :]
        p_re, p_im = pw_s[2, grow, :], pw_s[3, grow, :]
        bb_re, bb_im = _cmul(coef_re, coef_im, b_t[0, rows, :], b_t[1, rows, :])
        x_re, x_im = _cmul(bb_re, bb_im, p_re, p_im)
        xk_s[0, rows, :] = x_re[:, :SSM_STATE]
        xk_s[1, rows, :] = x_im[:, :SSM_STATE]
        a = g // OCT
        lrows = pl.ds(pl.multiple_of((g % OCT) * SSM_GROUP, SSM_GROUP), SSM_GROUP)
        m_in = col // SSM_STATE == g % OCT
        win_ref[0, a, 0, lrows, :] = jnp.concatenate(
            [jnp.where(m_in, x_re, 0.0), jnp.where(m_in, x_im, 0.0)], axis=1).astype(BF16)
        return carry

    lax.fori_loop(0, SSM_GROUPS, strip, 0, unroll=4)

    cr, ci = c_n[0], c_n[1]
    kfull = _dot3(xk_s[0], *_split_bf16(cr)) - _dot3(xk_s[1], *_split_bf16(ci))
    rowk = lax.broadcasted_iota(jnp.int32, (SSM_WIDTH, SSM_WIDTH), 0)
    colk = lax.broadcasted_iota(jnp.int32, (SSM_WIDTH, SSM_WIDTH), 1)
    kb_ref[0] = jnp.where(rowk // SSM_GROUP == colk // SSM_GROUP, kfull, 0.0).astype(BF16)

    p1_re, p1_im = pm_s[0], pm_s[1]
    w_re, w_im = _cmul(cr, ci, p1_re, p1_im)
    rowo = lax.broadcasted_iota(jnp.int32, (OCT_STATE, LANES), 0)
    colo = lax.broadcasted_iota(jnp.int32, (OCT_STATE, LANES), 1)
    m_out = rowo // SSM_STATE == colo // SSM_GROUP
    for a in range(N_OCT):
        ls = slice(LANES * a, LANES * (a + 1))
        wout_ref[0, a, :OCT_STATE, :] = jnp.where(m_out, jnp.tile(w_re[:, ls], (OCT, 1)), 0.0).astype(BF16)
        wout_ref[0, a, OCT_STATE:, :] = jnp.where(m_out, jnp.tile(-w_im[:, ls], (OCT, 1)), 0.0).astype(BF16)

    pw_s[2], pw_s[3] = _cmul(pw_s[2], pw_s[3], pw_s[4], pw_s[5])
    pm_s[0], pm_s[1] = _cmul(p1_re, p1_im, pm_s[2], pm_s[3])


def _s5_prep(lam_re, lam_im, b_re, b_im, c_re, c_im, log_dt):
    g, n, p = SSM_GROUPS, SSM_STATE, SSM_GROUP
    lam = jnp.stack([lam_re, lam_im, jnp.broadcast_to(log_dt[:, None], (g, n))])
    b = jnp.stack([b_re, b_im])
    c = jnp.stack([c_re, c_im])
    args = (jnp.tile(lam, (1, 1, OCT)),
            jnp.tile(b.transpose(0, 1, 3, 2).reshape(2, g * p, n), (1, 1, OCT)),
            c.transpose(0, 3, 1, 2).reshape(2, n, g * p),
            jnp.repeat(lam.transpose(0, 2, 1), p, axis=2),
            lam.reshape(3, N_OCT, OCT_STATE))
    mat = (S5_QP, N_OCT, 2, LANES, 2 * OCT_STATE)
    blk = (1, N_OCT, 1, LANES, 2 * OCT_STATE)
    tab = (N_OCT, S5_PT_ROWS, OCT_STATE)
    kb, win, woutp, ptre, ptim = pl.pallas_call(
        _s5_prep_kernel,
        grid=(S5_Q,),
        in_specs=[_const_spec(a.shape) for a in args],
        out_specs=[
            pl.BlockSpec((1, SSM_WIDTH, SSM_WIDTH), lambda t: (t, 0, 0)),
            pl.BlockSpec(blk, lambda t: ((S5_Q - 1 - t) // 2, 0, (S5_Q - 1 - t) % 2, 0, 0)),
            pl.BlockSpec((1, N_OCT, 2 * OCT_STATE, LANES), lambda t: (t // 2, 0, 0, t % 2)),
            pl.BlockSpec(tab, lambda t: (0, 0, 0)),
            pl.BlockSpec(tab, lambda t: (0, 0, 0)),
        ],
        out_shape=[
            jax.ShapeDtypeStruct((S5_Q, SSM_WIDTH, SSM_WIDTH), BF16),
            jax.ShapeDtypeStruct(mat, BF16),
            jax.ShapeDtypeStruct((S5_QP, N_OCT, 2 * OCT_STATE, 2 * LANES), BF16),
            jax.ShapeDtypeStruct(tab, F32),
            jax.ShapeDtypeStruct(tab, F32),
        ],
        scratch_shapes=[pltpu.VMEM((6, SSM_GROUPS, SSM_WIDTH), F32),
                        pltpu.VMEM((4, SSM_STATE, SSM_WIDTH), F32),
                        pltpu.VMEM((2, SSM_WIDTH, SSM_STATE), F32)],
        compiler_params=pltpu.CompilerParams(
            dimension_semantics=("arbitrary",), vmem_limit_bytes=VMEM_LIMIT),
        name="s5_prep",
    )(*args)

    kbo = jnp.stack([kb[:, LANES * a:LANES * (a + 1), LANES * a:LANES * (a + 1)] for a in range(N_OCT)], axis=1)
    zero = jnp.zeros_like(kbo[0])

    def pair_tile(d):
        top = jnp.concatenate([kbo[2 * d], kbo[2 * d + 1]], axis=-1)
        bot = jnp.concatenate([kbo[2 * d - 1] if d > 0 else zero, kbo[2 * d]], axis=-1)
        return jnp.concatenate([top, bot], axis=-2)

    wt = jnp.stack([pair_tile(d) for d in range(S5_QP)])
    winp = win.reshape(S5_QP, N_OCT, 2 * LANES, 2 * OCT_STATE)
    return wt, winp, woutp, ptre, ptim


def _gelu_tanh(x):
    c = 0.7978845608028654
    return x * (0.5 * (1.0 + jnp.tanh(c * (x + 0.044715 * (x * x * x)))))


def _s5_kernel(u0_ref, u1_ref, u2_ref, u3_ref, z_ref, wt_ref, winp_ref, woutp_ref, ptre_ref, ptim_ref,
               d_ref, wglu_ref, bglu_ref, o_ref, up0_s, up1_s, up2_s, up3_s, y0_s, y1_s, y2_s, y3_s,
               cre_scr, cim_scr):
    u_refs = (u0_ref, u1_ref, u2_ref, u3_ref)
    up_scrs = (up0_s, up1_s, up2_s, up3_s)
    y_scrs = (y0_s, y1_s, y2_s, y3_s)

    @pl.when(pl.program_id(1) == 0)
    def _():
        cre_scr[...] = jnp.zeros_like(cre_scr)
        cim_scr[...] = jnp.zeros_like(cim_scr)

    for a in range(N_OCT):
        for r in range(S5_SEG):
            up_scrs[a][S5_SEG_PITCH * r:S5_SEG_PITCH * r + S5_SEG_ROWS, :] = (
                u_refs[a][S5_SEG_ROWS * r:S5_SEG_ROWS * (r + 1), :])

    def tok(a, s):
        return jnp.concatenate(
            [up_scrs[a][pl.ds(s + S5_Q * v, S5_SEG, stride=S5_SEG_PITCH), :] for v in range(S5_NV)], axis=0)

    xp = [[jnp.concatenate([tok(a, 2 * sp), tok(a, 2 * sp + 1)], axis=1).astype(BF16) for a in range(N_OCT)]
          for sp in range(S5_QP)]

    def cmul_add(b_re, b_im, m_re, m_im, x_re, x_im):
        return b_re + m_re * x_re - m_im * x_im, b_im + m_re * x_im + m_im * x_re

    hs = []
    for a in range(N_OCT):
        acc = None
        for sp in range(S5_QP):
            part = jnp.dot(xp[sp][a], winp_ref[sp, a], preferred_element_type=F32)
            acc = part if acc is None else acc + part
        blk = lambda v: (acc[S5_SEG * v:S5_SEG * (v + 1), :OCT_STATE], acc[S5_SEG * v:S5_SEG * (v + 1), OCT_STATE:])
        m_re, m_im = ptre_ref[a, 1:2, :], ptim_ref[a, 1:2, :]
        loc = [blk(0)]
        for v in range(1, S5_NV):
            loc.append(cmul_add(*blk(v), m_re, m_im, *loc[-1]))
        l_re, l_im = ptre_ref[a, S5_NV:S5_NV + 1, :], ptim_ref[a, S5_NV:S5_NV + 1, :]
        c_re, c_im = cre_scr[a:a + 1, :], cim_scr[a:a + 1, :]
        carry = []
        for r in range(S5_SEG):
            carry.append((c_re, c_im))
            c_re, c_im = cmul_add(loc[-1][0][r:r + 1, :], loc[-1][1][r:r + 1, :], l_re, l_im, c_re, c_im)
        cre_scr[a:a + 1, :] = c_re
        cim_scr[a:a + 1, :] = c_im
        cs_re = jnp.concatenate([c[0] for c in carry], axis=0)
        cs_im = jnp.concatenate([c[1] for c in carry], axis=0)
        ent = [(cs_re, cs_im)]
        for v in range(S5_NV - 1):
            ent.append(cmul_add(*loc[v], ptre_ref[a, v + 1:v + 2, :], ptim_ref[a, v + 1:v + 2, :], cs_re, cs_im))
        hs.append(jnp.concatenate([jnp.concatenate([e[0] for e in ent], axis=0),
                                   jnp.concatenate([e[1] for e in ent], axis=0)], axis=1).astype(BF16))

    for a in range(N_OCT):
        for tp in range(S5_QP):
            acc = jnp.dot(hs[a], woutp_ref[tp, a], preferred_element_type=F32)
            for sp in range(tp + 1):
                acc = acc + jnp.dot(xp[sp][a], wt_ref[tp - sp, a], preferred_element_type=F32)
            for v in range(S5_NV):
                rows = slice(S5_SEG * v, S5_SEG * (v + 1))
                y_scrs[a][pl.ds(2 * tp + S5_Q * v, S5_SEG, stride=S5_SEG_PITCH), :] = acc[rows, :LANES]
                y_scrs[a][pl.ds(2 * tp + 1 + S5_Q * v, S5_SEG, stride=S5_SEG_PITCH), :] = acc[rows, LANES:]
        d_a = d_ref[:, LANES * a:LANES * (a + 1)]
        for r in range(S5_SEG):
            prow = slice(S5_SEG_PITCH * r, S5_SEG_PITCH * r + S5_SEG_ROWS)
            urow = slice(S5_SEG_ROWS * r, S5_SEG_ROWS * (r + 1))
            y_scrs[a][prow, :] = _gelu_tanh(y_scrs[a][prow, :] + d_a * u_refs[a][urow, :])

    unpad = lambda ref: jnp.concatenate(
        [ref[S5_SEG_PITCH * r:S5_SEG_PITCH * r + S5_SEG_ROWS, :] for r in range(S5_SEG)], axis=0)
    y = jnp.concatenate([unpad(r) for r in y_scrs], axis=1)
    gate = jnp.dot(y.astype(BF16), wglu_ref[...].astype(BF16), preferred_element_type=F32) + bglu_ref[...]
    y = y * _sigmoid(gate)
    z = z_ref[...].astype(F32)
    o_ref[...] = (y * (z * _sigmoid(z))).astype(BF16)


def _s5(u32, pbf, wt, winp, woutp, ptre, ptim, d_row, w_glu, b_glu, batch, seqlen):
    nb = seqlen // S5_TB
    t = batch * seqlen
    u_tile = lambda a: pl.BlockSpec((S5_TB, LANES), lambda b, i, a=a: (b * nb + i, a))
    return pl.pallas_call(
        _s5_kernel,
        grid=(batch, nb),
        in_specs=[u_tile(a) for a in range(N_OCT)] + [
            pl.BlockSpec((S5_TB, SSM_WIDTH), lambda b, i: (b * nb + i, PBF_ZA)),
            _const_spec(wt.shape), _const_spec(winp.shape), _const_spec(woutp.shape),
            _const_spec(ptre.shape), _const_spec(ptim.shape),
            _const_spec((1, SSM_WIDTH)), _const_spec((SSM_WIDTH, SSM_WIDTH)), _const_spec((1, SSM_WIDTH)),
        ],
        out_specs=pl.BlockSpec((S5_TB, SSM_WIDTH), lambda b, i: (b * nb + i, 0)),
        out_shape=jax.ShapeDtypeStruct((t, SSM_WIDTH), BF16),
        scratch_shapes=[pltpu.VMEM((S5_SEG * S5_SEG_PITCH, LANES), F32)] * (2 * N_OCT) + [
            pltpu.VMEM((8, OCT_STATE), F32),
            pltpu.VMEM((8, OCT_STATE), F32),
        ],
        compiler_params=pltpu.CompilerParams(
            dimension_semantics=("arbitrary", "arbitrary"), vmem_limit_bytes=VMEM_LIMIT),
        name="s5",
    )(u32, u32, u32, u32, pbf, wt, winp, woutp, ptre, ptim, d_row, w_glu, b_glu)


def _proj_hgrn_kernel(x_ref, nw_ref, w_ref, lbl_ref, hnw_ref, ou_ref, or_ref, o_ref,
                      ph_s, q0_s, qc_s, kt_s, k0_s, k1_s, k2_s, ke_s, v_s, dec_s, oacc_s, st_s, upd_s, sc_s, am_s,
                      *, blocks_per_seq):
    @pl.when(pl.program_id(0) % blocks_per_seq == 0)
    def _():
        st_s[...] = jnp.zeros_like(st_s)

    x = x_ref[...]
    ms = jnp.mean(x * x, axis=-1, keepdims=True)
    xn = (x * lax.rsqrt(ms + NORM_EPS) * nw_ref[...]).astype(BF16)
    proj = lambda lo, hi: jnp.dot(xn, w_ref[:, lo:hi].astype(BF16), preferred_element_type=F32)
    ph_s[...] = proj(COL_H, COL_G)
    ou_ref[...] = proj(0, COL_ZA)
    or_ref[:, :2 * D_MODEL] = proj(COL_G, IN_WIDTH).astype(BF16)
    or_ref[:, 2 * D_MODEL:] = proj(COL_ZA, COL_H).astype(BF16)
    col = lambda k: ph_s[:, HGRN_WIDTH * k:HGRN_WIDTH * (k + 1)]

    logits = lbl_ref[...]
    e = jnp.exp(logits - jnp.max(logits, axis=0, keepdims=True))
    lb = (e / jnp.sum(e, axis=0, keepdims=True))[0:1, :]

    q = col(PH_Q)
    qf = q * _sigmoid(q)
    forget = lb + (1.0 - lb) * _sigmoid(col(PH_F))
    lf = jnp.log(forget)
    key = 1.0 - forget

    row = lax.broadcasted_iota(jnp.int32, (HG_TB, HGRN_WIDTH), 0)
    r_sub = row % HG_SUB
    r_ch = row % HG_CH

    def down(x, d):
        return pltpu.roll(x, d, 0)

    def up(x, d):
        return pltpu.roll(x, HG_TB - d, 0)

    a = lf
    d = 1
    while d < HG_SUB:
        a = a + jnp.where(r_sub >= d, down(a, d), 0.0)
        d *= 2
    a3 = a.reshape(HG_TB // HG_SUB, HG_SUB, HGRN_WIDTH)
    tsub = jnp.broadcast_to(a3[:, HG_SUB - 1:HG_SUB, :], a3.shape).reshape(HG_TB, HGRN_WIDTH)
    n_sub = HG_CH // HG_SUB
    prev = [jnp.where(r_ch >= HG_SUB * k, down(tsub, HG_SUB * k), 0.0) for k in range(1, n_sub)]
    nxt = [jnp.where(r_ch < HG_CH - HG_SUB * k, up(tsub, HG_SUB * k), 0.0) for k in range(1, n_sub)]
    eprev = prev[0] + prev[1] + prev[2]
    enext = nxt[0] + nxt[1] + nxt[2]
    suf = tsub - a

    q0 = qf * jnp.exp(a)
    q0_s[...] = q0.astype(BF16)
    qc_s[...] = (q0 * jnp.exp(eprev)).astype(BF16)
    kt_s[...] = (key * jnp.exp(-a)).astype(BF16)
    k0 = key * jnp.exp(suf)
    k0_s[...] = k0.astype(BF16)
    k1 = k0 * jnp.exp(nxt[0])
    k1_s[...] = k1.astype(BF16)
    k2 = k1 * jnp.exp(nxt[1])
    k2_s[...] = k2.astype(BF16)
    ke_s[...] = (k2 * jnp.exp(nxt[2])).astype(BF16)
    dec_s[...] = jnp.exp(eprev + tsub + enext)
    v_s[...] = col(PH_I).astype(BF16)

    tq = lax.broadcasted_iota(jnp.int32, (HG_CH, n_sub * HG_CH), 0)
    cc = lax.broadcasted_iota(jnp.int32, (HG_CH, n_sub * HG_CH), 1)
    cls = cc // HG_CH
    ts = cc % HG_CH
    bi = tq // HG_SUB
    bj = ts // HG_SUB
    mask4 = ((cls == 0) & (bi == bj) & (ts <= tq)) | ((cls > 0) & ((bi - bj) == cls))

    n_ch = HG_TB // HG_CH
    half = n_sub * HG_CH // 2
    units = [(c, h, slice(c * HG_CH, (c + 1) * HG_CH), slice(HEAD_DIM * h, HEAD_DIM * (h + 1)))
             for c in range(n_ch) for h in range(HEADS)]
    for c, h, rows, ls in units:
        kcat = jnp.concatenate([kt_s[rows, ls], k0_s[rows, ls], k1_s[rows, ls], k2_s[rows, ls]], axis=0)
        sc_s[c * HEADS + h] = lax.dot_general(q0_s[rows, ls], kcat, (((1,), (1,)), ((), ())),
                                              preferred_element_type=F32)
    for c, h, rows, ls in units:
        upd_s[c * HEADS + h] = lax.dot_general(v_s[rows, ls], ke_s[rows, ls], (((0,), (0,)), ((), ())),
                                               preferred_element_type=F32)
    for c, h, rows, ls in units:
        sc = jnp.where(mask4, sc_s[c * HEADS + h], 0.0)
        am_s[c * HEADS + h] = (sc[:, :half] + sc[:, half:]).astype(BF16)
    for c, h, rows, ls in units:
        vv = v_s[rows, ls]
        oacc_s[rows, ls] = jnp.dot(am_s[c * HEADS + h], jnp.concatenate([vv, vv], axis=0),
                                   preferred_element_type=F32)
    for h in range(HEADS):
        ls = slice(HEAD_DIM * h, HEAD_DIM * (h + 1))
        st = st_s[h]
        for c in range(n_ch):
            rows = slice(c * HG_CH, (c + 1) * HG_CH)
            oacc_s[rows, ls] += lax.dot_general(qc_s[rows, ls], st.astype(BF16), (((1,), (1,)), ((), ())),
                                                preferred_element_type=F32)
            st = st * dec_s[c * HG_CH:c * HG_CH + 1, ls] + upd_s[c * HEADS + h]
        st_s[h] = st

    o = oacc_s[...] * _sigmoid(col(PH_OG))
    parts = []
    for h in range(HEADS):
        oh = o[:, HEAD_DIM * h:HEAD_DIM * (h + 1)]
        ms = jnp.mean(oh * oh, axis=-1, keepdims=True)
        parts.append(oh * lax.rsqrt(ms + NORM_EPS))
    o = jnp.concatenate(parts, axis=1) * hnw_ref[...]
    z = col(PH_ZB)
    o_ref[...] = (o * (z * _sigmoid(z))).astype(BF16)


def _proj_hgrn(x2, norm_w, w, lb_logits, hgrn_norm_w, seqlen):
    t = x2.shape[0]
    wide = (HG_TB, HGRN_WIDTH)
    units = HG_TB // HG_CH * HEADS
    n_cls = HG_CH // HG_SUB
    row_blk = lambda w: pl.BlockSpec((HG_TB, w), lambda i: (i, 0))
    return pl.pallas_call(
        functools.partial(_proj_hgrn_kernel, blocks_per_seq=seqlen // HG_TB),
        grid=(t // HG_TB,),
        in_specs=[
            row_blk(D_MODEL), _const_spec((1, D_MODEL)),
            _const_spec((D_MODEL, IN_WIDTH)),
            _const_spec(lb_logits.shape), _const_spec((1, HGRN_WIDTH)),
        ],
        out_specs=[row_blk(SSM_WIDTH), row_blk(PBF_WIDTH), row_blk(HGRN_WIDTH)],
        out_shape=[
            jax.ShapeDtypeStruct((t, SSM_WIDTH), F32),
            jax.ShapeDtypeStruct((t, PBF_WIDTH), BF16),
            jax.ShapeDtypeStruct((t, HGRN_WIDTH), BF16),
        ],
        scratch_shapes=[pltpu.VMEM((HG_TB, PH_WIDTH), F32)] + [pltpu.VMEM(wide, BF16)] * 8 + [
            pltpu.VMEM(wide, F32), pltpu.VMEM(wide, F32),
            pltpu.VMEM((HEADS, HEAD_DIM, HEAD_DIM), F32),
            pltpu.VMEM((units, HEAD_DIM, HEAD_DIM), F32),
            pltpu.VMEM((units, HG_CH, n_cls * HG_CH), F32),
            pltpu.VMEM((units, HG_CH, n_cls * HG_CH // 2), BF16)],
        compiler_params=pltpu.CompilerParams(
            dimension_semantics=("arbitrary",), vmem_limit_bytes=VMEM_LIMIT),
        name="proj_hgrn",
    )(x2, norm_w, w, lb_logits, hgrn_norm_w)


def _merge_kernel(x_ref, ya_ref, yb_ref, ga_ref, gb_ref, wpa_ref, wpb_ref, wo_ref, fnw_ref, o_ref):
    pa = jnp.dot(ya_ref[...], wpa_ref[...].astype(BF16), preferred_element_type=F32)
    pb = jnp.dot(yb_ref[...], wpb_ref[...].astype(BF16), preferred_element_type=F32)
    merged = _sigmoid(ga_ref[...].astype(F32)) * pa + _sigmoid(gb_ref[...].astype(F32)) * pb
    h = x_ref[...] + jnp.dot(merged.astype(BF16), wo_ref[...].astype(BF16), preferred_element_type=F32)
    ms = jnp.mean(h * h, axis=-1, keepdims=True)
    o_ref[...] = h * lax.rsqrt(ms + NORM_EPS) * fnw_ref[...]


def _merge(x2, ya, yb, pbf, w_pa, w_pb, w_out, fnw):
    t = x2.shape[0]
    return pl.pallas_call(
        _merge_kernel,
        grid=(t // TM_OUT,),
        in_specs=[
            pl.BlockSpec((TM_OUT, D_MODEL), lambda i: (i, 0)),
            pl.BlockSpec((TM_OUT, SSM_WIDTH), lambda i: (i, 0)),
            pl.BlockSpec((TM_OUT, HGRN_WIDTH), lambda i: (i, 0)),
            pl.BlockSpec((TM_OUT, D_MODEL), lambda i: (i, 0)),
            pl.BlockSpec((TM_OUT, D_MODEL), lambda i: (i, 1)),
            _const_spec((SSM_WIDTH, D_MODEL)), _const_spec((HGRN_WIDTH, D_MODEL)),
            _const_spec((D_MODEL, D_MODEL)), _const_spec((1, D_MODEL)),
        ],
        out_specs=pl.BlockSpec((TM_OUT, D_MODEL), lambda i: (i, 0)),
        out_shape=jax.ShapeDtypeStruct((t, D_MODEL), F32),
        compiler_params=pltpu.CompilerParams(
            dimension_semantics=("arbitrary",), vmem_limit_bytes=VMEM_LIMIT),
        name="merge",
    )(x2, ya, yb, pbf, pbf, w_pa, w_pb, w_out, fnw)


def kernel(x, norm_w, w_in, ssm_lambda_re, ssm_lambda_im, ssm_b_re, ssm_b_im, ssm_c_re, ssm_c_im, ssm_d,
           ssm_log_dt, ssm_w_glu, ssm_b_glu, hgrn_lb_logits, hgrn_norm_w, w_proj_a, w_proj_b, w_out,
           final_norm_w):
    batch, seqlen, _ = x.shape
    assert norm_w.shape[0] == 1, "single-layer block"
    assert seqlen % S5_TB == 0 and seqlen % HG_TB == 0
    x2 = x.reshape(batch * seqlen, D_MODEL)
    u32, pbf, yb = _proj_hgrn(x2, norm_w[0][None, :], w_in[0], hgrn_lb_logits,
                              hgrn_norm_w[0][None, :], seqlen)

    wt, winp, woutp, ptre, ptim = _s5_prep(ssm_lambda_re[0], ssm_lambda_im[0], ssm_b_re[0], ssm_b_im[0],
                                           ssm_c_re[0], ssm_c_im[0], ssm_log_dt[0])
    ya = _s5(u32, pbf, wt, winp, woutp, ptre, ptim, ssm_d[0].reshape(1, SSM_WIDTH),
             ssm_w_glu[0], ssm_b_glu[0][None, :], batch, seqlen)
    out = _merge(x2, ya, yb, pbf, w_proj_a[0], w_proj_b[0], w_out[0], final_norm_w[None, :])
    return out.reshape(batch, seqlen, D_MODEL)
```

```python
import functools

import jax
import jax.numpy as jnp
from jax import lax
from jax.experimental import pallas as pl
from jax.experimental.pallas import tpu as pltpu

F32 = jnp.float32
BF16 = jnp.bfloat16

D_MODEL = 1024
SSM_WIDTH = 512
SSM_GROUP = 16
SSM_GROUPS = 32
SSM_STATE = 64
HGRN_WIDTH = 512
HEAD_DIM = 128
HEADS = 4
NORM_EPS = 1e-6
LAMBDA_RE_MAX = -1e-4

IN_WIDTH = 2 * SSM_WIDTH + 5 * HGRN_WIDTH + 2 * D_MODEL
COL_ZA, COL_H, COL_G = SSM_WIDTH, 2 * SSM_WIDTH, 2 * SSM_WIDTH + 5 * HGRN_WIDTH
PH_WIDTH = 5 * HGRN_WIDTH
PH_Q, PH_F, PH_I, PH_OG, PH_ZB = 0, 1, 2, 3, 4
PBF_WIDTH = 2 * D_MODEL + SSM_WIDTH
PBF_ZA = 4

LANES = 128
SUBLANES = 8
OCT = LANES // SSM_GROUP
N_OCT = SSM_GROUPS // OCT
OCT_STATE = OCT * SSM_STATE

S5_Q = 8
S5_QP = S5_Q // 2
S5_TB = 2048
S5_NCH = S5_TB // S5_Q
S5_SEG = SUBLANES
S5_NV = S5_NCH // S5_SEG
S5_PT_ROWS = S5_NV + SUBLANES
S5_SEG_ROWS = S5_Q * S5_NV
S5_SEG_PITCH = S5_SEG_ROWS + SUBLANES
HG_TB = 512
HG_CH = 64
HG_SUB = 16
TM_OUT = 1024

V7X_VMEM_BYTES = 64 * 1024 * 1024
VMEM_LIMIT = V7X_VMEM_BYTES * 7 // 8


def _sigmoid(x):
    return 0.5 * jnp.tanh(0.5 * x) + 0.5


def _const_spec(shape):
    nd = len(shape)
    return pl.BlockSpec(shape, lambda *_: (0,) * nd, pipeline_mode=pl.Buffered(1))


def _disc(lam_re, lam_im, log_dt):
    return jnp.minimum(lam_re, LAMBDA_RE_MAX), lam_im, jnp.exp(log_dt)


def _cpow(lr, li, dt, k):
    mag = jnp.exp(k * (lr * dt))
    ang = k * (li * dt)
    return mag * jnp.cos(ang), mag * jnp.sin(ang)


def _split_bf16(x):
    hi = x.astype(BF16)
    return hi, (x - hi.astype(F32)).astype(BF16)


def _dot3(a, b_hi, b_lo):
    a_hi, a_lo = _split_bf16(a)
    d = lambda p, q: jnp.dot(p, q, preferred_element_type=F32)
    return d(a_hi, b_hi) + (d(a_hi, b_lo) + d(a_lo, b_hi))


def _cmul(a_re, a_im, b_re, b_im):
    return a_re * b_re - a_im * b_im, a_re * b_im + a_im * b_re


def _s5_prep_kernel(lam_c, b_t, c_n, lam_n, lam_r,
                    kb_ref, win_ref, wout_ref, ptre_ref, ptim_ref, pw_s, pm_s, xk_s):
    @pl.when(pl.program_id(0) == 0)
    def _():
        lr, li, dt = _disc(lam_c[0], lam_c[1], lam_c[2])
        ab_re, ab_im = _cpow(lr, li, dt, 1.0)
        den = lr * lr + li * li
        nr = ab_re - 1.0
        pw_s[0] = (nr * lr + ab_im * li) / den
        pw_s[1] = (ab_im * lr - nr * li) / den
        pw_s[2] = jnp.ones_like(ab_re)
        pw_s[3] = jnp.zeros_like(ab_re)
        pw_s[4] = ab_re
        pw_s[5] = ab_im
        m_re, m_im = _cpow(*_disc(lam_n[0], lam_n[1], lam_n[2]), 1.0)
        pm_s[0] = m_re
        pm_s[1] = m_im
        pm_s[2] = m_re
        pm_s[3] = m_im
        j = lax.broadcasted_iota(jnp.int32, (S5_PT_ROWS, OCT_STATE), 0).astype(F32) * float(S5_Q)
        for a in range(N_OCT):
            t_re, t_im = _cpow(*_disc(lam_r[0, a:a + 1, :], lam_r[1, a:a + 1, :], lam_r[2, a:a + 1, :]), j)
            ptre_ref[a] = t_re
            ptim_ref[a] = t_im

    col = lax.broadcasted_iota(jnp.int32, (SSM_GROUP, SSM_WIDTH), 1)

    def strip(g, carry):
        rows = pl.ds(pl.multiple_of(g * SSM_GROUP, SSM_GROUP), SSM_GROUP)
        grow = pl.ds(g, 1)
        coef_re, coef_im = pw_s[0, grow, :], pw_s[1, grow, :]
        p_re, p_im = pw_s[2, grow, :], pw_s[3, grow, :]
        bb_re, bb_im = _cmul(coef_re, coef_im, b_t[0, rows, :], b_t[1, rows, :])
        x_re, x_im = _cmul(bb_re, bb_im, p_re, p_im)
        xk_s[0, rows, :] = x_re[:, :SSM_STATE]
        xk_s[1, rows, :] = x_im[:, :SSM_STATE]
        a = g // OCT
        lrows = pl.ds(pl.multiple_of((g % OCT) * SSM_GROUP, SSM_GROUP), SSM_GROUP)
        m_in = col // SSM_STATE == g % OCT
        win_ref[0, a, 0, lrows, :] = jnp.concatenate(
            [jnp.where(m_in, x_re, 0.0), jnp.where(m_in, x_im, 0.0)], axis=1).astype(BF16)
        return carry

    lax.fori_loop(0, SSM_GROUPS, strip, 0, unroll=4)

    cr, ci = c_n[0], c_n[1]
    kfull = _dot3(xk_s[0], *_split_bf16(cr)) - _dot3(xk_s[1], *_split_bf16(ci))
    rowk = lax.broadcasted_iota(jnp.int32, (SSM_WIDTH, SSM_WIDTH), 0)
    colk = lax.broadcasted_iota(jnp.int32, (SSM_WIDTH, SSM_WIDTH), 1)
    kb_ref[0] = jnp.where(rowk // SSM_GROUP == colk // SSM_GROUP, kfull, 0.0).astype(BF16)

    p1_re, p1_im = pm_s[0], pm_s[1]
    w_re, w_im = _cmul(cr, ci, p1_re, p1_im)
    rowo = lax.broadcasted_iota(jnp.int32, (OCT_STATE, LANES), 0)
    colo = lax.broadcasted_iota(jnp.int32, (OCT_STATE, LANES), 1)
    m_out = rowo // SSM_STATE == colo // SSM_GROUP
    for a in range(N_OCT):
        ls = slice(LANES * a, LANES * (a + 1))
        wout_ref[0, a, :OCT_STATE, :] = jnp.where(m_out, jnp.tile(w_re[:, ls], (OCT, 1)), 0.0).astype(BF16)
        wout_ref[0, a, OCT_STATE:, :] = jnp.where(m_out, jnp.tile(-w_im[:, ls], (OCT, 1)), 0.0).astype(BF16)

    pw_s[2], pw_s[3] = _cmul(pw_s[2], pw_s[3], pw_s[4], pw_s[5])
    pm_s[0], pm_s[1] = _cmul(p1_re, p1_im, pm_s[2], pm_s[3])


def _s5_prep(lam_re, lam_im, b_re, b_im, c_re, c_im, log_dt):
    g, n, p = SSM_GROUPS, SSM_STATE, SSM_GROUP
    lam = jnp.stack([lam_re, lam_im, jnp.broadcast_to(log_dt[:, None], (g, n))])
    b = jnp.stack([b_re, b_im])
    c = jnp.stack([c_re, c_im])
    args = (jnp.tile(lam, (1, 1, OCT)),
            jnp.tile(b.transpose(0, 1, 3, 2).reshape(2, g * p, n), (1, 1, OCT)),
            c.transpose(0, 3, 1, 2).reshape(2, n, g * p),
            jnp.repeat(lam.transpose(0, 2, 1), p, axis=2),
            lam.reshape(3, N_OCT, OCT_STATE))
    mat = (S5_QP, N_OCT, 2, LANES, 2 * OCT_STATE)
    blk = (1, N_OCT, 1, LANES, 2 * OCT_STATE)
    tab = (N_OCT, S5_PT_ROWS, OCT_STATE)
    kb, win, woutp, ptre, ptim = pl.pallas_call(
        _s5_prep_kernel,
        grid=(S5_Q,),
        in_specs=[_const_spec(a.shape) for a in args],
        out_specs=[
            pl.BlockSpec((1, SSM_WIDTH, SSM_WIDTH), lambda t: (t, 0, 0)),
            pl.BlockSpec(blk, lambda t: ((S5_Q - 1 - t) // 2, 0, (S5_Q - 1 - t) % 2, 0, 0)),
            pl.BlockSpec((1, N_OCT, 2 * OCT_STATE, LANES), lambda t: (t // 2, 0, 0, t % 2)),
            pl.BlockSpec(tab, lambda t: (0, 0, 0)),
            pl.BlockSpec(tab, lambda t: (0, 0, 0)),
        ],
        out_shape=[
            jax.ShapeDtypeStruct((S5_Q, SSM_WIDTH, SSM_WIDTH), BF16),
            jax.ShapeDtypeStruct(mat, BF16),
            jax.ShapeDtypeStruct((S5_QP, N_OCT, 2 * OCT_STATE, 2 * LANES), BF16),
            jax.ShapeDtypeStruct(tab, F32),
            jax.ShapeDtypeStruct(tab, F32),
        ],
        scratch_shapes=[pltpu.VMEM((6, SSM_GROUPS, SSM_WIDTH), F32),
                        pltpu.VMEM((4, SSM_STATE, SSM_WIDTH), F32),
                        pltpu.VMEM((2, SSM_WIDTH, SSM_STATE), F32)],
        compiler_params=pltpu.CompilerParams(
            dimension_semantics=("arbitrary",), vmem_limit_bytes=VMEM_LIMIT),
        name="s5_prep",
    )(*args)

    kbo = jnp.stack([kb[:, LANES * a:LANES * (a + 1), LANES * a:LANES * (a + 1)] for a in range(N_OCT)], axis=1)
    zero = jnp.zeros_like(kbo[0])

    def pair_tile(d):
        top = jnp.concatenate([kbo[2 * d], kbo[2 * d + 1]], axis=-1)
        bot = jnp.concatenate([kbo[2 * d - 1] if d > 0 else zero, kbo[2 * d]], axis=-1)
        return jnp.concatenate([top, bot], axis=-2)

    wt = jnp.stack([pair_tile(d) for d in range(S5_QP)])
    winp = win.reshape(S5_QP, N_OCT, 2 * LANES, 2 * OCT_STATE)
    return wt, winp, woutp, ptre, ptim


def _gelu_tanh(x):
    c = 0.7978845608028654
    return x * (0.5 * (1.0 + jnp.tanh(c * (x + 0.044715 * (x * x * x)))))


def _s5_kernel(u0_ref, u1_ref, u2_ref, u3_ref, z_ref, wt_ref, winp_ref, woutp_ref, ptre_ref, ptim_ref,
               d_ref, wglu_ref, bglu_ref, o_ref, up0_s, up1_s, up2_s, up3_s, y0_s, y1_s, y2_s, y3_s,
               cre_scr, cim_scr):
    u_refs = (u0_ref, u1_ref, u2_ref, u3_ref)
    up_scrs = (up0_s, up1_s, up2_s, up3_s)
    y_scrs = (y0_s, y1_s, y2_s, y3_s)

    @pl.when(pl.program_id(1) == 0)
    def _():
        cre_scr[...] = jnp.zeros_like(cre_scr)
        cim_scr[...] = jnp.zeros_like(cim_scr)

    for a in range(N_OCT):
        for r in range(S5_SEG):
            up_scrs[a][S5_SEG_PITCH * r:S5_SEG_PITCH * r + S5_SEG_ROWS, :] = (
                u_refs[a][S5_SEG_ROWS * r:S5_SEG_ROWS * (r + 1), :])

    def tok(a, s):
        return jnp.concatenate(
            [up_scrs[a][pl.ds(s + S5_Q * v, S5_SEG, stride=S5_SEG_PITCH), :] for v in range(S5_NV)], axis=0)

    xp = [[jnp.concatenate([tok(a, 2 * sp), tok(a, 2 * sp + 1)], axis=1).astype(BF16) for a in range(N_OCT)]
          for sp in range(S5_QP)]

    def cmul_add(b_re, b_im, m_re, m_im, x_re, x_im):
        return b_re + m_re * x_re - m_im * x_im, b_im + m_re * x_im + m_im * x_re

    hs = []
    for a in range(N_OCT):
        acc = None
        for sp in range(S5_QP):
            part = jnp.dot(xp[sp][a], winp_ref[sp, a], preferred_element_type=F32)
            acc = part if acc is None else acc + part
        blk = lambda v: (acc[S5_SEG * v:S5_SEG * (v + 1), :OCT_STATE], acc[S5_SEG * v:S5_SEG * (v + 1), OCT_STATE:])
        m_re, m_im = ptre_ref[a, 1:2, :], ptim_ref[a, 1:2, :]
        loc = [blk(0)]
        for v in range(1, S5_NV):
            loc.append(cmul_add(*blk(v), m_re, m_im, *loc[-1]))
        l_re, l_im = ptre_ref[a, S5_NV:S5_NV + 1, :], ptim_ref[a, S5_NV:S5_NV + 1, :]
        c_re, c_im = cre_scr[a:a + 1, :], cim_scr[a:a + 1, :]
        carry = []
        for r in range(S5_SEG):
            carry.append((c_re, c_im))
            c_re, c_im = cmul_add(loc[-1][0][r:r + 1, :], loc[-1][1][r:r + 1, :], l_re, l_im, c_re, c_im)
        cre_scr[a:a + 1, :] = c_re
        cim_scr[a:a + 1, :] = c_im
        cs_re = jnp.concatenate([c[0] for c in carry], axis=0)
        cs_im = jnp.concatenate([c[1] for c in carry], axis=0)
        ent = [(cs_re, cs_im)]
        for v in range(S5_NV - 1):
            ent.append(cmul_add(*loc[v], ptre_ref[a, v + 1:v + 2, :], ptim_ref[a, v + 1:v + 2, :], cs_re, cs_im))
        hs.append(jnp.concatenate([jnp.concatenate([e[0] for e in ent], axis=0),
                                   jnp.concatenate([e[1] for e in ent], axis=0)], axis=1).astype(BF16))

    for a in range(N_OCT):
        for tp in range(S5_QP):
            acc = jnp.dot(hs[a], woutp_ref[tp, a], preferred_element_type=F32)
            for sp in range(tp + 1):
                acc = acc + jnp.dot(xp[sp][a], wt_ref[tp - sp, a], preferred_element_type=F32)
            for v in range(S5_NV):
                rows = slice(S5_SEG * v, S5_SEG * (v + 1))
                y_scrs[a][pl.ds(2 * tp + S5_Q * v, S5_SEG, stride=S5_SEG_PITCH), :] = acc[rows, :LANES]
                y_scrs[a][pl.ds(2 * tp + 1 + S5_Q * v, S5_SEG, stride=S5_SEG_PITCH), :] = acc[rows, LANES:]
        d_a = d_ref[:, LANES * a:LANES * (a + 1)]
        for r in range(S5_SEG):
            prow = slice(S5_SEG_PITCH * r, S5_SEG_PITCH * r + S5_SEG_ROWS)
            urow = slice(S5_SEG_ROWS * r, S5_SEG_ROWS * (r + 1))
            y_scrs[a][prow, :] = _gelu_tanh(y_scrs[a][prow, :] + d_a * u_refs[a][urow, :])

    unpad = lambda ref: jnp.concatenate(
        [ref[S5_SEG_PITCH * r:S5_SEG_PITCH * r + S5_SEG_ROWS, :] for r in range(S5_SEG)], axis=0)
    y = jnp.concatenate([unpad(r) for r in y_scrs], axis=1)
    gate = jnp.dot(y.astype(BF16), wglu_ref[...].astype(BF16), preferred_element_type=F32) + bglu_ref[...]
    y = y * _sigmoid(gate)
    z = z_ref[...].astype(F32)
    o_ref[...] = (y * (z * _sigmoid(z))).astype(BF16)


def _s5(u32, pbf, wt, winp, woutp, ptre, ptim, d_row, w_glu, b_glu, batch, seqlen):
    nb = seqlen // S5_TB
    t = batch * seqlen
    u_tile = lambda a: pl.BlockSpec((S5_TB, LANES), lambda b, i, a=a: (b * nb + i, a))
    return pl.pallas_call(
        _s5_kernel,
        grid=(batch, nb),
        in_specs=[u_tile(a) for a in range(N_OCT)] + [
            pl.BlockSpec((S5_TB, SSM_WIDTH), lambda b, i: (b * nb + i, PBF_ZA)),
            _const_spec(wt.shape), _const_spec(winp.shape), _const_spec(woutp.shape),
            _const_spec(ptre.shape), _const_spec(ptim.shape),
            _const_spec((1, SSM_WIDTH)), _const_spec((SSM_WIDTH, SSM_WIDTH)), _const_spec((1, SSM_WIDTH)),
        ],
        out_specs=pl.BlockSpec((S5_TB, SSM_WIDTH), lambda b, i: (b * nb + i, 0)),
        out_shape=jax.ShapeDtypeStruct((t, SSM_WIDTH), BF16),
        scratch_shapes=[pltpu.VMEM((S5_SEG * S5_SEG_PITCH, LANES), F32)] * (2 * N_OCT) + [
            pltpu.VMEM((SUBLANES, OCT_STATE), F32),
            pltpu.VMEM((SUBLANES, OCT_STATE), F32),
        ],
        compiler_params=pltpu.CompilerParams(
            dimension_semantics=("arbitrary", "arbitrary"), vmem_limit_bytes=VMEM_LIMIT),
        name="s5",
    )(u32, u32, u32, u32, pbf, wt, winp, woutp, ptre, ptim, d_row, w_glu, b_glu)


def _proj_hgrn_kernel(x_ref, nw_ref, w_ref, lbl_ref, hnw_ref, ou_ref, or_ref, o_ref,
                      ph_s, q0_s, qc_s, kt_s, k0_s, k1_s, k2_s, ke_s, v_s, dec_s, oacc_s, st_s, upd_s, sc_s, am_s,
                      *, blocks_per_seq):
    @pl.when(pl.program_id(0) % blocks_per_seq == 0)
    def _():
        st_s[...] = jnp.zeros_like(st_s)

    x = x_ref[...]
    ms = jnp.mean(x * x, axis=-1, keepdims=True)
    xn = (x * lax.rsqrt(ms + NORM_EPS) * nw_ref[...]).astype(BF16)
    proj = lambda lo, hi: jnp.dot(xn, w_ref[:, lo:hi].astype(BF16), preferred_element_type=F32)
    ph_s[...] = proj(COL_H, COL_G)
    ou_ref[...] = proj(0, COL_ZA)
    or_ref[:, :2 * D_MODEL] = proj(COL_G, IN_WIDTH).astype(BF16)
    or_ref[:, 2 * D_MODEL:] = proj(COL_ZA, COL_H).astype(BF16)
    col = lambda k: ph_s[:, HGRN_WIDTH * k:HGRN_WIDTH * (k + 1)]

    logits = lbl_ref[...]
    e = jnp.exp(logits - jnp.max(logits, axis=0, keepdims=True))
    lb = (e / jnp.sum(e, axis=0, keepdims=True))[0:1, :]

    q = col(PH_Q)
    qf = q * _sigmoid(q)
    forget = lb + (1.0 - lb) * _sigmoid(col(PH_F))
    lf = jnp.log(forget)
    key = 1.0 - forget

    row = lax.broadcasted_iota(jnp.int32, (HG_TB, HGRN_WIDTH), 0)
    r_sub = row % HG_SUB
    r_ch = row % HG_CH

    def down(x, d):
        return pltpu.roll(x, d, 0)

    def up(x, d):
        return pltpu.roll(x, HG_TB - d, 0)

    a = lf
    d = 1
    while d < HG_SUB:
        a = a + jnp.where(r_sub >= d, down(a, d), 0.0)
        d *= 2
    a3 = a.reshape(HG_TB // HG_SUB, HG_SUB, HGRN_WIDTH)
    tsub = jnp.broadcast_to(a3[:, HG_SUB - 1:HG_SUB, :], a3.shape).reshape(HG_TB, HGRN_WIDTH)
    n_sub = HG_CH // HG_SUB
    prev = [jnp.where(r_ch >= HG_SUB * k, down(tsub, HG_SUB * k), 0.0) for k in range(1, n_sub)]
    nxt = [jnp.where(r_ch < HG_CH - HG_SUB * k, up(tsub, HG_SUB * k), 0.0) for k in range(1, n_sub)]
    eprev = prev[0] + prev[1] + prev[2]
    enext = nxt[0] + nxt[1] + nxt[2]
    suf = tsub - a

    q0 = qf * jnp.exp(a)
    q0_s[...] = q0.astype(BF16)
    qc_s[...] = (q0 * jnp.exp(eprev)).astype(BF16)
    kt_s[...] = (key * jnp.exp(-a)).astype(BF16)
    k0 = key * jnp.exp(suf)
    k0_s[...] = k0.astype(BF16)
    k1 = k0 * jnp.exp(nxt[0])
    k1_s[...] = k1.astype(BF16)
    k2 = k1 * jnp.exp(nxt[1])
    k2_s[...] = k2.astype(BF16)
    ke_s[...] = (k2 * jnp.exp(nxt[2])).astype(BF16)
    dec_s[...] = jnp.exp(eprev + tsub + enext)
    v_s[...] = col(PH_I).astype(BF16)

    tq = lax.broadcasted_iota(jnp.int32, (HG_CH, n_sub * HG_CH), 0)
    cc = lax.broadcasted_iota(jnp.int32, (HG_CH, n_sub * HG_CH), 1)
    cls = cc // HG_CH
    ts = cc % HG_CH
    bi = tq // HG_SUB
    bj = ts // HG_SUB
    mask4 = ((cls == 0) & (bi == bj) & (ts <= tq)) | ((cls > 0) & ((bi - bj) == cls))

    n_ch = HG_TB // HG_CH
    half = n_sub * HG_CH // 2
    units = [(c, h, slice(c * HG_CH, (c + 1) * HG_CH), slice(HEAD_DIM * h, HEAD_DIM * (h + 1)))
             for c in range(n_ch) for h in range(HEADS)]
    for c, h, rows, ls in units:
        kcat = jnp.concatenate([kt_s[rows, ls], k0_s[rows, ls], k1_s[rows, ls], k2_s[rows, ls]], axis=0)
        sc_s[c * HEADS + h] = lax.dot_general(q0_s[rows, ls], kcat, (((1,), (1,)), ((), ())),
                                              preferred_element_type=F32)
    for c, h, rows, ls in units:
        upd_s[c * HEADS + h] = lax.dot_general(v_s[rows, ls], ke_s[rows, ls], (((0,), (0,)), ((), ())),
                                               preferred_element_type=F32)
    for c, h, rows, ls in units:
        sc = jnp.where(mask4, sc_s[c * HEADS + h], 0.0)
        am_s[c * HEADS + h] = (sc[:, :half] + sc[:, half:]).astype(BF16)
    for c, h, rows, ls in units:
        vv = v_s[rows, ls]
        oacc_s[rows, ls] = jnp.dot(am_s[c * HEADS + h], jnp.concatenate([vv, vv], axis=0),
                                   preferred_element_type=F32)
    for h in range(HEADS):
        ls = slice(HEAD_DIM * h, HEAD_DIM * (h + 1))
        st = st_s[h]
        for c in range(n_ch):
            rows = slice(c * HG_CH, (c + 1) * HG_CH)
            oacc_s[rows, ls] += lax.dot_general(qc_s[rows, ls], st.astype(BF16), (((1,), (1,)), ((), ())),
                                                preferred_element_type=F32)
            st = st * dec_s[c * HG_CH:c * HG_CH + 1, ls] + upd_s[c * HEADS + h]
        st_s[h] = st

    o = oacc_s[...] * _sigmoid(col(PH_OG))
    parts = []
    for h in range(HEADS):
        oh = o[:, HEAD_DIM * h:HEAD_DIM * (h + 1)]
        ms = jnp.mean(oh * oh, axis=-1, keepdims=True)
        parts.append(oh * lax.rsqrt(ms + NORM_EPS))
    o = jnp.concatenate(parts, axis=1) * hnw_ref[...]
    z = col(PH_ZB)
    o_ref[...] = (o * (z * _sigmoid(z))).astype(BF16)


def _proj_hgrn(x2, norm_w, w, lb_logits, hgrn_norm_w, seqlen):
    t = x2.shape[0]
    wide = (HG_TB, HGRN_WIDTH)
    units = HG_TB // HG_CH * HEADS
    n_cls = HG_CH // HG_SUB
    row_blk = lambda w: pl.BlockSpec((HG_TB, w), lambda i: (i, 0))
    return pl.pallas_call(
        functools.partial(_proj_hgrn_kernel, blocks_per_seq=seqlen // HG_TB),
        grid=(t // HG_TB,),
        in_specs=[
            row_blk(D_MODEL), _const_spec((1, D_MODEL)),
            _const_spec((D_MODEL, IN_WIDTH)),
            _const_spec(lb_logits.shape), _const_spec((1, HGRN_WIDTH)),
        ],
        out_specs=[row_blk(SSM_WIDTH), row_blk(PBF_WIDTH), row_blk(HGRN_WIDTH)],
        out_shape=[
            jax.ShapeDtypeStruct((t, SSM_WIDTH), F32),
            jax.ShapeDtypeStruct((t, PBF_WIDTH), BF16),
            jax.ShapeDtypeStruct((t, HGRN_WIDTH), BF16),
        ],
        scratch_shapes=[pltpu.VMEM((HG_TB, PH_WIDTH), F32)] + [pltpu.VMEM(wide, BF16)] * 8 + [
            pltpu.VMEM(wide, F32), pltpu.VMEM(wide, F32),
            pltpu.VMEM((HEADS, HEAD_DIM, HEAD_DIM), F32),
            pltpu.VMEM((units, HEAD_DIM, HEAD_DIM), F32),
            pltpu.VMEM((units, HG_CH, n_cls * HG_CH), F32),
            pltpu.VMEM((units, HG_CH, n_cls * HG_CH // 2), BF16)],
        compiler_params=pltpu.CompilerParams(
            dimension_semantics=("arbitrary",), vmem_limit_bytes=VMEM_LIMIT),
        name="proj_hgrn",
    )(x2, norm_w, w, lb_logits, hgrn_norm_w)


def _merge_kernel(x_ref, ya_ref, yb_ref, ga_ref, gb_ref, wpa_ref, wpb_ref, wo_ref, fnw_ref, o_ref):
    pa = jnp.dot(ya_ref[...], wpa_ref[...].astype(BF16), preferred_element_type=F32)
    pb = jnp.dot(yb_ref[...], wpb_ref[...].astype(BF16), preferred_element_type=F32)
    merged = _sigmoid(ga_ref[...].astype(F32)) * pa + _sigmoid(gb_ref[...].astype(F32)) * pb
    h = x_ref[...] + jnp.dot(merged.astype(BF16), wo_ref[...].astype(BF16), preferred_element_type=F32)
    ms = jnp.mean(h * h, axis=-1, keepdims=True)
    o_ref[...] = h * lax.rsqrt(ms + NORM_EPS) * fnw_ref[...]


def _merge(x2, ya, yb, pbf, w_pa, w_pb, w_out, fnw):
    t = x2.shape[0]
    return pl.pallas_call(
        _merge_kernel,
        grid=(t // TM_OUT,),
        in_specs=[
            pl.BlockSpec((TM_OUT, D_MODEL), lambda i: (i, 0)),
            pl.BlockSpec((TM_OUT, SSM_WIDTH), lambda i: (i, 0)),
            pl.BlockSpec((TM_OUT, HGRN_WIDTH), lambda i: (i, 0)),
            pl.BlockSpec((TM_OUT, D_MODEL), lambda i: (i, 0)),
            pl.BlockSpec((TM_OUT, D_MODEL), lambda i: (i, 1)),
            _const_spec((SSM_WIDTH, D_MODEL)), _const_spec((HGRN_WIDTH, D_MODEL)),
            _const_spec((D_MODEL, D_MODEL)), _const_spec((1, D_MODEL)),
        ],
        out_specs=pl.BlockSpec((TM_OUT, D_MODEL), lambda i: (i, 0)),
        out_shape=jax.ShapeDtypeStruct((t, D_MODEL), F32),
        compiler_params=pltpu.CompilerParams(
            dimension_semantics=("arbitrary",), vmem_limit_bytes=VMEM_LIMIT),
        name="merge",
    )(x2, ya, yb, pbf, pbf, w_pa, w_pb, w_out, fnw)


def kernel(x, norm_w, w_in, ssm_lambda_re, ssm_lambda_im, ssm_b_re, ssm_b_im, ssm_c_re, ssm_c_im, ssm_d,
           ssm_log_dt, ssm_w_glu, ssm_b_glu, hgrn_lb_logits, hgrn_norm_w, w_proj_a, w_proj_b, w_out,
           final_norm_w):
    batch, seqlen, _ = x.shape
    assert norm_w.shape[0] == 1, "single-layer block"
    assert seqlen % S5_TB == 0 and seqlen % HG_TB == 0
    x2 = x.reshape(batch * seqlen, D_MODEL)
    u32, pbf, yb = _proj_hgrn(x2, norm_w[0][None, :], w_in[0], hgrn_lb_logits,
                              hgrn_norm_w[0][None, :], seqlen)

    wt, winp, woutp, ptre, ptim = _s5_prep(ssm_lambda_re[0], ssm_lambda_im[0], ssm_b_re[0], ssm_b_im[0],
                                           ssm_c_re[0], ssm_c_im[0], ssm_log_dt[0])
    ya = _s5(u32, pbf, wt, winp, woutp, ptre, ptim, ssm_d[0].reshape(1, SSM_WIDTH),
             ssm_w_glu[0], ssm_b_glu[0][None, :], batch, seqlen)
    out = _merge(x2, ya, yb, pbf, w_proj_a[0], w_proj_b[0], w_out[0], final_norm_w[None, :])
    return out.reshape(batch, seqlen, D_MODEL)
```

```python
import functools

import jax
import jax.numpy as jnp
from jax import lax
from jax.experimental import pallas as pl
from jax.experimental.pallas import tpu as pltpu

F32 = jnp.float32
BF16 = jnp.bfloat16

D_MODEL = 1024
SSM_WIDTH = 512
SSM_GROUP = 16
SSM_GROUPS = 32
SSM_STATE = 64
HGRN_WIDTH = 512
HEAD_DIM = 128
HEADS = 4
NORM_EPS = 1e-6
LAMBDA_RE_MAX = -1e-4

IN_WIDTH = 2 * SSM_WIDTH + 5 * HGRN_WIDTH + 2 * D_MODEL
COL_ZA, COL_H, COL_G = SSM_WIDTH, 2 * SSM_WIDTH, 2 * SSM_WIDTH + 5 * HGRN_WIDTH
PH_WIDTH = 5 * HGRN_WIDTH
PH_Q, PH_F, PH_I, PH_OG, PH_ZB = 0, 1, 2, 3, 4
PBF_WIDTH = 2 * D_MODEL + SSM_WIDTH
PBF_ZA = 4

LANES = 128
SUBLANES = 8
OCT = LANES // SSM_GROUP
N_OCT = SSM_GROUPS // OCT
OCT_STATE = OCT * SSM_STATE

S5_Q = 8
S5_QP = S5_Q // 2
S5_TB = 2048
S5_NCH = S5_TB // S5_Q
S5_SEG = SUBLANES
S5_NV = S5_NCH // S5_SEG
S5_PT_ROWS = S5_NV + SUBLANES
S5_SEG_ROWS = S5_Q * S5_NV
S5_SEG_PITCH = S5_SEG_ROWS + SUBLANES
HG_TB = 512
HG_CH = 64
HG_SUB = 16
W_CHUNK = 512
TM_OUT = 1024

V7X_VMEM_BYTES = 64 * 1024 * 1024
VMEM_LIMIT = V7X_VMEM_BYTES * 7 // 8


def _sigmoid(x):
    return 0.5 * jnp.tanh(0.5 * x) + 0.5


def _const_spec(shape):
    nd = len(shape)
    return pl.BlockSpec(shape, lambda *_: (0,) * nd, pipeline_mode=pl.Buffered(1))


def _disc(lam_re, lam_im, log_dt):
    return jnp.minimum(lam_re, LAMBDA_RE_MAX), lam_im, jnp.exp(log_dt)


def _cpow(lr, li, dt, k):
    mag = jnp.exp(k * (lr * dt))
    ang = k * (li * dt)
    return mag * jnp.cos(ang), mag * jnp.sin(ang)


def _split_bf16(x):
    hi = x.astype(BF16)
    return hi, (x - hi.astype(F32)).astype(BF16)


def _dot3(a, b_hi, b_lo):
    a_hi, a_lo = _split_bf16(a)
    d = lambda p, q: jnp.dot(p, q, preferred_element_type=F32)
    return d(a_hi, b_hi) + (d(a_hi, b_lo) + d(a_lo, b_hi))


def _cmul(a_re, a_im, b_re, b_im):
    return a_re * b_re - a_im * b_im, a_re * b_im + a_im * b_re


def _s5_prep_kernel(lam_c, b_t, c_n, lam_n, lam_r,
                    kb_ref, win_ref, wout_ref, ptre_ref, ptim_ref, pw_s, pm_s, xk_s):
    @pl.when(pl.program_id(0) == 0)
    def _():
        lr, li, dt = _disc(lam_c[0], lam_c[1], lam_c[2])
        ab_re, ab_im = _cpow(lr, li, dt, 1.0)
        den = lr * lr + li * li
        nr = ab_re - 1.0
        pw_s[0] = (nr * lr + ab_im * li) / den
        pw_s[1] = (ab_im * lr - nr * li) / den
        pw_s[2] = jnp.ones_like(ab_re)
        pw_s[3] = jnp.zeros_like(ab_re)
        pw_s[4] = ab_re
        pw_s[5] = ab_im
        m_re, m_im = _cpow(*_disc(lam_n[0], lam_n[1], lam_n[2]), 1.0)
        pm_s[0] = m_re
        pm_s[1] = m_im
        pm_s[2] = m_re
        pm_s[3] = m_im
        j = lax.broadcasted_iota(jnp.int32, (S5_PT_ROWS, OCT_STATE), 0).astype(F32) * float(S5_Q)
        for a in range(N_OCT):
            t_re, t_im = _cpow(*_disc(lam_r[0, a:a + 1, :], lam_r[1, a:a + 1, :], lam_r[2, a:a + 1, :]), j)
            ptre_ref[a] = t_re
            ptim_ref[a] = t_im

    col = lax.broadcasted_iota(jnp.int32, (SSM_GROUP, SSM_WIDTH), 1)

    def strip(g, carry):
        rows = pl.ds(pl.multiple_of(g * SSM_GROUP, SSM_GROUP), SSM_GROUP)
        grow = pl.ds(g, 1)
        coef_re, coef_im = pw_s[0, grow, :], pw_s[1, grow, :]
        p_re, p_im = pw_s[2, grow, :], pw_s[3, grow, :]
        bb_re, bb_im = _cmul(coef_re, coef_im, b_t[0, rows, :], b_t[1, rows, :])
        x_re, x_im = _cmul(bb_re, bb_im, p_re, p_im)
        xk_s[0, rows, :] = x_re[:, :SSM_STATE]
        xk_s[1, rows, :] = x_im[:, :SSM_STATE]
        a = g // OCT
        lrows = pl.ds(pl.multiple_of((g % OCT) * SSM_GROUP, SSM_GROUP), SSM_GROUP)
        m_in = col // SSM_STATE == g % OCT
        win_ref[0, a, 0, lrows, :] = jnp.concatenate(
            [jnp.where(m_in, x_re, 0.0), jnp.where(m_in, x_im, 0.0)], axis=1).astype(BF16)
        return carry

    lax.fori_loop(0, SSM_GROUPS, strip, 0, unroll=4)

    cr, ci = c_n[0], c_n[1]
    kfull = _dot3(xk_s[0], *_split_bf16(cr)) - _dot3(xk_s[1], *_split_bf16(ci))
    rowk = lax.broadcasted_iota(jnp.int32, (SSM_WIDTH, SSM_WIDTH), 0)
    colk = lax.broadcasted_iota(jnp.int32, (SSM_WIDTH, SSM_WIDTH), 1)
    kb_ref[0] = jnp.where(rowk // SSM_GROUP == colk // SSM_GROUP, kfull, 0.0).astype(BF16)

    p1_re, p1_im = pm_s[0], pm_s[1]
    w_re, w_im = _cmul(cr, ci, p1_re, p1_im)
    rowo = lax.broadcasted_iota(jnp.int32, (OCT_STATE, LANES), 0)
    colo = lax.broadcasted_iota(jnp.int32, (OCT_STATE, LANES), 1)
    m_out = rowo // SSM_STATE == colo // SSM_GROUP
    for a in range(N_OCT):
        ls = slice(LANES * a, LANES * (a + 1))
        wout_ref[0, a, :OCT_STATE, :] = jnp.where(m_out, jnp.tile(w_re[:, ls], (OCT, 1)), 0.0).astype(BF16)
        wout_ref[0, a, OCT_STATE:, :] = jnp.where(m_out, jnp.tile(-w_im[:, ls], (OCT, 1)), 0.0).astype(BF16)

    pw_s[2], pw_s[3] = _cmul(pw_s[2], pw_s[3], pw_s[4], pw_s[5])
    pm_s[0], pm_s[1] = _cmul(p1_re, p1_im, pm_s[2], pm_s[3])


def _s5_prep(lam_re, lam_im, b_re, b_im, c_re, c_im, log_dt):
    g, n, p = SSM_GROUPS, SSM_STATE, SSM_GROUP
    lam = jnp.stack([lam_re, lam_im, jnp.broadcast_to(log_dt[:, None], (g, n))])
    b = jnp.stack([b_re, b_im])
    c = jnp.stack([c_re, c_im])
    args = (jnp.tile(lam, (1, 1, OCT)),
            jnp.tile(b.transpose(0, 1, 3, 2).reshape(2, g * p, n), (1, 1, OCT)),
            c.transpose(0, 3, 1, 2).reshape(2, n, g * p),
            jnp.repeat(lam.transpose(0, 2, 1), p, axis=2),
            lam.reshape(3, N_OCT, OCT_STATE))
    mat = (S5_QP, N_OCT, 2, LANES, 2 * OCT_STATE)
    blk = (1, N_OCT, 1, LANES, 2 * OCT_STATE)
    tab = (N_OCT, S5_PT_ROWS, OCT_STATE)
    kb, win, woutp, ptre, ptim = pl.pallas_call(
        _s5_prep_kernel,
        grid=(S5_Q,),
        in_specs=[_const_spec(a.shape) for a in args],
        out_specs=[
            pl.BlockSpec((1, SSM_WIDTH, SSM_WIDTH), lambda t: (t, 0, 0)),
            pl.BlockSpec(blk, lambda t: ((S5_Q - 1 - t) // 2, 0, (S5_Q - 1 - t) % 2, 0, 0)),
            pl.BlockSpec((1, N_OCT, 2 * OCT_STATE, LANES), lambda t: (t // 2, 0, 0, t % 2)),
            pl.BlockSpec(tab, lambda t: (0, 0, 0)),
            pl.BlockSpec(tab, lambda t: (0, 0, 0)),
        ],
        out_shape=[
            jax.ShapeDtypeStruct((S5_Q, SSM_WIDTH, SSM_WIDTH), BF16),
            jax.ShapeDtypeStruct(mat, BF16),
            jax.ShapeDtypeStruct((S5_QP, N_OCT, 2 * OCT_STATE, 2 * LANES), BF16),
            jax.ShapeDtypeStruct(tab, F32),
            jax.ShapeDtypeStruct(tab, F32),
        ],
        scratch_shapes=[pltpu.VMEM((6, SSM_GROUPS, SSM_WIDTH), F32),
                        pltpu.VMEM((4, SSM_STATE, SSM_WIDTH), F32),
                        pltpu.VMEM((2, SSM_WIDTH, SSM_STATE), F32)],
        compiler_params=pltpu.CompilerParams(
            dimension_semantics=("arbitrary",), vmem_limit_bytes=VMEM_LIMIT),
        name="s5_prep",
    )(*args)

    kbo = jnp.stack([kb[:, LANES * a:LANES * (a + 1), LANES * a:LANES * (a + 1)] for a in range(N_OCT)], axis=1)
    zero = jnp.zeros_like(kbo[0])

    def pair_tile(d):
        top = jnp.concatenate([kbo[2 * d], kbo[2 * d + 1]], axis=-1)
        bot = jnp.concatenate([kbo[2 * d - 1] if d > 0 else zero, kbo[2 * d]], axis=-1)
        return jnp.concatenate([top, bot], axis=-2)

    wt = jnp.stack([pair_tile(d) for d in range(S5_QP)])
    winp = win.reshape(S5_QP, N_OCT, 2 * LANES, 2 * OCT_STATE)
    return wt, winp, woutp, ptre, ptim


def _gelu_tanh(x):
    c = 0.7978845608028654
    return x * (0.5 * (1.0 + jnp.tanh(c * (x + 0.044715 * (x * x * x)))))


def _s5_kernel(u0_ref, u1_ref, u2_ref, u3_ref, z_ref, wt_ref, winp_ref, woutp_ref, ptre_ref, ptim_ref,
               d_ref, wglu_ref, bglu_ref, o_ref, up0_s, up1_s, up2_s, up3_s, y0_s, y1_s, y2_s, y3_s,
               cre_scr, cim_scr):
    u_refs = (u0_ref, u1_ref, u2_ref, u3_ref)
    up_scrs = (up0_s, up1_s, up2_s, up3_s)
    y_scrs = (y0_s, y1_s, y2_s, y3_s)

    @pl.when(pl.program_id(1) == 0)
    def _():
        cre_scr[...] = jnp.zeros_like(cre_scr)
        cim_scr[...] = jnp.zeros_like(cim_scr)

    for a in range(N_OCT):
        for r in range(S5_SEG):
            up_scrs[a][S5_SEG_PITCH * r:S5_SEG_PITCH * r + S5_SEG_ROWS, :] = (
                u_refs[a][S5_SEG_ROWS * r:S5_SEG_ROWS * (r + 1), :])

    def tok(a, s):
        return jnp.concatenate(
            [up_scrs[a][pl.ds(s + S5_Q * v, S5_SEG, stride=S5_SEG_PITCH), :] for v in range(S5_NV)], axis=0)

    xp = [[jnp.concatenate([tok(a, 2 * sp), tok(a, 2 * sp + 1)], axis=1).astype(BF16) for a in range(N_OCT)]
          for sp in range(S5_QP)]

    def cmul_add(b_re, b_im, m_re, m_im, x_re, x_im):
        return b_re + m_re * x_re - m_im * x_im, b_im + m_re * x_im + m_im * x_re

    hs = []
    for a in range(N_OCT):
        acc = None
        for sp in range(S5_QP):
            part = jnp.dot(xp[sp][a], winp_ref[sp, a], preferred_element_type=F32)
            acc = part if acc is None else acc + part
        blk = lambda v: (acc[S5_SEG * v:S5_SEG * (v + 1), :OCT_STATE], acc[S5_SEG * v:S5_SEG * (v + 1), OCT_STATE:])
        m_re, m_im = ptre_ref[a, 1:2, :], ptim_ref[a, 1:2, :]
        loc = [blk(0)]
        for v in range(1, S5_NV):
            loc.append(cmul_add(*blk(v), m_re, m_im, *loc[-1]))
        l_re, l_im = ptre_ref[a, S5_NV:S5_NV + 1, :], ptim_ref[a, S5_NV:S5_NV + 1, :]
        c_re, c_im = cre_scr[a:a + 1, :], cim_scr[a:a + 1, :]
        carry = []
        for r in range(S5_SEG):
            carry.append((c_re, c_im))
            c_re, c_im = cmul_add(loc[-1][0][r:r + 1, :], loc[-1][1][r:r + 1, :], l_re, l_im, c_re, c_im)
        cre_scr[a:a + 1, :] = c_re
        cim_scr[a:a + 1, :] = c_im
        cs_re = jnp.concatenate([c[0] for c in carry], axis=0)
        cs_im = jnp.concatenate([c[1] for c in carry], axis=0)
        ent = [(cs_re, cs_im)]
        for v in range(S5_NV - 1):
            ent.append(cmul_add(*loc[v], ptre_ref[a, v + 1:v + 2, :], ptim_ref[a, v + 1:v + 2, :], cs_re, cs_im))
        hs.append(jnp.concatenate([jnp.concatenate([e[0] for e in ent], axis=0),
                                   jnp.concatenate([e[1] for e in ent], axis=0)], axis=1).astype(BF16))

    for a in range(N_OCT):
        for tp in range(S5_QP):
            acc = jnp.dot(hs[a], woutp_ref[tp, a], preferred_element_type=F32)
            for sp in range(tp + 1):
                acc = acc + jnp.dot(xp[sp][a], wt_ref[tp - sp, a], preferred_element_type=F32)
            for v in range(S5_NV):
                rows = slice(S5_SEG * v, S5_SEG * (v + 1))
                y_scrs[a][pl.ds(2 * tp + S5_Q * v, S5_SEG, stride=S5_SEG_PITCH), :] = acc[rows, :LANES]
                y_scrs[a][pl.ds(2 * tp + 1 + S5_Q * v, S5_SEG, stride=S5_SEG_PITCH), :] = acc[rows, LANES:]
        d_a = d_ref[:, LANES * a:LANES * (a + 1)]
        for r in range(S5_SEG):
            prow = slice(S5_SEG_PITCH * r, S5_SEG_PITCH * r + S5_SEG_ROWS)
            urow = slice(S5_SEG_ROWS * r, S5_SEG_ROWS * (r + 1))
            y_scrs[a][prow, :] = _gelu_tanh(y_scrs[a][prow, :] + d_a * u_refs[a][urow, :])

    unpad = lambda ref: jnp.concatenate(
        [ref[S5_SEG_PITCH * r:S5_SEG_PITCH * r + S5_SEG_ROWS, :] for r in range(S5_SEG)], axis=0)
    y = jnp.concatenate([unpad(r) for r in y_scrs], axis=1)
    gate = jnp.dot(y.astype(BF16), wglu_ref[...].astype(BF16), preferred_element_type=F32) + bglu_ref[...]
    y = y * _sigmoid(gate)
    z = z_ref[...].astype(F32)
    o_ref[...] = (y * (z * _sigmoid(z))).astype(BF16)


def _s5(u32, pbf, wt, winp, woutp, ptre, ptim, d_row, w_glu, b_glu, batch, seqlen):
    nb = seqlen // S5_TB
    t = batch * seqlen
    u_tile = lambda a: pl.BlockSpec((S5_TB, LANES), lambda b, i, a=a: (b * nb + i, a))
    return pl.pallas_call(
        _s5_kernel,
        grid=(batch, nb),
        in_specs=[u_tile(a) for a in range(N_OCT)] + [
            pl.BlockSpec((S5_TB, SSM_WIDTH), lambda b, i: (b * nb + i, PBF_ZA)),
            _const_spec(wt.shape), _const_spec(winp.shape), _const_spec(woutp.shape),
            _const_spec(ptre.shape), _const_spec(ptim.shape),
            _const_spec((1, SSM_WIDTH)), _const_spec((SSM_WIDTH, SSM_WIDTH)), _const_spec((1, SSM_WIDTH)),
        ],
        out_specs=pl.BlockSpec((S5_TB, SSM_WIDTH), lambda b, i: (b * nb + i, 0)),
        out_shape=jax.ShapeDtypeStruct((t, SSM_WIDTH), BF16),
        scratch_shapes=[pltpu.VMEM((S5_SEG * S5_SEG_PITCH, LANES), F32)] * (2 * N_OCT) + [
            pltpu.VMEM((SUBLANES, OCT_STATE), F32),
            pltpu.VMEM((SUBLANES, OCT_STATE), F32),
        ],
        compiler_params=pltpu.CompilerParams(
            dimension_semantics=("arbitrary", "arbitrary"), vmem_limit_bytes=VMEM_LIMIT),
        name="s5",
    )(u32, u32, u32, u32, pbf, wt, winp, woutp, ptre, ptim, d_row, w_glu, b_glu)


def _stage_weights(w_hbm, wbf_s, stage_s, sem):
    n_chunks = IN_WIDTH // W_CHUNK

    def chunk_copy(c):
        return pltpu.make_async_copy(w_hbm.at[:, pl.ds(c * W_CHUNK, W_CHUNK)], stage_s.at[c % 2], sem.at[c % 2])

    chunk_copy(0).start()
    for c in range(n_chunks):
        if c + 1 < n_chunks:
            chunk_copy(c + 1).start()
        chunk_copy(c).wait()
        wbf_s[:, c * W_CHUNK:(c + 1) * W_CHUNK] = stage_s[c % 2].astype(BF16)


def _proj_hgrn_kernel(x_ref, nw_ref, w_hbm, lbl_ref, hnw_ref, ou_ref, or_ref, o_ref,
                      wbf_s, stage_s, stage_sem, ph_s, q0_s, qc_s, kt_s, k0_s, k1_s, k2_s, ke_s, v_s, dec_s, oacc_s,
                      st_s, upd_s, sc_s, am_s,
                      *, blocks_per_seq):
    @pl.when(pl.program_id(0) == 0)
    def _():
        _stage_weights(w_hbm, wbf_s, stage_s, stage_sem)

    @pl.when(pl.program_id(0) % blocks_per_seq == 0)
    def _():
        st_s[...] = jnp.zeros_like(st_s)

    x = x_ref[...]
    ms = jnp.mean(x * x, axis=-1, keepdims=True)
    xn = (x * lax.rsqrt(ms + NORM_EPS) * nw_ref[...]).astype(BF16)
    proj = lambda lo, hi: jnp.dot(xn, wbf_s[:, lo:hi], preferred_element_type=F32)
    ph_s[...] = proj(COL_H, COL_G)
    ou_ref[...] = proj(0, COL_ZA)
    or_ref[:, :2 * D_MODEL] = proj(COL_G, IN_WIDTH).astype(BF16)
    or_ref[:, 2 * D_MODEL:] = proj(COL_ZA, COL_H).astype(BF16)
    col = lambda k: ph_s[:, HGRN_WIDTH * k:HGRN_WIDTH * (k + 1)]

    logits = lbl_ref[...]
    e = jnp.exp(logits - jnp.max(logits, axis=0, keepdims=True))
    lb = (e / jnp.sum(e, axis=0, keepdims=True))[0:1, :]

    q = col(PH_Q)
    qf = q * _sigmoid(q)
    forget = lb + (1.0 - lb) * _sigmoid(col(PH_F))
    lf = jnp.log(forget)
    key = 1.0 - forget

    row = lax.broadcasted_iota(jnp.int32, (HG_TB, HGRN_WIDTH), 0)
    r_sub = row % HG_SUB
    r_ch = row % HG_CH

    def down(x, d):
        return pltpu.roll(x, d, 0)

    def up(x, d):
        return pltpu.roll(x, HG_TB - d, 0)

    a = lf
    d = 1
    while d < HG_SUB:
        a = a + jnp.where(r_sub >= d, down(a, d), 0.0)
        d *= 2
    a3 = a.reshape(HG_TB // HG_SUB, HG_SUB, HGRN_WIDTH)
    tsub = jnp.broadcast_to(a3[:, HG_SUB - 1:HG_SUB, :], a3.shape).reshape(HG_TB, HGRN_WIDTH)
    n_sub = HG_CH // HG_SUB
    prev = [jnp.where(r_ch >= HG_SUB * k, down(tsub, HG_SUB * k), 0.0) for k in range(1, n_sub)]
    nxt = [jnp.where(r_ch < HG_CH - HG_SUB * k, up(tsub, HG_SUB * k), 0.0) for k in range(1, n_sub)]
    eprev = prev[0] + prev[1] + prev[2]
    enext = nxt[0] + nxt[1] + nxt[2]
    suf = tsub - a

    q0 = qf * jnp.exp(a)
    q0_s[...] = q0.astype(BF16)
    qc_s[...] = (q0 * jnp.exp(eprev)).astype(BF16)
    kt_s[...] = (key * jnp.exp(-a)).astype(BF16)
    k0 = key * jnp.exp(suf)
    k0_s[...] = k0.astype(BF16)
    k1 = k0 * jnp.exp(nxt[0])
    k1_s[...] = k1.astype(BF16)
    k2 = k1 * jnp.exp(nxt[1])
    k2_s[...] = k2.astype(BF16)
    ke_s[...] = (k2 * jnp.exp(nxt[2])).astype(BF16)
    dec_s[...] = jnp.exp(eprev + tsub + enext)
    v_s[...] = col(PH_I).astype(BF16)

    tq = lax.broadcasted_iota(jnp.int32, (HG_CH, n_sub * HG_CH), 0)
    cc = lax.broadcasted_iota(jnp.int32, (HG_CH, n_sub * HG_CH), 1)
    cls = cc // HG_CH
    ts = cc % HG_CH
    bi = tq // HG_SUB
    bj = ts // HG_SUB
    mask4 = ((cls == 0) & (bi == bj) & (ts <= tq)) | ((cls > 0) & ((bi - bj) == cls))

    n_ch = HG_TB // HG_CH
    half = n_sub * HG_CH // 2
    units = [(c, h, slice(c * HG_CH, (c + 1) * HG_CH), slice(HEAD_DIM * h, HEAD_DIM * (h + 1)))
             for c in range(n_ch) for h in range(HEADS)]
    for c, h, rows, ls in units:
        kcat = jnp.concatenate([kt_s[rows, ls], k0_s[rows, ls], k1_s[rows, ls], k2_s[rows, ls]], axis=0)
        sc_s[c * HEADS + h] = lax.dot_general(q0_s[rows, ls], kcat, (((1,), (1,)), ((), ())),
                                              preferred_element_type=F32)
    for c, h, rows, ls in units:
        upd_s[c * HEADS + h] = lax.dot_general(v_s[rows, ls], ke_s[rows, ls], (((0,), (0,)), ((), ())),
                                               preferred_element_type=F32)
    for c, h, rows, ls in units:
        sc = jnp.where(mask4, sc_s[c * HEADS + h], 0.0)
        am_s[c * HEADS + h] = (sc[:, :half] + sc[:, half:]).astype(BF16)
    for c, h, rows, ls in units:
        vv = v_s[rows, ls]
        oacc_s[rows, ls] = jnp.dot(am_s[c * HEADS + h], jnp.concatenate([vv, vv], axis=0),
                                   preferred_element_type=F32)
    for h in range(HEADS):
        ls = slice(HEAD_DIM * h, HEAD_DIM * (h + 1))
        st = st_s[h]
        for c in range(n_ch):
            rows = slice(c * HG_CH, (c + 1) * HG_CH)
            oacc_s[rows, ls] += lax.dot_general(qc_s[rows, ls], st.astype(BF16), (((1,), (1,)), ((), ())),
                                                preferred_element_type=F32)
            st = st * dec_s[c * HG_CH:c * HG_CH + 1, ls] + upd_s[c * HEADS + h]
        st_s[h] = st

    o = oacc_s[...] * _sigmoid(col(PH_OG))
    parts = []
    for h in range(HEADS):
        oh = o[:, HEAD_DIM * h:HEAD_DIM * (h + 1)]
        ms = jnp.mean(oh * oh, axis=-1, keepdims=True)
        parts.append(oh * lax.rsqrt(ms + NORM_EPS))
    o = jnp.concatenate(parts, axis=1) * hnw_ref[...]
    z = col(PH_ZB)
    o_ref[...] = (o * (z * _sigmoid(z))).astype(BF16)


def _proj_hgrn(x2, norm_w, w, lb_logits, hgrn_norm_w, seqlen):
    t = x2.shape[0]
    wide = (HG_TB, HGRN_WIDTH)
    units = HG_TB // HG_CH * HEADS
    n_cls = HG_CH // HG_SUB
    row_blk = lambda w: pl.BlockSpec((HG_TB, w), lambda i: (i, 0))
    return pl.pallas_call(
        functools.partial(_proj_hgrn_kernel, blocks_per_seq=seqlen // HG_TB),
        grid=(t // HG_TB,),
        in_specs=[
            row_blk(D_MODEL), _const_spec((1, D_MODEL)),
            pl.BlockSpec(memory_space=pl.ANY),
            _const_spec(lb_logits.shape), _const_spec((1, HGRN_WIDTH)),
        ],
        out_specs=[row_blk(SSM_WIDTH), row_blk(PBF_WIDTH), row_blk(HGRN_WIDTH)],
        out_shape=[
            jax.ShapeDtypeStruct((t, SSM_WIDTH), F32),
            jax.ShapeDtypeStruct((t, PBF_WIDTH), BF16),
            jax.ShapeDtypeStruct((t, HGRN_WIDTH), BF16),
        ],
        scratch_shapes=[
            pltpu.VMEM((D_MODEL, IN_WIDTH), BF16), pltpu.VMEM((2, D_MODEL, W_CHUNK), F32),
            pltpu.SemaphoreType.DMA((2,)),
            pltpu.VMEM((HG_TB, PH_WIDTH), F32)] + [pltpu.VMEM(wide, BF16)] * 8 + [
            pltpu.VMEM(wide, F32), pltpu.VMEM(wide, F32),
            pltpu.VMEM((HEADS, HEAD_DIM, HEAD_DIM), F32),
            pltpu.VMEM((units, HEAD_DIM, HEAD_DIM), F32),
            pltpu.VMEM((units, HG_CH, n_cls * HG_CH), F32),
            pltpu.VMEM((units, HG_CH, n_cls * HG_CH // 2), BF16)],
        compiler_params=pltpu.CompilerParams(
            dimension_semantics=("arbitrary",), vmem_limit_bytes=VMEM_LIMIT),
        name="proj_hgrn",
    )(x2, norm_w, w, lb_logits, hgrn_norm_w)


def _merge_kernel(x_ref, ya_ref, yb_ref, ga_ref, gb_ref, wpa_ref, wpb_ref, wo_ref, fnw_ref, o_ref):
    pa = jnp.dot(ya_ref[...], wpa_ref[...].astype(BF16), preferred_element_type=F32)
    pb = jnp.dot(yb_ref[...], wpb_ref[...].astype(BF16), preferred_element_type=F32)
    merged = _sigmoid(ga_ref[...].astype(F32)) * pa + _sigmoid(gb_ref[...].astype(F32)) * pb
    h = x_ref[...] + jnp.dot(merged.astype(BF16), wo_ref[...].astype(BF16), preferred_element_type=F32)
    ms = jnp.mean(h * h, axis=-1, keepdims=True)
    o_ref[...] = h * lax.rsqrt(ms + NORM_EPS) * fnw_ref[...]


def _merge(x2, ya, yb, pbf, w_pa, w_pb, w_out, fnw):
    t = x2.shape[0]
    return pl.pallas_call(
        _merge_kernel,
        grid=(t // TM_OUT,),
        in_specs=[
            pl.BlockSpec((TM_OUT, D_MODEL), lambda i: (i, 0)),
            pl.BlockSpec((TM_OUT, SSM_WIDTH), lambda i: (i, 0)),
            pl.BlockSpec((TM_OUT, HGRN_WIDTH), lambda i: (i, 0)),
            pl.BlockSpec((TM_OUT, D_MODEL), lambda i: (i, 0)),
            pl.BlockSpec((TM_OUT, D_MODEL), lambda i: (i, 1)),
            _const_spec((SSM_WIDTH, D_MODEL)), _const_spec((HGRN_WIDTH, D_MODEL)),
            _const_spec((D_MODEL, D_MODEL)), _const_spec((1, D_MODEL)),
        ],
        out_specs=pl.BlockSpec((TM_OUT, D_MODEL), lambda i: (i, 0)),
        out_shape=jax.ShapeDtypeStruct((t, D_MODEL), F32),
        compiler_params=pltpu.CompilerParams(
            dimension_semantics=("arbitrary",), vmem_limit_bytes=VMEM_LIMIT),
        name="merge",
    )(x2, ya, yb, pbf, pbf, w_pa, w_pb, w_out, fnw)


def kernel(x, norm_w, w_in, ssm_lambda_re, ssm_lambda_im, ssm_b_re, ssm_b_im, ssm_c_re, ssm_c_im, ssm_d,
           ssm_log_dt, ssm_w_glu, ssm_b_glu, hgrn_lb_logits, hgrn_norm_w, w_proj_a, w_proj_b, w_out,
           final_norm_w):
    batch, seqlen, _ = x.shape
    assert norm_w.shape[0] == 1, "single-layer block"
    assert seqlen % S5_TB == 0 and seqlen % HG_TB == 0
    x2 = x.reshape(batch * seqlen, D_MODEL)
    u32, pbf, yb = _proj_hgrn(x2, norm_w[0][None, :], w_in[0], hgrn_lb_logits,
                              hgrn_norm_w[0][None, :], seqlen)

    wt, winp, woutp, ptre, ptim = _s5_prep(ssm_lambda_re[0], ssm_lambda_im[0], ssm_b_re[0], ssm_b_im[0],
                                           ssm_c_re[0], ssm_c_im[0], ssm_log_dt[0])
    ya = _s5(u32, pbf, wt, winp, woutp, ptre, ptim, ssm_d[0].reshape(1, SSM_WIDTH),
             ssm_w_glu[0], ssm_b_glu[0][None, :], batch, seqlen)
    out = _merge(x2, ya, yb, pbf, w_proj_a[0], w_proj_b[0], w_out[0], final_norm_w[None, :])
    return out.reshape(batch, seqlen, D_MODEL)
```

```python
import functools

import jax
import jax.numpy as jnp
from jax import lax
from jax.experimental import pallas as pl
from jax.experimental.pallas import tpu as pltpu

F32 = jnp.float32
BF16 = jnp.bfloat16

D_MODEL = 1024
SSM_WIDTH = 512
SSM_GROUP = 16
SSM_GROUPS = 32
SSM_STATE = 64
HGRN_WIDTH = 512
HEAD_DIM = 128
HEADS = 4
NORM_EPS = 1e-6
LAMBDA_RE_MAX = -1e-4

IN_WIDTH = 2 * SSM_WIDTH + 5 * HGRN_WIDTH + 2 * D_MODEL
COL_ZA, COL_H, COL_G = SSM_WIDTH, 2 * SSM_WIDTH, 2 * SSM_WIDTH + 5 * HGRN_WIDTH
PH_WIDTH = 5 * HGRN_WIDTH
PH_Q, PH_F, PH_I, PH_OG, PH_ZB = 0, 1, 2, 3, 4
PBF_WIDTH = 2 * D_MODEL + SSM_WIDTH
PBF_ZA = 4

LANES = 128
SUBLANES = 8
OCT = LANES // SSM_GROUP
N_OCT = SSM_GROUPS // OCT
OCT_STATE = OCT * SSM_STATE

S5_Q = 8
S5_QP = S5_Q // 2
S5_TB = 2048
S5_NCH = S5_TB // S5_Q
S5_SEG = SUBLANES
S5_NV = S5_NCH // S5_SEG
S5_PT_ROWS = S5_NV + SUBLANES
S5_SEG_ROWS = S5_Q * S5_NV
S5_SEG_PITCH = S5_SEG_ROWS + SUBLANES
HG_TB = 512
HG_CH = 64
HG_SUB = 16
W_CHUNK = 512
TM_OUT = 1024

V7X_VMEM_BYTES = 64 * 1024 * 1024
VMEM_LIMIT = V7X_VMEM_BYTES * 7 // 8


def _sigmoid(x):
    return 0.5 * jnp.tanh(0.5 * x) + 0.5


def _const_spec(shape):
    nd = len(shape)
    return pl.BlockSpec(shape, lambda *_: (0,) * nd, pipeline_mode=pl.Buffered(1))


def _disc(lam_re, lam_im, log_dt):
    return jnp.minimum(lam_re, LAMBDA_RE_MAX), lam_im, jnp.exp(log_dt)


def _cpow(lr, li, dt, k):
    mag = jnp.exp(k * (lr * dt))
    ang = k * (li * dt)
    return mag * jnp.cos(ang), mag * jnp.sin(ang)


def _split_bf16(x):
    hi = x.astype(BF16)
    return hi, (x - hi.astype(F32)).astype(BF16)


def _dot3(a, b_hi, b_lo):
    a_hi, a_lo = _split_bf16(a)
    d = lambda p, q: jnp.dot(p, q, preferred_element_type=F32)
    return d(a_hi, b_hi) + (d(a_hi, b_lo) + d(a_lo, b_hi))


def _cmul(a_re, a_im, b_re, b_im):
    return a_re * b_re - a_im * b_im, a_re * b_im + a_im * b_re


def _s5_prep_kernel(lam_c, b_t, c_n, lam_n, lam_r,
                    kb_ref, win_ref, wout_ref, ptre_ref, ptim_ref, pw_s, pm_s, xk_s):
    @pl.when(pl.program_id(0) == 0)
    def _():
        lr, li, dt = _disc(lam_c[0], lam_c[1], lam_c[2])
        ab_re, ab_im = _cpow(lr, li, dt, 1.0)
        den = lr * lr + li * li
        nr = ab_re - 1.0
        pw_s[0] = (nr * lr + ab_im * li) / den
        pw_s[1] = (ab_im * lr - nr * li) / den
        pw_s[2] = jnp.ones_like(ab_re)
        pw_s[3] = jnp.zeros_like(ab_re)
        pw_s[4] = ab_re
        pw_s[5] = ab_im
        m_re, m_im = _cpow(*_disc(lam_n[0], lam_n[1], lam_n[2]), 1.0)
        pm_s[0] = m_re
        pm_s[1] = m_im
        pm_s[2] = m_re
        pm_s[3] = m_im
        j = lax.broadcasted_iota(jnp.int32, (S5_PT_ROWS, OCT_STATE), 0).astype(F32) * float(S5_Q)
        for a in range(N_OCT):
            t_re, t_im = _cpow(*_disc(lam_r[0, a:a + 1, :], lam_r[1, a:a + 1, :], lam_r[2, a:a + 1, :]), j)
            ptre_ref[a] = t_re
            ptim_ref[a] = t_im

    col = lax.broadcasted_iota(jnp.int32, (SSM_GROUP, SSM_WIDTH), 1)

    def strip(g, carry):
        rows = pl.ds(pl.multiple_of(g * SSM_GROUP, SSM_GROUP), SSM_GROUP)
        grow = pl.ds(g, 1)
        coef_re, coef_im = pw_s[0, grow, :], pw_s[1, grow, :]
        p_re, p_im = pw_s[2, grow, :], pw_s[3, grow, :]
        bb_re, bb_im = _cmul(coef_re, coef_im, b_t[0, rows, :], b_t[1, rows, :])
        x_re, x_im = _cmul(bb_re, bb_im, p_re, p_im)
        xk_s[0, rows, :] = x_re[:, :SSM_STATE]
        xk_s[1, rows, :] = x_im[:, :SSM_STATE]
        a = g // OCT
        lrows = pl.ds(pl.multiple_of((g % OCT) * SSM_GROUP, SSM_GROUP), SSM_GROUP)
        m_in = col // SSM_STATE == g % OCT
        win_ref[0, a, 0, lrows, :] = jnp.concatenate(
            [jnp.where(m_in, x_re, 0.0), jnp.where(m_in, x_im, 0.0)], axis=1).astype(BF16)
        return carry

    lax.fori_loop(0, SSM_GROUPS, strip, 0, unroll=4)

    cr, ci = c_n[0], c_n[1]
    kfull = _dot3(xk_s[0], *_split_bf16(cr)) - _dot3(xk_s[1], *_split_bf16(ci))
    rowk = lax.broadcasted_iota(jnp.int32, (LANES, LANES), 0)
    colk = lax.broadcasted_iota(jnp.int32, (LANES, LANES), 1)
    m_k = rowk // SSM_GROUP == colk // SSM_GROUP
    for a in range(N_OCT):
        ls = slice(LANES * a, LANES * (a + 1))
        kb_ref[0, a] = jnp.where(m_k, kfull[ls, ls], 0.0).astype(BF16)

    p1_re, p1_im = pm_s[0], pm_s[1]
    w_re, w_im = _cmul(cr, ci, p1_re, p1_im)
    rowo = lax.broadcasted_iota(jnp.int32, (OCT_STATE, LANES), 0)
    colo = lax.broadcasted_iota(jnp.int32, (OCT_STATE, LANES), 1)
    m_out = rowo // SSM_STATE == colo // SSM_GROUP
    for a in range(N_OCT):
        ls = slice(LANES * a, LANES * (a + 1))
        wout_ref[0, a, :OCT_STATE, :] = jnp.where(m_out, jnp.tile(w_re[:, ls], (OCT, 1)), 0.0).astype(BF16)
        wout_ref[0, a, OCT_STATE:, :] = jnp.where(m_out, jnp.tile(-w_im[:, ls], (OCT, 1)), 0.0).astype(BF16)

    pw_s[2], pw_s[3] = _cmul(pw_s[2], pw_s[3], pw_s[4], pw_s[5])
    pm_s[0], pm_s[1] = _cmul(p1_re, p1_im, pm_s[2], pm_s[3])


def _s5_prep(lam_re, lam_im, b_re, b_im, c_re, c_im, log_dt):
    g, n, p = SSM_GROUPS, SSM_STATE, SSM_GROUP
    lam = jnp.stack([lam_re, lam_im, jnp.broadcast_to(log_dt[:, None], (g, n))])
    b = jnp.stack([b_re, b_im])
    c = jnp.stack([c_re, c_im])
    args = (jnp.tile(lam, (1, 1, OCT)),
            jnp.tile(b.transpose(0, 1, 3, 2).reshape(2, g * p, n), (1, 1, OCT)),
            c.transpose(0, 3, 1, 2).reshape(2, n, g * p),
            jnp.repeat(lam.transpose(0, 2, 1), p, axis=2),
            lam.reshape(3, N_OCT, OCT_STATE))
    mat = (S5_QP, N_OCT, 2, LANES, 2 * OCT_STATE)
    blk = (1, N_OCT, 1, LANES, 2 * OCT_STATE)
    tab = (N_OCT, S5_PT_ROWS, OCT_STATE)
    kb, win, woutp, ptre, ptim = pl.pallas_call(
        _s5_prep_kernel,
        grid=(S5_Q,),
        in_specs=[_const_spec(a.shape) for a in args],
        out_specs=[
            pl.BlockSpec((1, N_OCT, LANES, LANES), lambda t: (t, 0, 0, 0)),
            pl.BlockSpec(blk, lambda t: ((S5_Q - 1 - t) // 2, 0, (S5_Q - 1 - t) % 2, 0, 0)),
            pl.BlockSpec((1, N_OCT, 2 * OCT_STATE, LANES), lambda t: (t // 2, 0, 0, t % 2)),
            pl.BlockSpec(tab, lambda t: (0, 0, 0)),
            pl.BlockSpec(tab, lambda t: (0, 0, 0)),
        ],
        out_shape=[
            jax.ShapeDtypeStruct((S5_Q, N_OCT, LANES, LANES), BF16),
            jax.ShapeDtypeStruct(mat, BF16),
            jax.ShapeDtypeStruct((S5_QP, N_OCT, 2 * OCT_STATE, 2 * LANES), BF16),
            jax.ShapeDtypeStruct(tab, F32),
            jax.ShapeDtypeStruct(tab, F32),
        ],
        scratch_shapes=[pltpu.VMEM((6, SSM_GROUPS, SSM_WIDTH), F32),
                        pltpu.VMEM((4, SSM_STATE, SSM_WIDTH), F32),
                        pltpu.VMEM((2, SSM_WIDTH, SSM_STATE), F32)],
        compiler_params=pltpu.CompilerParams(
            dimension_semantics=("arbitrary",), vmem_limit_bytes=VMEM_LIMIT),
        name="s5_prep",
    )(*args)

    winp = win.reshape(S5_QP, N_OCT, 2 * LANES, 2 * OCT_STATE)
    return kb, winp, woutp, ptre, ptim


def _gelu_tanh(x):
    c = 0.7978845608028654
    return x * (0.5 * (1.0 + jnp.tanh(c * (x + 0.044715 * (x * x * x)))))


def _s5_kernel(u0_ref, u1_ref, u2_ref, u3_ref, z_ref, kb_ref, winp_ref, woutp_ref, ptre_ref, ptim_ref,
               d_ref, wglu_ref, bglu_ref, o_ref, y0_s, y1_s, y2_s, y3_s, cre_scr, cim_scr):
    u_refs = (u0_ref, u1_ref, u2_ref, u3_ref)
    y_scrs = (y0_s, y1_s, y2_s, y3_s)

    @pl.when(pl.program_id(1) == 0)
    def _():
        cre_scr[...] = jnp.zeros_like(cre_scr)
        cim_scr[...] = jnp.zeros_like(cim_scr)

    def tok(a, s):
        return jnp.concatenate(
            [u_refs[a][pl.ds(s + S5_Q * v, S5_SEG, stride=S5_SEG_PITCH), :] for v in range(S5_NV)], axis=0)

    xp = [[jnp.concatenate([tok(a, 2 * sp), tok(a, 2 * sp + 1)], axis=1).astype(BF16) for a in range(N_OCT)]
          for sp in range(S5_QP)]

    def cmul_add(b_re, b_im, m_re, m_im, x_re, x_im):
        return b_re + m_re * x_re - m_im * x_im, b_im + m_re * x_im + m_im * x_re

    hs = []
    for a in range(N_OCT):
        acc = None
        for sp in range(S5_QP):
            part = jnp.dot(xp[sp][a], winp_ref[sp, a], preferred_element_type=F32)
            acc = part if acc is None else acc + part
        blk = lambda v: (acc[S5_SEG * v:S5_SEG * (v + 1), :OCT_STATE], acc[S5_SEG * v:S5_SEG * (v + 1), OCT_STATE:])
        m_re, m_im = ptre_ref[a, 1:2, :], ptim_ref[a, 1:2, :]
        loc = [blk(0)]
        for v in range(1, S5_NV):
            loc.append(cmul_add(*blk(v), m_re, m_im, *loc[-1]))
        l_re, l_im = ptre_ref[a, S5_NV:S5_NV + 1, :], ptim_ref[a, S5_NV:S5_NV + 1, :]
        c_re, c_im = cre_scr[a:a + 1, :], cim_scr[a:a + 1, :]
        carry = []
        for r in range(S5_SEG):
            carry.append((c_re, c_im))
            c_re, c_im = cmul_add(loc[-1][0][r:r + 1, :], loc[-1][1][r:r + 1, :], l_re, l_im, c_re, c_im)
        cre_scr[a:a + 1, :] = c_re
        cim_scr[a:a + 1, :] = c_im
        cs_re = jnp.concatenate([c[0] for c in carry], axis=0)
        cs_im = jnp.concatenate([c[1] for c in carry], axis=0)
        ent = [(cs_re, cs_im)]
        for v in range(S5_NV - 1):
            ent.append(cmul_add(*loc[v], ptre_ref[a, v + 1:v + 2, :], ptim_ref[a, v + 1:v + 2, :], cs_re, cs_im))
        hs.append(jnp.concatenate([jnp.concatenate([e[0] for e in ent], axis=0),
                                   jnp.concatenate([e[1] for e in ent], axis=0)], axis=1).astype(BF16))

    def pair_tile(d, a):
        below = kb_ref[2 * d - 1, a] if d > 0 else jnp.zeros((LANES, LANES), BF16)
        return jnp.concatenate([jnp.concatenate([kb_ref[2 * d, a], kb_ref[2 * d + 1, a]], axis=1),
                                jnp.concatenate([below, kb_ref[2 * d, a]], axis=1)], axis=0)

    for a in range(N_OCT):
        tiles = [pair_tile(d, a) for d in range(S5_QP)]
        for tp in range(S5_QP):
            acc = jnp.dot(hs[a], woutp_ref[tp, a], preferred_element_type=F32)
            for sp in range(tp + 1):
                acc = acc + jnp.dot(xp[sp][a], tiles[tp - sp], preferred_element_type=F32)
            for v in range(S5_NV):
                rows = slice(S5_SEG * v, S5_SEG * (v + 1))
                y_scrs[a][pl.ds(2 * tp + S5_Q * v, S5_SEG, stride=S5_SEG_PITCH), :] = acc[rows, :LANES]
                y_scrs[a][pl.ds(2 * tp + 1 + S5_Q * v, S5_SEG, stride=S5_SEG_PITCH), :] = acc[rows, LANES:]
        d_a = d_ref[:, LANES * a:LANES * (a + 1)]
        for r in range(S5_SEG):
            prow = slice(S5_SEG_PITCH * r, S5_SEG_PITCH * r + S5_SEG_ROWS)
            y_scrs[a][prow, :] = _gelu_tanh(y_scrs[a][prow, :] + d_a * u_refs[a][prow, :])

    unpad = lambda ref: jnp.concatenate(
        [ref[S5_SEG_PITCH * r:S5_SEG_PITCH * r + S5_SEG_ROWS, :] for r in range(S5_SEG)], axis=0)
    y = jnp.concatenate([unpad(r) for r in y_scrs], axis=1)
    gate = jnp.dot(y.astype(BF16), wglu_ref[...].astype(BF16), preferred_element_type=F32) + bglu_ref[...]
    y = y * _sigmoid(gate)
    z = z_ref[...].astype(F32)
    o_ref[...] = (y * (z * _sigmoid(z))).astype(BF16)


def _s5(u32, pbf, kb, winp, woutp, ptre, ptim, d_row, w_glu, b_glu, batch, seqlen):
    nb = seqlen // S5_TB
    t = batch * seqlen
    u_tile = lambda a: pl.BlockSpec((S5_SEG * S5_SEG_PITCH, LANES), lambda b, i, a=a: (b * nb + i, a))
    return pl.pallas_call(
        _s5_kernel,
        grid=(batch, nb),
        in_specs=[u_tile(a) for a in range(N_OCT)] + [
            pl.BlockSpec((S5_TB, SSM_WIDTH), lambda b, i: (b * nb + i, PBF_ZA)),
            _const_spec(kb.shape), _const_spec(winp.shape), _const_spec(woutp.shape),
            _const_spec(ptre.shape), _const_spec(ptim.shape),
            _const_spec((1, SSM_WIDTH)), _const_spec((SSM_WIDTH, SSM_WIDTH)), _const_spec((1, SSM_WIDTH)),
        ],
        out_specs=pl.BlockSpec((S5_TB, SSM_WIDTH), lambda b, i: (b * nb + i, 0)),
        out_shape=jax.ShapeDtypeStruct((t, SSM_WIDTH), BF16),
        scratch_shapes=[pltpu.VMEM((S5_SEG * S5_SEG_PITCH, LANES), F32)] * N_OCT + [
            pltpu.VMEM((SUBLANES, OCT_STATE), F32),
            pltpu.VMEM((SUBLANES, OCT_STATE), F32),
        ],
        compiler_params=pltpu.CompilerParams(
            dimension_semantics=("arbitrary", "arbitrary"), vmem_limit_bytes=VMEM_LIMIT),
        name="s5",
    )(u32, u32, u32, u32, pbf, kb, winp, woutp, ptre, ptim, d_row, w_glu, b_glu)


def _stage_weights(w_hbm, wbf_s, stage_s, sem):
    n_chunks = IN_WIDTH // W_CHUNK

    def chunk_copy(c):
        return pltpu.make_async_copy(w_hbm.at[:, pl.ds(c * W_CHUNK, W_CHUNK)], stage_s.at[c % 2], sem.at[c % 2])

    chunk_copy(0).start()
    for c in range(n_chunks):
        if c + 1 < n_chunks:
            chunk_copy(c + 1).start()
        chunk_copy(c).wait()
        wbf_s[:, c * W_CHUNK:(c + 1) * W_CHUNK] = stage_s[c % 2].astype(BF16)


def _proj_hgrn_kernel(x_ref, nw_ref, w_hbm, lbl_ref, hnw_ref, ou_ref, or_ref, o_ref,
                      wbf_s, stage_s, stage_sem, ph_s, q0_s, qc_s, kt_s, k0_s, k1_s, k2_s, ke_s, v_s, dec_s, oacc_s,
                      st_s, upd_s, sc_s, am_s,
                      *, blocks_per_seq):
    @pl.when(pl.program_id(0) == 0)
    def _():
        _stage_weights(w_hbm, wbf_s, stage_s, stage_sem)

    @pl.when(pl.program_id(0) % blocks_per_seq == 0)
    def _():
        st_s[...] = jnp.zeros_like(st_s)

    x = x_ref[...]
    ms = jnp.mean(x * x, axis=-1, keepdims=True)
    xn = (x * lax.rsqrt(ms + NORM_EPS) * nw_ref[...]).astype(BF16)
    proj = lambda lo, hi: jnp.dot(xn, wbf_s[:, lo:hi], preferred_element_type=F32)
    ph_s[...] = proj(COL_H, COL_G)
    u = proj(0, COL_ZA)
    for r in range(HG_TB // S5_SEG_ROWS):
        ou_ref[S5_SEG_PITCH * r:S5_SEG_PITCH * r + S5_SEG_ROWS, :] = u[S5_SEG_ROWS * r:S5_SEG_ROWS * (r + 1), :]
        ou_ref[S5_SEG_PITCH * r + S5_SEG_ROWS:S5_SEG_PITCH * (r + 1), :] = jnp.zeros((SUBLANES, SSM_WIDTH), F32)
    or_ref[:, :2 * D_MODEL] = proj(COL_G, IN_WIDTH).astype(BF16)
    or_ref[:, 2 * D_MODEL:] = proj(COL_ZA, COL_H).astype(BF16)
    col = lambda k: ph_s[:, HGRN_WIDTH * k:HGRN_WIDTH * (k + 1)]

    logits = lbl_ref[...]
    e = jnp.exp(logits - jnp.max(logits, axis=0, keepdims=True))
    lb = (e / jnp.sum(e, axis=0, keepdims=True))[0:1, :]

    q = col(PH_Q)
    qf = q * _sigmoid(q)
    forget = lb + (1.0 - lb) * _sigmoid(col(PH_F))
    lf = jnp.log(forget)
    key = 1.0 - forget

    row = lax.broadcasted_iota(jnp.int32, (HG_TB, HGRN_WIDTH), 0)
    r_sub = row % HG_SUB
    r_ch = row % HG_CH

    def down(x, d):
        return pltpu.roll(x, d, 0)

    def up(x, d):
        return pltpu.roll(x, HG_TB - d, 0)

    a = lf
    d = 1
    while d < HG_SUB:
        a = a + jnp.where(r_sub >= d, down(a, d), 0.0)
        d *= 2
    a3 = a.reshape(HG_TB // HG_SUB, HG_SUB, HGRN_WIDTH)
    tsub = jnp.broadcast_to(a3[:, HG_SUB - 1:HG_SUB, :], a3.shape).reshape(HG_TB, HGRN_WIDTH)
    n_sub = HG_CH // HG_SUB
    prev = [jnp.where(r_ch >= HG_SUB * k, down(tsub, HG_SUB * k), 0.0) for k in range(1, n_sub)]
    nxt = [jnp.where(r_ch < HG_CH - HG_SUB * k, up(tsub, HG_SUB * k), 0.0) for k in range(1, n_sub)]
    eprev = prev[0] + prev[1] + prev[2]
    enext = nxt[0] + nxt[1] + nxt[2]
    suf = tsub - a

    q0 = qf * jnp.exp(a)
    q0_s[...] = q0.astype(BF16)
    qc_s[...] = (q0 * jnp.exp(eprev)).astype(BF16)
    kt_s[...] = (key * jnp.exp(-a)).astype(BF16)
    k0 = key * jnp.exp(suf)
    k0_s[...] = k0.astype(BF16)
    k1 = k0 * jnp.exp(nxt[0])
    k1_s[...] = k1.astype(BF16)
    k2 = k1 * jnp.exp(nxt[1])
    k2_s[...] = k2.astype(BF16)
    ke_s[...] = (k2 * jnp.exp(nxt[2])).astype(BF16)
    dec_s[...] = jnp.exp(eprev + tsub + enext)
    v_s[...] = col(PH_I).astype(BF16)

    tq = lax.broadcasted_iota(jnp.int32, (HG_CH, n_sub * HG_CH), 0)
    cc = lax.broadcasted_iota(jnp.int32, (HG_CH, n_sub * HG_CH), 1)
    cls = cc // HG_CH
    ts = cc % HG_CH
    bi = tq // HG_SUB
    bj = ts // HG_SUB
    mask4 = ((cls == 0) & (bi == bj) & (ts <= tq)) | ((cls > 0) & ((bi - bj) == cls))

    n_ch = HG_TB // HG_CH
    half = n_sub * HG_CH // 2
    units = [(c, h, slice(c * HG_CH, (c + 1) * HG_CH), slice(HEAD_DIM * h, HEAD_DIM * (h + 1)))
             for c in range(n_ch) for h in range(HEADS)]
    for c, h, rows, ls in units:
        kcat = jnp.concatenate([kt_s[rows, ls], k0_s[rows, ls], k1_s[rows, ls], k2_s[rows, ls]], axis=0)
        sc_s[c * HEADS + h] = lax.dot_general(q0_s[rows, ls], kcat, (((1,), (1,)), ((), ())),
                                              preferred_element_type=F32)
    for c, h, rows, ls in units:
        upd_s[c * HEADS + h] = lax.dot_general(v_s[rows, ls], ke_s[rows, ls], (((0,), (0,)), ((), ())),
                                               preferred_element_type=F32)
    for c, h, rows, ls in units:
        sc = jnp.where(mask4, sc_s[c * HEADS + h], 0.0)
        am_s[c * HEADS + h] = (sc[:, :half] + sc[:, half:]).astype(BF16)
    for c, h, rows, ls in units:
        vv = v_s[rows, ls]
        oacc_s[rows, ls] = jnp.dot(am_s[c * HEADS + h], jnp.concatenate([vv, vv], axis=0),
                                   preferred_element_type=F32)
    for h in range(HEADS):
        ls = slice(HEAD_DIM * h, HEAD_DIM * (h + 1))
        st = st_s[h]
        for c in range(n_ch):
            rows = slice(c * HG_CH, (c + 1) * HG_CH)
            oacc_s[rows, ls] += lax.dot_general(qc_s[rows, ls], st.astype(BF16), (((1,), (1,)), ((), ())),
                                                preferred_element_type=F32)
            st = st * dec_s[c * HG_CH:c * HG_CH + 1, ls] + upd_s[c * HEADS + h]
        st_s[h] = st

    o = oacc_s[...] * _sigmoid(col(PH_OG))
    parts = []
    for h in range(HEADS):
        oh = o[:, HEAD_DIM * h:HEAD_DIM * (h + 1)]
        ms = jnp.mean(oh * oh, axis=-1, keepdims=True)
        parts.append(oh * lax.rsqrt(ms + NORM_EPS))
    o = jnp.concatenate(parts, axis=1) * hnw_ref[...]
    z = col(PH_ZB)
    o_ref[...] = (o * (z * _sigmoid(z))).astype(BF16)


def _proj_hgrn(x2, norm_w, w, lb_logits, hgrn_norm_w, seqlen):
    t = x2.shape[0]
    wide = (HG_TB, HGRN_WIDTH)
    units = HG_TB // HG_CH * HEADS
    n_cls = HG_CH // HG_SUB
    row_blk = lambda w: pl.BlockSpec((HG_TB, w), lambda i: (i, 0))
    u_rows = HG_TB // S5_SEG_ROWS * S5_SEG_PITCH
    return pl.pallas_call(
        functools.partial(_proj_hgrn_kernel, blocks_per_seq=seqlen // HG_TB),
        grid=(t // HG_TB,),
        in_specs=[
            row_blk(D_MODEL), _const_spec((1, D_MODEL)),
            pl.BlockSpec(memory_space=pl.ANY),
            _const_spec(lb_logits.shape), _const_spec((1, HGRN_WIDTH)),
        ],
        out_specs=[pl.BlockSpec((u_rows, SSM_WIDTH), lambda i: (i, 0)), row_blk(PBF_WIDTH), row_blk(HGRN_WIDTH)],
        out_shape=[
            jax.ShapeDtypeStruct((t // HG_TB * u_rows, SSM_WIDTH), F32),
            jax.ShapeDtypeStruct((t, PBF_WIDTH), BF16),
            jax.ShapeDtypeStruct((t, HGRN_WIDTH), BF16),
        ],
        scratch_shapes=[
            pltpu.VMEM((D_MODEL, IN_WIDTH), BF16), pltpu.VMEM((2, D_MODEL, W_CHUNK), F32),
            pltpu.SemaphoreType.DMA((2,)),
            pltpu.VMEM((HG_TB, PH_WIDTH), F32)] + [pltpu.VMEM(wide, BF16)] * 8 + [
            pltpu.VMEM(wide, F32), pltpu.VMEM(wide, F32),
            pltpu.VMEM((HEADS, HEAD_DIM, HEAD_DIM), F32),
            pltpu.VMEM((units, HEAD_DIM, HEAD_DIM), F32),
            pltpu.VMEM((units, HG_CH, n_cls * HG_CH), F32),
            pltpu.VMEM((units, HG_CH, n_cls * HG_CH // 2), BF16)],
        compiler_params=pltpu.CompilerParams(
            dimension_semantics=("arbitrary",), vmem_limit_bytes=VMEM_LIMIT),
        name="proj_hgrn",
    )(x2, norm_w, w, lb_logits, hgrn_norm_w)


def _merge_kernel(x_ref, ya_ref, yb_ref, ga_ref, gb_ref, wpa_ref, wpb_ref, wo_ref, fnw_ref, o_ref):
    pa = jnp.dot(ya_ref[...], wpa_ref[...].astype(BF16), preferred_element_type=F32)
    pb = jnp.dot(yb_ref[...], wpb_ref[...].astype(BF16), preferred_element_type=F32)
    merged = _sigmoid(ga_ref[...].astype(F32)) * pa + _sigmoid(gb_ref[...].astype(F32)) * pb
    h = x_ref[...] + jnp.dot(merged.astype(BF16), wo_ref[...].astype(BF16), preferred_element_type=F32)
    ms = jnp.mean(h * h, axis=-1, keepdims=True)
    o_ref[...] = h * lax.rsqrt(ms + NORM_EPS) * fnw_ref[...]


def _merge(x2, ya, yb, pbf, w_pa, w_pb, w_out, fnw):
    t = x2.shape[0]
    return pl.pallas_call(
        _merge_kernel,
        grid=(t // TM_OUT,),
        in_specs=[
            pl.BlockSpec((TM_OUT, D_MODEL), lambda i: (i, 0)),
            pl.BlockSpec((TM_OUT, SSM_WIDTH), lambda i: (i, 0)),
            pl.BlockSpec((TM_OUT, HGRN_WIDTH), lambda i: (i, 0)),
            pl.BlockSpec((TM_OUT, D_MODEL), lambda i: (i, 0)),
            pl.BlockSpec((TM_OUT, D_MODEL), lambda i: (i, 1)),
            _const_spec((SSM_WIDTH, D_MODEL)), _const_spec((HGRN_WIDTH, D_MODEL)),
            _const_spec((D_MODEL, D_MODEL)), _const_spec((1, D_MODEL)),
        ],
        out_specs=pl.BlockSpec((TM_OUT, D_MODEL), lambda i: (i, 0)),
        out_shape=jax.ShapeDtypeStruct((t, D_MODEL), F32),
        compiler_params=pltpu.CompilerParams(
            dimension_semantics=("arbitrary",), vmem_limit_bytes=VMEM_LIMIT),
        name="merge",
    )(x2, ya, yb, pbf, pbf, w_pa, w_pb, w_out, fnw)


def kernel(x, norm_w, w_in, ssm_lambda_re, ssm_lambda_im, ssm_b_re, ssm_b_im, ssm_c_re, ssm_c_im, ssm_d,
           ssm_log_dt, ssm_w_glu, ssm_b_glu, hgrn_lb_logits, hgrn_norm_w, w_proj_a, w_proj_b, w_out,
           final_norm_w):
    batch, seqlen, _ = x.shape
    assert norm_w.shape[0] == 1, "single-layer block"
    assert seqlen % S5_TB == 0 and seqlen % HG_TB == 0 and HG_TB % S5_SEG_ROWS == 0
    x2 = x.reshape(batch * seqlen, D_MODEL)
    u32, pbf, yb = _proj_hgrn(x2, norm_w[0][None, :], w_in[0], hgrn_lb_logits,
                              hgrn_norm_w[0][None, :], seqlen)

    kb, winp, woutp, ptre, ptim = _s5_prep(ssm_lambda_re[0], ssm_lambda_im[0], ssm_b_re[0], ssm_b_im[0],
                                           ssm_c_re[0], ssm_c_im[0], ssm_log_dt[0])
    ya = _s5(u32, pbf, kb, winp, woutp, ptre, ptim, ssm_d[0].reshape(1, SSM_WIDTH),
             ssm_w_glu[0], ssm_b_glu[0][None, :], batch, seqlen)
    out = _merge(x2, ya, yb, pbf, w_proj_a[0], w_proj_b[0], w_out[0], final_norm_w[None, :])
    return out.reshape(batch, seqlen, D_MODEL)
```

```python
import functools

import jax
import jax.numpy as jnp
from jax import lax
from jax.experimental import pallas as pl
from jax.experimental.pallas import tpu as pltpu

F32 = jnp.float32
BF16 = jnp.bfloat16

D_MODEL = 1024
SSM_WIDTH = 512
SSM_GROUP = 16
SSM_GROUPS = 32
SSM_STATE = 64
HGRN_WIDTH = 512
HEAD_DIM = 128
HEADS = 4
NORM_EPS = 1e-6
LAMBDA_RE_MAX = -1e-4

IN_WIDTH = 2 * SSM_WIDTH + 5 * HGRN_WIDTH + 2 * D_MODEL
COL_ZA, COL_H, COL_G = SSM_WIDTH, 2 * SSM_WIDTH, 2 * SSM_WIDTH + 5 * HGRN_WIDTH
PH_WIDTH = 5 * HGRN_WIDTH
PH_Q, PH_F, PH_I, PH_OG, PH_ZB = 0, 1, 2, 3, 4
PBF_WIDTH = 2 * D_MODEL + SSM_WIDTH
PBF_ZA = 4

LANES = 128
SUBLANES = 8
OCT = LANES // SSM_GROUP
N_OCT = SSM_GROUPS // OCT
OCT_STATE = OCT * SSM_STATE

S5_Q = 4
S5_QP = S5_Q // 2
S5_TB = 2048
S5_NCH = S5_TB // S5_Q
S5_SEG = SUBLANES
S5_NV = S5_NCH // S5_SEG
S5_PT_ROWS = S5_NV + SUBLANES
S5_SEG_ROWS = S5_Q * S5_NV
S5_SEG_PITCH = S5_SEG_ROWS + SUBLANES
HG_TB = 512
HG_CH = 64
HG_SUB = 16
W_CHUNK = 512
TM_OUT = 1024

V7X_VMEM_BYTES = 64 * 1024 * 1024
VMEM_LIMIT = V7X_VMEM_BYTES * 7 // 8


def _sigmoid(x):
    return 0.5 * jnp.tanh(0.5 * x) + 0.5


def _const_spec(shape):
    nd = len(shape)
    return pl.BlockSpec(shape, lambda *_: (0,) * nd, pipeline_mode=pl.Buffered(1))


def _disc(lam_re, lam_im, log_dt):
    return jnp.minimum(lam_re, LAMBDA_RE_MAX), lam_im, jnp.exp(log_dt)


def _cpow(lr, li, dt, k):
    mag = jnp.exp(k * (lr * dt))
    ang = k * (li * dt)
    return mag * jnp.cos(ang), mag * jnp.sin(ang)


def _split_bf16(x):
    hi = x.astype(BF16)
    return hi, (x - hi.astype(F32)).astype(BF16)


def _dot3(a, b_hi, b_lo):
    a_hi, a_lo = _split_bf16(a)
    d = lambda p, q: jnp.dot(p, q, preferred_element_type=F32)
    return d(a_hi, b_hi) + (d(a_hi, b_lo) + d(a_lo, b_hi))


def _cmul(a_re, a_im, b_re, b_im):
    return a_re * b_re - a_im * b_im, a_re * b_im + a_im * b_re


def _s5_prep_kernel(lam_c, b_t, c_n, lam_n, lam_r,
                    kb_ref, win_ref, wout_ref, ptre_ref, ptim_ref, pw_s, pm_s, xk_s):
    @pl.when(pl.program_id(0) == 0)
    def _():
        lr, li, dt = _disc(lam_c[0], lam_c[1], lam_c[2])
        ab_re, ab_im = _cpow(lr, li, dt, 1.0)
        den = lr * lr + li * li
        nr = ab_re - 1.0
        pw_s[0] = (nr * lr + ab_im * li) / den
        pw_s[1] = (ab_im * lr - nr * li) / den
        pw_s[2] = jnp.ones_like(ab_re)
        pw_s[3] = jnp.zeros_like(ab_re)
        pw_s[4] = ab_re
        pw_s[5] = ab_im
        m_re, m_im = _cpow(*_disc(lam_n[0], lam_n[1], lam_n[2]), 1.0)
        pm_s[0] = m_re
        pm_s[1] = m_im
        pm_s[2] = m_re
        pm_s[3] = m_im
        j = lax.broadcasted_iota(jnp.int32, (S5_PT_ROWS, OCT_STATE), 0).astype(F32) * float(S5_Q)
        for a in range(N_OCT):
            t_re, t_im = _cpow(*_disc(lam_r[0, a:a + 1, :], lam_r[1, a:a + 1, :], lam_r[2, a:a + 1, :]), j)
            ptre_ref[a] = t_re
            ptim_ref[a] = t_im

    col = lax.broadcasted_iota(jnp.int32, (SSM_GROUP, SSM_WIDTH), 1)

    def strip(g, carry):
        rows = pl.ds(pl.multiple_of(g * SSM_GROUP, SSM_GROUP), SSM_GROUP)
        grow = pl.ds(g, 1)
        coef_re, coef_im = pw_s[0, grow, :], pw_s[1, grow, :]
        p_re, p_im = pw_s[2, grow, :], pw_s[3, grow, :]
        bb_re, bb_im = _cmul(coef_re, coef_im, b_t[0, rows, :], b_t[1, rows, :])
        x_re, x_im = _cmul(bb_re, bb_im, p_re, p_im)
        xk_s[0, rows, :] = x_re[:, :SSM_STATE]
        xk_s[1, rows, :] = x_im[:, :SSM_STATE]
        a = g // OCT
        lrows = pl.ds(pl.multiple_of((g % OCT) * SSM_GROUP, SSM_GROUP), SSM_GROUP)
        m_in = col // SSM_STATE == g % OCT
        win_ref[0, a, 0, lrows, :] = jnp.concatenate(
            [jnp.where(m_in, x_re, 0.0), jnp.where(m_in, x_im, 0.0)], axis=1).astype(BF16)
        return carry

    lax.fori_loop(0, SSM_GROUPS, strip, 0, unroll=4)

    cr, ci = c_n[0], c_n[1]
    kfull = _dot3(xk_s[0], *_split_bf16(cr)) - _dot3(xk_s[1], *_split_bf16(ci))
    rowk = lax.broadcasted_iota(jnp.int32, (LANES, LANES), 0)
    colk = lax.broadcasted_iota(jnp.int32, (LANES, LANES), 1)
    m_k = rowk // SSM_GROUP == colk // SSM_GROUP
    for a in range(N_OCT):
        ls = slice(LANES * a, LANES * (a + 1))
        kb_ref[0, a] = jnp.where(m_k, kfull[ls, ls], 0.0).astype(BF16)

    p1_re, p1_im = pm_s[0], pm_s[1]
    w_re, w_im = _cmul(cr, ci, p1_re, p1_im)
    rowo = lax.broadcasted_iota(jnp.int32, (OCT_STATE, LANES), 0)
    colo = lax.broadcasted_iota(jnp.int32, (OCT_STATE, LANES), 1)
    m_out = rowo // SSM_STATE == colo // SSM_GROUP
    for a in range(N_OCT):
        ls = slice(LANES * a, LANES * (a + 1))
        wout_ref[0, a, :OCT_STATE, :] = jnp.where(m_out, jnp.tile(w_re[:, ls], (OCT, 1)), 0.0).astype(BF16)
        wout_ref[0, a, OCT_STATE:, :] = jnp.where(m_out, jnp.tile(-w_im[:, ls], (OCT, 1)), 0.0).astype(BF16)

    pw_s[2], pw_s[3] = _cmul(pw_s[2], pw_s[3], pw_s[4], pw_s[5])
    pm_s[0], pm_s[1] = _cmul(p1_re, p1_im, pm_s[2], pm_s[3])


def _s5_prep(lam_re, lam_im, b_re, b_im, c_re, c_im, log_dt):
    g, n, p = SSM_GROUPS, SSM_STATE, SSM_GROUP
    lam = jnp.stack([lam_re, lam_im, jnp.broadcast_to(log_dt[:, None], (g, n))])
    b = jnp.stack([b_re, b_im])
    c = jnp.stack([c_re, c_im])
    args = (jnp.tile(lam, (1, 1, OCT)),
            jnp.tile(b.transpose(0, 1, 3, 2).reshape(2, g * p, n), (1, 1, OCT)),
            c.transpose(0, 3, 1, 2).reshape(2, n, g * p),
            jnp.repeat(lam.transpose(0, 2, 1), p, axis=2),
            lam.reshape(3, N_OCT, OCT_STATE))
    mat = (S5_QP, N_OCT, 2, LANES, 2 * OCT_STATE)
    blk = (1, N_OCT, 1, LANES, 2 * OCT_STATE)
    tab = (N_OCT, S5_PT_ROWS, OCT_STATE)
    kb, win, woutp, ptre, ptim = pl.pallas_call(
        _s5_prep_kernel,
        grid=(S5_Q,),
        in_specs=[_const_spec(a.shape) for a in args],
        out_specs=[
            pl.BlockSpec((1, N_OCT, LANES, LANES), lambda t: (t, 0, 0, 0)),
            pl.BlockSpec(blk, lambda t: ((S5_Q - 1 - t) // 2, 0, (S5_Q - 1 - t) % 2, 0, 0)),
            pl.BlockSpec((1, N_OCT, 2 * OCT_STATE, LANES), lambda t: (t // 2, 0, 0, t % 2)),
            pl.BlockSpec(tab, lambda t: (0, 0, 0)),
            pl.BlockSpec(tab, lambda t: (0, 0, 0)),
        ],
        out_shape=[
            jax.ShapeDtypeStruct((S5_Q, N_OCT, LANES, LANES), BF16),
            jax.ShapeDtypeStruct(mat, BF16),
            jax.ShapeDtypeStruct((S5_QP, N_OCT, 2 * OCT_STATE, 2 * LANES), BF16),
            jax.ShapeDtypeStruct(tab, F32),
            jax.ShapeDtypeStruct(tab, F32),
        ],
        scratch_shapes=[pltpu.VMEM((6, SSM_GROUPS, SSM_WIDTH), F32),
                        pltpu.VMEM((4, SSM_STATE, SSM_WIDTH), F32),
                        pltpu.VMEM((2, SSM_WIDTH, SSM_STATE), F32)],
        compiler_params=pltpu.CompilerParams(
            dimension_semantics=("arbitrary",), vmem_limit_bytes=VMEM_LIMIT),
        name="s5_prep",
    )(*args)

    winp = win.reshape(S5_QP, N_OCT, 2 * LANES, 2 * OCT_STATE)
    return kb, winp, woutp, ptre, ptim


def _gelu_tanh(x):
    c = 0.7978845608028654
    return x * (0.5 * (1.0 + jnp.tanh(c * (x + 0.044715 * (x * x * x)))))


def _s5_kernel(u0_ref, u1_ref, u2_ref, u3_ref, z_ref, kb_ref, winp_ref, woutp_ref, ptre_ref, ptim_ref,
               d_ref, wglu_ref, bglu_ref, o_ref, y0_s, y1_s, y2_s, y3_s, cre_scr, cim_scr):
    u_refs = (u0_ref, u1_ref, u2_ref, u3_ref)
    y_scrs = (y0_s, y1_s, y2_s, y3_s)

    @pl.when(pl.program_id(1) == 0)
    def _():
        cre_scr[...] = jnp.zeros_like(cre_scr)
        cim_scr[...] = jnp.zeros_like(cim_scr)

    def tok(a, s):
        return jnp.concatenate(
            [u_refs[a][pl.ds(s + S5_Q * v, S5_SEG, stride=S5_SEG_PITCH), :] for v in range(S5_NV)], axis=0)

    xp = [[jnp.concatenate([tok(a, 2 * sp), tok(a, 2 * sp + 1)], axis=1).astype(BF16) for a in range(N_OCT)]
          for sp in range(S5_QP)]

    def cmul_add(b_re, b_im, m_re, m_im, x_re, x_im):
        return b_re + m_re * x_re - m_im * x_im, b_im + m_re * x_im + m_im * x_re

    hs = []
    for a in range(N_OCT):
        acc = None
        for sp in range(S5_QP):
            part = jnp.dot(xp[sp][a], winp_ref[sp, a], preferred_element_type=F32)
            acc = part if acc is None else acc + part
        blk = lambda v: (acc[S5_SEG * v:S5_SEG * (v + 1), :OCT_STATE], acc[S5_SEG * v:S5_SEG * (v + 1), OCT_STATE:])
        m_re, m_im = ptre_ref[a, 1:2, :], ptim_ref[a, 1:2, :]
        loc = [blk(0)]
        for v in range(1, S5_NV):
            loc.append(cmul_add(*blk(v), m_re, m_im, *loc[-1]))
        l_re, l_im = ptre_ref[a, S5_NV:S5_NV + 1, :], ptim_ref[a, S5_NV:S5_NV + 1, :]
        c_re, c_im = cre_scr[a:a + 1, :], cim_scr[a:a + 1, :]
        carry = []
        for r in range(S5_SEG):
            carry.append((c_re, c_im))
            c_re, c_im = cmul_add(loc[-1][0][r:r + 1, :], loc[-1][1][r:r + 1, :], l_re, l_im, c_re, c_im)
        cre_scr[a:a + 1, :] = c_re
        cim_scr[a:a + 1, :] = c_im
        cs_re = jnp.concatenate([c[0] for c in carry], axis=0)
        cs_im = jnp.concatenate([c[1] for c in carry], axis=0)
        ent = [(cs_re, cs_im)]
        for v in range(S5_NV - 1):
            ent.append(cmul_add(*loc[v], ptre_ref[a, v + 1:v + 2, :], ptim_ref[a, v + 1:v + 2, :], cs_re, cs_im))
        hs.append(jnp.concatenate([jnp.concatenate([e[0] for e in ent], axis=0),
                                   jnp.concatenate([e[1] for e in ent], axis=0)], axis=1).astype(BF16))

    def pair_tile(d, a):
        below = kb_ref[2 * d - 1, a] if d > 0 else jnp.zeros((LANES, LANES), BF16)
        return jnp.concatenate([jnp.concatenate([kb_ref[2 * d, a], kb_ref[2 * d + 1, a]], axis=1),
                                jnp.concatenate([below, kb_ref[2 * d, a]], axis=1)], axis=0)

    for a in range(N_OCT):
        tiles = [pair_tile(d, a) for d in range(S5_QP)]
        for tp in range(S5_QP):
            acc = jnp.dot(hs[a], woutp_ref[tp, a], preferred_element_type=F32)
            for sp in range(tp + 1):
                acc = acc + jnp.dot(xp[sp][a], tiles[tp - sp], preferred_element_type=F32)
            for v in range(S5_NV):
                rows = slice(S5_SEG * v, S5_SEG * (v + 1))
                y_scrs[a][pl.ds(2 * tp + S5_Q * v, S5_SEG, stride=S5_SEG_PITCH), :] = acc[rows, :LANES]
                y_scrs[a][pl.ds(2 * tp + 1 + S5_Q * v, S5_SEG, stride=S5_SEG_PITCH), :] = acc[rows, LANES:]
        d_a = d_ref[:, LANES * a:LANES * (a + 1)]
        for r in range(S5_SEG):
            prow = slice(S5_SEG_PITCH * r, S5_SEG_PITCH * r + S5_SEG_ROWS)
            y_scrs[a][prow, :] = _gelu_tanh(y_scrs[a][prow, :] + d_a * u_refs[a][prow, :])

    unpad = lambda ref: jnp.concatenate(
        [ref[S5_SEG_PITCH * r:S5_SEG_PITCH * r + S5_SEG_ROWS, :] for r in range(S5_SEG)], axis=0)
    y = jnp.concatenate([unpad(r) for r in y_scrs], axis=1)
    gate = jnp.dot(y.astype(BF16), wglu_ref[...].astype(BF16), preferred_element_type=F32) + bglu_ref[...]
    y = y * _sigmoid(gate)
    z = z_ref[...].astype(F32)
    o_ref[...] = (y * (z * _sigmoid(z))).astype(BF16)


def _s5(u32, pbf, kb, winp, woutp, ptre, ptim, d_row, w_glu, b_glu, batch, seqlen):
    nb = seqlen // S5_TB
    t = batch * seqlen
    u_tile = lambda a: pl.BlockSpec((S5_SEG * S5_SEG_PITCH, LANES), lambda b, i, a=a: (b * nb + i, a))
    return pl.pallas_call(
        _s5_kernel,
        grid=(batch, nb),
        in_specs=[u_tile(a) for a in range(N_OCT)] + [
            pl.BlockSpec((S5_TB, SSM_WIDTH), lambda b, i: (b * nb + i, PBF_ZA)),
            _const_spec(kb.shape), _const_spec(winp.shape), _const_spec(woutp.shape),
            _const_spec(ptre.shape), _const_spec(ptim.shape),
            _const_spec((1, SSM_WIDTH)), _const_spec((SSM_WIDTH, SSM_WIDTH)), _const_spec((1, SSM_WIDTH)),
        ],
        out_specs=pl.BlockSpec((S5_TB, SSM_WIDTH), lambda b, i: (b * nb + i, 0)),
        out_shape=jax.ShapeDtypeStruct((t, SSM_WIDTH), BF16),
        scratch_shapes=[pltpu.VMEM((S5_SEG * S5_SEG_PITCH, LANES), F32)] * N_OCT + [
            pltpu.VMEM((SUBLANES, OCT_STATE), F32),
            pltpu.VMEM((SUBLANES, OCT_STATE), F32),
        ],
        compiler_params=pltpu.CompilerParams(
            dimension_semantics=("arbitrary", "arbitrary"), vmem_limit_bytes=VMEM_LIMIT),
        name="s5",
    )(u32, u32, u32, u32, pbf, kb, winp, woutp, ptre, ptim, d_row, w_glu, b_glu)


def _stage_weights(w_hbm, wbf_s, stage_s, sem):
    n_chunks = IN_WIDTH // W_CHUNK

    def chunk_copy(c):
        return pltpu.make_async_copy(w_hbm.at[:, pl.ds(c * W_CHUNK, W_CHUNK)], stage_s.at[c % 2], sem.at[c % 2])

    chunk_copy(0).start()
    for c in range(n_chunks):
        if c + 1 < n_chunks:
            chunk_copy(c + 1).start()
        chunk_copy(c).wait()
        wbf_s[:, c * W_CHUNK:(c + 1) * W_CHUNK] = stage_s[c % 2].astype(BF16)


def _proj_hgrn_kernel(x_ref, nw_ref, w_hbm, lbl_ref, hnw_ref, ou_ref, or_ref, o_ref,
                      wbf_s, stage_s, stage_sem, ph_s, q0_s, qc_s, kt_s, k0_s, k1_s, k2_s, ke_s, v_s, dec_s, oacc_s,
                      st_s, upd_s, sc_s, am_s,
                      *, blocks_per_seq):
    @pl.when(pl.program_id(0) == 0)
    def _():
        _stage_weights(w_hbm, wbf_s, stage_s, stage_sem)

    @pl.when(pl.program_id(0) % blocks_per_seq == 0)
    def _():
        st_s[...] = jnp.zeros_like(st_s)

    x = x_ref[...]
    ms = jnp.mean(x * x, axis=-1, keepdims=True)
    xn = (x * lax.rsqrt(ms + NORM_EPS) * nw_ref[...]).astype(BF16)
    proj = lambda lo, hi: jnp.dot(xn, wbf_s[:, lo:hi], preferred_element_type=F32)
    ph_s[...] = proj(COL_H, COL_G)
    u = proj(0, COL_ZA)
    for r in range(HG_TB // S5_SEG_ROWS):
        ou_ref[S5_SEG_PITCH * r:S5_SEG_PITCH * r + S5_SEG_ROWS, :] = u[S5_SEG_ROWS * r:S5_SEG_ROWS * (r + 1), :]
        ou_ref[S5_SEG_PITCH * r + S5_SEG_ROWS:S5_SEG_PITCH * (r + 1), :] = jnp.zeros((SUBLANES, SSM_WIDTH), F32)
    or_ref[:, :2 * D_MODEL] = proj(COL_G, IN_WIDTH).astype(BF16)
    or_ref[:, 2 * D_MODEL:] = proj(COL_ZA, COL_H).astype(BF16)
    col = lambda k: ph_s[:, HGRN_WIDTH * k:HGRN_WIDTH * (k + 1)]

    logits = lbl_ref[...]
    e = jnp.exp(logits - jnp.max(logits, axis=0, keepdims=True))
    lb = (e / jnp.sum(e, axis=0, keepdims=True))[0:1, :]

    q = col(PH_Q)
    qf = q * _sigmoid(q)
    forget = lb + (1.0 - lb) * _sigmoid(col(PH_F))
    lf = jnp.log(forget)
    key = 1.0 - forget

    row = lax.broadcasted_iota(jnp.int32, (HG_TB, HGRN_WIDTH), 0)
    r_sub = row % HG_SUB
    r_ch = row % HG_CH

    def down(x, d):
        return pltpu.roll(x, d, 0)

    def up(x, d):
        return pltpu.roll(x, HG_TB - d, 0)

    a = lf
    d = 1
    while d < HG_SUB:
        a = a + jnp.where(r_sub >= d, down(a, d), 0.0)
        d *= 2
    a3 = a.reshape(HG_TB // HG_SUB, HG_SUB, HGRN_WIDTH)
    tsub = jnp.broadcast_to(a3[:, HG_SUB - 1:HG_SUB, :], a3.shape).reshape(HG_TB, HGRN_WIDTH)
    n_sub = HG_CH // HG_SUB
    prev = [jnp.where(r_ch >= HG_SUB * k, down(tsub, HG_SUB * k), 0.0) for k in range(1, n_sub)]
    nxt = [jnp.where(r_ch < HG_CH - HG_SUB * k, up(tsub, HG_SUB * k), 0.0) for k in range(1, n_sub)]
    eprev = prev[0] + prev[1] + prev[2]
    enext = nxt[0] + nxt[1] + nxt[2]
    suf = tsub - a

    q0 = qf * jnp.exp(a)
    q0_s[...] = q0.astype(BF16)
    qc_s[...] = (q0 * jnp.exp(eprev)).astype(BF16)
    kt_s[...] = (key * jnp.exp(-a)).astype(BF16)
    k0 = key * jnp.exp(suf)
    k0_s[...] = k0.astype(BF16)
    k1 = k0 * jnp.exp(nxt[0])
    k1_s[...] = k1.astype(BF16)
    k2 = k1 * jnp.exp(nxt[1])
    k2_s[...] = k2.astype(BF16)
    ke_s[...] = (k2 * jnp.exp(nxt[2])).astype(BF16)
    dec_s[...] = jnp.exp(eprev + tsub + enext)
    v_s[...] = col(PH_I).astype(BF16)

    tq = lax.broadcasted_iota(jnp.int32, (HG_CH, n_sub * HG_CH), 0)
    cc = lax.broadcasted_iota(jnp.int32, (HG_CH, n_sub * HG_CH), 1)
    cls = cc // HG_CH
    ts = cc % HG_CH
    bi = tq // HG_SUB
    bj = ts // HG_SUB
    mask4 = ((cls == 0) & (bi == bj) & (ts <= tq)) | ((cls > 0) & ((bi - bj) == cls))

    n_ch = HG_TB // HG_CH
    half = n_sub * HG_CH // 2
    units = [(c, h, slice(c * HG_CH, (c + 1) * HG_CH), slice(HEAD_DIM * h, HEAD_DIM * (h + 1)))
             for c in range(n_ch) for h in range(HEADS)]
    for c, h, rows, ls in units:
        kcat = jnp.concatenate([kt_s[rows, ls], k0_s[rows, ls], k1_s[rows, ls], k2_s[rows, ls]], axis=0)
        sc_s[c * HEADS + h] = lax.dot_general(q0_s[rows, ls], kcat, (((1,), (1,)), ((), ())),
                                              preferred_element_type=F32)
    for c, h, rows, ls in units:
        upd_s[c * HEADS + h] = lax.dot_general(v_s[rows, ls], ke_s[rows, ls], (((0,), (0,)), ((), ())),
                                               preferred_element_type=F32)
    for c, h, rows, ls in units:
        sc = jnp.where(mask4, sc_s[c * HEADS + h], 0.0)
        am_s[c * HEADS + h] = (sc[:, :half] + sc[:, half:]).astype(BF16)
    for c, h, rows, ls in units:
        vv = v_s[rows, ls]
        oacc_s[rows, ls] = jnp.dot(am_s[c * HEADS + h], jnp.concatenate([vv, vv], axis=0),
                                   preferred_element_type=F32)
    for h in range(HEADS):
        ls = slice(HEAD_DIM * h, HEAD_DIM * (h + 1))
        st = st_s[h]
        for c in range(n_ch):
            rows = slice(c * HG_CH, (c + 1) * HG_CH)
            oacc_s[rows, ls] += lax.dot_general(qc_s[rows, ls], st.astype(BF16), (((1,), (1,)), ((), ())),
                                                preferred_element_type=F32)
            st = st * dec_s[c * HG_CH:c * HG_CH + 1, ls] + upd_s[c * HEADS + h]
        st_s[h] = st

    o = oacc_s[...] * _sigmoid(col(PH_OG))
    parts = []
    for h in range(HEADS):
        oh = o[:, HEAD_DIM * h:HEAD_DIM * (h + 1)]
        ms = jnp.mean(oh * oh, axis=-1, keepdims=True)
        parts.append(oh * lax.rsqrt(ms + NORM_EPS))
    o = jnp.concatenate(parts, axis=1) * hnw_ref[...]
    z = col(PH_ZB)
    o_ref[...] = (o * (z * _sigmoid(z))).astype(BF16)


def _proj_hgrn(x2, norm_w, w, lb_logits, hgrn_norm_w, seqlen):
    t = x2.shape[0]
    wide = (HG_TB, HGRN_WIDTH)
    units = HG_TB // HG_CH * HEADS
    n_cls = HG_CH // HG_SUB
    row_blk = lambda w: pl.BlockSpec((HG_TB, w), lambda i: (i, 0))
    u_rows = HG_TB // S5_SEG_ROWS * S5_SEG_PITCH
    return pl.pallas_call(
        functools.partial(_proj_hgrn_kernel, blocks_per_seq=seqlen // HG_TB),
        grid=(t // HG_TB,),
        in_specs=[
            row_blk(D_MODEL), _const_spec((1, D_MODEL)),
            pl.BlockSpec(memory_space=pl.ANY),
            _const_spec(lb_logits.shape), _const_spec((1, HGRN_WIDTH)),
        ],
        out_specs=[pl.BlockSpec((u_rows, SSM_WIDTH), lambda i: (i, 0)), row_blk(PBF_WIDTH), row_blk(HGRN_WIDTH)],
        out_shape=[
            jax.ShapeDtypeStruct((t // HG_TB * u_rows, SSM_WIDTH), F32),
            jax.ShapeDtypeStruct((t, PBF_WIDTH), BF16),
            jax.ShapeDtypeStruct((t, HGRN_WIDTH), BF16),
        ],
        scratch_shapes=[
            pltpu.VMEM((D_MODEL, IN_WIDTH), BF16), pltpu.VMEM((2, D_MODEL, W_CHUNK), F32),
            pltpu.SemaphoreType.DMA((2,)),
            pltpu.VMEM((HG_TB, PH_WIDTH), F32)] + [pltpu.VMEM(wide, BF16)] * 8 + [
            pltpu.VMEM(wide, F32), pltpu.VMEM(wide, F32),
            pltpu.VMEM((HEADS, HEAD_DIM, HEAD_DIM), F32),
            pltpu.VMEM((units, HEAD_DIM, HEAD_DIM), F32),
            pltpu.VMEM((units, HG_CH, n_cls * HG_CH), F32),
            pltpu.VMEM((units, HG_CH, n_cls * HG_CH // 2), BF16)],
        compiler_params=pltpu.CompilerParams(
            dimension_semantics=("arbitrary",), vmem_limit_bytes=VMEM_LIMIT),
        name="proj_hgrn",
    )(x2, norm_w, w, lb_logits, hgrn_norm_w)


def _merge_kernel(x_ref, ya_ref, yb_ref, ga_ref, gb_ref, wpa_ref, wpb_ref, wo_ref, fnw_ref, o_ref):
    pa = jnp.dot(ya_ref[...], wpa_ref[...].astype(BF16), preferred_element_type=F32)
    pb = jnp.dot(yb_ref[...], wpb_ref[...].astype(BF16), preferred_element_type=F32)
    merged = _sigmoid(ga_ref[...].astype(F32)) * pa + _sigmoid(gb_ref[...].astype(F32)) * pb
    h = x_ref[...] + jnp.dot(merged.astype(BF16), wo_ref[...].astype(BF16), preferred_element_type=F32)
    ms = jnp.mean(h * h, axis=-1, keepdims=True)
    o_ref[...] = h * lax.rsqrt(ms + NORM_EPS) * fnw_ref[...]


def _merge(x2, ya, yb, pbf, w_pa, w_pb, w_out, fnw):
    t = x2.shape[0]
    return pl.pallas_call(
        _merge_kernel,
        grid=(t // TM_OUT,),
        in_specs=[
            pl.BlockSpec((TM_OUT, D_MODEL), lambda i: (i, 0)),
            pl.BlockSpec((TM_OUT, SSM_WIDTH), lambda i: (i, 0)),
            pl.BlockSpec((TM_OUT, HGRN_WIDTH), lambda i: (i, 0)),
            pl.BlockSpec((TM_OUT, D_MODEL), lambda i: (i, 0)),
            pl.BlockSpec((TM_OUT, D_MODEL), lambda i: (i, 1)),
            _const_spec((SSM_WIDTH, D_MODEL)), _const_spec((HGRN_WIDTH, D_MODEL)),
            _const_spec((D_MODEL, D_MODEL)), _const_spec((1, D_MODEL)),
        ],
        out_specs=pl.BlockSpec((TM_OUT, D_MODEL), lambda i: (i, 0)),
        out_shape=jax.ShapeDtypeStruct((t, D_MODEL), F32),
        compiler_params=pltpu.CompilerParams(
            dimension_semantics=("arbitrary",), vmem_limit_bytes=VMEM_LIMIT),
        name="merge",
    )(x2, ya, yb, pbf, pbf, w_pa, w_pb, w_out, fnw)


def kernel(x, norm_w, w_in, ssm_lambda_re, ssm_lambda_im, ssm_b_re, ssm_b_im, ssm_c_re, ssm_c_im, ssm_d,
           ssm_log_dt, ssm_w_glu, ssm_b_glu, hgrn_lb_logits, hgrn_norm_w, w_proj_a, w_proj_b, w_out,
           final_norm_w):
    batch, seqlen, _ = x.shape
    assert norm_w.shape[0] == 1, "single-layer block"
    assert seqlen % S5_TB == 0 and seqlen % HG_TB == 0 and HG_TB % S5_SEG_ROWS == 0
    x2 = x.reshape(batch * seqlen, D_MODEL)
    u32, pbf, yb = _proj_hgrn(x2, norm_w[0][None, :], w_in[0], hgrn_lb_logits,
                              hgrn_norm_w[0][None, :], seqlen)

    kb, winp, woutp, ptre, ptim = _s5_prep(ssm_lambda_re[0], ssm_lambda_im[0], ssm_b_re[0], ssm_b_im[0],
                                           ssm_c_re[0], ssm_c_im[0], ssm_log_dt[0])
    ya = _s5(u32, pbf, kb, winp, woutp, ptre, ptim, ssm_d[0].reshape(1, SSM_WIDTH),
             ssm_w_glu[0], ssm_b_glu[0][None, :], batch, seqlen)
    out = _merge(x2, ya, yb, pbf, w_proj_a[0], w_proj_b[0], w_out[0], final_norm_w[None, :])
    return out.reshape(batch, seqlen, D_MODEL)
```

```python
import functools

import jax
import jax.numpy as jnp
from jax import lax
from jax.experimental import pallas as pl
from jax.experimental.pallas import tpu as pltpu

F32 = jnp.float32
BF16 = jnp.bfloat16

D_MODEL = 1024
SSM_WIDTH = 512
SSM_GROUP = 16
SSM_GROUPS = 32
SSM_STATE = 64
HGRN_WIDTH = 512
HEAD_DIM = 128
HEADS = 4
NORM_EPS = 1e-6
LAMBDA_RE_MAX = -1e-4

IN_WIDTH = 2 * SSM_WIDTH + 5 * HGRN_WIDTH + 2 * D_MODEL
COL_ZA, COL_H, COL_G = SSM_WIDTH, 2 * SSM_WIDTH, 2 * SSM_WIDTH + 5 * HGRN_WIDTH
PH_WIDTH = 5 * HGRN_WIDTH
PH_Q, PH_F, PH_I, PH_OG, PH_ZB = 0, 1, 2, 3, 4
PBF_WIDTH = 2 * D_MODEL + SSM_WIDTH
PBF_ZA = 4

LANES = 128
SUBLANES = 8
OCT = LANES // SSM_GROUP
N_OCT = SSM_GROUPS // OCT
OCT_STATE = OCT * SSM_STATE

S5_Q = 4
S5_QP = S5_Q // 2
S5_TB = 2048
S5_NCH = S5_TB // S5_Q
S5_SEG = SUBLANES
S5_NV = S5_NCH // S5_SEG
S5_PT_ROWS = S5_NV + SUBLANES
S5_SEG_ROWS = S5_Q * S5_NV
S5_SEG_PITCH = S5_SEG_ROWS + SUBLANES
HG_TB = 512
HG_CH = 64
HG_SUB = 16
W_CHUNK = 512
TM_OUT = 1024

V7X_VMEM_BYTES = 64 * 1024 * 1024
VMEM_LIMIT = V7X_VMEM_BYTES * 7 // 8


def _sigmoid(x):
    return 0.5 * jnp.tanh(0.5 * x) + 0.5


def _const_spec(shape):
    nd = len(shape)
    return pl.BlockSpec(shape, lambda *_: (0,) * nd, pipeline_mode=pl.Buffered(1))


def _disc(lam_re, lam_im, log_dt):
    return jnp.minimum(lam_re, LAMBDA_RE_MAX), lam_im, jnp.exp(log_dt)


def _cpow(lr, li, dt, k):
    mag = jnp.exp(k * (lr * dt))
    ang = k * (li * dt)
    return mag * jnp.cos(ang), mag * jnp.sin(ang)


def _split_bf16(x):
    hi = x.astype(BF16)
    return hi, (x - hi.astype(F32)).astype(BF16)


def _dot3(a, b_hi, b_lo):
    a_hi, a_lo = _split_bf16(a)
    d = lambda p, q: jnp.dot(p, q, preferred_element_type=F32)
    return d(a_hi, b_hi) + (d(a_hi, b_lo) + d(a_lo, b_hi))


def _cmul(a_re, a_im, b_re, b_im):
    return a_re * b_re - a_im * b_im, a_re * b_im + a_im * b_re


def _s5_prep_kernel(lam_c, b_t, c_n, lam_n, lam_r,
                    kb_ref, win_ref, wout_ref, ptre_ref, ptim_ref, pw_s, xk_s):
    lr, li, dt = _disc(lam_c[0], lam_c[1], lam_c[2])
    ab_re, ab_im = _cpow(lr, li, dt, 1.0)
    den = lr * lr + li * li
    nr = ab_re - 1.0
    pw_s[0] = (nr * lr + ab_im * li) / den
    pw_s[1] = (ab_im * lr - nr * li) / den
    m_re, m_im = _cpow(*_disc(lam_n[0], lam_n[1], lam_n[2]), 1.0)

    j = lax.broadcasted_iota(jnp.int32, (S5_PT_ROWS, OCT_STATE), 0).astype(F32) * float(S5_Q)
    for a in range(N_OCT):
        t_re, t_im = _cpow(*_disc(lam_r[0, a:a + 1, :], lam_r[1, a:a + 1, :], lam_r[2, a:a + 1, :]), j)
        ptre_ref[a] = t_re
        ptim_ref[a] = t_im

    col = lax.broadcasted_iota(jnp.int32, (SSM_GROUP, SSM_WIDTH), 1)
    rowk = lax.broadcasted_iota(jnp.int32, (LANES, LANES), 0)
    colk = lax.broadcasted_iota(jnp.int32, (LANES, LANES), 1)
    m_k = rowk // SSM_GROUP == colk // SSM_GROUP
    rowo = lax.broadcasted_iota(jnp.int32, (OCT_STATE, LANES), 0)
    colo = lax.broadcasted_iota(jnp.int32, (OCT_STATE, LANES), 1)
    m_out = rowo // SSM_STATE == colo // SSM_GROUP
    cr, ci = c_n[0], c_n[1]
    cr_split, ci_split = _split_bf16(cr), _split_bf16(ci)

    p_re, p_im = jnp.ones_like(ab_re), jnp.zeros_like(ab_re)
    p1_re, p1_im = m_re, m_im
    for tau in range(S5_Q):
        pw_s[2] = p_re
        pw_s[3] = p_im
        s_tok = S5_Q - 1 - tau

        def strip(g, carry):
            rows = pl.ds(pl.multiple_of(g * SSM_GROUP, SSM_GROUP), SSM_GROUP)
            grow = pl.ds(g, 1)
            bb_re, bb_im = _cmul(pw_s[0, grow, :], pw_s[1, grow, :], b_t[0, rows, :], b_t[1, rows, :])
            x_re, x_im = _cmul(bb_re, bb_im, pw_s[2, grow, :], pw_s[3, grow, :])
            xk_s[0, rows, :] = x_re[:, :SSM_STATE]
            xk_s[1, rows, :] = x_im[:, :SSM_STATE]
            lrows = pl.ds(pl.multiple_of((g % OCT) * SSM_GROUP, SSM_GROUP), SSM_GROUP)
            m_in = col // SSM_STATE == g % OCT
            win_ref[s_tok // 2, g // OCT, s_tok % 2, lrows, :] = jnp.concatenate(
                [jnp.where(m_in, x_re, 0.0), jnp.where(m_in, x_im, 0.0)], axis=1).astype(BF16)
            return carry

        lax.fori_loop(0, SSM_GROUPS, strip, 0, unroll=4)

        kfull = _dot3(xk_s[0], *cr_split) - _dot3(xk_s[1], *ci_split)
        w_re, w_im = _cmul(cr, ci, p1_re, p1_im)
        tcol = slice(LANES * (tau % 2), LANES * (tau % 2 + 1))
        for a in range(N_OCT):
            ls = slice(LANES * a, LANES * (a + 1))
            kb_ref[tau, a] = jnp.where(m_k, kfull[ls, ls], 0.0).astype(BF16)
            tiled = lambda w: jnp.where(m_out, jnp.tile(w[:, ls], (OCT, 1)), 0.0).astype(BF16)
            wout_ref[tau // 2, a, :OCT_STATE, tcol] = tiled(w_re)
            wout_ref[tau // 2, a, OCT_STATE:, tcol] = tiled(-w_im)

        p_re, p_im = _cmul(p_re, p_im, ab_re, ab_im)
        p1_re, p1_im = _cmul(p1_re, p1_im, m_re, m_im)


def _s5_prep(lam_re, lam_im, b_re, b_im, c_re, c_im, log_dt):
    g, n, p = SSM_GROUPS, SSM_STATE, SSM_GROUP
    lam = jnp.stack([lam_re, lam_im, jnp.broadcast_to(log_dt[:, None], (g, n))])
    b = jnp.stack([b_re, b_im])
    c = jnp.stack([c_re, c_im])
    args = (jnp.tile(lam, (1, 1, OCT)),
            jnp.tile(b.transpose(0, 1, 3, 2).reshape(2, g * p, n), (1, 1, OCT)),
            c.transpose(0, 3, 1, 2).reshape(2, n, g * p),
            jnp.repeat(lam.transpose(0, 2, 1), p, axis=2),
            lam.reshape(3, N_OCT, OCT_STATE))
    mat = (S5_QP, N_OCT, 2, LANES, 2 * OCT_STATE)
    tab = (N_OCT, S5_PT_ROWS, OCT_STATE)
    kb, win, woutp, ptre, ptim = pl.pallas_call(
        _s5_prep_kernel,
        out_shape=[
            jax.ShapeDtypeStruct((S5_Q, N_OCT, LANES, LANES), BF16),
            jax.ShapeDtypeStruct(mat, BF16),
            jax.ShapeDtypeStruct((S5_QP, N_OCT, 2 * OCT_STATE, 2 * LANES), BF16),
            jax.ShapeDtypeStruct(tab, F32),
            jax.ShapeDtypeStruct(tab, F32),
        ],
        scratch_shapes=[pltpu.VMEM((4, SSM_GROUPS, SSM_WIDTH), F32),
                        pltpu.VMEM((2, SSM_WIDTH, SSM_STATE), F32)],
        compiler_params=pltpu.CompilerParams(vmem_limit_bytes=VMEM_LIMIT),
        name="s5_prep",
    )(*args)

    winp = win.reshape(S5_QP, N_OCT, 2 * LANES, 2 * OCT_STATE)
    return kb, winp, woutp, ptre, ptim


def _gelu_tanh(x):
    c = 0.7978845608028654
    return x * (0.5 * (1.0 + jnp.tanh(c * (x + 0.044715 * (x * x * x)))))


def _s5_kernel(u0_ref, u1_ref, u2_ref, u3_ref, z_ref, kb_ref, winp_ref, woutp_ref, ptre_ref, ptim_ref,
               d_ref, wglu_ref, bglu_ref, o_ref, y0_s, y1_s, y2_s, y3_s, cre_scr, cim_scr):
    u_refs = (u0_ref, u1_ref, u2_ref, u3_ref)
    y_scrs = (y0_s, y1_s, y2_s, y3_s)

    @pl.when(pl.program_id(1) == 0)
    def _():
        cre_scr[...] = jnp.zeros_like(cre_scr)
        cim_scr[...] = jnp.zeros_like(cim_scr)

    def tok(a, s):
        return jnp.concatenate(
            [u_refs[a][pl.ds(s + S5_Q * v, S5_SEG, stride=S5_SEG_PITCH), :] for v in range(S5_NV)], axis=0)

    xp = [[jnp.concatenate([tok(a, 2 * sp), tok(a, 2 * sp + 1)], axis=1).astype(BF16) for a in range(N_OCT)]
          for sp in range(S5_QP)]

    def cmul_add(b_re, b_im, m_re, m_im, x_re, x_im):
        return b_re + m_re * x_re - m_im * x_im, b_im + m_re * x_im + m_im * x_re

    hs = []
    for a in range(N_OCT):
        acc = None
        for sp in range(S5_QP):
            part = jnp.dot(xp[sp][a], winp_ref[sp, a], preferred_element_type=F32)
            acc = part if acc is None else acc + part
        blk = lambda v: (acc[S5_SEG * v:S5_SEG * (v + 1), :OCT_STATE], acc[S5_SEG * v:S5_SEG * (v + 1), OCT_STATE:])
        m_re, m_im = ptre_ref[a, 1:2, :], ptim_ref[a, 1:2, :]
        loc = [blk(0)]
        for v in range(1, S5_NV):
            loc.append(cmul_add(*blk(v), m_re, m_im, *loc[-1]))
        l_re, l_im = ptre_ref[a, S5_NV:S5_NV + 1, :], ptim_ref[a, S5_NV:S5_NV + 1, :]
        c_re, c_im = cre_scr[a:a + 1, :], cim_scr[a:a + 1, :]
        carry = []
        for r in range(S5_SEG):
            carry.append((c_re, c_im))
            c_re, c_im = cmul_add(loc[-1][0][r:r + 1, :], loc[-1][1][r:r + 1, :], l_re, l_im, c_re, c_im)
        cre_scr[a:a + 1, :] = c_re
        cim_scr[a:a + 1, :] = c_im
        cs_re = jnp.concatenate([c[0] for c in carry], axis=0)
        cs_im = jnp.concatenate([c[1] for c in carry], axis=0)
        ent = [(cs_re, cs_im)]
        for v in range(S5_NV - 1):
            ent.append(cmul_add(*loc[v], ptre_ref[a, v + 1:v + 2, :], ptim_ref[a, v + 1:v + 2, :], cs_re, cs_im))
        hs.append(jnp.concatenate([jnp.concatenate([e[0] for e in ent], axis=0),
                                   jnp.concatenate([e[1] for e in ent], axis=0)], axis=1).astype(BF16))

    def pair_tile(d, a):
        below = kb_ref[2 * d - 1, a] if d > 0 else jnp.zeros((LANES, LANES), BF16)
        return jnp.concatenate([jnp.concatenate([kb_ref[2 * d, a], kb_ref[2 * d + 1, a]], axis=1),
                                jnp.concatenate([below, kb_ref[2 * d, a]], axis=1)], axis=0)

    for a in range(N_OCT):
        tiles = [pair_tile(d, a) for d in range(S5_QP)]
        for tp in range(S5_QP):
            acc = jnp.dot(hs[a], woutp_ref[tp, a], preferred_element_type=F32)
            for sp in range(tp + 1):
                acc = acc + jnp.dot(xp[sp][a], tiles[tp - sp], preferred_element_type=F32)
            for v in range(S5_NV):
                rows = slice(S5_SEG * v, S5_SEG * (v + 1))
                y_scrs[a][pl.ds(2 * tp + S5_Q * v, S5_SEG, stride=S5_SEG_PITCH), :] = acc[rows, :LANES]
                y_scrs[a][pl.ds(2 * tp + 1 + S5_Q * v, S5_SEG, stride=S5_SEG_PITCH), :] = acc[rows, LANES:]
        d_a = d_ref[:, LANES * a:LANES * (a + 1)]
        for r in range(S5_SEG):
            prow = slice(S5_SEG_PITCH * r, S5_SEG_PITCH * r + S5_SEG_ROWS)
            y_scrs[a][prow, :] = _gelu_tanh(y_scrs[a][prow, :] + d_a * u_refs[a][prow, :])

    unpad = lambda ref: jnp.concatenate(
        [ref[S5_SEG_PITCH * r:S5_SEG_PITCH * r + S5_SEG_ROWS, :] for r in range(S5_SEG)], axis=0)
    y = jnp.concatenate([unpad(r) for r in y_scrs], axis=1)
    gate = jnp.dot(y.astype(BF16), wglu_ref[...].astype(BF16), preferred_element_type=F32) + bglu_ref[...]
    y = y * _sigmoid(gate)
    z = z_ref[...].astype(F32)
    o_ref[...] = (y * (z * _sigmoid(z))).astype(BF16)


def _s5(u32, pbf, kb, winp, woutp, ptre, ptim, d_row, w_glu, b_glu, batch, seqlen):
    nb = seqlen // S5_TB
    t = batch * seqlen
    u_tile = lambda a: pl.BlockSpec((S5_SEG * S5_SEG_PITCH, LANES), lambda b, i, a=a: (b * nb + i, a))
    return pl.pallas_call(
        _s5_kernel,
        grid=(batch, nb),
        in_specs=[u_tile(a) for a in range(N_OCT)] + [
            pl.BlockSpec((S5_TB, SSM_WIDTH), lambda b, i: (b * nb + i, PBF_ZA)),
            _const_spec(kb.shape), _const_spec(winp.shape), _const_spec(woutp.shape),
            _const_spec(ptre.shape), _const_spec(ptim.shape),
            _const_spec((1, SSM_WIDTH)), _const_spec((SSM_WIDTH, SSM_WIDTH)), _const_spec((1, SSM_WIDTH)),
        ],
        out_specs=pl.BlockSpec((S5_TB, SSM_WIDTH), lambda b, i: (b * nb + i, 0)),
        out_shape=jax.ShapeDtypeStruct((t, SSM_WIDTH), BF16),
        scratch_shapes=[pltpu.VMEM((S5_SEG * S5_SEG_PITCH, LANES), F32)] * N_OCT + [
            pltpu.VMEM((SUBLANES, OCT_STATE), F32),
            pltpu.VMEM((SUBLANES, OCT_STATE), F32),
        ],
        compiler_params=pltpu.CompilerParams(
            dimension_semantics=("arbitrary", "arbitrary"), vmem_limit_bytes=VMEM_LIMIT),
        name="s5",
    )(u32, u32, u32, u32, pbf, kb, winp, woutp, ptre, ptim, d_row, w_glu, b_glu)


def _stage_weights(w_hbm, wbf_s, stage_s, sem):
    n_chunks = IN_WIDTH // W_CHUNK

    def chunk_copy(c):
        return pltpu.make_async_copy(w_hbm.at[:, pl.ds(c * W_CHUNK, W_CHUNK)], stage_s.at[c % 2], sem.at[c % 2])

    chunk_copy(0).start()
    for c in range(n_chunks):
        if c + 1 < n_chunks:
            chunk_copy(c + 1).start()
        chunk_copy(c).wait()
        wbf_s[:, c * W_CHUNK:(c + 1) * W_CHUNK] = stage_s[c % 2].astype(BF16)


def _proj_hgrn_kernel(x_ref, nw_ref, w_hbm, lbl_ref, hnw_ref, ou_ref, or_ref, o_ref,
                      wbf_s, stage_s, stage_sem, ph_s, q0_s, qc_s, kt_s, k0_s, k1_s, k2_s, ke_s, v_s, dec_s, oacc_s,
                      st_s, upd_s, sc_s, am_s,
                      *, blocks_per_seq):
    @pl.when(pl.program_id(0) == 0)
    def _():
        _stage_weights(w_hbm, wbf_s, stage_s, stage_sem)

    @pl.when(pl.program_id(0) % blocks_per_seq == 0)
    def _():
        st_s[...] = jnp.zeros_like(st_s)

    x = x_ref[...]
    ms = jnp.mean(x * x, axis=-1, keepdims=True)
    xn = (x * lax.rsqrt(ms + NORM_EPS) * nw_ref[...]).astype(BF16)
    proj = lambda lo, hi: jnp.dot(xn, wbf_s[:, lo:hi], preferred_element_type=F32)
    ph_s[...] = proj(COL_H, COL_G)
    u = proj(0, COL_ZA)
    for r in range(HG_TB // S5_SEG_ROWS):
        ou_ref[S5_SEG_PITCH * r:S5_SEG_PITCH * r + S5_SEG_ROWS, :] = u[S5_SEG_ROWS * r:S5_SEG_ROWS * (r + 1), :]
        ou_ref[S5_SEG_PITCH * r + S5_SEG_ROWS:S5_SEG_PITCH * (r + 1), :] = jnp.zeros((SUBLANES, SSM_WIDTH), F32)
    or_ref[:, :2 * D_MODEL] = proj(COL_G, IN_WIDTH).astype(BF16)
    or_ref[:, 2 * D_MODEL:] = proj(COL_ZA, COL_H).astype(BF16)
    col = lambda k: ph_s[:, HGRN_WIDTH * k:HGRN_WIDTH * (k + 1)]

    logits = lbl_ref[...]
    e = jnp.exp(logits - jnp.max(logits, axis=0, keepdims=True))
    lb = (e / jnp.sum(e, axis=0, keepdims=True))[0:1, :]

    q = col(PH_Q)
    qf = q * _sigmoid(q)
    forget = lb + (1.0 - lb) * _sigmoid(col(PH_F))
    lf = jnp.log(forget)
    key = 1.0 - forget

    row = lax.broadcasted_iota(jnp.int32, (HG_TB, HGRN_WIDTH), 0)
    r_sub = row % HG_SUB
    r_ch = row % HG_CH

    def down(x, d):
        return pltpu.roll(x, d, 0)

    def up(x, d):
        return pltpu.roll(x, HG_TB - d, 0)

    a = lf
    d = 1
    while d < HG_SUB:
        a = a + jnp.where(r_sub >= d, down(a, d), 0.0)
        d *= 2
    a3 = a.reshape(HG_TB // HG_SUB, HG_SUB, HGRN_WIDTH)
    tsub = jnp.broadcast_to(a3[:, HG_SUB - 1:HG_SUB, :], a3.shape).reshape(HG_TB, HGRN_WIDTH)
    n_sub = HG_CH // HG_SUB
    prev = [jnp.where(r_ch >= HG_SUB * k, down(tsub, HG_SUB * k), 0.0) for k in range(1, n_sub)]
    nxt = [jnp.where(r_ch < HG_CH - HG_SUB * k, up(tsub, HG_SUB * k), 0.0) for k in range(1, n_sub)]
    eprev = prev[0] + prev[1] + prev[2]
    enext = nxt[0] + nxt[1] + nxt[2]
    suf = tsub - a

    q0 = qf * jnp.exp(a)
    q0_s[...] = q0.astype(BF16)
    qc_s[...] = (q0 * jnp.exp(eprev)).astype(BF16)
    kt_s[...] = (key * jnp.exp(-a)).astype(BF16)
    k0 = key * jnp.exp(suf)
    k0_s[...] = k0.astype(BF16)
    k1 = k0 * jnp.exp(nxt[0])
    k1_s[...] = k1.astype(BF16)
    k2 = k1 * jnp.exp(nxt[1])
    k2_s[...] = k2.astype(BF16)
    ke_s[...] = (k2 * jnp.exp(nxt[2])).astype(BF16)
    dec_s[...] = jnp.exp(eprev + tsub + enext)
    v_s[...] = col(PH_I).astype(BF16)

    tq = lax.broadcasted_iota(jnp.int32, (HG_CH, n_sub * HG_CH), 0)
    cc = lax.broadcasted_iota(jnp.int32, (HG_CH, n_sub * HG_CH), 1)
    cls = cc // HG_CH
    ts = cc % HG_CH
    bi = tq // HG_SUB
    bj = ts // HG_SUB
    mask4 = ((cls == 0) & (bi == bj) & (ts <= tq)) | ((cls > 0) & ((bi - bj) == cls))

    n_ch = HG_TB // HG_CH
    half = n_sub * HG_CH // 2
    units = [(c, h, slice(c * HG_CH, (c + 1) * HG_CH), slice(HEAD_DIM * h, HEAD_DIM * (h + 1)))
             for c in range(n_ch) for h in range(HEADS)]
    for c, h, rows, ls in units:
        kcat = jnp.concatenate([kt_s[rows, ls], k0_s[rows, ls], k1_s[rows, ls], k2_s[rows, ls]], axis=0)
        sc_s[c * HEADS + h] = lax.dot_general(q0_s[rows, ls], kcat, (((1,), (1,)), ((), ())),
                                              preferred_element_type=F32)
    for c, h, rows, ls in units:
        upd_s[c * HEADS + h] = lax.dot_general(v_s[rows, ls], ke_s[rows, ls], (((0,), (0,)), ((), ())),
                                               preferred_element_type=F32)
    for c, h, rows, ls in units:
        sc = jnp.where(mask4, sc_s[c * HEADS + h], 0.0)
        am_s[c * HEADS + h] = (sc[:, :half] + sc[:, half:]).astype(BF16)
    for c, h, rows, ls in units:
        vv = v_s[rows, ls]
        oacc_s[rows, ls] = jnp.dot(am_s[c * HEADS + h], jnp.concatenate([vv, vv], axis=0),
                                   preferred_element_type=F32)
    for h in range(HEADS):
        ls = slice(HEAD_DIM * h, HEAD_DIM * (h + 1))
        st = st_s[h]
        for c in range(n_ch):
            rows = slice(c * HG_CH, (c + 1) * HG_CH)
            oacc_s[rows, ls] += lax.dot_general(qc_s[rows, ls], st.astype(BF16), (((1,), (1,)), ((), ())),
                                                preferred_element_type=F32)
            st = st * dec_s[c * HG_CH:c * HG_CH + 1, ls] + upd_s[c * HEADS + h]
        st_s[h] = st

    o = oacc_s[...] * _sigmoid(col(PH_OG))
    parts = []
    for h in range(HEADS):
        oh = o[:, HEAD_DIM * h:HEAD_DIM * (h + 1)]
        ms = jnp.mean(oh * oh, axis=-1, keepdims=True)
        parts.append(oh * lax.rsqrt(ms + NORM_EPS))
    o = jnp.concatenate(parts, axis=1) * hnw_ref[...]
    z = col(PH_ZB)
    o_ref[...] = (o * (z * _sigmoid(z))).astype(BF16)


def _proj_hgrn(x2, norm_w, w, lb_logits, hgrn_norm_w, seqlen):
    t = x2.shape[0]
    wide = (HG_TB, HGRN_WIDTH)
    units = HG_TB // HG_CH * HEADS
    n_cls = HG_CH // HG_SUB
    row_blk = lambda w: pl.BlockSpec((HG_TB, w), lambda i: (i, 0))
    u_rows = HG_TB // S5_SEG_ROWS * S5_SEG_PITCH
    return pl.pallas_call(
        functools.partial(_proj_hgrn_kernel, blocks_per_seq=seqlen // HG_TB),
        grid=(t // HG_TB,),
        in_specs=[
            row_blk(D_MODEL), _const_spec((1, D_MODEL)),
            pl.BlockSpec(memory_space=pl.ANY),
            _const_spec(lb_logits.shape), _const_spec((1, HGRN_WIDTH)),
        ],
        out_specs=[pl.BlockSpec((u_rows, SSM_WIDTH), lambda i: (i, 0)), row_blk(PBF_WIDTH), row_blk(HGRN_WIDTH)],
        out_shape=[
            jax.ShapeDtypeStruct((t // HG_TB * u_rows, SSM_WIDTH), F32),
            jax.ShapeDtypeStruct((t, PBF_WIDTH), BF16),
            jax.ShapeDtypeStruct((t, HGRN_WIDTH), BF16),
        ],
        scratch_shapes=[
            pltpu.VMEM((D_MODEL, IN_WIDTH), BF16), pltpu.VMEM((2, D_MODEL, W_CHUNK), F32),
            pltpu.SemaphoreType.DMA((2,)),
            pltpu.VMEM((HG_TB, PH_WIDTH), F32)] + [pltpu.VMEM(wide, BF16)] * 8 + [
            pltpu.VMEM(wide, F32), pltpu.VMEM(wide, F32),
            pltpu.VMEM((HEADS, HEAD_DIM, HEAD_DIM), F32),
            pltpu.VMEM((units, HEAD_DIM, HEAD_DIM), F32),
            pltpu.VMEM((units, HG_CH, n_cls * HG_CH), F32),
            pltpu.VMEM((units, HG_CH, n_cls * HG_CH // 2), BF16)],
        compiler_params=pltpu.CompilerParams(
            dimension_semantics=("arbitrary",), vmem_limit_bytes=VMEM_LIMIT),
        name="proj_hgrn",
    )(x2, norm_w, w, lb_logits, hgrn_norm_w)


def _merge_kernel(x_ref, ya_ref, yb_ref, ga_ref, gb_ref, wpa_ref, wpb_ref, wo_ref, fnw_ref, o_ref):
    pa = jnp.dot(ya_ref[...], wpa_ref[...].astype(BF16), preferred_element_type=F32)
    pb = jnp.dot(yb_ref[...], wpb_ref[...].astype(BF16), preferred_element_type=F32)
    merged = _sigmoid(ga_ref[...].astype(F32)) * pa + _sigmoid(gb_ref[...].astype(F32)) * pb
    h = x_ref[...] + jnp.dot(merged.astype(BF16), wo_ref[...].astype(BF16), preferred_element_type=F32)
    ms = jnp.mean(h * h, axis=-1, keepdims=True)
    o_ref[...] = h * lax.rsqrt(ms + NORM_EPS) * fnw_ref[...]


def _merge(x2, ya, yb, pbf, w_pa, w_pb, w_out, fnw):
    t = x2.shape[0]
    return pl.pallas_call(
        _merge_kernel,
        grid=(t // TM_OUT,),
        in_specs=[
            pl.BlockSpec((TM_OUT, D_MODEL), lambda i: (i, 0)),
            pl.BlockSpec((TM_OUT, SSM_WIDTH), lambda i: (i, 0)),
            pl.BlockSpec((TM_OUT, HGRN_WIDTH), lambda i: (i, 0)),
            pl.BlockSpec((TM_OUT, D_MODEL), lambda i: (i, 0)),
            pl.BlockSpec((TM_OUT, D_MODEL), lambda i: (i, 1)),
            _const_spec((SSM_WIDTH, D_MODEL)), _const_spec((HGRN_WIDTH, D_MODEL)),
            _const_spec((D_MODEL, D_MODEL)), _const_spec((1, D_MODEL)),
        ],
        out_specs=pl.BlockSpec((TM_OUT, D_MODEL), lambda i: (i, 0)),
        out_shape=jax.ShapeDtypeStruct((t, D_MODEL), F32),
        compiler_params=pltpu.CompilerParams(
            dimension_semantics=("arbitrary",), vmem_limit_bytes=VMEM_LIMIT),
        name="merge",
    )(x2, ya, yb, pbf, pbf, w_pa, w_pb, w_out, fnw)


def kernel(x, norm_w, w_in, ssm_lambda_re, ssm_lambda_im, ssm_b_re, ssm_b_im, ssm_c_re, ssm_c_im, ssm_d,
           ssm_log_dt, ssm_w_glu, ssm_b_glu, hgrn_lb_logits, hgrn_norm_w, w_proj_a, w_proj_b, w_out,
           final_norm_w):
    batch, seqlen, _ = x.shape
    assert norm_w.shape[0] == 1, "single-layer block"
    assert seqlen % S5_TB == 0 and seqlen % HG_TB == 0 and HG_TB % S5_SEG_ROWS == 0
    x2 = x.reshape(batch * seqlen, D_MODEL)
    u32, pbf, yb = _proj_hgrn(x2, norm_w[0][None, :], w_in[0], hgrn_lb_logits,
                              hgrn_norm_w[0][None, :], seqlen)

    kb, winp, woutp, ptre, ptim = _s5_prep(ssm_lambda_re[0], ssm_lambda_im[0], ssm_b_re[0], ssm_b_im[0],
                                           ssm_c_re[0], ssm_c_im[0], ssm_log_dt[0])
    ya = _s5(u32, pbf, kb, winp, woutp, ptre, ptim, ssm_d[0].reshape(1, SSM_WIDTH),
             ssm_w_glu[0], ssm_b_glu[0][None, :], batch, seqlen)
    out = _merge(x2, ya, yb, pbf, w_proj_a[0], w_proj_b[0], w_out[0], final_norm_w[None, :])
    return out.reshape(batch, seqlen, D_MODEL)
```

```python
import functools

import jax
import jax.numpy as jnp
from jax import lax
from jax.experimental import pallas as pl
from jax.experimental.pallas import tpu as pltpu

F32 = jnp.float32
BF16 = jnp.bfloat16

D_MODEL = 1024
SSM_WIDTH = 512
SSM_GROUP = 16
SSM_GROUPS = 32
SSM_STATE = 64
HGRN_WIDTH = 512
HEAD_DIM = 128
HEADS = 4
NORM_EPS = 1e-6
LAMBDA_RE_MAX = -1e-4

IN_WIDTH = 2 * SSM_WIDTH + 5 * HGRN_WIDTH + 2 * D_MODEL
COL_ZA, COL_H, COL_G = SSM_WIDTH, 2 * SSM_WIDTH, 2 * SSM_WIDTH + 5 * HGRN_WIDTH
PH_WIDTH = 5 * HGRN_WIDTH
PH_Q, PH_F, PH_I, PH_OG, PH_ZB = 0, 1, 2, 3, 4
PBF_WIDTH = 2 * D_MODEL + SSM_WIDTH
PBF_ZA = 4

LANES = 128
SUBLANES = 8
OCT = LANES // SSM_GROUP
N_OCT = SSM_GROUPS // OCT
OCT_STATE = OCT * SSM_STATE

S5_Q = 4
S5_QP = S5_Q // 2
S5_TB = 2048
S5_NCH = S5_TB // S5_Q
S5_SEG = SUBLANES
S5_NV = S5_NCH // S5_SEG
S5_PT_ROWS = S5_NV + SUBLANES
S5_SEG_ROWS = S5_Q * S5_NV
S5_SEG_PITCH = S5_SEG_ROWS + SUBLANES
HG_TB = 512
HG_CH = 64
HG_SUB = 16
W_CHUNK = 512
TM_OUT = 1024

V7X_VMEM_BYTES = 64 * 1024 * 1024
VMEM_LIMIT = V7X_VMEM_BYTES * 7 // 8


def _sigmoid(x):
    return 0.5 * jnp.tanh(0.5 * x) + 0.5


def _const_spec(shape):
    nd = len(shape)
    return pl.BlockSpec(shape, lambda *_: (0,) * nd, pipeline_mode=pl.Buffered(1))


def _disc(lam_re, lam_im, log_dt):
    return jnp.minimum(lam_re, LAMBDA_RE_MAX), lam_im, jnp.exp(log_dt)


def _cpow(lr, li, dt, k):
    mag = jnp.exp(k * (lr * dt))
    ang = k * (li * dt)
    return mag * jnp.cos(ang), mag * jnp.sin(ang)


def _split_bf16(x):
    hi = x.astype(BF16)
    return hi, (x - hi.astype(F32)).astype(BF16)


def _dot3(a, b_hi, b_lo):
    a_hi, a_lo = _split_bf16(a)
    d = lambda p, q: jnp.dot(p, q, preferred_element_type=F32)
    return d(a_hi, b_hi) + (d(a_hi, b_lo) + d(a_lo, b_hi))


def _cmul(a_re, a_im, b_re, b_im):
    return a_re * b_re - a_im * b_im, a_re * b_im + a_im * b_re


def _s5_prep_kernel(lam_c, b_t, c_n, lam_n, lam_r,
                    kb_ref, win_ref, wout_ref, ptre_ref, ptim_ref, pw_s, pm_s, xk_s):
    @pl.when(pl.program_id(0) == 0)
    def _():
        lr, li, dt = _disc(lam_c[0], lam_c[1], lam_c[2])
        ab_re, ab_im = _cpow(lr, li, dt, 1.0)
        den = lr * lr + li * li
        nr = ab_re - 1.0
        pw_s[0] = (nr * lr + ab_im * li) / den
        pw_s[1] = (ab_im * lr - nr * li) / den
        pw_s[2] = jnp.ones_like(ab_re)
        pw_s[3] = jnp.zeros_like(ab_re)
        pw_s[4] = ab_re
        pw_s[5] = ab_im
        m_re, m_im = _cpow(*_disc(lam_n[0], lam_n[1], lam_n[2]), 1.0)
        pm_s[0] = m_re
        pm_s[1] = m_im
        pm_s[2] = m_re
        pm_s[3] = m_im
        j = lax.broadcasted_iota(jnp.int32, (S5_PT_ROWS, OCT_STATE), 0).astype(F32) * float(S5_Q)
        for a in range(N_OCT):
            t_re, t_im = _cpow(*_disc(lam_r[0, a:a + 1, :], lam_r[1, a:a + 1, :], lam_r[2, a:a + 1, :]), j)
            ptre_ref[a] = t_re
            ptim_ref[a] = t_im

    col = lax.broadcasted_iota(jnp.int32, (SSM_GROUP, SSM_WIDTH), 1)

    def strip(g, carry):
        rows = pl.ds(pl.multiple_of(g * SSM_GROUP, SSM_GROUP), SSM_GROUP)
        grow = pl.ds(g, 1)
        coef_re, coef_im = pw_s[0, grow, :], pw_s[1, grow, :]
        p_re, p_im = pw_s[2, grow, :], pw_s[3, grow, :]
        bb_re, bb_im = _cmul(coef_re, coef_im, b_t[0, rows, :], b_t[1, rows, :])
        x_re, x_im = _cmul(bb_re, bb_im, p_re, p_im)
        xk_s[0, rows, :] = x_re[:, :SSM_STATE]
        xk_s[1, rows, :] = x_im[:, :SSM_STATE]
        a = g // OCT
        lrows = pl.ds(pl.multiple_of((g % OCT) * SSM_GROUP, SSM_GROUP), SSM_GROUP)
        m_in = col // SSM_STATE == g % OCT
        win_ref[0, a, 0, lrows, :] = jnp.concatenate(
            [jnp.where(m_in, x_re, 0.0), jnp.where(m_in, x_im, 0.0)], axis=1).astype(BF16)
        return carry

    lax.fori_loop(0, SSM_GROUPS, strip, 0, unroll=4)

    cr, ci = c_n[0], c_n[1]
    kfull = _dot3(xk_s[0], *_split_bf16(cr)) - _dot3(xk_s[1], *_split_bf16(ci))
    rowk = lax.broadcasted_iota(jnp.int32, (LANES, LANES), 0)
    colk = lax.broadcasted_iota(jnp.int32, (LANES, LANES), 1)
    m_k = rowk // SSM_GROUP == colk // SSM_GROUP
    for a in range(N_OCT):
        ls = slice(LANES * a, LANES * (a + 1))
        kb_ref[0, a] = jnp.where(m_k, kfull[ls, ls], 0.0).astype(BF16)

    p1_re, p1_im = pm_s[0], pm_s[1]
    w_re, w_im = _cmul(cr, ci, p1_re, p1_im)
    rowo = lax.broadcasted_iota(jnp.int32, (OCT_STATE, LANES), 0)
    colo = lax.broadcasted_iota(jnp.int32, (OCT_STATE, LANES), 1)
    m_out = rowo // SSM_STATE == colo // SSM_GROUP
    for a in range(N_OCT):
        ls = slice(LANES * a, LANES * (a + 1))
        wout_ref[0, a, :OCT_STATE, :] = jnp.where(m_out, jnp.tile(w_re[:, ls], (OCT, 1)), 0.0).astype(BF16)
        wout_ref[0, a, OCT_STATE:, :] = jnp.where(m_out, jnp.tile(-w_im[:, ls], (OCT, 1)), 0.0).astype(BF16)

    pw_s[2], pw_s[3] = _cmul(pw_s[2], pw_s[3], pw_s[4], pw_s[5])
    pm_s[0], pm_s[1] = _cmul(p1_re, p1_im, pm_s[2], pm_s[3])


def _s5_prep(lam_re, lam_im, b_re, b_im, c_re, c_im, log_dt):
    g, n, p = SSM_GROUPS, SSM_STATE, SSM_GROUP
    lam = jnp.stack([lam_re, lam_im, jnp.broadcast_to(log_dt[:, None], (g, n))])
    b = jnp.stack([b_re, b_im])
    c = jnp.stack([c_re, c_im])
    args = (jnp.tile(lam, (1, 1, OCT)),
            jnp.tile(b.transpose(0, 1, 3, 2).reshape(2, g * p, n), (1, 1, OCT)),
            c.transpose(0, 3, 1, 2).reshape(2, n, g * p),
            jnp.repeat(lam.transpose(0, 2, 1), p, axis=2),
            lam.reshape(3, N_OCT, OCT_STATE))
    mat = (S5_QP, N_OCT, 2, LANES, 2 * OCT_STATE)
    blk = (1, N_OCT, 1, LANES, 2 * OCT_STATE)
    tab = (N_OCT, S5_PT_ROWS, OCT_STATE)
    kb, win, woutp, ptre, ptim = pl.pallas_call(
        _s5_prep_kernel,
        grid=(S5_Q,),
        in_specs=[_const_spec(a.shape) for a in args],
        out_specs=[
            pl.BlockSpec((1, N_OCT, LANES, LANES), lambda t: (t, 0, 0, 0)),
            pl.BlockSpec(blk, lambda t: ((S5_Q - 1 - t) // 2, 0, (S5_Q - 1 - t) % 2, 0, 0)),
            pl.BlockSpec((1, N_OCT, 2 * OCT_STATE, LANES), lambda t: (t // 2, 0, 0, t % 2)),
            pl.BlockSpec(tab, lambda t: (0, 0, 0)),
            pl.BlockSpec(tab, lambda t: (0, 0, 0)),
        ],
        out_shape=[
            jax.ShapeDtypeStruct((S5_Q, N_OCT, LANES, LANES), BF16),
            jax.ShapeDtypeStruct(mat, BF16),
            jax.ShapeDtypeStruct((S5_QP, N_OCT, 2 * OCT_STATE, 2 * LANES), BF16),
            jax.ShapeDtypeStruct(tab, F32),
            jax.ShapeDtypeStruct(tab, F32),
        ],
        scratch_shapes=[pltpu.VMEM((6, SSM_GROUPS, SSM_WIDTH), F32),
                        pltpu.VMEM((4, SSM_STATE, SSM_WIDTH), F32),
                        pltpu.VMEM((2, SSM_WIDTH, SSM_STATE), F32)],
        compiler_params=pltpu.CompilerParams(
            dimension_semantics=("arbitrary",), vmem_limit_bytes=VMEM_LIMIT),
        name="s5_prep",
    )(*args)

    winp = win.reshape(S5_QP, N_OCT, 2 * LANES, 2 * OCT_STATE)
    return kb, winp, woutp, ptre, ptim


def _gelu_tanh(x):
    c = 0.7978845608028654
    return x * (0.5 * (1.0 + jnp.tanh(c * (x + 0.044715 * (x * x * x)))))


def _s5_kernel(u0_ref, u1_ref, u2_ref, u3_ref, z_ref, kb_ref, winp_ref, woutp_ref, ptre_ref, ptim_ref,
               d_ref, wglu_ref, bglu_ref, o_ref, y0_s, y1_s, y2_s, y3_s, cre_scr, cim_scr):
    u_refs = (u0_ref, u1_ref, u2_ref, u3_ref)
    y_scrs = (y0_s, y1_s, y2_s, y3_s)

    @pl.when(pl.program_id(1) == 0)
    def _():
        cre_scr[...] = jnp.zeros_like(cre_scr)
        cim_scr[...] = jnp.zeros_like(cim_scr)

    def tok(a, s):
        return jnp.concatenate(
            [u_refs[a][pl.ds(s + S5_Q * v, S5_SEG, stride=S5_SEG_PITCH), :] for v in range(S5_NV)], axis=0)

    xp = [[jnp.concatenate([tok(a, 2 * sp), tok(a, 2 * sp + 1)], axis=1).astype(BF16) for a in range(N_OCT)]
          for sp in range(S5_QP)]

    def cmul_add(b_re, b_im, m_re, m_im, x_re, x_im):
        return b_re + m_re * x_re - m_im * x_im, b_im + m_re * x_im + m_im * x_re

    hs = []
    for a in range(N_OCT):
        acc = None
        for sp in range(S5_QP):
            part = jnp.dot(xp[sp][a], winp_ref[sp, a], preferred_element_type=F32)
            acc = part if acc is None else acc + part
        blk = lambda v: (acc[S5_SEG * v:S5_SEG * (v + 1), :OCT_STATE], acc[S5_SEG * v:S5_SEG * (v + 1), OCT_STATE:])
        m_re, m_im = ptre_ref[a, 1:2, :], ptim_ref[a, 1:2, :]
        loc = [blk(0)]
        for v in range(1, S5_NV):
            loc.append(cmul_add(*blk(v), m_re, m_im, *loc[-1]))
        l_re, l_im = ptre_ref[a, S5_NV:S5_NV + 1, :], ptim_ref[a, S5_NV:S5_NV + 1, :]
        c_re, c_im = cre_scr[a:a + 1, :], cim_scr[a:a + 1, :]
        carry = []
        for r in range(S5_SEG):
            carry.append((c_re, c_im))
            c_re, c_im = cmul_add(loc[-1][0][r:r + 1, :], loc[-1][1][r:r + 1, :], l_re, l_im, c_re, c_im)
        cre_scr[a:a + 1, :] = c_re
        cim_scr[a:a + 1, :] = c_im
        cs_re = jnp.concatenate([c[0] for c in carry], axis=0)
        cs_im = jnp.concatenate([c[1] for c in carry], axis=0)
        ent = [(cs_re, cs_im)]
        for v in range(S5_NV - 1):
            ent.append(cmul_add(*loc[v], ptre_ref[a, v + 1:v + 2, :], ptim_ref[a, v + 1:v + 2, :], cs_re, cs_im))
        hs.append(jnp.concatenate([jnp.concatenate([e[0] for e in ent], axis=0),
                                   jnp.concatenate([e[1] for e in ent], axis=0)], axis=1).astype(BF16))

    def pair_tile(d, a):
        below = kb_ref[2 * d - 1, a] if d > 0 else jnp.zeros((LANES, LANES), BF16)
        return jnp.concatenate([jnp.concatenate([kb_ref[2 * d, a], kb_ref[2 * d + 1, a]], axis=1),
                                jnp.concatenate([below, kb_ref[2 * d, a]], axis=1)], axis=0)

    for a in range(N_OCT):
        tiles = [pair_tile(d, a) for d in range(S5_QP)]
        for tp in range(S5_QP):
            acc = jnp.dot(hs[a], woutp_ref[tp, a], preferred_element_type=F32)
            for sp in range(tp + 1):
                acc = acc + jnp.dot(xp[sp][a], tiles[tp - sp], preferred_element_type=F32)
            for v in range(S5_NV):
                rows = slice(S5_SEG * v, S5_SEG * (v + 1))
                y_scrs[a][pl.ds(2 * tp + S5_Q * v, S5_SEG, stride=S5_SEG_PITCH), :] = acc[rows, :LANES]
                y_scrs[a][pl.ds(2 * tp + 1 + S5_Q * v, S5_SEG, stride=S5_SEG_PITCH), :] = acc[rows, LANES:]
        d_a = d_ref[:, LANES * a:LANES * (a + 1)]
        for r in range(S5_SEG):
            prow = slice(S5_SEG_PITCH * r, S5_SEG_PITCH * r + S5_SEG_ROWS)
            y_scrs[a][prow, :] = _gelu_tanh(y_scrs[a][prow, :] + d_a * u_refs[a][prow, :])

    unpad = lambda ref: jnp.concatenate(
        [ref[S5_SEG_PITCH * r:S5_SEG_PITCH * r + S5_SEG_ROWS, :] for r in range(S5_SEG)], axis=0)
    y = jnp.concatenate([unpad(r) for r in y_scrs], axis=1)
    gate = jnp.dot(y.astype(BF16), wglu_ref[...].astype(BF16), preferred_element_type=F32) + bglu_ref[...]
    y = y * _sigmoid(gate)
    z = z_ref[...].astype(F32)
    o_ref[...] = (y * (z * _sigmoid(z))).astype(BF16)


def _s5(u32, pbf, kb, winp, woutp, ptre, ptim, d_row, w_glu, b_glu, batch, seqlen):
    nb = seqlen // S5_TB
    t = batch * seqlen
    u_tile = lambda a: pl.BlockSpec((S5_SEG * S5_SEG_PITCH, LANES), lambda b, i, a=a: (b * nb + i, a))
    return pl.pallas_call(
        _s5_kernel,
        grid=(batch, nb),
        in_specs=[u_tile(a) for a in range(N_OCT)] + [
            pl.BlockSpec((S5_TB, SSM_WIDTH), lambda b, i: (b * nb + i, PBF_ZA)),
            _const_spec(kb.shape), _const_spec(winp.shape), _const_spec(woutp.shape),
            _const_spec(ptre.shape), _const_spec(ptim.shape),
            _const_spec((1, SSM_WIDTH)), _const_spec((SSM_WIDTH, SSM_WIDTH)), _const_spec((1, SSM_WIDTH)),
        ],
        out_specs=pl.BlockSpec((S5_TB, SSM_WIDTH), lambda b, i: (b * nb + i, 0)),
        out_shape=jax.ShapeDtypeStruct((t, SSM_WIDTH), BF16),
        scratch_shapes=[pltpu.VMEM((S5_SEG * S5_SEG_PITCH, LANES), F32)] * N_OCT + [
            pltpu.VMEM((SUBLANES, OCT_STATE), F32),
            pltpu.VMEM((SUBLANES, OCT_STATE), F32),
        ],
        compiler_params=pltpu.CompilerParams(
            dimension_semantics=("arbitrary", "arbitrary"), vmem_limit_bytes=VMEM_LIMIT),
        name="s5",
    )(u32, u32, u32, u32, pbf, kb, winp, woutp, ptre, ptim, d_row, w_glu, b_glu)


def _stage_weights(w_hbm, wbf_s, stage_s, sem):
    n_chunks = IN_WIDTH // W_CHUNK

    def chunk_copy(c):
        return pltpu.make_async_copy(w_hbm.at[:, pl.ds(c * W_CHUNK, W_CHUNK)], stage_s.at[c % 2], sem.at[c % 2])

    chunk_copy(0).start()
    for c in range(n_chunks):
        if c + 1 < n_chunks:
            chunk_copy(c + 1).start()
        chunk_copy(c).wait()
        wbf_s[:, c * W_CHUNK:(c + 1) * W_CHUNK] = stage_s[c % 2].astype(BF16)


def _proj_hgrn_kernel(x_ref, nw_ref, w_hbm, lbl_ref, hnw_ref, ou_ref, or_ref, o_ref,
                      wbf_s, stage_s, stage_sem, ph_s, q0_s, qc_s, kt_s, k0_s, k1_s, k2_s, ke_s, v_s, dec_s, oacc_s,
                      st_s, upd_s, sc_s, am_s,
                      *, blocks_per_seq):
    @pl.when(pl.program_id(0) == 0)
    def _():
        _stage_weights(w_hbm, wbf_s, stage_s, stage_sem)

    @pl.when(pl.program_id(0) % blocks_per_seq == 0)
    def _():
        st_s[...] = jnp.zeros_like(st_s)

    x = x_ref[...]
    ms = jnp.mean(x * x, axis=-1, keepdims=True)
    xn = (x * lax.rsqrt(ms + NORM_EPS) * nw_ref[...]).astype(BF16)
    proj = lambda lo, hi: jnp.dot(xn, wbf_s[:, lo:hi], preferred_element_type=F32)
    ph_s[...] = proj(COL_H, COL_G)
    u = proj(0, COL_ZA)
    for r in range(HG_TB // S5_SEG_ROWS):
        ou_ref[S5_SEG_PITCH * r:S5_SEG_PITCH * r + S5_SEG_ROWS, :] = u[S5_SEG_ROWS * r:S5_SEG_ROWS * (r + 1), :]
        ou_ref[S5_SEG_PITCH * r + S5_SEG_ROWS:S5_SEG_PITCH * (r + 1), :] = jnp.zeros((SUBLANES, SSM_WIDTH), F32)
    or_ref[:, :2 * D_MODEL] = proj(COL_G, IN_WIDTH).astype(BF16)
    or_ref[:, 2 * D_MODEL:] = proj(COL_ZA, COL_H).astype(BF16)
    col = lambda k: ph_s[:, HGRN_WIDTH * k:HGRN_WIDTH * (k + 1)]

    logits = lbl_ref[...]
    e = jnp.exp(logits - jnp.max(logits, axis=0, keepdims=True))
    lb = (e / jnp.sum(e, axis=0, keepdims=True))[0:1, :]

    q = col(PH_Q)
    qf = q * _sigmoid(q)
    forget = lb + (1.0 - lb) * _sigmoid(col(PH_F))
    lf = jnp.log(forget)
    key = 1.0 - forget

    row = lax.broadcasted_iota(jnp.int32, (HG_TB, HGRN_WIDTH), 0)
    r_sub = row % HG_SUB
    r_ch = row % HG_CH

    def down(x, d):
        return pltpu.roll(x, d, 0)

    def up(x, d):
        return pltpu.roll(x, HG_TB - d, 0)

    a = lf
    d = 1
    while d < HG_SUB:
        a = a + jnp.where(r_sub >= d, down(a, d), 0.0)
        d *= 2
    a3 = a.reshape(HG_TB // HG_SUB, HG_SUB, HGRN_WIDTH)
    tsub = jnp.broadcast_to(a3[:, HG_SUB - 1:HG_SUB, :], a3.shape).reshape(HG_TB, HGRN_WIDTH)
    n_sub = HG_CH // HG_SUB
    prev = [jnp.where(r_ch >= HG_SUB * k, down(tsub, HG_SUB * k), 0.0) for k in range(1, n_sub)]
    nxt = [jnp.where(r_ch < HG_CH - HG_SUB * k, up(tsub, HG_SUB * k), 0.0) for k in range(1, n_sub)]
    eprev = prev[0] + prev[1] + prev[2]
    enext = nxt[0] + nxt[1] + nxt[2]
    suf = tsub - a

    hsub = 0.5 * tsub
    q0_s[...] = (qf * jnp.exp(a - hsub)).astype(BF16)
    qc_s[...] = (qf * jnp.exp(a + eprev)).astype(BF16)
    kt_s[...] = (key * jnp.exp(hsub - a)).astype(BF16)
    k0 = key * jnp.exp(suf)
    e0, e1, e2 = (jnp.exp(0.5 * n) for n in nxt)
    k0_s[...] = (k0 * e0).astype(BF16)
    k1 = k0 * (e0 * e0)
    k1_s[...] = (k1 * e1).astype(BF16)
    k2 = k1 * (e1 * e1)
    k2_s[...] = (k2 * e2).astype(BF16)
    ke_s[...] = (k2 * (e2 * e2)).astype(BF16)
    dec_s[...] = jnp.exp(eprev + tsub + enext)
    v_s[...] = col(PH_I).astype(BF16)

    tq = lax.broadcasted_iota(jnp.int32, (HG_CH, n_sub * HG_CH), 0)
    cc = lax.broadcasted_iota(jnp.int32, (HG_CH, n_sub * HG_CH), 1)
    cls = cc // HG_CH
    ts = cc % HG_CH
    bi = tq // HG_SUB
    bj = ts // HG_SUB
    mask4 = ((cls == 0) & (bi == bj) & (ts <= tq)) | ((cls > 0) & ((bi - bj) == cls))

    n_ch = HG_TB // HG_CH
    half = n_sub * HG_CH // 2
    units = [(c, h, slice(c * HG_CH, (c + 1) * HG_CH), slice(HEAD_DIM * h, HEAD_DIM * (h + 1)))
             for c in range(n_ch) for h in range(HEADS)]
    for c, h, rows, ls in units:
        kcat = jnp.concatenate([kt_s[rows, ls], k0_s[rows, ls], k1_s[rows, ls], k2_s[rows, ls]], axis=0)
        sc_s[c * HEADS + h] = lax.dot_general(q0_s[rows, ls], kcat, (((1,), (1,)), ((), ())),
                                              preferred_element_type=F32)
    for c, h, rows, ls in units:
        upd_s[c * HEADS + h] = lax.dot_general(v_s[rows, ls], ke_s[rows, ls], (((0,), (0,)), ((), ())),
                                               preferred_element_type=F32)
    for c, h, rows, ls in units:
        sc = jnp.where(mask4, sc_s[c * HEADS + h], 0.0)
        am_s[c * HEADS + h] = (sc[:, :half] + sc[:, half:]).astype(BF16)
    for c, h, rows, ls in units:
        vv = v_s[rows, ls]
        oacc_s[rows, ls] = jnp.dot(am_s[c * HEADS + h], jnp.concatenate([vv, vv], axis=0),
                                   preferred_element_type=F32)
    for h in range(HEADS):
        ls = slice(HEAD_DIM * h, HEAD_DIM * (h + 1))
        st = st_s[h]
        for c in range(n_ch):
            rows = slice(c * HG_CH, (c + 1) * HG_CH)
            oacc_s[rows, ls] += lax.dot_general(qc_s[rows, ls], st.astype(BF16), (((1,), (1,)), ((), ())),
                                                preferred_element_type=F32)
            st = st * dec_s[c * HG_CH:c * HG_CH + 1, ls] + upd_s[c * HEADS + h]
        st_s[h] = st

    o = oacc_s[...] * _sigmoid(col(PH_OG))
    parts = []
    for h in range(HEADS):
        oh = o[:, HEAD_DIM * h:HEAD_DIM * (h + 1)]
        ms = jnp.mean(oh * oh, axis=-1, keepdims=True)
        parts.append(oh * lax.rsqrt(ms + NORM_EPS))
    o = jnp.concatenate(parts, axis=1) * hnw_ref[...]
    z = col(PH_ZB)
    o_ref[...] = (o * (z * _sigmoid(z))).astype(BF16)


def _proj_hgrn(x2, norm_w, w, lb_logits, hgrn_norm_w, seqlen):
    t = x2.shape[0]
    wide = (HG_TB, HGRN_WIDTH)
    units = HG_TB // HG_CH * HEADS
    n_cls = HG_CH // HG_SUB
    row_blk = lambda w: pl.BlockSpec((HG_TB, w), lambda i: (i, 0))
    u_rows = HG_TB // S5_SEG_ROWS * S5_SEG_PITCH
    return pl.pallas_call(
        functools.partial(_proj_hgrn_kernel, blocks_per_seq=seqlen // HG_TB),
        grid=(t // HG_TB,),
        in_specs=[
            row_blk(D_MODEL), _const_spec((1, D_MODEL)),
            pl.BlockSpec(memory_space=pl.ANY),
            _const_spec(lb_logits.shape), _const_spec((1, HGRN_WIDTH)),
        ],
        out_specs=[pl.BlockSpec((u_rows, SSM_WIDTH), lambda i: (i, 0)), row_blk(PBF_WIDTH), row_blk(HGRN_WIDTH)],
        out_shape=[
            jax.ShapeDtypeStruct((t // HG_TB * u_rows, SSM_WIDTH), F32),
            jax.ShapeDtypeStruct((t, PBF_WIDTH), BF16),
            jax.ShapeDtypeStruct((t, HGRN_WIDTH), BF16),
        ],
        scratch_shapes=[
            pltpu.VMEM((D_MODEL, IN_WIDTH), BF16), pltpu.VMEM((2, D_MODEL, W_CHUNK), F32),
            pltpu.SemaphoreType.DMA((2,)),
            pltpu.VMEM((HG_TB, PH_WIDTH), F32)] + [pltpu.VMEM(wide, BF16)] * 8 + [
            pltpu.VMEM(wide, F32), pltpu.VMEM(wide, F32),
            pltpu.VMEM((HEADS, HEAD_DIM, HEAD_DIM), F32),
            pltpu.VMEM((units, HEAD_DIM, HEAD_DIM), F32),
            pltpu.VMEM((units, HG_CH, n_cls * HG_CH), F32),
            pltpu.VMEM((units, HG_CH, n_cls * HG_CH // 2), BF16)],
        compiler_params=pltpu.CompilerParams(
            dimension_semantics=("arbitrary",), vmem_limit_bytes=VMEM_LIMIT),
        name="proj_hgrn",
    )(x2, norm_w, w, lb_logits, hgrn_norm_w)


def _merge_kernel(x_ref, ya_ref, yb_ref, ga_ref, gb_ref, wpa_ref, wpb_ref, wo_ref, fnw_ref, o_ref):
    pa = jnp.dot(ya_ref[...], wpa_ref[...].astype(BF16), preferred_element_type=F32)
    pb = jnp.dot(yb_ref[...], wpb_ref[...].astype(BF16), preferred_element_type=F32)
    merged = _sigmoid(ga_ref[...].astype(F32)) * pa + _sigmoid(gb_ref[...].astype(F32)) * pb
    h = x_ref[...] + jnp.dot(merged.astype(BF16), wo_ref[...].astype(BF16), preferred_element_type=F32)
    ms = jnp.mean(h * h, axis=-1, keepdims=True)
    o_ref[...] = h * lax.rsqrt(ms + NORM_EPS) * fnw_ref[...]


def _merge(x2, ya, yb, pbf, w_pa, w_pb, w_out, fnw):
    t = x2.shape[0]
    return pl.pallas_call(
        _merge_kernel,
        grid=(t // TM_OUT,),
        in_specs=[
            pl.BlockSpec((TM_OUT, D_MODEL), lambda i: (i, 0)),
            pl.BlockSpec((TM_OUT, SSM_WIDTH), lambda i: (i, 0)),
            pl.BlockSpec((TM_OUT, HGRN_WIDTH), lambda i: (i, 0)),
            pl.BlockSpec((TM_OUT, D_MODEL), lambda i: (i, 0)),
            pl.BlockSpec((TM_OUT, D_MODEL), lambda i: (i, 1)),
            _const_spec((SSM_WIDTH, D_MODEL)), _const_spec((HGRN_WIDTH, D_MODEL)),
            _const_spec((D_MODEL, D_MODEL)), _const_spec((1, D_MODEL)),
        ],
        out_specs=pl.BlockSpec((TM_OUT, D_MODEL), lambda i: (i, 0)),
        out_shape=jax.ShapeDtypeStruct((t, D_MODEL), F32),
        compiler_params=pltpu.CompilerParams(
            dimension_semantics=("arbitrary",), vmem_limit_bytes=VMEM_LIMIT),
        name="merge",
    )(x2, ya, yb, pbf, pbf, w_pa, w_pb, w_out, fnw)


def kernel(x, norm_w, w_in, ssm_lambda_re, ssm_lambda_im, ssm_b_re, ssm_b_im, ssm_c_re, ssm_c_im, ssm_d,
           ssm_log_dt, ssm_w_glu, ssm_b_glu, hgrn_lb_logits, hgrn_norm_w, w_proj_a, w_proj_b, w_out,
           final_norm_w):
    batch, seqlen, _ = x.shape
    assert norm_w.shape[0] == 1, "single-layer block"
    assert seqlen % S5_TB == 0 and seqlen % HG_TB == 0 and HG_TB % S5_SEG_ROWS == 0
    x2 = x.reshape(batch * seqlen, D_MODEL)
    u32, pbf, yb = _proj_hgrn(x2, norm_w[0][None, :], w_in[0], hgrn_lb_logits,
                              hgrn_norm_w[0][None, :], seqlen)

    kb, winp, woutp, ptre, ptim = _s5_prep(ssm_lambda_re[0], ssm_lambda_im[0], ssm_b_re[0], ssm_b_im[0],
                                           ssm_c_re[0], ssm_c_im[0], ssm_log_dt[0])
    ya = _s5(u32, pbf, kb, winp, woutp, ptre, ptim, ssm_d[0].reshape(1, SSM_WIDTH),
             ssm_w_glu[0], ssm_b_glu[0][None, :], batch, seqlen)
    out = _merge(x2, ya, yb, pbf, w_proj_a[0], w_proj_b[0], w_out[0], final_norm_w[None, :])
    return out.reshape(batch, seqlen, D_MODEL)
```

```python
import functools

import jax
import jax.numpy as jnp
from jax import lax
from jax.experimental import pallas as pl
from jax.experimental.pallas import tpu as pltpu

F32 = jnp.float32
BF16 = jnp.bfloat16

D_MODEL = 1024
SSM_WIDTH = 512
SSM_GROUP = 16
SSM_GROUPS = 32
SSM_STATE = 64
HGRN_WIDTH = 512
HEAD_DIM = 128
HEADS = 4
NORM_EPS = 1e-6
LAMBDA_RE_MAX = -1e-4

IN_WIDTH = 2 * SSM_WIDTH + 5 * HGRN_WIDTH + 2 * D_MODEL
COL_ZA, COL_H, COL_G = SSM_WIDTH, 2 * SSM_WIDTH, 2 * SSM_WIDTH + 5 * HGRN_WIDTH
PH_WIDTH = 5 * HGRN_WIDTH
PH_Q, PH_F, PH_I, PH_OG, PH_ZB = 0, 1, 2, 3, 4
PBF_WIDTH = 2 * D_MODEL + SSM_WIDTH
PBF_ZA = 4

LANES = 128
SUBLANES = 8
OCT = LANES // SSM_GROUP
N_OCT = SSM_GROUPS // OCT
OCT_STATE = OCT * SSM_STATE

S5_Q = 4
S5_QP = S5_Q // 2
S5_TB = 2048
S5_NCH = S5_TB // S5_Q
S5_SEG = SUBLANES
S5_NV = S5_NCH // S5_SEG
S5_PT_ROWS = S5_NV + SUBLANES
S5_SEG_ROWS = S5_Q * S5_NV
S5_SEG_PITCH = S5_SEG_ROWS + SUBLANES
HG_TB = 512
HG_CH = 64
HG_SUB = 16
W_CHUNK = 512
TM_OUT = 1024

V7X_VMEM_BYTES = 64 * 1024 * 1024
VMEM_LIMIT = V7X_VMEM_BYTES * 7 // 8


def _sigmoid(x):
    return 0.5 * jnp.tanh(0.5 * x) + 0.5


def _const_spec(shape):
    nd = len(shape)
    return pl.BlockSpec(shape, lambda *_: (0,) * nd, pipeline_mode=pl.Buffered(1))


def _disc(lam_re, lam_im, log_dt):
    return jnp.minimum(lam_re, LAMBDA_RE_MAX), lam_im, jnp.exp(log_dt)


def _cpow(lr, li, dt, k):
    mag = jnp.exp(k * (lr * dt))
    ang = k * (li * dt)
    return mag * jnp.cos(ang), mag * jnp.sin(ang)


def _split_bf16(x):
    hi = x.astype(BF16)
    return hi, (x - hi.astype(F32)).astype(BF16)


def _dot3(a, b_hi, b_lo):
    a_hi, a_lo = _split_bf16(a)
    d = lambda p, q: jnp.dot(p, q, preferred_element_type=F32)
    return d(a_hi, b_hi) + (d(a_hi, b_lo) + d(a_lo, b_hi))


def _cmul(a_re, a_im, b_re, b_im):
    return a_re * b_re - a_im * b_im, a_re * b_im + a_im * b_re


def _s5_prep_kernel(lam_c, b_t, c_n, lam_n, lam_r,
                    kb_ref, win_ref, wout_ref, ptre_ref, ptim_ref, pw_s, pm_s, xk_s):
    @pl.when(pl.program_id(0) == 0)
    def _():
        lr, li, dt = _disc(lam_c[0], lam_c[1], lam_c[2])
        ab_re, ab_im = _cpow(lr, li, dt, 1.0)
        den = lr * lr + li * li
        nr = ab_re - 1.0
        pw_s[0] = (nr * lr + ab_im * li) / den
        pw_s[1] = (ab_im * lr - nr * li) / den
        pw_s[2] = jnp.ones_like(ab_re)
        pw_s[3] = jnp.zeros_like(ab_re)
        pw_s[4] = ab_re
        pw_s[5] = ab_im
        m_re, m_im = _cpow(*_disc(lam_n[0], lam_n[1], lam_n[2]), 1.0)
        pm_s[0] = m_re
        pm_s[1] = m_im
        pm_s[2] = m_re
        pm_s[3] = m_im
        j = lax.broadcasted_iota(jnp.int32, (S5_PT_ROWS, OCT_STATE), 0).astype(F32) * float(S5_Q)
        for a in range(N_OCT):
            t_re, t_im = _cpow(*_disc(lam_r[0, a:a + 1, :], lam_r[1, a:a + 1, :], lam_r[2, a:a + 1, :]), j)
            ptre_ref[a] = t_re
            ptim_ref[a] = t_im

    col = lax.broadcasted_iota(jnp.int32, (SSM_GROUP, SSM_WIDTH), 1)

    def strip(g, carry):
        rows = pl.ds(pl.multiple_of(g * SSM_GROUP, SSM_GROUP), SSM_GROUP)
        grow = pl.ds(g, 1)
        coef_re, coef_im = pw_s[0, grow, :], pw_s[1, grow, :]
        p_re, p_im = pw_s[2, grow, :], pw_s[3, grow, :]
        bb_re, bb_im = _cmul(coef_re, coef_im, b_t[0, rows, :], b_t[1, rows, :])
        x_re, x_im = _cmul(bb_re, bb_im, p_re, p_im)
        xk_s[0, rows, :] = x_re[:, :SSM_STATE]
        xk_s[1, rows, :] = x_im[:, :SSM_STATE]
        a = g // OCT
        lrows = pl.ds(pl.multiple_of((g % OCT) * SSM_GROUP, SSM_GROUP), SSM_GROUP)
        m_in = col // SSM_STATE == g % OCT
        win_ref[0, a, 0, lrows, :] = jnp.concatenate(
            [jnp.where(m_in, x_re, 0.0), jnp.where(m_in, x_im, 0.0)], axis=1).astype(BF16)
        return carry

    lax.fori_loop(0, SSM_GROUPS, strip, 0, unroll=4)

    cr, ci = c_n[0], c_n[1]
    kfull = _dot3(xk_s[0], *_split_bf16(cr)) - _dot3(xk_s[1], *_split_bf16(ci))
    rowk = lax.broadcasted_iota(jnp.int32, (LANES, LANES), 0)
    colk = lax.broadcasted_iota(jnp.int32, (LANES, LANES), 1)
    m_k = rowk // SSM_GROUP == colk // SSM_GROUP
    for a in range(N_OCT):
        ls = slice(LANES * a, LANES * (a + 1))
        kb_ref[0, a] = jnp.where(m_k, kfull[ls, ls], 0.0).astype(BF16)

    p1_re, p1_im = pm_s[0], pm_s[1]
    w_re, w_im = _cmul(cr, ci, p1_re, p1_im)
    rowo = lax.broadcasted_iota(jnp.int32, (OCT_STATE, LANES), 0)
    colo = lax.broadcasted_iota(jnp.int32, (OCT_STATE, LANES), 1)
    m_out = rowo // SSM_STATE == colo // SSM_GROUP
    for a in range(N_OCT):
        ls = slice(LANES * a, LANES * (a + 1))
        wout_ref[0, a, :OCT_STATE, :] = jnp.where(m_out, jnp.tile(w_re[:, ls], (OCT, 1)), 0.0).astype(BF16)
        wout_ref[0, a, OCT_STATE:, :] = jnp.where(m_out, jnp.tile(-w_im[:, ls], (OCT, 1)), 0.0).astype(BF16)

    pw_s[2], pw_s[3] = _cmul(pw_s[2], pw_s[3], pw_s[4], pw_s[5])
    pm_s[0], pm_s[1] = _cmul(p1_re, p1_im, pm_s[2], pm_s[3])


def _s5_prep(lam_re, lam_im, b_re, b_im, c_re, c_im, log_dt):
    g, n, p = SSM_GROUPS, SSM_STATE, SSM_GROUP
    lam = jnp.stack([lam_re, lam_im, jnp.broadcast_to(log_dt[:, None], (g, n))])
    b = jnp.stack([b_re, b_im])
    c = jnp.stack([c_re, c_im])
    args = (jnp.tile(lam, (1, 1, OCT)),
            jnp.tile(b.transpose(0, 1, 3, 2).reshape(2, g * p, n), (1, 1, OCT)),
            c.transpose(0, 3, 1, 2).reshape(2, n, g * p),
            jnp.repeat(lam.transpose(0, 2, 1), p, axis=2),
            lam.reshape(3, N_OCT, OCT_STATE))
    mat = (S5_QP, N_OCT, 2, LANES, 2 * OCT_STATE)
    blk = (1, N_OCT, 1, LANES, 2 * OCT_STATE)
    tab = (N_OCT, S5_PT_ROWS, OCT_STATE)
    kb, win, woutp, ptre, ptim = pl.pallas_call(
        _s5_prep_kernel,
        grid=(S5_Q,),
        in_specs=[_const_spec(a.shape) for a in args],
        out_specs=[
            pl.BlockSpec((1, N_OCT, LANES, LANES), lambda t: (t, 0, 0, 0)),
            pl.BlockSpec(blk, lambda t: ((S5_Q - 1 - t) // 2, 0, (S5_Q - 1 - t) % 2, 0, 0)),
            pl.BlockSpec((1, N_OCT, 2 * OCT_STATE, LANES), lambda t: (t // 2, 0, 0, t % 2)),
            pl.BlockSpec(tab, lambda t: (0, 0, 0)),
            pl.BlockSpec(tab, lambda t: (0, 0, 0)),
        ],
        out_shape=[
            jax.ShapeDtypeStruct((S5_Q, N_OCT, LANES, LANES), BF16),
            jax.ShapeDtypeStruct(mat, BF16),
            jax.ShapeDtypeStruct((S5_QP, N_OCT, 2 * OCT_STATE, 2 * LANES), BF16),
            jax.ShapeDtypeStruct(tab, F32),
            jax.ShapeDtypeStruct(tab, F32),
        ],
        scratch_shapes=[pltpu.VMEM((6, SSM_GROUPS, SSM_WIDTH), F32),
                        pltpu.VMEM((4, SSM_STATE, SSM_WIDTH), F32),
                        pltpu.VMEM((2, SSM_WIDTH, SSM_STATE), F32)],
        compiler_params=pltpu.CompilerParams(
            dimension_semantics=("arbitrary",), vmem_limit_bytes=VMEM_LIMIT),
        name="s5_prep",
    )(*args)

    winp = win.reshape(S5_QP, N_OCT, 2 * LANES, 2 * OCT_STATE)
    return kb, winp, woutp, ptre, ptim


def _gelu_tanh(x):
    c = 0.7978845608028654
    return x * (0.5 * (1.0 + jnp.tanh(c * (x + 0.044715 * (x * x * x)))))


def _s5_kernel(u0_ref, u1_ref, u2_ref, u3_ref, z_ref, kb_ref, winp_ref, woutp_ref, ptre_ref, ptim_ref,
               d_ref, wglu_ref, bglu_ref, o_ref, y0_s, y1_s, y2_s, y3_s, cre_scr, cim_scr):
    u_refs = (u0_ref, u1_ref, u2_ref, u3_ref)
    y_scrs = (y0_s, y1_s, y2_s, y3_s)

    @pl.when(pl.program_id(1) == 0)
    def _():
        cre_scr[...] = jnp.zeros_like(cre_scr)
        cim_scr[...] = jnp.zeros_like(cim_scr)

    def tok(a, s):
        return jnp.concatenate(
            [u_refs[a][pl.ds(s + S5_Q * v, S5_SEG, stride=S5_SEG_PITCH), :] for v in range(S5_NV)], axis=0)

    xp = [[jnp.concatenate([tok(a, 2 * sp), tok(a, 2 * sp + 1)], axis=1).astype(BF16) for a in range(N_OCT)]
          for sp in range(S5_QP)]

    def cmul_add(b_re, b_im, m_re, m_im, x_re, x_im):
        return b_re + m_re * x_re - m_im * x_im, b_im + m_re * x_im + m_im * x_re

    hs = []
    for a in range(N_OCT):
        acc = None
        for sp in range(S5_QP):
            part = jnp.dot(xp[sp][a], winp_ref[sp, a], preferred_element_type=F32)
            acc = part if acc is None else acc + part
        blk = lambda v: (acc[S5_SEG * v:S5_SEG * (v + 1), :OCT_STATE], acc[S5_SEG * v:S5_SEG * (v + 1), OCT_STATE:])
        m_re, m_im = ptre_ref[a, 1:2, :], ptim_ref[a, 1:2, :]
        loc = [blk(0)]
        for v in range(1, S5_NV):
            loc.append(cmul_add(*blk(v), m_re, m_im, *loc[-1]))
        l_re, l_im = ptre_ref[a, S5_NV:S5_NV + 1, :], ptim_ref[a, S5_NV:S5_NV + 1, :]
        c_re, c_im = cre_scr[a:a + 1, :], cim_scr[a:a + 1, :]
        carry = []
        for r in range(S5_SEG):
            carry.append((c_re, c_im))
            c_re, c_im = cmul_add(loc[-1][0][r:r + 1, :], loc[-1][1][r:r + 1, :], l_re, l_im, c_re, c_im)
        cre_scr[a:a + 1, :] = c_re
        cim_scr[a:a + 1, :] = c_im
        cs_re = jnp.concatenate([c[0] for c in carry], axis=0)
        cs_im = jnp.concatenate([c[1] for c in carry], axis=0)
        ent = [(cs_re, cs_im)]
        for v in range(S5_NV - 1):
            ent.append(cmul_add(*loc[v], ptre_ref[a, v + 1:v + 2, :], ptim_ref[a, v + 1:v + 2, :], cs_re, cs_im))
        hs.append(jnp.concatenate([jnp.concatenate([e[0] for e in ent], axis=0),
                                   jnp.concatenate([e[1] for e in ent], axis=0)], axis=1).astype(BF16))

    def pair_tile(d, a):
        below = kb_ref[2 * d - 1, a] if d > 0 else jnp.zeros((LANES, LANES), BF16)
        return jnp.concatenate([jnp.concatenate([kb_ref[2 * d, a], kb_ref[2 * d + 1, a]], axis=1),
                                jnp.concatenate([below, kb_ref[2 * d, a]], axis=1)], axis=0)

    for a in range(N_OCT):
        tiles = [pair_tile(d, a) for d in range(S5_QP)]
        for tp in range(S5_QP):
            acc = jnp.dot(hs[a], woutp_ref[tp, a], preferred_element_type=F32)
            for sp in range(tp + 1):
                acc = acc + jnp.dot(xp[sp][a], tiles[tp - sp], preferred_element_type=F32)
            for v in range(S5_NV):
                rows = slice(S5_SEG * v, S5_SEG * (v + 1))
                y_scrs[a][pl.ds(2 * tp + S5_Q * v, S5_SEG, stride=S5_SEG_PITCH), :] = acc[rows, :LANES]
                y_scrs[a][pl.ds(2 * tp + 1 + S5_Q * v, S5_SEG, stride=S5_SEG_PITCH), :] = acc[rows, LANES:]
        d_a = d_ref[:, LANES * a:LANES * (a + 1)]
        for r in range(S5_SEG):
            prow = slice(S5_SEG_PITCH * r, S5_SEG_PITCH * r + S5_SEG_ROWS)
            y_scrs[a][prow, :] = _gelu_tanh(y_scrs[a][prow, :] + d_a * u_refs[a][prow, :])

    unpad = lambda ref: jnp.concatenate(
        [ref[S5_SEG_PITCH * r:S5_SEG_PITCH * r + S5_SEG_ROWS, :] for r in range(S5_SEG)], axis=0)
    y = jnp.concatenate([unpad(r) for r in y_scrs], axis=1)
    gate = jnp.dot(y.astype(BF16), wglu_ref[...].astype(BF16), preferred_element_type=F32) + bglu_ref[...]
    z = z_ref[...].astype(F32)
    o_ref[...] = ((0.25 * y) * z * ((1.0 + jnp.tanh(0.5 * gate)) * (1.0 + jnp.tanh(0.5 * z)))).astype(BF16)


def _s5(u32, pbf, kb, winp, woutp, ptre, ptim, d_row, w_glu, b_glu, batch, seqlen):
    nb = seqlen // S5_TB
    t = batch * seqlen
    u_tile = lambda a: pl.BlockSpec((S5_SEG * S5_SEG_PITCH, LANES), lambda b, i, a=a: (b * nb + i, a))
    return pl.pallas_call(
        _s5_kernel,
        grid=(batch, nb),
        in_specs=[u_tile(a) for a in range(N_OCT)] + [
            pl.BlockSpec((S5_TB, SSM_WIDTH), lambda b, i: (b * nb + i, PBF_ZA)),
            _const_spec(kb.shape), _const_spec(winp.shape), _const_spec(woutp.shape),
            _const_spec(ptre.shape), _const_spec(ptim.shape),
            _const_spec((1, SSM_WIDTH)), _const_spec((SSM_WIDTH, SSM_WIDTH)), _const_spec((1, SSM_WIDTH)),
        ],
        out_specs=pl.BlockSpec((S5_TB, SSM_WIDTH), lambda b, i: (b * nb + i, 0)),
        out_shape=jax.ShapeDtypeStruct((t, SSM_WIDTH), BF16),
        scratch_shapes=[pltpu.VMEM((S5_SEG * S5_SEG_PITCH, LANES), F32)] * N_OCT + [
            pltpu.VMEM((SUBLANES, OCT_STATE), F32),
            pltpu.VMEM((SUBLANES, OCT_STATE), F32),
        ],
        compiler_params=pltpu.CompilerParams(
            dimension_semantics=("arbitrary", "arbitrary"), vmem_limit_bytes=VMEM_LIMIT),
        name="s5",
    )(u32, u32, u32, u32, pbf, kb, winp, woutp, ptre, ptim, d_row, w_glu, b_glu)


def _stage_weights(w_hbm, wbf_s, stage_s, sem):
    n_chunks = IN_WIDTH // W_CHUNK

    def chunk_copy(c):
        return pltpu.make_async_copy(w_hbm.at[:, pl.ds(c * W_CHUNK, W_CHUNK)], stage_s.at[c % 2], sem.at[c % 2])

    chunk_copy(0).start()
    for c in range(n_chunks):
        if c + 1 < n_chunks:
            chunk_copy(c + 1).start()
        chunk_copy(c).wait()
        wbf_s[:, c * W_CHUNK:(c + 1) * W_CHUNK] = stage_s[c % 2].astype(BF16)


def _proj_hgrn_kernel(x_ref, nw_ref, w_hbm, lbl_ref, hnw_ref, ou_ref, or_ref, o_ref,
                      wbf_s, stage_s, stage_sem, ph_s, q0_s, qc_s, kt_s, k0_s, k1_s, k2_s, ke_s, v_s, dec_s, oacc_s,
                      st_s, upd_s, sc_s, am_s,
                      *, blocks_per_seq):
    @pl.when(pl.program_id(0) == 0)
    def _():
        _stage_weights(w_hbm, wbf_s, stage_s, stage_sem)

    @pl.when(pl.program_id(0) % blocks_per_seq == 0)
    def _():
        st_s[...] = jnp.zeros_like(st_s)

    x = x_ref[...]
    ms = jnp.mean(x * x, axis=-1, keepdims=True)
    xn = (x * lax.rsqrt(ms + NORM_EPS) * nw_ref[...]).astype(BF16)
    proj = lambda lo, hi: jnp.dot(xn, wbf_s[:, lo:hi], preferred_element_type=F32)
    ph_s[...] = proj(COL_H, COL_G)
    u = proj(0, COL_ZA)
    for r in range(HG_TB // S5_SEG_ROWS):
        ou_ref[S5_SEG_PITCH * r:S5_SEG_PITCH * r + S5_SEG_ROWS, :] = u[S5_SEG_ROWS * r:S5_SEG_ROWS * (r + 1), :]
        ou_ref[S5_SEG_PITCH * r + S5_SEG_ROWS:S5_SEG_PITCH * (r + 1), :] = jnp.zeros((SUBLANES, SSM_WIDTH), F32)
    or_ref[:, :2 * D_MODEL] = proj(COL_G, IN_WIDTH).astype(BF16)
    or_ref[:, 2 * D_MODEL:] = proj(COL_ZA, COL_H).astype(BF16)
    col = lambda k: ph_s[:, HGRN_WIDTH * k:HGRN_WIDTH * (k + 1)]

    logits = lbl_ref[...]
    e = jnp.exp(logits - jnp.max(logits, axis=0, keepdims=True))
    lb = (e / jnp.sum(e, axis=0, keepdims=True))[0:1, :]

    q = col(PH_Q)
    qf = q * _sigmoid(q)
    forget = lb + (1.0 - lb) * _sigmoid(col(PH_F))
    lf = jnp.log(forget)
    key = 1.0 - forget

    row = lax.broadcasted_iota(jnp.int32, (HG_TB, HGRN_WIDTH), 0)
    r_sub = row % HG_SUB
    r_ch = row % HG_CH

    def down(x, d):
        return pltpu.roll(x, d, 0)

    def up(x, d):
        return pltpu.roll(x, HG_TB - d, 0)

    a = lf
    d = 1
    while d < HG_SUB:
        a = a + jnp.where(r_sub >= d, down(a, d), 0.0)
        d *= 2
    a3 = a.reshape(HG_TB // HG_SUB, HG_SUB, HGRN_WIDTH)
    tsub = jnp.broadcast_to(a3[:, HG_SUB - 1:HG_SUB, :], a3.shape).reshape(HG_TB, HGRN_WIDTH)
    n_sub = HG_CH // HG_SUB
    prev = [jnp.where(r_ch >= HG_SUB * k, down(tsub, HG_SUB * k), 0.0) for k in range(1, n_sub)]
    nxt = [jnp.where(r_ch < HG_CH - HG_SUB * k, up(tsub, HG_SUB * k), 0.0) for k in range(1, n_sub)]
    eprev = prev[0] + prev[1] + prev[2]
    enext = nxt[0] + nxt[1] + nxt[2]
    suf = tsub - a

    hsub = 0.5 * tsub
    q0_s[...] = (qf * jnp.exp(a - hsub)).astype(BF16)
    qc_s[...] = (qf * jnp.exp(a + eprev)).astype(BF16)
    kt_s[...] = (key * jnp.exp(hsub - a)).astype(BF16)
    k0 = key * jnp.exp(suf)
    e0, e1, e2 = (jnp.exp(0.5 * n) for n in nxt)
    k0_s[...] = (k0 * e0).astype(BF16)
    k1 = k0 * (e0 * e0)
    k1_s[...] = (k1 * e1).astype(BF16)
    k2 = k1 * (e1 * e1)
    k2_s[...] = (k2 * e2).astype(BF16)
    ke_s[...] = (k2 * (e2 * e2)).astype(BF16)
    dec_s[...] = jnp.exp(eprev + tsub + enext)
    v_s[...] = col(PH_I).astype(BF16)

    tq = lax.broadcasted_iota(jnp.int32, (HG_CH, n_sub * HG_CH), 0)
    cc = lax.broadcasted_iota(jnp.int32, (HG_CH, n_sub * HG_CH), 1)
    cls = cc // HG_CH
    ts = cc % HG_CH
    bi = tq // HG_SUB
    bj = ts // HG_SUB
    mask4 = ((cls == 0) & (bi == bj) & (ts <= tq)) | ((cls > 0) & ((bi - bj) == cls))

    n_ch = HG_TB // HG_CH
    half = n_sub * HG_CH // 2
    units = [(c, h, slice(c * HG_CH, (c + 1) * HG_CH), slice(HEAD_DIM * h, HEAD_DIM * (h + 1)))
             for c in range(n_ch) for h in range(HEADS)]
    for c, h, rows, ls in units:
        kcat = jnp.concatenate([kt_s[rows, ls], k0_s[rows, ls], k1_s[rows, ls], k2_s[rows, ls]], axis=0)
        sc_s[c * HEADS + h] = lax.dot_general(q0_s[rows, ls], kcat, (((1,), (1,)), ((), ())),
                                              preferred_element_type=F32)
    for c, h, rows, ls in units:
        upd_s[c * HEADS + h] = lax.dot_general(v_s[rows, ls], ke_s[rows, ls], (((0,), (0,)), ((), ())),
                                               preferred_element_type=F32)
    for c, h, rows, ls in units:
        sc = jnp.where(mask4, sc_s[c * HEADS + h], 0.0)
        am_s[c * HEADS + h] = (sc[:, :half] + sc[:, half:]).astype(BF16)
    for c, h, rows, ls in units:
        vv = v_s[rows, ls]
        oacc_s[rows, ls] = jnp.dot(am_s[c * HEADS + h], jnp.concatenate([vv, vv], axis=0),
                                   preferred_element_type=F32)
    for h in range(HEADS):
        ls = slice(HEAD_DIM * h, HEAD_DIM * (h + 1))
        st = st_s[h]
        for c in range(n_ch):
            rows = slice(c * HG_CH, (c + 1) * HG_CH)
            oacc_s[rows, ls] += lax.dot_general(qc_s[rows, ls], st.astype(BF16), (((1,), (1,)), ((), ())),
                                                preferred_element_type=F32)
            st = st * dec_s[c * HG_CH:c * HG_CH + 1, ls] + upd_s[c * HEADS + h]
        st_s[h] = st

    o = oacc_s[...] * _sigmoid(col(PH_OG))
    parts = []
    for h in range(HEADS):
        oh = o[:, HEAD_DIM * h:HEAD_DIM * (h + 1)]
        ms = jnp.mean(oh * oh, axis=-1, keepdims=True)
        parts.append(oh * lax.rsqrt(ms + NORM_EPS))
    o = jnp.concatenate(parts, axis=1) * hnw_ref[...]
    z = col(PH_ZB)
    o_ref[...] = (o * (z * _sigmoid(z))).astype(BF16)


def _proj_hgrn(x2, norm_w, w, lb_logits, hgrn_norm_w, seqlen):
    t = x2.shape[0]
    wide = (HG_TB, HGRN_WIDTH)
    units = HG_TB // HG_CH * HEADS
    n_cls = HG_CH // HG_SUB
    row_blk = lambda w: pl.BlockSpec((HG_TB, w), lambda i: (i, 0))
    u_rows = HG_TB // S5_SEG_ROWS * S5_SEG_PITCH
    return pl.pallas_call(
        functools.partial(_proj_hgrn_kernel, blocks_per_seq=seqlen // HG_TB),
        grid=(t // HG_TB,),
        in_specs=[
            row_blk(D_MODEL), _const_spec((1, D_MODEL)),
            pl.BlockSpec(memory_space=pl.ANY),
            _const_spec(lb_logits.shape), _const_spec((1, HGRN_WIDTH)),
        ],
        out_specs=[pl.BlockSpec((u_rows, SSM_WIDTH), lambda i: (i, 0)), row_blk(PBF_WIDTH), row_blk(HGRN_WIDTH)],
        out_shape=[
            jax.ShapeDtypeStruct((t // HG_TB * u_rows, SSM_WIDTH), F32),
            jax.ShapeDtypeStruct((t, PBF_WIDTH), BF16),
            jax.ShapeDtypeStruct((t, HGRN_WIDTH), BF16),
        ],
        scratch_shapes=[
            pltpu.VMEM((D_MODEL, IN_WIDTH), BF16), pltpu.VMEM((2, D_MODEL, W_CHUNK), F32),
            pltpu.SemaphoreType.DMA((2,)),
            pltpu.VMEM((HG_TB, PH_WIDTH), F32)] + [pltpu.VMEM(wide, BF16)] * 8 + [
            pltpu.VMEM(wide, F32), pltpu.VMEM(wide, F32),
            pltpu.VMEM((HEADS, HEAD_DIM, HEAD_DIM), F32),
            pltpu.VMEM((units, HEAD_DIM, HEAD_DIM), F32),
            pltpu.VMEM((units, HG_CH, n_cls * HG_CH), F32),
            pltpu.VMEM((units, HG_CH, n_cls * HG_CH // 2), BF16)],
        compiler_params=pltpu.CompilerParams(
            dimension_semantics=("arbitrary",), vmem_limit_bytes=VMEM_LIMIT),
        name="proj_hgrn",
    )(x2, norm_w, w, lb_logits, hgrn_norm_w)


def _merge_kernel(x_ref, ya_ref, yb_ref, ga_ref, gb_ref, wpa_ref, wpb_ref, wo_ref, fnw_ref, o_ref):
    pa = jnp.dot(ya_ref[...], wpa_ref[...].astype(BF16), preferred_element_type=F32)
    pb = jnp.dot(yb_ref[...], wpb_ref[...].astype(BF16), preferred_element_type=F32)
    merged = _sigmoid(ga_ref[...].astype(F32)) * pa + _sigmoid(gb_ref[...].astype(F32)) * pb
    h = x_ref[...] + jnp.dot(merged.astype(BF16), wo_ref[...].astype(BF16), preferred_element_type=F32)
    ms = jnp.mean(h * h, axis=-1, keepdims=True)
    o_ref[...] = h * lax.rsqrt(ms + NORM_EPS) * fnw_ref[...]


def _merge(x2, ya, yb, pbf, w_pa, w_pb, w_out, fnw):
    t = x2.shape[0]
    return pl.pallas_call(
        _merge_kernel,
        grid=(t // TM_OUT,),
        in_specs=[
            pl.BlockSpec((TM_OUT, D_MODEL), lambda i: (i, 0)),
            pl.BlockSpec((TM_OUT, SSM_WIDTH), lambda i: (i, 0)),
            pl.BlockSpec((TM_OUT, HGRN_WIDTH), lambda i: (i, 0)),
            pl.BlockSpec((TM_OUT, D_MODEL), lambda i: (i, 0)),
            pl.BlockSpec((TM_OUT, D_MODEL), lambda i: (i, 1)),
            _const_spec((SSM_WIDTH, D_MODEL)), _const_spec((HGRN_WIDTH, D_MODEL)),
            _const_spec((D_MODEL, D_MODEL)), _const_spec((1, D_MODEL)),
        ],
        out_specs=pl.BlockSpec((TM_OUT, D_MODEL), lambda i: (i, 0)),
        out_shape=jax.ShapeDtypeStruct((t, D_MODEL), F32),
        compiler_params=pltpu.CompilerParams(
            dimension_semantics=("arbitrary",), vmem_limit_bytes=VMEM_LIMIT),
        name="merge",
    )(x2, ya, yb, pbf, pbf, w_pa, w_pb, w_out, fnw)


def kernel(x, norm_w, w_in, ssm_lambda_re, ssm_lambda_im, ssm_b_re, ssm_b_im, ssm_c_re, ssm_c_im, ssm_d,
           ssm_log_dt, ssm_w_glu, ssm_b_glu, hgrn_lb_logits, hgrn_norm_w, w_proj_a, w_proj_b, w_out,
           final_norm_w):
    batch, seqlen, _ = x.shape
    assert norm_w.shape[0] == 1, "single-layer block"
    assert seqlen % S5_TB == 0 and seqlen % HG_TB == 0 and HG_TB % S5_SEG_ROWS == 0
    x2 = x.reshape(batch * seqlen, D_MODEL)
    u32, pbf, yb = _proj_hgrn(x2, norm_w[0][None, :], w_in[0], hgrn_lb_logits,
                              hgrn_norm_w[0][None, :], seqlen)

    kb, winp, woutp, ptre, ptim = _s5_prep(ssm_lambda_re[0], ssm_lambda_im[0], ssm_b_re[0], ssm_b_im[0],
                                           ssm_c_re[0], ssm_c_im[0], ssm_log_dt[0])
    ya = _s5(u32, pbf, kb, winp, woutp, ptre, ptim, ssm_d[0].reshape(1, SSM_WIDTH),
             ssm_w_glu[0], ssm_b_glu[0][None, :], batch, seqlen)
    out = _merge(x2, ya, yb, pbf, w_proj_a[0], w_proj_b[0], w_out[0], final_norm_w[None, :])
    return out.reshape(batch, seqlen, D_MODEL)
```

```python
import functools

import jax
import jax.numpy as jnp
from jax import lax
from jax.experimental import pallas as pl
from jax.experimental.pallas import tpu as pltpu

F32 = jnp.float32
BF16 = jnp.bfloat16

D_MODEL = 1024
SSM_WIDTH = 512
SSM_GROUP = 16
SSM_GROUPS = 32
SSM_STATE = 64
HGRN_WIDTH = 512
HEAD_DIM = 128
HEADS = 4
NORM_EPS = 1e-6
LAMBDA_RE_MAX = -1e-4

IN_WIDTH = 2 * SSM_WIDTH + 5 * HGRN_WIDTH + 2 * D_MODEL
COL_ZA, COL_H, COL_G = SSM_WIDTH, 2 * SSM_WIDTH, 2 * SSM_WIDTH + 5 * HGRN_WIDTH
PH_WIDTH = 5 * HGRN_WIDTH
PH_Q, PH_F, PH_I, PH_OG, PH_ZB = 0, 1, 2, 3, 4
PBF_WIDTH = 2 * D_MODEL + SSM_WIDTH
PBF_ZA = 4

LANES = 128
SUBLANES = 8
OCT = LANES // SSM_GROUP
N_OCT = SSM_GROUPS // OCT
OCT_STATE = OCT * SSM_STATE

S5_Q = 4
S5_QP = S5_Q // 2
S5_TB = 2048
S5_NCH = S5_TB // S5_Q
S5_SEG = SUBLANES
S5_NV = S5_NCH // S5_SEG
S5_PT_ROWS = S5_NV + SUBLANES
S5_SEG_ROWS = S5_Q * S5_NV
S5_TAIL_SEGS = 2
S5_SEG_PITCH = S5_SEG_ROWS + SUBLANES
HG_TB = 512
HG_CH = 64
HG_SUB = 16
W_CHUNK = 512
TM_OUT = 1024
MERGE_ROWS = 512

V7X_VMEM_BYTES = 64 * 1024 * 1024
VMEM_LIMIT = V7X_VMEM_BYTES * 7 // 8


def _sigmoid(x):
    return 0.5 * jnp.tanh(0.5 * x) + 0.5


def _const_spec(shape):
    nd = len(shape)
    return pl.BlockSpec(shape, lambda *_: (0,) * nd, pipeline_mode=pl.Buffered(1))


def _disc(lam_re, lam_im, log_dt):
    return jnp.minimum(lam_re, LAMBDA_RE_MAX), lam_im, jnp.exp(log_dt)


def _cpow(lr, li, dt, k):
    mag = jnp.exp(k * (lr * dt))
    ang = k * (li * dt)
    return mag * jnp.cos(ang), mag * jnp.sin(ang)


def _split_bf16(x):
    hi = x.astype(BF16)
    return hi, (x - hi.astype(F32)).astype(BF16)


def _dot3(a, b_hi, b_lo):
    a_hi, a_lo = _split_bf16(a)
    d = lambda p, q: jnp.dot(p, q, preferred_element_type=F32)
    return d(a_hi, b_hi) + (d(a_hi, b_lo) + d(a_lo, b_hi))


def _cmul(a_re, a_im, b_re, b_im):
    return a_re * b_re - a_im * b_im, a_re * b_im + a_im * b_re


def _s5_prep_kernel(lam_c, b_t, c_n, lam_n, lam_r,
                    kb_ref, win_ref, wout_ref, ptre_ref, ptim_ref, pw_s, pm_s, xk_s):
    @pl.when(pl.program_id(0) == 0)
    def _():
        lr, li, dt = _disc(lam_c[0], lam_c[1], lam_c[2])
        ab_re, ab_im = _cpow(lr, li, dt, 1.0)
        den = lr * lr + li * li
        nr = ab_re - 1.0
        pw_s[0] = (nr * lr + ab_im * li) / den
        pw_s[1] = (ab_im * lr - nr * li) / den
        pw_s[2] = jnp.ones_like(ab_re)
        pw_s[3] = jnp.zeros_like(ab_re)
        pw_s[4] = ab_re
        pw_s[5] = ab_im
        m_re, m_im = _cpow(*_disc(lam_n[0], lam_n[1], lam_n[2]), 1.0)
        pm_s[0] = m_re
        pm_s[1] = m_im
        pm_s[2] = m_re
        pm_s[3] = m_im
        j = lax.broadcasted_iota(jnp.int32, (S5_PT_ROWS, OCT_STATE), 0).astype(F32) * float(S5_Q)
        for a in range(N_OCT):
            t_re, t_im = _cpow(*_disc(lam_r[0, a:a + 1, :], lam_r[1, a:a + 1, :], lam_r[2, a:a + 1, :]), j)
            ptre_ref[a] = t_re
            ptim_ref[a] = t_im

    col = lax.broadcasted_iota(jnp.int32, (SSM_GROUP, SSM_WIDTH), 1)

    def strip(g, carry):
        rows = pl.ds(pl.multiple_of(g * SSM_GROUP, SSM_GROUP), SSM_GROUP)
        grow = pl.ds(g, 1)
        coef_re, coef_im = pw_s[0, grow, :], pw_s[1, grow, :]
        p_re, p_im = pw_s[2, grow, :], pw_s[3, grow, :]
        bb_re, bb_im = _cmul(coef_re, coef_im, b_t[0, rows, :], b_t[1, rows, :])
        x_re, x_im = _cmul(bb_re, bb_im, p_re, p_im)
        xk_s[0, rows, :] = x_re[:, :SSM_STATE]
        xk_s[1, rows, :] = x_im[:, :SSM_STATE]
        a = g // OCT
        lrows = pl.ds(pl.multiple_of((g % OCT) * SSM_GROUP, SSM_GROUP), SSM_GROUP)
        m_in = col // SSM_STATE == g % OCT
        win_ref[0, a, 0, lrows, :] = jnp.concatenate(
            [jnp.where(m_in, x_re, 0.0), jnp.where(m_in, x_im, 0.0)], axis=1).astype(BF16)
        return carry

    lax.fori_loop(0, SSM_GROUPS, strip, 0, unroll=4)

    cr, ci = c_n[0], c_n[1]
    kfull = _dot3(xk_s[0], *_split_bf16(cr)) - _dot3(xk_s[1], *_split_bf16(ci))
    rowk = lax.broadcasted_iota(jnp.int32, (LANES, LANES), 0)
    colk = lax.broadcasted_iota(jnp.int32, (LANES, LANES), 1)
    m_k = rowk // SSM_GROUP == colk // SSM_GROUP
    for a in range(N_OCT):
        ls = slice(LANES * a, LANES * (a + 1))
        kb_ref[0, a] = jnp.where(m_k, kfull[ls, ls], 0.0).astype(BF16)

    p1_re, p1_im = pm_s[0], pm_s[1]
    w_re, w_im = _cmul(cr, ci, p1_re, p1_im)
    rowo = lax.broadcasted_iota(jnp.int32, (OCT_STATE, LANES), 0)
    colo = lax.broadcasted_iota(jnp.int32, (OCT_STATE, LANES), 1)
    m_out = rowo // SSM_STATE == colo // SSM_GROUP
    for a in range(N_OCT):
        ls = slice(LANES * a, LANES * (a + 1))
        wout_ref[0, a, :OCT_STATE, :] = jnp.where(m_out, jnp.tile(w_re[:, ls], (OCT, 1)), 0.0).astype(BF16)
        wout_ref[0, a, OCT_STATE:, :] = jnp.where(m_out, jnp.tile(-w_im[:, ls], (OCT, 1)), 0.0).astype(BF16)

    pw_s[2], pw_s[3] = _cmul(pw_s[2], pw_s[3], pw_s[4], pw_s[5])
    pm_s[0], pm_s[1] = _cmul(p1_re, p1_im, pm_s[2], pm_s[3])


def _s5_prep(lam_re, lam_im, b_re, b_im, c_re, c_im, log_dt):
    g, n, p = SSM_GROUPS, SSM_STATE, SSM_GROUP
    lam = jnp.stack([lam_re, lam_im, jnp.broadcast_to(log_dt[:, None], (g, n))])
    b = jnp.stack([b_re, b_im])
    c = jnp.stack([c_re, c_im])
    args = (jnp.tile(lam, (1, 1, OCT)),
            jnp.tile(b.transpose(0, 1, 3, 2).reshape(2, g * p, n), (1, 1, OCT)),
            c.transpose(0, 3, 1, 2).reshape(2, n, g * p),
            jnp.repeat(lam.transpose(0, 2, 1), p, axis=2),
            lam.reshape(3, N_OCT, OCT_STATE))
    mat = (S5_QP, N_OCT, 2, LANES, 2 * OCT_STATE)
    blk = (1, N_OCT, 1, LANES, 2 * OCT_STATE)
    tab = (N_OCT, S5_PT_ROWS, OCT_STATE)
    kb, win, woutp, ptre, ptim = pl.pallas_call(
        _s5_prep_kernel,
        grid=(S5_Q,),
        in_specs=[_const_spec(a.shape) for a in args],
        out_specs=[
            pl.BlockSpec((1, N_OCT, LANES, LANES), lambda t: (t, 0, 0, 0)),
            pl.BlockSpec(blk, lambda t: ((S5_Q - 1 - t) // 2, 0, (S5_Q - 1 - t) % 2, 0, 0)),
            pl.BlockSpec((1, N_OCT, 2 * OCT_STATE, LANES), lambda t: (t // 2, 0, 0, t % 2)),
            pl.BlockSpec(tab, lambda t: (0, 0, 0)),
            pl.BlockSpec(tab, lambda t: (0, 0, 0)),
        ],
        out_shape=[
            jax.ShapeDtypeStruct((S5_Q, N_OCT, LANES, LANES), BF16),
            jax.ShapeDtypeStruct(mat, BF16),
            jax.ShapeDtypeStruct((S5_QP, N_OCT, 2 * OCT_STATE, 2 * LANES), BF16),
            jax.ShapeDtypeStruct(tab, F32),
            jax.ShapeDtypeStruct(tab, F32),
        ],
        scratch_shapes=[pltpu.VMEM((6, SSM_GROUPS, SSM_WIDTH), F32),
                        pltpu.VMEM((4, SSM_STATE, SSM_WIDTH), F32),
                        pltpu.VMEM((2, SSM_WIDTH, SSM_STATE), F32)],
        compiler_params=pltpu.CompilerParams(
            dimension_semantics=("arbitrary",), vmem_limit_bytes=VMEM_LIMIT),
        name="s5_prep",
    )(*args)

    winp = win.reshape(S5_QP, N_OCT, 2 * LANES, 2 * OCT_STATE)
    return kb, winp, woutp, ptre, ptim


def _gelu_tanh(x):
    c = 0.7978845608028654
    return x * (0.5 * (1.0 + jnp.tanh(c * (x + 0.044715 * (x * x * x)))))


def _s5_kernel(u0_ref, u1_ref, u2_ref, u3_ref, z_ref, kb_ref, winp_ref, woutp_ref, ptre_ref, ptim_ref,
               d_ref, wglu_ref, bglu_ref, o_ref, y0_s, y1_s, y2_s, y3_s, cre_scr, cim_scr):
    u_refs = (u0_ref, u1_ref, u2_ref, u3_ref)
    y_scrs = (y0_s, y1_s, y2_s, y3_s)

    @pl.when(pl.program_id(1) == 0)
    def _():
        cre_scr[...] = jnp.zeros_like(cre_scr)
        cim_scr[...] = jnp.zeros_like(cim_scr)

    def tok(a, s):
        return jnp.concatenate(
            [u_refs[a][pl.ds(s + S5_Q * v, S5_SEG, stride=S5_SEG_PITCH), :] for v in range(S5_NV)], axis=0)

    xp = [[jnp.concatenate([tok(a, 2 * sp), tok(a, 2 * sp + 1)], axis=1).astype(BF16) for a in range(N_OCT)]
          for sp in range(S5_QP)]

    def cmul_add(b_re, b_im, m_re, m_im, x_re, x_im):
        return b_re + m_re * x_re - m_im * x_im, b_im + m_re * x_im + m_im * x_re

    hs = []
    for a in range(N_OCT):
        acc = None
        for sp in range(S5_QP):
            part = jnp.dot(xp[sp][a], winp_ref[sp, a], preferred_element_type=F32)
            acc = part if acc is None else acc + part
        blk = lambda v: (acc[S5_SEG * v:S5_SEG * (v + 1), :OCT_STATE], acc[S5_SEG * v:S5_SEG * (v + 1), OCT_STATE:])
        m_re, m_im = ptre_ref[a, 1:2, :], ptim_ref[a, 1:2, :]
        loc = [blk(0)]
        for v in range(1, S5_NV):
            loc.append(cmul_add(*blk(v), m_re, m_im, *loc[-1]))
        l_re, l_im = ptre_ref[a, S5_NV:S5_NV + 1, :], ptim_ref[a, S5_NV:S5_NV + 1, :]
        c_re, c_im = cre_scr[a:a + 1, :], cim_scr[a:a + 1, :]
        carry = []
        for r in range(S5_SEG):
            carry.append((c_re, c_im))
            c_re, c_im = cmul_add(loc[-1][0][r:r + 1, :], loc[-1][1][r:r + 1, :], l_re, l_im, c_re, c_im)
        cre_scr[a:a + 1, :] = c_re
        cim_scr[a:a + 1, :] = c_im
        cs_re = jnp.concatenate([c[0] for c in carry], axis=0)
        cs_im = jnp.concatenate([c[1] for c in carry], axis=0)
        ent = [(cs_re, cs_im)]
        for v in range(S5_NV - 1):
            ent.append(cmul_add(*loc[v], ptre_ref[a, v + 1:v + 2, :], ptim_ref[a, v + 1:v + 2, :], cs_re, cs_im))
        hs.append(jnp.concatenate([jnp.concatenate([e[0] for e in ent], axis=0),
                                   jnp.concatenate([e[1] for e in ent], axis=0)], axis=1).astype(BF16))

    def pair_tile(d, a):
        below = kb_ref[2 * d - 1, a] if d > 0 else jnp.zeros((LANES, LANES), BF16)
        return jnp.concatenate([jnp.concatenate([kb_ref[2 * d, a], kb_ref[2 * d + 1, a]], axis=1),
                                jnp.concatenate([below, kb_ref[2 * d, a]], axis=1)], axis=0)

    for a in range(N_OCT):
        tiles = [pair_tile(d, a) for d in range(S5_QP)]
        for tp in range(S5_QP):
            acc = jnp.dot(hs[a], woutp_ref[tp, a], preferred_element_type=F32)
            for sp in range(tp + 1):
                acc = acc + jnp.dot(xp[sp][a], tiles[tp - sp], preferred_element_type=F32)
            for v in range(S5_NV):
                rows = slice(S5_SEG * v, S5_SEG * (v + 1))
                y_scrs[a][pl.ds(2 * tp + S5_Q * v, S5_SEG, stride=S5_SEG_PITCH), :] = acc[rows, :LANES]
                y_scrs[a][pl.ds(2 * tp + 1 + S5_Q * v, S5_SEG, stride=S5_SEG_PITCH), :] = acc[rows, LANES:]
        d_a = d_ref[:, LANES * a:LANES * (a + 1)]
        for r in range(S5_SEG):
            prow = slice(S5_SEG_PITCH * r, S5_SEG_PITCH * r + S5_SEG_ROWS)
            y_scrs[a][prow, :] = _gelu_tanh(y_scrs[a][prow, :] + d_a * u_refs[a][prow, :])

    wglu = wglu_ref[...].astype(BF16)
    for r0 in range(0, S5_SEG, S5_TAIL_SEGS):
        segs = range(r0, r0 + S5_TAIL_SEGS)
        trow = slice(S5_SEG_ROWS * r0, S5_SEG_ROWS * (r0 + S5_TAIL_SEGS))
        y = jnp.concatenate(
            [jnp.concatenate([ref[S5_SEG_PITCH * r:S5_SEG_PITCH * r + S5_SEG_ROWS, :] for r in segs], axis=0)
             for ref in y_scrs], axis=1)
        gate = jnp.dot(y.astype(BF16), wglu, preferred_element_type=F32) + bglu_ref[...]
        z = z_ref[trow, :].astype(F32)
        o_ref[trow, :] = ((0.25 * y) * z * ((1.0 + jnp.tanh(0.5 * gate)) * (1.0 + jnp.tanh(0.5 * z)))).astype(BF16)


def _s5(u32, pbf, kb, winp, woutp, ptre, ptim, d_row, w_glu, b_glu, batch, seqlen):
    nb = seqlen // S5_TB
    t = batch * seqlen
    u_tile = lambda a: pl.BlockSpec((S5_SEG * S5_SEG_PITCH, LANES), lambda b, i, a=a: (b * nb + i, a))
    return pl.pallas_call(
        _s5_kernel,
        grid=(batch, nb),
        in_specs=[u_tile(a) for a in range(N_OCT)] + [
            pl.BlockSpec((S5_TB, SSM_WIDTH), lambda b, i: (b * nb + i, PBF_ZA)),
            _const_spec(kb.shape), _const_spec(winp.shape), _const_spec(woutp.shape),
            _const_spec(ptre.shape), _const_spec(ptim.shape),
            _const_spec((1, SSM_WIDTH)), _const_spec((SSM_WIDTH, SSM_WIDTH)), _const_spec((1, SSM_WIDTH)),
        ],
        out_specs=pl.BlockSpec((S5_TB, SSM_WIDTH), lambda b, i: (b * nb + i, 0)),
        out_shape=jax.ShapeDtypeStruct((t, SSM_WIDTH), BF16),
        scratch_shapes=[pltpu.VMEM((S5_SEG * S5_SEG_PITCH, LANES), F32)] * N_OCT + [
            pltpu.VMEM((SUBLANES, OCT_STATE), F32),
            pltpu.VMEM((SUBLANES, OCT_STATE), F32),
        ],
        compiler_params=pltpu.CompilerParams(
            dimension_semantics=("arbitrary", "arbitrary"), vmem_limit_bytes=VMEM_LIMIT),
        name="s5",
    )(u32, u32, u32, u32, pbf, kb, winp, woutp, ptre, ptim, d_row, w_glu, b_glu)


def _stage_weights(w_hbm, wbf_s, stage_s, sem):
    n_chunks = IN_WIDTH // W_CHUNK

    def chunk_copy(c):
        return pltpu.make_async_copy(w_hbm.at[:, pl.ds(c * W_CHUNK, W_CHUNK)], stage_s.at[c % 2], sem.at[c % 2])

    chunk_copy(0).start()
    for c in range(n_chunks):
        if c + 1 < n_chunks:
            chunk_copy(c + 1).start()
        chunk_copy(c).wait()
        wbf_s[:, c * W_CHUNK:(c + 1) * W_CHUNK] = stage_s[c % 2].astype(BF16)


def _proj_hgrn_kernel(x_ref, nw_ref, w_hbm, lbl_ref, hnw_ref, ou_ref, or_ref, o_ref,
                      wbf_s, stage_s, stage_sem, ph_s, q0_s, qc_s, kt_s, k0_s, k1_s, k2_s, ke_s, v_s, dec_s, oacc_s,
                      st_s, upd_s, sc_s, am_s,
                      *, blocks_per_seq):
    @pl.when(pl.program_id(0) == 0)
    def _():
        _stage_weights(w_hbm, wbf_s, stage_s, stage_sem)

    @pl.when(pl.program_id(0) % blocks_per_seq == 0)
    def _():
        st_s[...] = jnp.zeros_like(st_s)

    x = x_ref[...]
    ms = jnp.mean(x * x, axis=-1, keepdims=True)
    xn = (x * lax.rsqrt(ms + NORM_EPS) * nw_ref[...]).astype(BF16)
    proj = lambda lo, hi: jnp.dot(xn, wbf_s[:, lo:hi], preferred_element_type=F32)
    ph_s[...] = proj(COL_H, COL_G)
    u = proj(0, COL_ZA)
    for r in range(HG_TB // S5_SEG_ROWS):
        ou_ref[S5_SEG_PITCH * r:S5_SEG_PITCH * r + S5_SEG_ROWS, :] = u[S5_SEG_ROWS * r:S5_SEG_ROWS * (r + 1), :]
        ou_ref[S5_SEG_PITCH * r + S5_SEG_ROWS:S5_SEG_PITCH * (r + 1), :] = jnp.zeros((SUBLANES, SSM_WIDTH), F32)
    or_ref[:, :2 * D_MODEL] = proj(COL_G, IN_WIDTH).astype(BF16)
    or_ref[:, 2 * D_MODEL:] = proj(COL_ZA, COL_H).astype(BF16)
    col = lambda k: ph_s[:, HGRN_WIDTH * k:HGRN_WIDTH * (k + 1)]

    logits = lbl_ref[...]
    e = jnp.exp(logits - jnp.max(logits, axis=0, keepdims=True))
    lb = (e / jnp.sum(e, axis=0, keepdims=True))[0:1, :]

    q = col(PH_Q)
    qf = q * _sigmoid(q)
    forget = lb + (1.0 - lb) * _sigmoid(col(PH_F))
    lf = jnp.log(forget)
    key = 1.0 - forget

    row = lax.broadcasted_iota(jnp.int32, (HG_TB, HGRN_WIDTH), 0)
    r_sub = row % HG_SUB
    r_ch = row % HG_CH

    def down(x, d):
        return pltpu.roll(x, d, 0)

    def up(x, d):
        return pltpu.roll(x, HG_TB - d, 0)

    a = lf
    d = 1
    while d < HG_SUB:
        a = a + jnp.where(r_sub >= d, down(a, d), 0.0)
        d *= 2
    a3 = a.reshape(HG_TB // HG_SUB, HG_SUB, HGRN_WIDTH)
    tsub = jnp.broadcast_to(a3[:, HG_SUB - 1:HG_SUB, :], a3.shape).reshape(HG_TB, HGRN_WIDTH)
    n_sub = HG_CH // HG_SUB
    prev = [jnp.where(r_ch >= HG_SUB * k, down(tsub, HG_SUB * k), 0.0) for k in range(1, n_sub)]
    nxt = [jnp.where(r_ch < HG_CH - HG_SUB * k, up(tsub, HG_SUB * k), 0.0) for k in range(1, n_sub)]
    eprev = prev[0] + prev[1] + prev[2]
    enext = nxt[0] + nxt[1] + nxt[2]
    suf = tsub - a

    hsub = 0.5 * tsub
    q0_s[...] = (qf * jnp.exp(a - hsub)).astype(BF16)
    qc_s[...] = (qf * jnp.exp(a + eprev)).astype(BF16)
    kt_s[...] = (key * jnp.exp(hsub - a)).astype(BF16)
    k0 = key * jnp.exp(suf)
    e0, e1, e2 = (jnp.exp(0.5 * n) for n in nxt)
    k0_s[...] = (k0 * e0).astype(BF16)
    k1 = k0 * (e0 * e0)
    k1_s[...] = (k1 * e1).astype(BF16)
    k2 = k1 * (e1 * e1)
    k2_s[...] = (k2 * e2).astype(BF16)
    ke_s[...] = (k2 * (e2 * e2)).astype(BF16)
    dec_s[...] = jnp.exp(eprev + tsub + enext)
    v_s[...] = col(PH_I).astype(BF16)

    tq = lax.broadcasted_iota(jnp.int32, (HG_CH, n_sub * HG_CH), 0)
    cc = lax.broadcasted_iota(jnp.int32, (HG_CH, n_sub * HG_CH), 1)
    cls = cc // HG_CH
    ts = cc % HG_CH
    bi = tq // HG_SUB
    bj = ts // HG_SUB
    mask4 = ((cls == 0) & (bi == bj) & (ts <= tq)) | ((cls > 0) & ((bi - bj) == cls))

    n_ch = HG_TB // HG_CH
    half = n_sub * HG_CH // 2
    units = [(c, h, slice(c * HG_CH, (c + 1) * HG_CH), slice(HEAD_DIM * h, HEAD_DIM * (h + 1)))
             for c in range(n_ch) for h in range(HEADS)]
    for c, h, rows, ls in units:
        kcat = jnp.concatenate([kt_s[rows, ls], k0_s[rows, ls], k1_s[rows, ls], k2_s[rows, ls]], axis=0)
        sc_s[c * HEADS + h] = lax.dot_general(q0_s[rows, ls], kcat, (((1,), (1,)), ((), ())),
                                              preferred_element_type=F32)
    for c, h, rows, ls in units:
        upd_s[c * HEADS + h] = lax.dot_general(v_s[rows, ls], ke_s[rows, ls], (((0,), (0,)), ((), ())),
                                               preferred_element_type=F32)
    for c, h, rows, ls in units:
        sc = jnp.where(mask4, sc_s[c * HEADS + h], 0.0)
        am_s[c * HEADS + h] = (sc[:, :half] + sc[:, half:]).astype(BF16)
    for c, h, rows, ls in units:
        vv = v_s[rows, ls]
        oacc_s[rows, ls] = jnp.dot(am_s[c * HEADS + h], jnp.concatenate([vv, vv], axis=0),
                                   preferred_element_type=F32)
    for h in range(HEADS):
        ls = slice(HEAD_DIM * h, HEAD_DIM * (h + 1))
        st = st_s[h]
        for c in range(n_ch):
            rows = slice(c * HG_CH, (c + 1) * HG_CH)
            oacc_s[rows, ls] += lax.dot_general(qc_s[rows, ls], st.astype(BF16), (((1,), (1,)), ((), ())),
                                                preferred_element_type=F32)
            st = st * dec_s[c * HG_CH:c * HG_CH + 1, ls] + upd_s[c * HEADS + h]
        st_s[h] = st

    o = oacc_s[...] * _sigmoid(col(PH_OG))
    parts = []
    for h in range(HEADS):
        oh = o[:, HEAD_DIM * h:HEAD_DIM * (h + 1)]
        ms = jnp.mean(oh * oh, axis=-1, keepdims=True)
        parts.append(oh * lax.rsqrt(ms + NORM_EPS))
    o = jnp.concatenate(parts, axis=1) * hnw_ref[...]
    z = col(PH_ZB)
    o_ref[...] = (o * (z * _sigmoid(z))).astype(BF16)


def _proj_hgrn(x2, norm_w, w, lb_logits, hgrn_norm_w, seqlen):
    t = x2.shape[0]
    wide = (HG_TB, HGRN_WIDTH)
    units = HG_TB // HG_CH * HEADS
    n_cls = HG_CH // HG_SUB
    row_blk = lambda w: pl.BlockSpec((HG_TB, w), lambda i: (i, 0))
    u_rows = HG_TB // S5_SEG_ROWS * S5_SEG_PITCH
    return pl.pallas_call(
        functools.partial(_proj_hgrn_kernel, blocks_per_seq=seqlen // HG_TB),
        grid=(t // HG_TB,),
        in_specs=[
            row_blk(D_MODEL), _const_spec((1, D_MODEL)),
            pl.BlockSpec(memory_space=pl.ANY),
            _const_spec(lb_logits.shape), _const_spec((1, HGRN_WIDTH)),
        ],
        out_specs=[pl.BlockSpec((u_rows, SSM_WIDTH), lambda i: (i, 0)), row_blk(PBF_WIDTH), row_blk(HGRN_WIDTH)],
        out_shape=[
            jax.ShapeDtypeStruct((t // HG_TB * u_rows, SSM_WIDTH), F32),
            jax.ShapeDtypeStruct((t, PBF_WIDTH), BF16),
            jax.ShapeDtypeStruct((t, HGRN_WIDTH), BF16),
        ],
        scratch_shapes=[
            pltpu.VMEM((D_MODEL, IN_WIDTH), BF16), pltpu.VMEM((2, D_MODEL, W_CHUNK), F32),
            pltpu.SemaphoreType.DMA((2,)),
            pltpu.VMEM((HG_TB, PH_WIDTH), F32)] + [pltpu.VMEM(wide, BF16)] * 8 + [
            pltpu.VMEM(wide, F32), pltpu.VMEM(wide, F32),
            pltpu.VMEM((HEADS, HEAD_DIM, HEAD_DIM), F32),
            pltpu.VMEM((units, HEAD_DIM, HEAD_DIM), F32),
            pltpu.VMEM((units, HG_CH, n_cls * HG_CH), F32),
            pltpu.VMEM((units, HG_CH, n_cls * HG_CH // 2), BF16)],
        compiler_params=pltpu.CompilerParams(
            dimension_semantics=("arbitrary",), vmem_limit_bytes=VMEM_LIMIT),
        name="proj_hgrn",
    )(x2, norm_w, w, lb_logits, hgrn_norm_w)


def _merge_kernel(x_ref, ya_ref, yb_ref, ga_ref, gb_ref, wpa_ref, wpb_ref, wo_ref, fnw_ref, o_ref):
    wpa, wpb, wo = wpa_ref[...].astype(BF16), wpb_ref[...].astype(BF16), wo_ref[...].astype(BF16)
    for r in range(TM_OUT // MERGE_ROWS):
        rows = slice(MERGE_ROWS * r, MERGE_ROWS * (r + 1))
        pa = jnp.dot(ya_ref[rows, :], wpa, preferred_element_type=F32)
        pb = jnp.dot(yb_ref[rows, :], wpb, preferred_element_type=F32)
        merged = _sigmoid(ga_ref[rows, :].astype(F32)) * pa + _sigmoid(gb_ref[rows, :].astype(F32)) * pb
        h = x_ref[rows, :] + jnp.dot(merged.astype(BF16), wo, preferred_element_type=F32)
        ms = jnp.mean(h * h, axis=-1, keepdims=True)
        o_ref[rows, :] = h * lax.rsqrt(ms + NORM_EPS) * fnw_ref[...]


def _merge(x2, ya, yb, pbf, w_pa, w_pb, w_out, fnw):
    t = x2.shape[0]
    return pl.pallas_call(
        _merge_kernel,
        grid=(t // TM_OUT,),
        in_specs=[
            pl.BlockSpec((TM_OUT, D_MODEL), lambda i: (i, 0)),
            pl.BlockSpec((TM_OUT, SSM_WIDTH), lambda i: (i, 0)),
            pl.BlockSpec((TM_OUT, HGRN_WIDTH), lambda i: (i, 0)),
            pl.BlockSpec((TM_OUT, D_MODEL), lambda i: (i, 0)),
            pl.BlockSpec((TM_OUT, D_MODEL), lambda i: (i, 1)),
            _const_spec((SSM_WIDTH, D_MODEL)), _const_spec((HGRN_WIDTH, D_MODEL)),
            _const_spec((D_MODEL, D_MODEL)), _const_spec((1, D_MODEL)),
        ],
        out_specs=pl.BlockSpec((TM_OUT, D_MODEL), lambda i: (i, 0)),
        out_shape=jax.ShapeDtypeStruct((t, D_MODEL), F32),
        compiler_params=pltpu.CompilerParams(
            dimension_semantics=("arbitrary",), vmem_limit_bytes=VMEM_LIMIT),
        name="merge",
    )(x2, ya, yb, pbf, pbf, w_pa, w_pb, w_out, fnw)


def kernel(x, norm_w, w_in, ssm_lambda_re, ssm_lambda_im, ssm_b_re, ssm_b_im, ssm_c_re, ssm_c_im, ssm_d,
           ssm_log_dt, ssm_w_glu, ssm_b_glu, hgrn_lb_logits, hgrn_norm_w, w_proj_a, w_proj_b, w_out,
           final_norm_w):
    batch, seqlen, _ = x.shape
    assert norm_w.shape[0] == 1, "single-layer block"
    assert seqlen % S5_TB == 0 and seqlen % HG_TB == 0 and HG_TB % S5_SEG_ROWS == 0
    x2 = x.reshape(batch * seqlen, D_MODEL)
    u32, pbf, yb = _proj_hgrn(x2, norm_w[0][None, :], w_in[0], hgrn_lb_logits,
                              hgrn_norm_w[0][None, :], seqlen)

    kb, winp, woutp, ptre, ptim = _s5_prep(ssm_lambda_re[0], ssm_lambda_im[0], ssm_b_re[0], ssm_b_im[0],
                                           ssm_c_re[0], ssm_c_im[0], ssm_log_dt[0])
    ya = _s5(u32, pbf, kb, winp, woutp, ptre, ptim, ssm_d[0].reshape(1, SSM_WIDTH),
             ssm_w_glu[0], ssm_b_glu[0][None, :], batch, seqlen)
    out = _merge(x2, ya, yb, pbf, w_proj_a[0], w_proj_b[0], w_out[0], final_norm_w[None, :])
    return out.reshape(batch, seqlen, D_MODEL)
```

```python
import functools

import jax
import jax.numpy as jnp
from jax import lax
from jax.experimental import pallas as pl
from jax.experimental.pallas import tpu as pltpu

F32 = jnp.float32
BF16 = jnp.bfloat16

D_MODEL = 1024
SSM_WIDTH = 512
SSM_GROUP = 16
SSM_GROUPS = 32
SSM_STATE = 64
HGRN_WIDTH = 512
HEAD_DIM = 128
HEADS = 4
NORM_EPS = 1e-6
LAMBDA_RE_MAX = -1e-4

IN_WIDTH = 2 * SSM_WIDTH + 5 * HGRN_WIDTH + 2 * D_MODEL
COL_ZA, COL_H, COL_G = SSM_WIDTH, 2 * SSM_WIDTH, 2 * SSM_WIDTH + 5 * HGRN_WIDTH
PH_WIDTH = 5 * HGRN_WIDTH
PH_Q, PH_F, PH_I, PH_OG, PH_ZB = 0, 1, 2, 3, 4
PBF_WIDTH = 2 * D_MODEL + SSM_WIDTH
PBF_ZA = 4

LANES = 128
SUBLANES = 8
OCT = LANES // SSM_GROUP
N_OCT = SSM_GROUPS // OCT
OCT_STATE = OCT * SSM_STATE

S5_Q = 4
S5_QP = S5_Q // 2
S5_TB = 2048
S5_NCH = S5_TB // S5_Q
S5_SEG = SUBLANES
S5_NV = S5_NCH // S5_SEG
S5_PT_ROWS = S5_NV + SUBLANES
S5_SEG_ROWS = S5_Q * S5_NV
S5_TAIL_SEGS = 2
S5_SEG_PITCH = S5_SEG_ROWS + SUBLANES
HG_TB = 512
HG_CH = 64
HG_SUB = 16
W_CHUNK = 512
TM_OUT = 1024
MERGE_ROWS = 512

V7X_VMEM_BYTES = 64 * 1024 * 1024
VMEM_LIMIT = V7X_VMEM_BYTES * 7 // 8


def _sigmoid(x):
    return 0.5 * jnp.tanh(0.5 * x) + 0.5


def _const_spec(shape):
    nd = len(shape)
    return pl.BlockSpec(shape, lambda *_: (0,) * nd, pipeline_mode=pl.Buffered(1))


def _disc(lam_re, lam_im, log_dt):
    return jnp.minimum(lam_re, LAMBDA_RE_MAX), lam_im, jnp.exp(log_dt)


def _cpow(lr, li, dt, k):
    mag = jnp.exp(k * (lr * dt))
    ang = k * (li * dt)
    return mag * jnp.cos(ang), mag * jnp.sin(ang)


def _split_bf16(x):
    hi = x.astype(BF16)
    return hi, (x - hi.astype(F32)).astype(BF16)


def _dot3(a, b_hi, b_lo):
    a_hi, a_lo = _split_bf16(a)
    d = lambda p, q: jnp.dot(p, q, preferred_element_type=F32)
    return d(a_hi, b_hi) + (d(a_hi, b_lo) + d(a_lo, b_hi))


def _cmul(a_re, a_im, b_re, b_im):
    return a_re * b_re - a_im * b_im, a_re * b_im + a_im * b_re


def _s5_prep_kernel(lam_c, b_t, c_n, lam_n, lam_r,
                    kb_ref, win_ref, wout_ref, ptre_ref, ptim_ref, pw_s, pm_s, xk_s):
    @pl.when(pl.program_id(0) == 0)
    def _():
        lr, li, dt = _disc(lam_c[0], lam_c[1], lam_c[2])
        ab_re, ab_im = _cpow(lr, li, dt, 1.0)
        den = lr * lr + li * li
        nr = ab_re - 1.0
        pw_s[0] = (nr * lr + ab_im * li) / den
        pw_s[1] = (ab_im * lr - nr * li) / den
        pw_s[2] = jnp.ones_like(ab_re)
        pw_s[3] = jnp.zeros_like(ab_re)
        pw_s[4] = ab_re
        pw_s[5] = ab_im
        m_re, m_im = _cpow(*_disc(lam_n[0], lam_n[1], lam_n[2]), 1.0)
        pm_s[0] = m_re
        pm_s[1] = m_im
        pm_s[2] = m_re
        pm_s[3] = m_im
        j = lax.broadcasted_iota(jnp.int32, (S5_PT_ROWS, OCT_STATE), 0).astype(F32) * float(S5_Q)
        for a in range(N_OCT):
            t_re, t_im = _cpow(*_disc(lam_r[0, a:a + 1, :], lam_r[1, a:a + 1, :], lam_r[2, a:a + 1, :]), j)
            ptre_ref[a] = t_re
            ptim_ref[a] = t_im

    col = lax.broadcasted_iota(jnp.int32, (SSM_GROUP, SSM_WIDTH), 1)

    def strip(g, carry):
        rows = pl.ds(pl.multiple_of(g * SSM_GROUP, SSM_GROUP), SSM_GROUP)
        grow = pl.ds(g, 1)
        coef_re, coef_im = pw_s[0, grow, :], pw_s[1, grow, :]
        p_re, p_im = pw_s[2, grow, :], pw_s[3, grow, :]
        bb_re, bb_im = _cmul(coef_re, coef_im, b_t[0, rows, :], b_t[1, rows, :])
        x_re, x_im = _cmul(bb_re, bb_im, p_re, p_im)
        xk_s[0, rows, :] = x_re[:, :SSM_STATE]
        xk_s[1, rows, :] = x_im[:, :SSM_STATE]
        a = g // OCT
        lrows = pl.ds(pl.multiple_of((g % OCT) * SSM_GROUP, SSM_GROUP), SSM_GROUP)
        m_in = col // SSM_STATE == g % OCT
        win_ref[0, a, 0, lrows, :] = jnp.concatenate(
            [jnp.where(m_in, x_re, 0.0), jnp.where(m_in, x_im, 0.0)], axis=1).astype(BF16)
        return carry

    lax.fori_loop(0, SSM_GROUPS, strip, 0, unroll=4)

    cr, ci = c_n[0], c_n[1]
    kfull = _dot3(xk_s[0], *_split_bf16(cr)) - _dot3(xk_s[1], *_split_bf16(ci))
    rowk = lax.broadcasted_iota(jnp.int32, (LANES, LANES), 0)
    colk = lax.broadcasted_iota(jnp.int32, (LANES, LANES), 1)
    m_k = rowk // SSM_GROUP == colk // SSM_GROUP
    for a in range(N_OCT):
        ls = slice(LANES * a, LANES * (a + 1))
        kb_ref[0, a] = jnp.where(m_k, kfull[ls, ls], 0.0).astype(BF16)

    p1_re, p1_im = pm_s[0], pm_s[1]
    w_re, w_im = _cmul(cr, ci, p1_re, p1_im)
    rowo = lax.broadcasted_iota(jnp.int32, (OCT_STATE, LANES), 0)
    colo = lax.broadcasted_iota(jnp.int32, (OCT_STATE, LANES), 1)
    m_out = rowo // SSM_STATE == colo // SSM_GROUP
    for a in range(N_OCT):
        ls = slice(LANES * a, LANES * (a + 1))
        wout_ref[0, a, :OCT_STATE, :] = jnp.where(m_out, jnp.tile(w_re[:, ls], (OCT, 1)), 0.0).astype(BF16)
        wout_ref[0, a, OCT_STATE:, :] = jnp.where(m_out, jnp.tile(-w_im[:, ls], (OCT, 1)), 0.0).astype(BF16)

    pw_s[2], pw_s[3] = _cmul(pw_s[2], pw_s[3], pw_s[4], pw_s[5])
    pm_s[0], pm_s[1] = _cmul(p1_re, p1_im, pm_s[2], pm_s[3])


def _s5_prep(lam_re, lam_im, b_re, b_im, c_re, c_im, log_dt):
    g, n, p = SSM_GROUPS, SSM_STATE, SSM_GROUP
    lam = jnp.stack([lam_re, lam_im, jnp.broadcast_to(log_dt[:, None], (g, n))])
    b = jnp.stack([b_re, b_im])
    c = jnp.stack([c_re, c_im])
    args = (jnp.tile(lam, (1, 1, OCT)),
            jnp.tile(b.transpose(0, 1, 3, 2).reshape(2, g * p, n), (1, 1, OCT)),
            c.transpose(0, 3, 1, 2).reshape(2, n, g * p),
            jnp.repeat(lam.transpose(0, 2, 1), p, axis=2),
            lam.reshape(3, N_OCT, OCT_STATE))
    mat = (S5_QP, N_OCT, 2, LANES, 2 * OCT_STATE)
    blk = (1, N_OCT, 1, LANES, 2 * OCT_STATE)
    tab = (N_OCT, S5_PT_ROWS, OCT_STATE)
    kb, win, woutp, ptre, ptim = pl.pallas_call(
        _s5_prep_kernel,
        grid=(S5_Q,),
        in_specs=[_const_spec(a.shape) for a in args],
        out_specs=[
            pl.BlockSpec((1, N_OCT, LANES, LANES), lambda t: (t, 0, 0, 0)),
            pl.BlockSpec(blk, lambda t: ((S5_Q - 1 - t) // 2, 0, (S5_Q - 1 - t) % 2, 0, 0)),
            pl.BlockSpec((1, N_OCT, 2 * OCT_STATE, LANES), lambda t: (t // 2, 0, 0, t % 2)),
            pl.BlockSpec(tab, lambda t: (0, 0, 0)),
            pl.BlockSpec(tab, lambda t: (0, 0, 0)),
        ],
        out_shape=[
            jax.ShapeDtypeStruct((S5_Q, N_OCT, LANES, LANES), BF16),
            jax.ShapeDtypeStruct(mat, BF16),
            jax.ShapeDtypeStruct((S5_QP, N_OCT, 2 * OCT_STATE, 2 * LANES), BF16),
            jax.ShapeDtypeStruct(tab, F32),
            jax.ShapeDtypeStruct(tab, F32),
        ],
        scratch_shapes=[pltpu.VMEM((6, SSM_GROUPS, SSM_WIDTH), F32),
                        pltpu.VMEM((4, SSM_STATE, SSM_WIDTH), F32),
                        pltpu.VMEM((2, SSM_WIDTH, SSM_STATE), F32)],
        compiler_params=pltpu.CompilerParams(
            dimension_semantics=("arbitrary",), vmem_limit_bytes=VMEM_LIMIT),
        name="s5_prep",
    )(*args)

    winp = win.reshape(S5_QP, N_OCT, 2 * LANES, 2 * OCT_STATE)
    return kb, winp, woutp, ptre, ptim


def _gelu_tanh(x):
    c = 0.7978845608028654
    return x * (0.5 * (1.0 + jnp.tanh(x * (c + (c * 0.044715) * (x * x)))))


def _s5_kernel(u0_ref, u1_ref, u2_ref, u3_ref, z_ref, kb_ref, winp_ref, woutp_ref, ptre_ref, ptim_ref,
               d_ref, wglu_ref, bglu_ref, o_ref, y0_s, y1_s, y2_s, y3_s, cre_scr, cim_scr):
    u_refs = (u0_ref, u1_ref, u2_ref, u3_ref)
    y_scrs = (y0_s, y1_s, y2_s, y3_s)

    @pl.when(pl.program_id(1) == 0)
    def _():
        cre_scr[...] = jnp.zeros_like(cre_scr)
        cim_scr[...] = jnp.zeros_like(cim_scr)

    def tok(a, s):
        return jnp.concatenate(
            [u_refs[a][pl.ds(s + S5_Q * v, S5_SEG, stride=S5_SEG_PITCH), :] for v in range(S5_NV)], axis=0)

    xp = [[jnp.concatenate([tok(a, 2 * sp), tok(a, 2 * sp + 1)], axis=1).astype(BF16) for a in range(N_OCT)]
          for sp in range(S5_QP)]

    def cmul_add(b_re, b_im, m_re, m_im, x_re, x_im):
        return b_re + m_re * x_re - m_im * x_im, b_im + m_re * x_im + m_im * x_re

    hs = []
    for a in range(N_OCT):
        acc = None
        for sp in range(S5_QP):
            part = jnp.dot(xp[sp][a], winp_ref[sp, a], preferred_element_type=F32)
            acc = part if acc is None else acc + part
        blk = lambda v: (acc[S5_SEG * v:S5_SEG * (v + 1), :OCT_STATE], acc[S5_SEG * v:S5_SEG * (v + 1), OCT_STATE:])
        m_re, m_im = ptre_ref[a, 1:2, :], ptim_ref[a, 1:2, :]
        loc = [blk(0)]
        for v in range(1, S5_NV):
            loc.append(cmul_add(*blk(v), m_re, m_im, *loc[-1]))
        l_re, l_im = ptre_ref[a, S5_NV:S5_NV + 1, :], ptim_ref[a, S5_NV:S5_NV + 1, :]
        c_re, c_im = cre_scr[a:a + 1, :], cim_scr[a:a + 1, :]
        carry = []
        for r in range(S5_SEG):
            carry.append((c_re, c_im))
            c_re, c_im = cmul_add(loc[-1][0][r:r + 1, :], loc[-1][1][r:r + 1, :], l_re, l_im, c_re, c_im)
        cre_scr[a:a + 1, :] = c_re
        cim_scr[a:a + 1, :] = c_im
        cs_re = jnp.concatenate([c[0] for c in carry], axis=0)
        cs_im = jnp.concatenate([c[1] for c in carry], axis=0)
        ent = [(cs_re, cs_im)]
        for v in range(S5_NV - 1):
            ent.append(cmul_add(*loc[v], ptre_ref[a, v + 1:v + 2, :], ptim_ref[a, v + 1:v + 2, :], cs_re, cs_im))
        hs.append(jnp.concatenate([jnp.concatenate([e[0] for e in ent], axis=0),
                                   jnp.concatenate([e[1] for e in ent], axis=0)], axis=1).astype(BF16))

    def pair_tile(d, a):
        below = kb_ref[2 * d - 1, a] if d > 0 else jnp.zeros((LANES, LANES), BF16)
        return jnp.concatenate([jnp.concatenate([kb_ref[2 * d, a], kb_ref[2 * d + 1, a]], axis=1),
                                jnp.concatenate([below, kb_ref[2 * d, a]], axis=1)], axis=0)

    for a in range(N_OCT):
        tiles = [pair_tile(d, a) for d in range(S5_QP)]
        for tp in range(S5_QP):
            acc = jnp.dot(hs[a], woutp_ref[tp, a], preferred_element_type=F32)
            for sp in range(tp + 1):
                acc = acc + jnp.dot(xp[sp][a], tiles[tp - sp], preferred_element_type=F32)
            for v in range(S5_NV):
                rows = slice(S5_SEG * v, S5_SEG * (v + 1))
                y_scrs[a][pl.ds(2 * tp + S5_Q * v, S5_SEG, stride=S5_SEG_PITCH), :] = acc[rows, :LANES]
                y_scrs[a][pl.ds(2 * tp + 1 + S5_Q * v, S5_SEG, stride=S5_SEG_PITCH), :] = acc[rows, LANES:]
        d_a = d_ref[:, LANES * a:LANES * (a + 1)]
        for r in range(S5_SEG):
            prow = slice(S5_SEG_PITCH * r, S5_SEG_PITCH * r + S5_SEG_ROWS)
            y_scrs[a][prow, :] = _gelu_tanh(y_scrs[a][prow, :] + d_a * u_refs[a][prow, :])

    wglu = wglu_ref[...].astype(BF16)
    for r0 in range(0, S5_SEG, S5_TAIL_SEGS):
        segs = range(r0, r0 + S5_TAIL_SEGS)
        trow = slice(S5_SEG_ROWS * r0, S5_SEG_ROWS * (r0 + S5_TAIL_SEGS))
        y = jnp.concatenate(
            [jnp.concatenate([ref[S5_SEG_PITCH * r:S5_SEG_PITCH * r + S5_SEG_ROWS, :] for r in segs], axis=0)
             for ref in y_scrs], axis=1)
        hy = 0.5 * y
        hgate = jnp.dot(hy.astype(BF16), wglu, preferred_element_type=F32) + 0.5 * bglu_ref[...]
        hz = 0.5 * z_ref[trow, :].astype(F32)
        o_ref[trow, :] = (hy * hz * ((1.0 + jnp.tanh(hgate)) * (1.0 + jnp.tanh(hz)))).astype(BF16)


def _s5(u32, pbf, kb, winp, woutp, ptre, ptim, d_row, w_glu, b_glu, batch, seqlen):
    nb = seqlen // S5_TB
    t = batch * seqlen
    u_tile = lambda a: pl.BlockSpec((S5_SEG * S5_SEG_PITCH, LANES), lambda b, i, a=a: (b * nb + i, a))
    return pl.pallas_call(
        _s5_kernel,
        grid=(batch, nb),
        in_specs=[u_tile(a) for a in range(N_OCT)] + [
            pl.BlockSpec((S5_TB, SSM_WIDTH), lambda b, i: (b * nb + i, PBF_ZA)),
            _const_spec(kb.shape), _const_spec(winp.shape), _const_spec(woutp.shape),
            _const_spec(ptre.shape), _const_spec(ptim.shape),
            _const_spec((1, SSM_WIDTH)), _const_spec((SSM_WIDTH, SSM_WIDTH)), _const_spec((1, SSM_WIDTH)),
        ],
        out_specs=pl.BlockSpec((S5_TB, SSM_WIDTH), lambda b, i: (b * nb + i, 0)),
        out_shape=jax.ShapeDtypeStruct((t, SSM_WIDTH), BF16),
        scratch_shapes=[pltpu.VMEM((S5_SEG * S5_SEG_PITCH, LANES), F32)] * N_OCT + [
            pltpu.VMEM((SUBLANES, OCT_STATE), F32),
            pltpu.VMEM((SUBLANES, OCT_STATE), F32),
        ],
        compiler_params=pltpu.CompilerParams(
            dimension_semantics=("arbitrary", "arbitrary"), vmem_limit_bytes=VMEM_LIMIT),
        name="s5",
    )(u32, u32, u32, u32, pbf, kb, winp, woutp, ptre, ptim, d_row, w_glu, b_glu)


def _stage_weights(w_hbm, wbf_s, stage_s, sem):
    n_chunks = IN_WIDTH // W_CHUNK

    def chunk_copy(c):
        return pltpu.make_async_copy(w_hbm.at[:, pl.ds(c * W_CHUNK, W_CHUNK)], stage_s.at[c % 2], sem.at[c % 2])

    chunk_copy(0).start()
    for c in range(n_chunks):
        if c + 1 < n_chunks:
            chunk_copy(c + 1).start()
        chunk_copy(c).wait()
        wbf_s[:, c * W_CHUNK:(c + 1) * W_CHUNK] = stage_s[c % 2].astype(BF16)


def _proj_hgrn_kernel(x_ref, nw_ref, w_hbm, lbl_ref, hnw_ref, ou_ref, or_ref, o_ref,
                      wbf_s, stage_s, stage_sem, ph_s, q0_s, qc_s, kt_s, k0_s, k1_s, k2_s, ke_s, v_s, dec_s, oacc_s,
                      st_s, upd_s, sc_s, am_s,
                      *, blocks_per_seq):
    @pl.when(pl.program_id(0) == 0)
    def _():
        _stage_weights(w_hbm, wbf_s, stage_s, stage_sem)

    @pl.when(pl.program_id(0) % blocks_per_seq == 0)
    def _():
        st_s[...] = jnp.zeros_like(st_s)

    x = x_ref[...]
    ms = jnp.mean(x * x, axis=-1, keepdims=True)
    xn = (x * lax.rsqrt(ms + NORM_EPS) * nw_ref[...]).astype(BF16)
    proj = lambda lo, hi: jnp.dot(xn, wbf_s[:, lo:hi], preferred_element_type=F32)
    ph_s[...] = proj(COL_H, COL_G)
    u = proj(0, COL_ZA)
    for r in range(HG_TB // S5_SEG_ROWS):
        ou_ref[S5_SEG_PITCH * r:S5_SEG_PITCH * r + S5_SEG_ROWS, :] = u[S5_SEG_ROWS * r:S5_SEG_ROWS * (r + 1), :]
        ou_ref[S5_SEG_PITCH * r + S5_SEG_ROWS:S5_SEG_PITCH * (r + 1), :] = jnp.zeros((SUBLANES, SSM_WIDTH), F32)
    or_ref[:, :2 * D_MODEL] = proj(COL_G, IN_WIDTH).astype(BF16)
    or_ref[:, 2 * D_MODEL:] = proj(COL_ZA, COL_H).astype(BF16)
    col = lambda k: ph_s[:, HGRN_WIDTH * k:HGRN_WIDTH * (k + 1)]

    logits = lbl_ref[...]
    e = jnp.exp(logits - jnp.max(logits, axis=0, keepdims=True))
    lb = (e / jnp.sum(e, axis=0, keepdims=True))[0:1, :]

    q = col(PH_Q)
    qf = q * _sigmoid(q)
    forget = lb + (1.0 - lb) * _sigmoid(col(PH_F))
    lf = jnp.log(forget)
    key = 1.0 - forget

    row = lax.broadcasted_iota(jnp.int32, (HG_TB, HGRN_WIDTH), 0)
    r_sub = row % HG_SUB
    r_ch = row % HG_CH

    def down(x, d):
        return pltpu.roll(x, d, 0)

    def up(x, d):
        return pltpu.roll(x, HG_TB - d, 0)

    a = lf
    d = 1
    while d < HG_SUB:
        a = a + jnp.where(r_sub >= d, down(a, d), 0.0)
        d *= 2
    a3 = a.reshape(HG_TB // HG_SUB, HG_SUB, HGRN_WIDTH)
    tsub = jnp.broadcast_to(a3[:, HG_SUB - 1:HG_SUB, :], a3.shape).reshape(HG_TB, HGRN_WIDTH)
    n_sub = HG_CH // HG_SUB
    prev = [jnp.where(r_ch >= HG_SUB * k, down(tsub, HG_SUB * k), 0.0) for k in range(1, n_sub)]
    nxt = [jnp.where(r_ch < HG_CH - HG_SUB * k, up(tsub, HG_SUB * k), 0.0) for k in range(1, n_sub)]
    eprev = prev[0] + prev[1] + prev[2]
    enext = nxt[0] + nxt[1] + nxt[2]
    suf = tsub - a

    hsub = 0.5 * tsub
    q0_s[...] = (qf * jnp.exp(a - hsub)).astype(BF16)
    qc_s[...] = (qf * jnp.exp(a + eprev)).astype(BF16)
    kt_s[...] = (key * jnp.exp(hsub - a)).astype(BF16)
    k0 = key * jnp.exp(suf)
    e0, e1, e2 = (jnp.exp(0.5 * n) for n in nxt)
    k0_s[...] = (k0 * e0).astype(BF16)
    k1 = k0 * (e0 * e0)
    k1_s[...] = (k1 * e1).astype(BF16)
    k2 = k1 * (e1 * e1)
    k2_s[...] = (k2 * e2).astype(BF16)
    ke_s[...] = (k2 * (e2 * e2)).astype(BF16)
    dec_s[...] = jnp.exp(eprev + tsub + enext)
    v_s[...] = col(PH_I).astype(BF16)

    tq = lax.broadcasted_iota(jnp.int32, (HG_CH, n_sub * HG_CH), 0)
    cc = lax.broadcasted_iota(jnp.int32, (HG_CH, n_sub * HG_CH), 1)
    cls = cc // HG_CH
    ts = cc % HG_CH
    bi = tq // HG_SUB
    bj = ts // HG_SUB
    mask4 = ((cls == 0) & (bi == bj) & (ts <= tq)) | ((cls > 0) & ((bi - bj) == cls))

    n_ch = HG_TB // HG_CH
    half = n_sub * HG_CH // 2
    units = [(c, h, slice(c * HG_CH, (c + 1) * HG_CH), slice(HEAD_DIM * h, HEAD_DIM * (h + 1)))
             for c in range(n_ch) for h in range(HEADS)]
    for c, h, rows, ls in units:
        kcat = jnp.concatenate([kt_s[rows, ls], k0_s[rows, ls], k1_s[rows, ls], k2_s[rows, ls]], axis=0)
        sc_s[c * HEADS + h] = lax.dot_general(q0_s[rows, ls], kcat, (((1,), (1,)), ((), ())),
                                              preferred_element_type=F32)
    for c, h, rows, ls in units:
        upd_s[c * HEADS + h] = lax.dot_general(v_s[rows, ls], ke_s[rows, ls], (((0,), (0,)), ((), ())),
                                               preferred_element_type=F32)
    for c, h, rows, ls in units:
        sc = jnp.where(mask4, sc_s[c * HEADS + h], 0.0)
        am_s[c * HEADS + h] = (sc[:, :half] + sc[:, half:]).astype(BF16)
    for c, h, rows, ls in units:
        vv = v_s[rows, ls]
        oacc_s[rows, ls] = jnp.dot(am_s[c * HEADS + h], jnp.concatenate([vv, vv], axis=0),
                                   preferred_element_type=F32)
    for h in range(HEADS):
        ls = slice(HEAD_DIM * h, HEAD_DIM * (h + 1))
        st = st_s[h]
        for c in range(n_ch):
            rows = slice(c * HG_CH, (c + 1) * HG_CH)
            oacc_s[rows, ls] += lax.dot_general(qc_s[rows, ls], st.astype(BF16), (((1,), (1,)), ((), ())),
                                                preferred_element_type=F32)
            st = st * dec_s[c * HG_CH:c * HG_CH + 1, ls] + upd_s[c * HEADS + h]
        st_s[h] = st

    o = oacc_s[...] * _sigmoid(col(PH_OG))
    parts = []
    for h in range(HEADS):
        oh = o[:, HEAD_DIM * h:HEAD_DIM * (h + 1)]
        ms = jnp.mean(oh * oh, axis=-1, keepdims=True)
        parts.append(oh * lax.rsqrt(ms + NORM_EPS))
    o = jnp.concatenate(parts, axis=1) * hnw_ref[...]
    hz = 0.5 * col(PH_ZB)
    o_ref[...] = (o * (hz * (1.0 + jnp.tanh(hz)))).astype(BF16)


def _proj_hgrn(x2, norm_w, w, lb_logits, hgrn_norm_w, seqlen):
    t = x2.shape[0]
    wide = (HG_TB, HGRN_WIDTH)
    units = HG_TB // HG_CH * HEADS
    n_cls = HG_CH // HG_SUB
    row_blk = lambda w: pl.BlockSpec((HG_TB, w), lambda i: (i, 0))
    u_rows = HG_TB // S5_SEG_ROWS * S5_SEG_PITCH
    return pl.pallas_call(
        functools.partial(_proj_hgrn_kernel, blocks_per_seq=seqlen // HG_TB),
        grid=(t // HG_TB,),
        in_specs=[
            row_blk(D_MODEL), _const_spec((1, D_MODEL)),
            pl.BlockSpec(memory_space=pl.ANY),
            _const_spec(lb_logits.shape), _const_spec((1, HGRN_WIDTH)),
        ],
        out_specs=[pl.BlockSpec((u_rows, SSM_WIDTH), lambda i: (i, 0)), row_blk(PBF_WIDTH), row_blk(HGRN_WIDTH)],
        out_shape=[
            jax.ShapeDtypeStruct((t // HG_TB * u_rows, SSM_WIDTH), F32),
            jax.ShapeDtypeStruct((t, PBF_WIDTH), BF16),
            jax.ShapeDtypeStruct((t, HGRN_WIDTH), BF16),
        ],
        scratch_shapes=[
            pltpu.VMEM((D_MODEL, IN_WIDTH), BF16), pltpu.VMEM((2, D_MODEL, W_CHUNK), F32),
            pltpu.SemaphoreType.DMA((2,)),
            pltpu.VMEM((HG_TB, PH_WIDTH), F32)] + [pltpu.VMEM(wide, BF16)] * 8 + [
            pltpu.VMEM(wide, F32), pltpu.VMEM(wide, F32),
            pltpu.VMEM((HEADS, HEAD_DIM, HEAD_DIM), F32),
            pltpu.VMEM((units, HEAD_DIM, HEAD_DIM), F32),
            pltpu.VMEM((units, HG_CH, n_cls * HG_CH), F32),
            pltpu.VMEM((units, HG_CH, n_cls * HG_CH // 2), BF16)],
        compiler_params=pltpu.CompilerParams(
            dimension_semantics=("arbitrary",), vmem_limit_bytes=VMEM_LIMIT),
        name="proj_hgrn",
    )(x2, norm_w, w, lb_logits, hgrn_norm_w)


def _merge_kernel(x_ref, ya_ref, yb_ref, ga_ref, gb_ref, wpa_ref, wpb_ref, wo_ref, fnw_ref, o_ref):
    wpa, wpb, wo = wpa_ref[...].astype(BF16), wpb_ref[...].astype(BF16), wo_ref[...].astype(BF16)
    for r in range(TM_OUT // MERGE_ROWS):
        rows = slice(MERGE_ROWS * r, MERGE_ROWS * (r + 1))
        pa = jnp.dot(ya_ref[rows, :], wpa, preferred_element_type=F32)
        pb = jnp.dot(yb_ref[rows, :], wpb, preferred_element_type=F32)
        merged = _sigmoid(ga_ref[rows, :].astype(F32)) * pa + _sigmoid(gb_ref[rows, :].astype(F32)) * pb
        h = x_ref[rows, :] + jnp.dot(merged.astype(BF16), wo, preferred_element_type=F32)
        ms = jnp.mean(h * h, axis=-1, keepdims=True)
        o_ref[rows, :] = h * lax.rsqrt(ms + NORM_EPS) * fnw_ref[...]


def _merge(x2, ya, yb, pbf, w_pa, w_pb, w_out, fnw):
    t = x2.shape[0]
    return pl.pallas_call(
        _merge_kernel,
        grid=(t // TM_OUT,),
        in_specs=[
            pl.BlockSpec((TM_OUT, D_MODEL), lambda i: (i, 0)),
            pl.BlockSpec((TM_OUT, SSM_WIDTH), lambda i: (i, 0)),
            pl.BlockSpec((TM_OUT, HGRN_WIDTH), lambda i: (i, 0)),
            pl.BlockSpec((TM_OUT, D_MODEL), lambda i: (i, 0)),
            pl.BlockSpec((TM_OUT, D_MODEL), lambda i: (i, 1)),
            _const_spec((SSM_WIDTH, D_MODEL)), _const_spec((HGRN_WIDTH, D_MODEL)),
            _const_spec((D_MODEL, D_MODEL)), _const_spec((1, D_MODEL)),
        ],
        out_specs=pl.BlockSpec((TM_OUT, D_MODEL), lambda i: (i, 0)),
        out_shape=jax.ShapeDtypeStruct((t, D_MODEL), F32),
        compiler_params=pltpu.CompilerParams(
            dimension_semantics=("arbitrary",), vmem_limit_bytes=VMEM_LIMIT),
        name="merge",
    )(x2, ya, yb, pbf, pbf, w_pa, w_pb, w_out, fnw)


def kernel(x, norm_w, w_in, ssm_lambda_re, ssm_lambda_im, ssm_b_re, ssm_b_im, ssm_c_re, ssm_c_im, ssm_d,
           ssm_log_dt, ssm_w_glu, ssm_b_glu, hgrn_lb_logits, hgrn_norm_w, w_proj_a, w_proj_b, w_out,
           final_norm_w):
    batch, seqlen, _ = x.shape
    assert norm_w.shape[0] == 1, "single-layer block"
    assert seqlen % S5_TB == 0 and seqlen % HG_TB == 0 and HG_TB % S5_SEG_ROWS == 0
    x2 = x.reshape(batch * seqlen, D_MODEL)
    u32, pbf, yb = _proj_hgrn(x2, norm_w[0][None, :], w_in[0], hgrn_lb_logits,
                              hgrn_norm_w[0][None, :], seqlen)

    kb, winp, woutp, ptre, ptim = _s5_prep(ssm_lambda_re[0], ssm_lambda_im[0], ssm_b_re[0], ssm_b_im[0],
                                           ssm_c_re[0], ssm_c_im[0], ssm_log_dt[0])
    ya = _s5(u32, pbf, kb, winp, woutp, ptre, ptim, ssm_d[0].reshape(1, SSM_WIDTH),
             ssm_w_glu[0], ssm_b_glu[0][None, :], batch, seqlen)
    out = _merge(x2, ya, yb, pbf, w_proj_a[0], w_proj_b[0], w_out[0], final_norm_w[None, :])
    return out.reshape(batch, seqlen, D_MODEL)
```

```python
import functools

import jax
import jax.numpy as jnp
from jax import lax
from jax.experimental import pallas as pl
from jax.experimental.pallas import tpu as pltpu

F32 = jnp.float32
BF16 = jnp.bfloat16

D_MODEL = 1024
SSM_WIDTH = 512
SSM_GROUP = 16
SSM_GROUPS = 32
SSM_STATE = 64
HGRN_WIDTH = 512
HEAD_DIM = 128
HEADS = 4
NORM_EPS = 1e-6
LAMBDA_RE_MAX = -1e-4

IN_WIDTH = 2 * SSM_WIDTH + 5 * HGRN_WIDTH + 2 * D_MODEL
COL_ZA, COL_H, COL_G = SSM_WIDTH, 2 * SSM_WIDTH, 2 * SSM_WIDTH + 5 * HGRN_WIDTH
PH_WIDTH = 5 * HGRN_WIDTH
PH_Q, PH_F, PH_I, PH_OG, PH_ZB = 0, 1, 2, 3, 4
PBF_WIDTH = 2 * D_MODEL + SSM_WIDTH
PBF_ZA = 4

LANES = 128
SUBLANES = 8
OCT = LANES // SSM_GROUP
N_OCT = SSM_GROUPS // OCT
OCT_STATE = OCT * SSM_STATE

S5_Q = 4
S5_QP = S5_Q // 2
S5_TB = 2048
S5_NCH = S5_TB // S5_Q
S5_SEG = SUBLANES
S5_NV = S5_NCH // S5_SEG
S5_PT_ROWS = S5_NV + SUBLANES
S5_SEG_ROWS = S5_Q * S5_NV
S5_TAIL_SEGS = 2
S5_SEG_PITCH = S5_SEG_ROWS + SUBLANES
HG_TB = 512
HG_CH = 64
HG_SUB = 16
W_CHUNK = 512
TM_OUT = 1024
MERGE_ROWS = 512

V7X_VMEM_BYTES = 64 * 1024 * 1024
VMEM_LIMIT = V7X_VMEM_BYTES * 7 // 8


def _sigmoid(x):
    return 0.5 * jnp.tanh(0.5 * x) + 0.5


def _const_spec(shape):
    nd = len(shape)
    return pl.BlockSpec(shape, lambda *_: (0,) * nd, pipeline_mode=pl.Buffered(1))


def _disc(lam_re, lam_im, log_dt):
    return jnp.minimum(lam_re, LAMBDA_RE_MAX), lam_im, jnp.exp(log_dt)


def _cpow(lr, li, dt, k):
    mag = jnp.exp(k * (lr * dt))
    ang = k * (li * dt)
    return mag * jnp.cos(ang), mag * jnp.sin(ang)


def _split_bf16(x):
    hi = x.astype(BF16)
    return hi, (x - hi.astype(F32)).astype(BF16)


def _dot3(a, b_hi, b_lo):
    a_hi, a_lo = _split_bf16(a)
    d = lambda p, q: jnp.dot(p, q, preferred_element_type=F32)
    return d(a_hi, b_hi) + (d(a_hi, b_lo) + d(a_lo, b_hi))


def _cmul(a_re, a_im, b_re, b_im):
    return a_re * b_re - a_im * b_im, a_re * b_im + a_im * b_re


def _s5_prep_kernel(lam_c, b_t, c_n, lam_n, lam_r,
                    kb_ref, win_ref, wout_ref, ptre_ref, ptim_ref, pw_s, pm_s, xk_s):
    @pl.when(pl.program_id(0) == 0)
    def _():
        lr, li, dt = _disc(lam_c[0], lam_c[1], lam_c[2])
        ab_re, ab_im = _cpow(lr, li, dt, 1.0)
        den = lr * lr + li * li
        nr = ab_re - 1.0
        pw_s[0] = (nr * lr + ab_im * li) / den
        pw_s[1] = (ab_im * lr - nr * li) / den
        pw_s[2] = jnp.ones_like(ab_re)
        pw_s[3] = jnp.zeros_like(ab_re)
        pw_s[4] = ab_re
        pw_s[5] = ab_im
        m_re, m_im = _cpow(*_disc(lam_n[0], lam_n[1], lam_n[2]), 1.0)
        pm_s[0] = m_re
        pm_s[1] = m_im
        pm_s[2] = m_re
        pm_s[3] = m_im
        j = lax.broadcasted_iota(jnp.int32, (S5_PT_ROWS, OCT_STATE), 0).astype(F32) * float(S5_Q)
        for a in range(N_OCT):
            t_re, t_im = _cpow(*_disc(lam_r[0, a:a + 1, :], lam_r[1, a:a + 1, :], lam_r[2, a:a + 1, :]), j)
            ptre_ref[a] = t_re
            ptim_ref[a] = t_im

    col = lax.broadcasted_iota(jnp.int32, (SSM_GROUP, SSM_WIDTH), 1)

    def strip(g, carry):
        rows = pl.ds(pl.multiple_of(g * SSM_GROUP, SSM_GROUP), SSM_GROUP)
        grow = pl.ds(g, 1)
        coef_re, coef_im = pw_s[0, grow, :], pw_s[1, grow, :]
        p_re, p_im = pw_s[2, grow, :], pw_s[3, grow, :]
        bb_re, bb_im = _cmul(coef_re, coef_im, b_t[0, rows, :], b_t[1, rows, :])
        x_re, x_im = _cmul(bb_re, bb_im, p_re, p_im)
        xk_s[0, rows, :] = x_re[:, :SSM_STATE]
        xk_s[1, rows, :] = x_im[:, :SSM_STATE]
        a = g // OCT
        lrows = pl.ds(pl.multiple_of((g % OCT) * SSM_GROUP, SSM_GROUP), SSM_GROUP)
        m_in = col // SSM_STATE == g % OCT
        win_ref[0, a, 0, lrows, :] = jnp.concatenate(
            [jnp.where(m_in, x_re, 0.0), jnp.where(m_in, x_im, 0.0)], axis=1).astype(BF16)
        return carry

    lax.fori_loop(0, SSM_GROUPS, strip, 0, unroll=4)

    cr, ci = c_n[0], c_n[1]
    kfull = _dot3(xk_s[0], *_split_bf16(cr)) - _dot3(xk_s[1], *_split_bf16(ci))
    rowk = lax.broadcasted_iota(jnp.int32, (LANES, LANES), 0)
    colk = lax.broadcasted_iota(jnp.int32, (LANES, LANES), 1)
    m_k = rowk // SSM_GROUP == colk // SSM_GROUP
    for a in range(N_OCT):
        ls = slice(LANES * a, LANES * (a + 1))
        kb_ref[0, a] = jnp.where(m_k, kfull[ls, ls], 0.0).astype(BF16)

    p1_re, p1_im = pm_s[0], pm_s[1]
    w_re, w_im = _cmul(cr, ci, p1_re, p1_im)
    rowo = lax.broadcasted_iota(jnp.int32, (OCT_STATE, LANES), 0)
    colo = lax.broadcasted_iota(jnp.int32, (OCT_STATE, LANES), 1)
    m_out = rowo // SSM_STATE == colo // SSM_GROUP
    for a in range(N_OCT):
        ls = slice(LANES * a, LANES * (a + 1))
        wout_ref[0, a, :OCT_STATE, :] = jnp.where(m_out, jnp.tile(w_re[:, ls], (OCT, 1)), 0.0).astype(BF16)
        wout_ref[0, a, OCT_STATE:, :] = jnp.where(m_out, jnp.tile(-w_im[:, ls], (OCT, 1)), 0.0).astype(BF16)

    pw_s[2], pw_s[3] = _cmul(pw_s[2], pw_s[3], pw_s[4], pw_s[5])
    pm_s[0], pm_s[1] = _cmul(p1_re, p1_im, pm_s[2], pm_s[3])


def _s5_prep(lam_re, lam_im, b_re, b_im, c_re, c_im, log_dt):
    g, n, p = SSM_GROUPS, SSM_STATE, SSM_GROUP
    lam = jnp.stack([lam_re, lam_im, jnp.broadcast_to(log_dt[:, None], (g, n))])
    b = jnp.stack([b_re, b_im])
    c = jnp.stack([c_re, c_im])
    args = (jnp.tile(lam, (1, 1, OCT)),
            jnp.tile(b.transpose(0, 1, 3, 2).reshape(2, g * p, n), (1, 1, OCT)),
            c.transpose(0, 3, 1, 2).reshape(2, n, g * p),
            jnp.repeat(lam.transpose(0, 2, 1), p, axis=2),
            lam.reshape(3, N_OCT, OCT_STATE))
    mat = (S5_QP, N_OCT, 2, LANES, 2 * OCT_STATE)
    blk = (1, N_OCT, 1, LANES, 2 * OCT_STATE)
    tab = (N_OCT, S5_PT_ROWS, OCT_STATE)
    kb, win, woutp, ptre, ptim = pl.pallas_call(
        _s5_prep_kernel,
        grid=(S5_Q,),
        in_specs=[_const_spec(a.shape) for a in args],
        out_specs=[
            pl.BlockSpec((1, N_OCT, LANES, LANES), lambda t: (t, 0, 0, 0)),
            pl.BlockSpec(blk, lambda t: ((S5_Q - 1 - t) // 2, 0, (S5_Q - 1 - t) % 2, 0, 0)),
            pl.BlockSpec((1, N_OCT, 2 * OCT_STATE, LANES), lambda t: (t // 2, 0, 0, t % 2)),
            pl.BlockSpec(tab, lambda t: (0, 0, 0)),
            pl.BlockSpec(tab, lambda t: (0, 0, 0)),
        ],
        out_shape=[
            jax.ShapeDtypeStruct((S5_Q, N_OCT, LANES, LANES), BF16),
            jax.ShapeDtypeStruct(mat, BF16),
            jax.ShapeDtypeStruct((S5_QP, N_OCT, 2 * OCT_STATE, 2 * LANES), BF16),
            jax.ShapeDtypeStruct(tab, F32),
            jax.ShapeDtypeStruct(tab, F32),
        ],
        scratch_shapes=[pltpu.VMEM((6, SSM_GROUPS, SSM_WIDTH), F32),
                        pltpu.VMEM((4, SSM_STATE, SSM_WIDTH), F32),
                        pltpu.VMEM((2, SSM_WIDTH, SSM_STATE), F32)],
        compiler_params=pltpu.CompilerParams(
            dimension_semantics=("arbitrary",), vmem_limit_bytes=VMEM_LIMIT),
        name="s5_prep",
    )(*args)

    winp = win.reshape(S5_QP, N_OCT, 2 * LANES, 2 * OCT_STATE)
    return kb, winp, woutp, ptre, ptim


def _half_gelu_tanh(x):
    c = 0.7978845608028654
    return x * (0.25 * (1.0 + jnp.tanh(x * (c + (c * 0.044715) * (x * x)))))


def _s5_kernel(u0_ref, u1_ref, u2_ref, u3_ref, z_ref, kb_ref, winp_ref, woutp_ref, ptre_ref, ptim_ref,
               d_ref, wglu_ref, bglu_ref, o_ref, y0_s, y1_s, y2_s, y3_s, cre_scr, cim_scr):
    u_refs = (u0_ref, u1_ref, u2_ref, u3_ref)
    y_scrs = (y0_s, y1_s, y2_s, y3_s)

    @pl.when(pl.program_id(1) == 0)
    def _():
        cre_scr[...] = jnp.zeros_like(cre_scr)
        cim_scr[...] = jnp.zeros_like(cim_scr)

    def tok(a, s):
        return jnp.concatenate(
            [u_refs[a][pl.ds(s + S5_Q * v, S5_SEG, stride=S5_SEG_PITCH), :] for v in range(S5_NV)], axis=0)

    xp = [[jnp.concatenate([tok(a, 2 * sp), tok(a, 2 * sp + 1)], axis=1).astype(BF16) for a in range(N_OCT)]
          for sp in range(S5_QP)]

    def cmul_add(b_re, b_im, m_re, m_im, x_re, x_im):
        return b_re + m_re * x_re - m_im * x_im, b_im + m_re * x_im + m_im * x_re

    hs = []
    for a in range(N_OCT):
        acc = None
        for sp in range(S5_QP):
            part = jnp.dot(xp[sp][a], winp_ref[sp, a], preferred_element_type=F32)
            acc = part if acc is None else acc + part
        blk = lambda v: (acc[S5_SEG * v:S5_SEG * (v + 1), :OCT_STATE], acc[S5_SEG * v:S5_SEG * (v + 1), OCT_STATE:])
        m_re, m_im = ptre_ref[a, 1:2, :], ptim_ref[a, 1:2, :]
        loc = [blk(0)]
        for v in range(1, S5_NV):
            loc.append(cmul_add(*blk(v), m_re, m_im, *loc[-1]))
        l_re, l_im = ptre_ref[a, S5_NV:S5_NV + 1, :], ptim_ref[a, S5_NV:S5_NV + 1, :]
        c_re, c_im = cre_scr[a:a + 1, :], cim_scr[a:a + 1, :]
        carry = []
        for r in range(S5_SEG):
            carry.append((c_re, c_im))
            c_re, c_im = cmul_add(loc[-1][0][r:r + 1, :], loc[-1][1][r:r + 1, :], l_re, l_im, c_re, c_im)
        cre_scr[a:a + 1, :] = c_re
        cim_scr[a:a + 1, :] = c_im
        cs_re = jnp.concatenate([c[0] for c in carry], axis=0)
        cs_im = jnp.concatenate([c[1] for c in carry], axis=0)
        ent = [(cs_re, cs_im)]
        for v in range(S5_NV - 1):
            ent.append(cmul_add(*loc[v], ptre_ref[a, v + 1:v + 2, :], ptim_ref[a, v + 1:v + 2, :], cs_re, cs_im))
        hs.append(jnp.concatenate([jnp.concatenate([e[0] for e in ent], axis=0),
                                   jnp.concatenate([e[1] for e in ent], axis=0)], axis=1).astype(BF16))

    def pair_tile(d, a):
        below = kb_ref[2 * d - 1, a] if d > 0 else jnp.zeros((LANES, LANES), BF16)
        return jnp.concatenate([jnp.concatenate([kb_ref[2 * d, a], kb_ref[2 * d + 1, a]], axis=1),
                                jnp.concatenate([below, kb_ref[2 * d, a]], axis=1)], axis=0)

    for a in range(N_OCT):
        tiles = [pair_tile(d, a) for d in range(S5_QP)]
        for tp in range(S5_QP):
            acc = jnp.dot(hs[a], woutp_ref[tp, a], preferred_element_type=F32)
            for sp in range(tp + 1):
                acc = acc + jnp.dot(xp[sp][a], tiles[tp - sp], preferred_element_type=F32)
            for v in range(S5_NV):
                rows = slice(S5_SEG * v, S5_SEG * (v + 1))
                y_scrs[a][pl.ds(2 * tp + S5_Q * v, S5_SEG, stride=S5_SEG_PITCH), :] = acc[rows, :LANES]
                y_scrs[a][pl.ds(2 * tp + 1 + S5_Q * v, S5_SEG, stride=S5_SEG_PITCH), :] = acc[rows, LANES:]
        d_a = d_ref[:, LANES * a:LANES * (a + 1)]
        for r in range(S5_SEG):
            prow = slice(S5_SEG_PITCH * r, S5_SEG_PITCH * r + S5_SEG_ROWS)
            y_scrs[a][prow, :] = _half_gelu_tanh(y_scrs[a][prow, :] + d_a * u_refs[a][prow, :])

    wglu = wglu_ref[...].astype(BF16)
    for r0 in range(0, S5_SEG, S5_TAIL_SEGS):
        segs = range(r0, r0 + S5_TAIL_SEGS)
        trow = slice(S5_SEG_ROWS * r0, S5_SEG_ROWS * (r0 + S5_TAIL_SEGS))
        hy = jnp.concatenate(
            [jnp.concatenate([ref[S5_SEG_PITCH * r:S5_SEG_PITCH * r + S5_SEG_ROWS, :] for r in segs], axis=0)
             for ref in y_scrs], axis=1)
        hgate = jnp.dot(hy.astype(BF16), wglu, preferred_element_type=F32) + 0.5 * bglu_ref[...]
        hz = z_ref[trow, :].astype(F32)
        o_ref[trow, :] = (hy * hz * ((1.0 + jnp.tanh(hgate)) * (1.0 + jnp.tanh(hz)))).astype(BF16)


def _s5(u32, pbf, kb, winp, woutp, ptre, ptim, d_row, w_glu, b_glu, batch, seqlen):
    nb = seqlen // S5_TB
    t = batch * seqlen
    u_tile = lambda a: pl.BlockSpec((S5_SEG * S5_SEG_PITCH, LANES), lambda b, i, a=a: (b * nb + i, a))
    return pl.pallas_call(
        _s5_kernel,
        grid=(batch, nb),
        in_specs=[u_tile(a) for a in range(N_OCT)] + [
            pl.BlockSpec((S5_TB, SSM_WIDTH), lambda b, i: (b * nb + i, PBF_ZA)),
            _const_spec(kb.shape), _const_spec(winp.shape), _const_spec(woutp.shape),
            _const_spec(ptre.shape), _const_spec(ptim.shape),
            _const_spec((1, SSM_WIDTH)), _const_spec((SSM_WIDTH, SSM_WIDTH)), _const_spec((1, SSM_WIDTH)),
        ],
        out_specs=pl.BlockSpec((S5_TB, SSM_WIDTH), lambda b, i: (b * nb + i, 0)),
        out_shape=jax.ShapeDtypeStruct((t, SSM_WIDTH), BF16),
        scratch_shapes=[pltpu.VMEM((S5_SEG * S5_SEG_PITCH, LANES), F32)] * N_OCT + [
            pltpu.VMEM((SUBLANES, OCT_STATE), F32),
            pltpu.VMEM((SUBLANES, OCT_STATE), F32),
        ],
        compiler_params=pltpu.CompilerParams(
            dimension_semantics=("arbitrary", "arbitrary"), vmem_limit_bytes=VMEM_LIMIT),
        name="s5",
    )(u32, u32, u32, u32, pbf, kb, winp, woutp, ptre, ptim, d_row, w_glu, b_glu)


def _stage_weights(w_hbm, wbf_s, stage_s, sem):
    n_chunks = IN_WIDTH // W_CHUNK

    def chunk_copy(c):
        return pltpu.make_async_copy(w_hbm.at[:, pl.ds(c * W_CHUNK, W_CHUNK)], stage_s.at[c % 2], sem.at[c % 2])

    chunk_copy(0).start()
    for c in range(n_chunks):
        if c + 1 < n_chunks:
            chunk_copy(c + 1).start()
        chunk_copy(c).wait()
        wbf_s[:, c * W_CHUNK:(c + 1) * W_CHUNK] = stage_s[c % 2].astype(BF16)


def _proj_hgrn_kernel(x_ref, nw_ref, w_hbm, lbl_ref, hnw_ref, ou_ref, or_ref, o_ref,
                      wbf_s, stage_s, stage_sem, ph_s, q0_s, qc_s, kt_s, k0_s, k1_s, k2_s, ke_s, v_s, dec_s, oacc_s,
                      st_s, upd_s, sc_s, am_s,
                      *, blocks_per_seq):
    @pl.when(pl.program_id(0) == 0)
    def _():
        _stage_weights(w_hbm, wbf_s, stage_s, stage_sem)

    @pl.when(pl.program_id(0) % blocks_per_seq == 0)
    def _():
        st_s[...] = jnp.zeros_like(st_s)

    x = x_ref[...]
    ms = jnp.mean(x * x, axis=-1, keepdims=True)
    xn = (x * lax.rsqrt(ms + NORM_EPS) * nw_ref[...]).astype(BF16)
    proj = lambda lo, hi: jnp.dot(xn, wbf_s[:, lo:hi], preferred_element_type=F32)
    ph_s[...] = proj(COL_H, COL_G)
    u = proj(0, COL_ZA)
    for r in range(HG_TB // S5_SEG_ROWS):
        ou_ref[S5_SEG_PITCH * r:S5_SEG_PITCH * r + S5_SEG_ROWS, :] = u[S5_SEG_ROWS * r:S5_SEG_ROWS * (r + 1), :]
        ou_ref[S5_SEG_PITCH * r + S5_SEG_ROWS:S5_SEG_PITCH * (r + 1), :] = jnp.zeros((SUBLANES, SSM_WIDTH), F32)
    or_ref[:, :2 * D_MODEL] = proj(COL_G, IN_WIDTH).astype(BF16)
    or_ref[:, 2 * D_MODEL:] = (0.5 * proj(COL_ZA, COL_H)).astype(BF16)
    col = lambda k: ph_s[:, HGRN_WIDTH * k:HGRN_WIDTH * (k + 1)]

    logits = lbl_ref[...]
    e = jnp.exp(logits - jnp.max(logits, axis=0, keepdims=True))
    lb = (e / jnp.sum(e, axis=0, keepdims=True))[0:1, :]

    q = col(PH_Q)
    qf = q * _sigmoid(q)
    forget = lb + (1.0 - lb) * _sigmoid(col(PH_F))
    lf = jnp.log(forget)
    key = 1.0 - forget

    row = lax.broadcasted_iota(jnp.int32, (HG_TB, HGRN_WIDTH), 0)
    r_sub = row % HG_SUB
    r_ch = row % HG_CH

    def down(x, d):
        return pltpu.roll(x, d, 0)

    def up(x, d):
        return pltpu.roll(x, HG_TB - d, 0)

    a = lf
    d = 1
    while d < HG_SUB:
        a = a + jnp.where(r_sub >= d, down(a, d), 0.0)
        d *= 2
    a3 = a.reshape(HG_TB // HG_SUB, HG_SUB, HGRN_WIDTH)
    tsub = jnp.broadcast_to(a3[:, HG_SUB - 1:HG_SUB, :], a3.shape).reshape(HG_TB, HGRN_WIDTH)
    n_sub = HG_CH // HG_SUB
    prev = [jnp.where(r_ch >= HG_SUB * k, down(tsub, HG_SUB * k), 0.0) for k in range(1, n_sub)]
    nxt = [jnp.where(r_ch < HG_CH - HG_SUB * k, up(tsub, HG_SUB * k), 0.0) for k in range(1, n_sub)]
    eprev = prev[0] + prev[1] + prev[2]
    enext = nxt[0] + nxt[1] + nxt[2]
    suf = tsub - a

    hsub = 0.5 * tsub
    q0_s[...] = (qf * jnp.exp(a - hsub)).astype(BF16)
    qc_s[...] = (qf * jnp.exp(a + eprev)).astype(BF16)
    kt_s[...] = (key * jnp.exp(hsub - a)).astype(BF16)
    k0 = key * jnp.exp(suf)
    e0, e1, e2 = (jnp.exp(0.5 * n) for n in nxt)
    k0_s[...] = (k0 * e0).astype(BF16)
    k1 = k0 * (e0 * e0)
    k1_s[...] = (k1 * e1).astype(BF16)
    k2 = k1 * (e1 * e1)
    k2_s[...] = (k2 * e2).astype(BF16)
    ke_s[...] = (k2 * (e2 * e2)).astype(BF16)
    dec_s[...] = jnp.exp(eprev + tsub + enext)
    v_s[...] = col(PH_I).astype(BF16)

    tq = lax.broadcasted_iota(jnp.int32, (HG_CH, n_sub * HG_CH), 0)
    cc = lax.broadcasted_iota(jnp.int32, (HG_CH, n_sub * HG_CH), 1)
    cls = cc // HG_CH
    ts = cc % HG_CH
    bi = tq // HG_SUB
    bj = ts // HG_SUB
    mask4 = ((cls == 0) & (bi == bj) & (ts <= tq)) | ((cls > 0) & ((bi - bj) == cls))

    n_ch = HG_TB // HG_CH
    half = n_sub * HG_CH // 2
    units = [(c, h, slice(c * HG_CH, (c + 1) * HG_CH), slice(HEAD_DIM * h, HEAD_DIM * (h + 1)))
             for c in range(n_ch) for h in range(HEADS)]
    for c, h, rows, ls in units:
        kcat = jnp.concatenate([kt_s[rows, ls], k0_s[rows, ls], k1_s[rows, ls], k2_s[rows, ls]], axis=0)
        sc_s[c * HEADS + h] = lax.dot_general(q0_s[rows, ls], kcat, (((1,), (1,)), ((), ())),
                                              preferred_element_type=F32)
    for c, h, rows, ls in units:
        upd_s[c * HEADS + h] = lax.dot_general(v_s[rows, ls], ke_s[rows, ls], (((0,), (0,)), ((), ())),
                                               preferred_element_type=F32)
    for c, h, rows, ls in units:
        sc = jnp.where(mask4, sc_s[c * HEADS + h], 0.0)
        am_s[c * HEADS + h] = (sc[:, :half] + sc[:, half:]).astype(BF16)
    for c, h, rows, ls in units:
        vv = v_s[rows, ls]
        oacc_s[rows, ls] = jnp.dot(am_s[c * HEADS + h], jnp.concatenate([vv, vv], axis=0),
                                   preferred_element_type=F32)
    for h in range(HEADS):
        ls = slice(HEAD_DIM * h, HEAD_DIM * (h + 1))
        st = st_s[h]
        for c in range(n_ch):
            rows = slice(c * HG_CH, (c + 1) * HG_CH)
            oacc_s[rows, ls] += lax.dot_general(qc_s[rows, ls], st.astype(BF16), (((1,), (1,)), ((), ())),
                                                preferred_element_type=F32)
            st = st * dec_s[c * HG_CH:c * HG_CH + 1, ls] + upd_s[c * HEADS + h]
        st_s[h] = st

    o = oacc_s[...] * _sigmoid(col(PH_OG))
    parts = []
    for h in range(HEADS):
        oh = o[:, HEAD_DIM * h:HEAD_DIM * (h + 1)]
        ms = jnp.mean(oh * oh, axis=-1, keepdims=True)
        parts.append(oh * lax.rsqrt(ms + NORM_EPS))
    o = jnp.concatenate(parts, axis=1) * hnw_ref[...]
    hz = 0.5 * col(PH_ZB)
    o_ref[...] = (o * (hz * (1.0 + jnp.tanh(hz)))).astype(BF16)


def _proj_hgrn(x2, norm_w, w, lb_logits, hgrn_norm_w, seqlen):
    t = x2.shape[0]
    wide = (HG_TB, HGRN_WIDTH)
    units = HG_TB // HG_CH * HEADS
    n_cls = HG_CH // HG_SUB
    row_blk = lambda w: pl.BlockSpec((HG_TB, w), lambda i: (i, 0))
    u_rows = HG_TB // S5_SEG_ROWS * S5_SEG_PITCH
    return pl.pallas_call(
        functools.partial(_proj_hgrn_kernel, blocks_per_seq=seqlen // HG_TB),
        grid=(t // HG_TB,),
        in_specs=[
            row_blk(D_MODEL), _const_spec((1, D_MODEL)),
            pl.BlockSpec(memory_space=pl.ANY),
            _const_spec(lb_logits.shape), _const_spec((1, HGRN_WIDTH)),
        ],
        out_specs=[pl.BlockSpec((u_rows, SSM_WIDTH), lambda i: (i, 0)), row_blk(PBF_WIDTH), row_blk(HGRN_WIDTH)],
        out_shape=[
            jax.ShapeDtypeStruct((t // HG_TB * u_rows, SSM_WIDTH), F32),
            jax.ShapeDtypeStruct((t, PBF_WIDTH), BF16),
            jax.ShapeDtypeStruct((t, HGRN_WIDTH), BF16),
        ],
        scratch_shapes=[
            pltpu.VMEM((D_MODEL, IN_WIDTH), BF16), pltpu.VMEM((2, D_MODEL, W_CHUNK), F32),
            pltpu.SemaphoreType.DMA((2,)),
            pltpu.VMEM((HG_TB, PH_WIDTH), F32)] + [pltpu.VMEM(wide, BF16)] * 8 + [
            pltpu.VMEM(wide, F32), pltpu.VMEM(wide, F32),
            pltpu.VMEM((HEADS, HEAD_DIM, HEAD_DIM), F32),
            pltpu.VMEM((units, HEAD_DIM, HEAD_DIM), F32),
            pltpu.VMEM((units, HG_CH, n_cls * HG_CH), F32),
            pltpu.VMEM((units, HG_CH, n_cls * HG_CH // 2), BF16)],
        compiler_params=pltpu.CompilerParams(
            dimension_semantics=("arbitrary",), vmem_limit_bytes=VMEM_LIMIT),
        name="proj_hgrn",
    )(x2, norm_w, w, lb_logits, hgrn_norm_w)


def _merge_kernel(x_ref, ya_ref, yb_ref, ga_ref, gb_ref, wpa_ref, wpb_ref, wo_ref, fnw_ref, o_ref):
    wpa, wpb, wo = wpa_ref[...].astype(BF16), wpb_ref[...].astype(BF16), wo_ref[...].astype(BF16)
    for r in range(TM_OUT // MERGE_ROWS):
        rows = slice(MERGE_ROWS * r, MERGE_ROWS * (r + 1))
        pa = jnp.dot(ya_ref[rows, :], wpa, preferred_element_type=F32)
        pb = jnp.dot(yb_ref[rows, :], wpb, preferred_element_type=F32)
        merged = _sigmoid(ga_ref[rows, :].astype(F32)) * pa + _sigmoid(gb_ref[rows, :].astype(F32)) * pb
        h = x_ref[rows, :] + jnp.dot(merged.astype(BF16), wo, preferred_element_type=F32)
        ms = jnp.mean(h * h, axis=-1, keepdims=True)
        o_ref[rows, :] = h * lax.rsqrt(ms + NORM_EPS) * fnw_ref[...]


def _merge(x2, ya, yb, pbf, w_pa, w_pb, w_out, fnw):
    t = x2.shape[0]
    return pl.pallas_call(
        _merge_kernel,
        grid=(t // TM_OUT,),
        in_specs=[
            pl.BlockSpec((TM_OUT, D_MODEL), lambda i: (i, 0)),
            pl.BlockSpec((TM_OUT, SSM_WIDTH), lambda i: (i, 0)),
            pl.BlockSpec((TM_OUT, HGRN_WIDTH), lambda i: (i, 0)),
            pl.BlockSpec((TM_OUT, D_MODEL), lambda i: (i, 0)),
            pl.BlockSpec((TM_OUT, D_MODEL), lambda i: (i, 1)),
            _const_spec((SSM_WIDTH, D_MODEL)), _const_spec((HGRN_WIDTH, D_MODEL)),
            _const_spec((D_MODEL, D_MODEL)), _const_spec((1, D_MODEL)),
        ],
        out_specs=pl.BlockSpec((TM_OUT, D_MODEL), lambda i: (i, 0)),
        out_shape=jax.ShapeDtypeStruct((t, D_MODEL), F32),
        compiler_params=pltpu.CompilerParams(
            dimension_semantics=("arbitrary",), vmem_limit_bytes=VMEM_LIMIT),
        name="merge",
    )(x2, ya, yb, pbf, pbf, w_pa, w_pb, w_out, fnw)


def kernel(x, norm_w, w_in, ssm_lambda_re, ssm_lambda_im, ssm_b_re, ssm_b_im, ssm_c_re, ssm_c_im, ssm_d,
           ssm_log_dt, ssm_w_glu, ssm_b_glu, hgrn_lb_logits, hgrn_norm_w, w_proj_a, w_proj_b, w_out,
           final_norm_w):
    batch, seqlen, _ = x.shape
    assert norm_w.shape[0] == 1, "single-layer block"
    assert seqlen % S5_TB == 0 and seqlen % HG_TB == 0 and HG_TB % S5_SEG_ROWS == 0
    x2 = x.reshape(batch * seqlen, D_MODEL)
    u32, pbf, yb = _proj_hgrn(x2, norm_w[0][None, :], w_in[0], hgrn_lb_logits,
                              hgrn_norm_w[0][None, :], seqlen)

    kb, winp, woutp, ptre, ptim = _s5_prep(ssm_lambda_re[0], ssm_lambda_im[0], ssm_b_re[0], ssm_b_im[0],
                                           ssm_c_re[0], ssm_c_im[0], ssm_log_dt[0])
    ya = _s5(u32, pbf, kb, winp, woutp, ptre, ptim, ssm_d[0].reshape(1, SSM_WIDTH),
             ssm_w_glu[0], ssm_b_glu[0][None, :], batch, seqlen)
    out = _merge(x2, ya, yb, pbf, w_proj_a[0], w_proj_b[0], w_out[0], final_norm_w[None, :])
    return out.reshape(batch, seqlen, D_MODEL)
```

```python
import functools

import jax
import jax.numpy as jnp
from jax import lax
from jax.experimental import pallas as pl
from jax.experimental.pallas import tpu as pltpu

F32 = jnp.float32
BF16 = jnp.bfloat16

D_MODEL = 1024
SSM_WIDTH = 512
SSM_GROUP = 16
SSM_GROUPS = 32
SSM_STATE = 64
HGRN_WIDTH = 512
HEAD_DIM = 128
HEADS = 4
NORM_EPS = 1e-6
LAMBDA_RE_MAX = -1e-4

IN_WIDTH = 2 * SSM_WIDTH + 5 * HGRN_WIDTH + 2 * D_MODEL
COL_ZA, COL_H, COL_G = SSM_WIDTH, 2 * SSM_WIDTH, 2 * SSM_WIDTH + 5 * HGRN_WIDTH
PH_WIDTH = 5 * HGRN_WIDTH
PH_Q, PH_F, PH_I, PH_OG, PH_ZB = 0, 1, 2, 3, 4
PBF_WIDTH = 2 * D_MODEL + SSM_WIDTH
PBF_ZA = 4

LANES = 128
SUBLANES = 8
OCT = LANES // SSM_GROUP
N_OCT = SSM_GROUPS // OCT
OCT_STATE = OCT * SSM_STATE

S5_Q = 4
S5_QP = S5_Q // 2
S5_TB = 2048
S5_NCH = S5_TB // S5_Q
S5_SEG = SUBLANES
S5_NV = S5_NCH // S5_SEG
S5_PT_ROWS = S5_NV + SUBLANES
S5_SEG_ROWS = S5_Q * S5_NV
S5_TAIL_SEGS = 2
S5_SEG_PITCH = S5_SEG_ROWS + SUBLANES
HG_TB = 512
HG_CH = 64
HG_SUB = 16
W_CHUNK = 512
TM_OUT = 1024
MERGE_ROWS = 512
MERGE_X_BUFS = 3

V7X_VMEM_BYTES = 64 * 1024 * 1024
VMEM_LIMIT = V7X_VMEM_BYTES * 7 // 8


def _sigmoid(x):
    return 0.5 * jnp.tanh(0.5 * x) + 0.5


def _const_spec(shape):
    nd = len(shape)
    return pl.BlockSpec(shape, lambda *_: (0,) * nd, pipeline_mode=pl.Buffered(1))


def _disc(lam_re, lam_im, log_dt):
    return jnp.minimum(lam_re, LAMBDA_RE_MAX), lam_im, jnp.exp(log_dt)


def _cpow(lr, li, dt, k):
    mag = jnp.exp(k * (lr * dt))
    ang = k * (li * dt)
    return mag * jnp.cos(ang), mag * jnp.sin(ang)


def _split_bf16(x):
    hi = x.astype(BF16)
    return hi, (x - hi.astype(F32)).astype(BF16)


def _dot3(a, b_hi, b_lo):
    a_hi, a_lo = _split_bf16(a)
    d = lambda p, q: jnp.dot(p, q, preferred_element_type=F32)
    return d(a_hi, b_hi) + (d(a_hi, b_lo) + d(a_lo, b_hi))


def _cmul(a_re, a_im, b_re, b_im):
    return a_re * b_re - a_im * b_im, a_re * b_im + a_im * b_re


def _s5_prep_kernel(lam_c, b_t, c_n, lam_n, lam_r,
                    kb_ref, win_ref, wout_ref, ptre_ref, ptim_ref, pw_s, pm_s, xk_s):
    @pl.when(pl.program_id(0) == 0)
    def _():
        lr, li, dt = _disc(lam_c[0], lam_c[1], lam_c[2])
        ab_re, ab_im = _cpow(lr, li, dt, 1.0)
        den = lr * lr + li * li
        nr = ab_re - 1.0
        pw_s[0] = (nr * lr + ab_im * li) / den
        pw_s[1] = (ab_im * lr - nr * li) / den
        pw_s[2] = jnp.ones_like(ab_re)
        pw_s[3] = jnp.zeros_like(ab_re)
        pw_s[4] = ab_re
        pw_s[5] = ab_im
        m_re, m_im = _cpow(*_disc(lam_n[0], lam_n[1], lam_n[2]), 1.0)
        pm_s[0] = m_re
        pm_s[1] = m_im
        pm_s[2] = m_re
        pm_s[3] = m_im
        j = lax.broadcasted_iota(jnp.int32, (S5_PT_ROWS, OCT_STATE), 0).astype(F32) * float(S5_Q)
        for a in range(N_OCT):
            t_re, t_im = _cpow(*_disc(lam_r[0, a:a + 1, :], lam_r[1, a:a + 1, :], lam_r[2, a:a + 1, :]), j)
            ptre_ref[a] = t_re
            ptim_ref[a] = t_im

    col = lax.broadcasted_iota(jnp.int32, (SSM_GROUP, SSM_WIDTH), 1)

    def strip(g, carry):
        rows = pl.ds(pl.multiple_of(g * SSM_GROUP, SSM_GROUP), SSM_GROUP)
        grow = pl.ds(g, 1)
        coef_re, coef_im = pw_s[0, grow, :], pw_s[1, grow, :]
        p_re, p_im = pw_s[2, grow, :], pw_s[3, grow, :]
        bb_re, bb_im = _cmul(coef_re, coef_im, b_t[0, rows, :], b_t[1, rows, :])
        x_re, x_im = _cmul(bb_re, bb_im, p_re, p_im)
        xk_s[0, rows, :] = x_re[:, :SSM_STATE]
        xk_s[1, rows, :] = x_im[:, :SSM_STATE]
        a = g // OCT
        lrows = pl.ds(pl.multiple_of((g % OCT) * SSM_GROUP, SSM_GROUP), SSM_GROUP)
        m_in = col // SSM_STATE == g % OCT
        win_ref[0, a, 0, lrows, :] = jnp.concatenate(
            [jnp.where(m_in, x_re, 0.0), jnp.where(m_in, x_im, 0.0)], axis=1).astype(BF16)
        return carry

    lax.fori_loop(0, SSM_GROUPS, strip, 0, unroll=4)

    cr, ci = c_n[0], c_n[1]
    kfull = _dot3(xk_s[0], *_split_bf16(cr)) - _dot3(xk_s[1], *_split_bf16(ci))
    rowk = lax.broadcasted_iota(jnp.int32, (LANES, LANES), 0)
    colk = lax.broadcasted_iota(jnp.int32, (LANES, LANES), 1)
    m_k = rowk // SSM_GROUP == colk // SSM_GROUP
    for a in range(N_OCT):
        ls = slice(LANES * a, LANES * (a + 1))
        kb_ref[0, a] = jnp.where(m_k, kfull[ls, ls], 0.0).astype(BF16)

    p1_re, p1_im = pm_s[0], pm_s[1]
    w_re, w_im = _cmul(cr, ci, p1_re, p1_im)
    rowo = lax.broadcasted_iota(jnp.int32, (OCT_STATE, LANES), 0)
    colo = lax.broadcasted_iota(jnp.int32, (OCT_STATE, LANES), 1)
    m_out = rowo // SSM_STATE == colo // SSM_GROUP
    for a in range(N_OCT):
        ls = slice(LANES * a, LANES * (a + 1))
        wout_ref[0, a, :OCT_STATE, :] = jnp.where(m_out, jnp.tile(w_re[:, ls], (OCT, 1)), 0.0).astype(BF16)
        wout_ref[0, a, OCT_STATE:, :] = jnp.where(m_out, jnp.tile(-w_im[:, ls], (OCT, 1)), 0.0).astype(BF16)

    pw_s[2], pw_s[3] = _cmul(pw_s[2], pw_s[3], pw_s[4], pw_s[5])
    pm_s[0], pm_s[1] = _cmul(p1_re, p1_im, pm_s[2], pm_s[3])


def _s5_prep(lam_re, lam_im, b_re, b_im, c_re, c_im, log_dt):
    g, n, p = SSM_GROUPS, SSM_STATE, SSM_GROUP
    lam = jnp.stack([lam_re, lam_im, jnp.broadcast_to(log_dt[:, None], (g, n))])
    b = jnp.stack([b_re, b_im])
    c = jnp.stack([c_re, c_im])
    args = (jnp.tile(lam, (1, 1, OCT)),
            jnp.tile(b.transpose(0, 1, 3, 2).reshape(2, g * p, n), (1, 1, OCT)),
            c.transpose(0, 3, 1, 2).reshape(2, n, g * p),
            jnp.repeat(lam.transpose(0, 2, 1), p, axis=2),
            lam.reshape(3, N_OCT, OCT_STATE))
    mat = (S5_QP, N_OCT, 2, LANES, 2 * OCT_STATE)
    blk = (1, N_OCT, 1, LANES, 2 * OCT_STATE)
    tab = (N_OCT, S5_PT_ROWS, OCT_STATE)
    kb, win, woutp, ptre, ptim = pl.pallas_call(
        _s5_prep_kernel,
        grid=(S5_Q,),
        in_specs=[_const_spec(a.shape) for a in args],
        out_specs=[
            pl.BlockSpec((1, N_OCT, LANES, LANES), lambda t: (t, 0, 0, 0)),
            pl.BlockSpec(blk, lambda t: ((S5_Q - 1 - t) // 2, 0, (S5_Q - 1 - t) % 2, 0, 0)),
            pl.BlockSpec((1, N_OCT, 2 * OCT_STATE, LANES), lambda t: (t // 2, 0, 0, t % 2)),
            pl.BlockSpec(tab, lambda t: (0, 0, 0)),
            pl.BlockSpec(tab, lambda t: (0, 0, 0)),
        ],
        out_shape=[
            jax.ShapeDtypeStruct((S5_Q, N_OCT, LANES, LANES), BF16),
            jax.ShapeDtypeStruct(mat, BF16),
            jax.ShapeDtypeStruct((S5_QP, N_OCT, 2 * OCT_STATE, 2 * LANES), BF16),
            jax.ShapeDtypeStruct(tab, F32),
            jax.ShapeDtypeStruct(tab, F32),
        ],
        scratch_shapes=[pltpu.VMEM((6, SSM_GROUPS, SSM_WIDTH), F32),
                        pltpu.VMEM((4, SSM_STATE, SSM_WIDTH), F32),
                        pltpu.VMEM((2, SSM_WIDTH, SSM_STATE), F32)],
        compiler_params=pltpu.CompilerParams(
            dimension_semantics=("arbitrary",), vmem_limit_bytes=VMEM_LIMIT),
        name="s5_prep",
    )(*args)

    winp = win.reshape(S5_QP, N_OCT, 2 * LANES, 2 * OCT_STATE)
    return kb, winp, woutp, ptre, ptim


def _gelu_tanh(x):
    c = 0.7978845608028654
    return x * (0.5 * (1.0 + jnp.tanh(x * (c + (c * 0.044715) * (x * x)))))


def _s5_kernel(u0_ref, u1_ref, u2_ref, u3_ref, z_ref, kb_ref, winp_ref, woutp_ref, ptre_ref, ptim_ref,
               d_ref, wglu_ref, bglu_ref, o_ref, y0_s, y1_s, y2_s, y3_s, cre_scr, cim_scr):
    u_refs = (u0_ref, u1_ref, u2_ref, u3_ref)
    y_scrs = (y0_s, y1_s, y2_s, y3_s)

    @pl.when(pl.program_id(1) == 0)
    def _():
        cre_scr[...] = jnp.zeros_like(cre_scr)
        cim_scr[...] = jnp.zeros_like(cim_scr)

    def tok(a, s):
        return jnp.concatenate(
            [u_refs[a][pl.ds(s + S5_Q * v, S5_SEG, stride=S5_SEG_PITCH), :] for v in range(S5_NV)], axis=0)

    xp = [[jnp.concatenate([tok(a, 2 * sp), tok(a, 2 * sp + 1)], axis=1).astype(BF16) for a in range(N_OCT)]
          for sp in range(S5_QP)]

    def cmul_add(b_re, b_im, m_re, m_im, x_re, x_im):
        return b_re + m_re * x_re - m_im * x_im, b_im + m_re * x_im + m_im * x_re

    hs = []
    for a in range(N_OCT):
        acc = None
        for sp in range(S5_QP):
            part = jnp.dot(xp[sp][a], winp_ref[sp, a], preferred_element_type=F32)
            acc = part if acc is None else acc + part
        blk = lambda v: (acc[S5_SEG * v:S5_SEG * (v + 1), :OCT_STATE], acc[S5_SEG * v:S5_SEG * (v + 1), OCT_STATE:])
        m_re, m_im = ptre_ref[a, 1:2, :], ptim_ref[a, 1:2, :]
        loc = [blk(0)]
        for v in range(1, S5_NV):
            loc.append(cmul_add(*blk(v), m_re, m_im, *loc[-1]))
        l_re, l_im = ptre_ref[a, S5_NV:S5_NV + 1, :], ptim_ref[a, S5_NV:S5_NV + 1, :]
        c_re, c_im = cre_scr[a:a + 1, :], cim_scr[a:a + 1, :]
        carry = []
        for r in range(S5_SEG):
            carry.append((c_re, c_im))
            c_re, c_im = cmul_add(loc[-1][0][r:r + 1, :], loc[-1][1][r:r + 1, :], l_re, l_im, c_re, c_im)
        cre_scr[a:a + 1, :] = c_re
        cim_scr[a:a + 1, :] = c_im
        cs_re = jnp.concatenate([c[0] for c in carry], axis=0)
        cs_im = jnp.concatenate([c[1] for c in carry], axis=0)
        ent = [(cs_re, cs_im)]
        for v in range(S5_NV - 1):
            ent.append(cmul_add(*loc[v], ptre_ref[a, v + 1:v + 2, :], ptim_ref[a, v + 1:v + 2, :], cs_re, cs_im))
        hs.append(jnp.concatenate([jnp.concatenate([e[0] for e in ent], axis=0),
                                   jnp.concatenate([e[1] for e in ent], axis=0)], axis=1).astype(BF16))

    def pair_tile(d, a):
        below = kb_ref[2 * d - 1, a] if d > 0 else jnp.zeros((LANES, LANES), BF16)
        return jnp.concatenate([jnp.concatenate([kb_ref[2 * d, a], kb_ref[2 * d + 1, a]], axis=1),
                                jnp.concatenate([below, kb_ref[2 * d, a]], axis=1)], axis=0)

    for a in range(N_OCT):
        tiles = [pair_tile(d, a) for d in range(S5_QP)]
        for tp in range(S5_QP):
            acc = jnp.dot(hs[a], woutp_ref[tp, a], preferred_element_type=F32)
            for sp in range(tp + 1):
                acc = acc + jnp.dot(xp[sp][a], tiles[tp - sp], preferred_element_type=F32)
            for v in range(S5_NV):
                rows = slice(S5_SEG * v, S5_SEG * (v + 1))
                y_scrs[a][pl.ds(2 * tp + S5_Q * v, S5_SEG, stride=S5_SEG_PITCH), :] = acc[rows, :LANES]
                y_scrs[a][pl.ds(2 * tp + 1 + S5_Q * v, S5_SEG, stride=S5_SEG_PITCH), :] = acc[rows, LANES:]
        d_a = d_ref[:, LANES * a:LANES * (a + 1)]
        for r in range(S5_SEG):
            prow = slice(S5_SEG_PITCH * r, S5_SEG_PITCH * r + S5_SEG_ROWS)
            y_scrs[a][prow, :] = _gelu_tanh(y_scrs[a][prow, :] + d_a * u_refs[a][prow, :])

    wglu = wglu_ref[...].astype(BF16)
    for r0 in range(0, S5_SEG, S5_TAIL_SEGS):
        segs = range(r0, r0 + S5_TAIL_SEGS)
        trow = slice(S5_SEG_ROWS * r0, S5_SEG_ROWS * (r0 + S5_TAIL_SEGS))
        y = jnp.concatenate(
            [jnp.concatenate([ref[S5_SEG_PITCH * r:S5_SEG_PITCH * r + S5_SEG_ROWS, :] for r in segs], axis=0)
             for ref in y_scrs], axis=1)
        hy = 0.5 * y
        hgate = jnp.dot(hy.astype(BF16), wglu, preferred_element_type=F32) + 0.5 * bglu_ref[...]
        hz = 0.5 * z_ref[trow, :].astype(F32)
        o_ref[trow, :] = (hy * hz * ((1.0 + jnp.tanh(hgate)) * (1.0 + jnp.tanh(hz)))).astype(BF16)


def _s5(u32, pbf, kb, winp, woutp, ptre, ptim, d_row, w_glu, b_glu, batch, seqlen):
    nb = seqlen // S5_TB
    t = batch * seqlen
    u_tile = lambda a: pl.BlockSpec((S5_SEG * S5_SEG_PITCH, LANES), lambda b, i, a=a: (b * nb + i, a))
    return pl.pallas_call(
        _s5_kernel,
        grid=(batch, nb),
        in_specs=[u_tile(a) for a in range(N_OCT)] + [
            pl.BlockSpec((S5_TB, SSM_WIDTH), lambda b, i: (b * nb + i, PBF_ZA)),
            _const_spec(kb.shape), _const_spec(winp.shape), _const_spec(woutp.shape),
            _const_spec(ptre.shape), _const_spec(ptim.shape),
            _const_spec((1, SSM_WIDTH)), _const_spec((SSM_WIDTH, SSM_WIDTH)), _const_spec((1, SSM_WIDTH)),
        ],
        out_specs=pl.BlockSpec((S5_TB, SSM_WIDTH), lambda b, i: (b * nb + i, 0)),
        out_shape=jax.ShapeDtypeStruct((t, SSM_WIDTH), BF16),
        scratch_shapes=[pltpu.VMEM((S5_SEG * S5_SEG_PITCH, LANES), F32)] * N_OCT + [
            pltpu.VMEM((SUBLANES, OCT_STATE), F32),
            pltpu.VMEM((SUBLANES, OCT_STATE), F32),
        ],
        compiler_params=pltpu.CompilerParams(
            dimension_semantics=("arbitrary", "arbitrary"), vmem_limit_bytes=VMEM_LIMIT),
        name="s5",
    )(u32, u32, u32, u32, pbf, kb, winp, woutp, ptre, ptim, d_row, w_glu, b_glu)


def _stage_weights(w_hbm, wbf_s, stage_s, sem):
    n_chunks = IN_WIDTH // W_CHUNK

    def chunk_copy(c):
        return pltpu.make_async_copy(w_hbm.at[:, pl.ds(c * W_CHUNK, W_CHUNK)], stage_s.at[c % 2], sem.at[c % 2])

    chunk_copy(0).start()
    for c in range(n_chunks):
        if c + 1 < n_chunks:
            chunk_copy(c + 1).start()
        chunk_copy(c).wait()
        wbf_s[:, c * W_CHUNK:(c + 1) * W_CHUNK] = stage_s[c % 2].astype(BF16)


def _proj_hgrn_kernel(x_ref, nw_ref, w_hbm, lbl_ref, hnw_ref, ou_ref, or_ref, o_ref,
                      wbf_s, stage_s, stage_sem, ph_s, q0_s, qc_s, kt_s, k0_s, k1_s, k2_s, ke_s, v_s, dec_s, oacc_s,
                      st_s, upd_s, sc_s, am_s,
                      *, blocks_per_seq):
    @pl.when(pl.program_id(0) == 0)
    def _():
        _stage_weights(w_hbm, wbf_s, stage_s, stage_sem)

    @pl.when(pl.program_id(0) % blocks_per_seq == 0)
    def _():
        st_s[...] = jnp.zeros_like(st_s)

    x = x_ref[...]
    ms = jnp.mean(x * x, axis=-1, keepdims=True)
    xn = (x * lax.rsqrt(ms + NORM_EPS) * nw_ref[...]).astype(BF16)
    proj = lambda lo, hi: jnp.dot(xn, wbf_s[:, lo:hi], preferred_element_type=F32)
    ph_s[...] = proj(COL_H, COL_G)
    u = proj(0, COL_ZA)
    for r in range(HG_TB // S5_SEG_ROWS):
        ou_ref[S5_SEG_PITCH * r:S5_SEG_PITCH * r + S5_SEG_ROWS, :] = u[S5_SEG_ROWS * r:S5_SEG_ROWS * (r + 1), :]
        ou_ref[S5_SEG_PITCH * r + S5_SEG_ROWS:S5_SEG_PITCH * (r + 1), :] = jnp.zeros((SUBLANES, SSM_WIDTH), F32)
    or_ref[:, :2 * D_MODEL] = proj(COL_G, IN_WIDTH).astype(BF16)
    or_ref[:, 2 * D_MODEL:] = proj(COL_ZA, COL_H).astype(BF16)
    col = lambda k: ph_s[:, HGRN_WIDTH * k:HGRN_WIDTH * (k + 1)]

    logits = lbl_ref[...]
    e = jnp.exp(logits - jnp.max(logits, axis=0, keepdims=True))
    lb = (e / jnp.sum(e, axis=0, keepdims=True))[0:1, :]

    q = col(PH_Q)
    qf = q * _sigmoid(q)
    forget = lb + (1.0 - lb) * _sigmoid(col(PH_F))
    lf = jnp.log(forget)
    key = 1.0 - forget

    row = lax.broadcasted_iota(jnp.int32, (HG_TB, HGRN_WIDTH), 0)
    r_sub = row % HG_SUB
    r_ch = row % HG_CH

    def down(x, d):
        return pltpu.roll(x, d, 0)

    def up(x, d):
        return pltpu.roll(x, HG_TB - d, 0)

    a = lf
    d = 1
    while d < HG_SUB:
        a = a + jnp.where(r_sub >= d, down(a, d), 0.0)
        d *= 2
    a3 = a.reshape(HG_TB // HG_SUB, HG_SUB, HGRN_WIDTH)
    tsub = jnp.broadcast_to(a3[:, HG_SUB - 1:HG_SUB, :], a3.shape).reshape(HG_TB, HGRN_WIDTH)
    n_sub = HG_CH // HG_SUB
    prev = [jnp.where(r_ch >= HG_SUB * k, down(tsub, HG_SUB * k), 0.0) for k in range(1, n_sub)]
    nxt = [jnp.where(r_ch < HG_CH - HG_SUB * k, up(tsub, HG_SUB * k), 0.0) for k in range(1, n_sub)]
    eprev = prev[0] + prev[1] + prev[2]
    enext = nxt[0] + nxt[1] + nxt[2]
    suf = tsub - a

    hsub = 0.5 * tsub
    q0_s[...] = (qf * jnp.exp(a - hsub)).astype(BF16)
    qc_s[...] = (qf * jnp.exp(a + eprev)).astype(BF16)
    kt_s[...] = (key * jnp.exp(hsub - a)).astype(BF16)
    k0 = key * jnp.exp(suf)
    e0, e1, e2 = (jnp.exp(0.5 * n) for n in nxt)
    k0_s[...] = (k0 * e0).astype(BF16)
    k1 = k0 * (e0 * e0)
    k1_s[...] = (k1 * e1).astype(BF16)
    k2 = k1 * (e1 * e1)
    k2_s[...] = (k2 * e2).astype(BF16)
    ke_s[...] = (k2 * (e2 * e2)).astype(BF16)
    dec_s[...] = jnp.exp(eprev + tsub + enext)
    v_s[...] = col(PH_I).astype(BF16)

    tq = lax.broadcasted_iota(jnp.int32, (HG_CH, n_sub * HG_CH), 0)
    cc = lax.broadcasted_iota(jnp.int32, (HG_CH, n_sub * HG_CH), 1)
    cls = cc // HG_CH
    ts = cc % HG_CH
    bi = tq // HG_SUB
    bj = ts // HG_SUB
    mask4 = ((cls == 0) & (bi == bj) & (ts <= tq)) | ((cls > 0) & ((bi - bj) == cls))

    n_ch = HG_TB // HG_CH
    half = n_sub * HG_CH // 2
    units = [(c, h, slice(c * HG_CH, (c + 1) * HG_CH), slice(HEAD_DIM * h, HEAD_DIM * (h + 1)))
             for c in range(n_ch) for h in range(HEADS)]
    for c, h, rows, ls in units:
        kcat = jnp.concatenate([kt_s[rows, ls], k0_s[rows, ls], k1_s[rows, ls], k2_s[rows, ls]], axis=0)
        sc_s[c * HEADS + h] = lax.dot_general(q0_s[rows, ls], kcat, (((1,), (1,)), ((), ())),
                                              preferred_element_type=F32)
    for c, h, rows, ls in units:
        upd_s[c * HEADS + h] = lax.dot_general(v_s[rows, ls], ke_s[rows, ls], (((0,), (0,)), ((), ())),
                                               preferred_element_type=F32)
    for c, h, rows, ls in units:
        sc = jnp.where(mask4, sc_s[c * HEADS + h], 0.0)
        am_s[c * HEADS + h] = (sc[:, :half] + sc[:, half:]).astype(BF16)
    for c, h, rows, ls in units:
        vv = v_s[rows, ls]
        oacc_s[rows, ls] = jnp.dot(am_s[c * HEADS + h], jnp.concatenate([vv, vv], axis=0),
                                   preferred_element_type=F32)
    for h in range(HEADS):
        ls = slice(HEAD_DIM * h, HEAD_DIM * (h + 1))
        st = st_s[h]
        for c in range(n_ch):
            rows = slice(c * HG_CH, (c + 1) * HG_CH)
            oacc_s[rows, ls] += lax.dot_general(qc_s[rows, ls], st.astype(BF16), (((1,), (1,)), ((), ())),
                                                preferred_element_type=F32)
            st = st * dec_s[c * HG_CH:c * HG_CH + 1, ls] + upd_s[c * HEADS + h]
        st_s[h] = st

    o = oacc_s[...] * _sigmoid(col(PH_OG))
    parts = []
    for h in range(HEADS):
        oh = o[:, HEAD_DIM * h:HEAD_DIM * (h + 1)]
        ms = jnp.mean(oh * oh, axis=-1, keepdims=True)
        parts.append(oh * lax.rsqrt(ms + NORM_EPS))
    o = jnp.concatenate(parts, axis=1) * hnw_ref[...]
    hz = 0.5 * col(PH_ZB)
    o_ref[...] = (o * (hz * (1.0 + jnp.tanh(hz)))).astype(BF16)


def _proj_hgrn(x2, norm_w, w, lb_logits, hgrn_norm_w, seqlen):
    t = x2.shape[0]
    wide = (HG_TB, HGRN_WIDTH)
    units = HG_TB // HG_CH * HEADS
    n_cls = HG_CH // HG_SUB
    row_blk = lambda w: pl.BlockSpec((HG_TB, w), lambda i: (i, 0))
    u_rows = HG_TB // S5_SEG_ROWS * S5_SEG_PITCH
    return pl.pallas_call(
        functools.partial(_proj_hgrn_kernel, blocks_per_seq=seqlen // HG_TB),
        grid=(t // HG_TB,),
        in_specs=[
            row_blk(D_MODEL), _const_spec((1, D_MODEL)),
            pl.BlockSpec(memory_space=pl.ANY),
            _const_spec(lb_logits.shape), _const_spec((1, HGRN_WIDTH)),
        ],
        out_specs=[pl.BlockSpec((u_rows, SSM_WIDTH), lambda i: (i, 0)), row_blk(PBF_WIDTH), row_blk(HGRN_WIDTH)],
        out_shape=[
            jax.ShapeDtypeStruct((t // HG_TB * u_rows, SSM_WIDTH), F32),
            jax.ShapeDtypeStruct((t, PBF_WIDTH), BF16),
            jax.ShapeDtypeStruct((t, HGRN_WIDTH), BF16),
        ],
        scratch_shapes=[
            pltpu.VMEM((D_MODEL, IN_WIDTH), BF16), pltpu.VMEM((2, D_MODEL, W_CHUNK), F32),
            pltpu.SemaphoreType.DMA((2,)),
            pltpu.VMEM((HG_TB, PH_WIDTH), F32)] + [pltpu.VMEM(wide, BF16)] * 8 + [
            pltpu.VMEM(wide, F32), pltpu.VMEM(wide, F32),
            pltpu.VMEM((HEADS, HEAD_DIM, HEAD_DIM), F32),
            pltpu.VMEM((units, HEAD_DIM, HEAD_DIM), F32),
            pltpu.VMEM((units, HG_CH, n_cls * HG_CH), F32),
            pltpu.VMEM((units, HG_CH, n_cls * HG_CH // 2), BF16)],
        compiler_params=pltpu.CompilerParams(
            dimension_semantics=("arbitrary",), vmem_limit_bytes=VMEM_LIMIT),
        name="proj_hgrn",
    )(x2, norm_w, w, lb_logits, hgrn_norm_w)


def _x_block_copy(x_hbm, x_s, sem, step):
    slot = step % MERGE_X_BUFS
    return pltpu.make_async_copy(x_hbm.at[pl.ds(step * TM_OUT, TM_OUT), :], x_s.at[slot], sem.at[slot])


def _merge_kernel(x_hbm, ya_ref, yb_ref, ga_ref, gb_ref, wpa_ref, wpb_ref, wo_ref, fnw_ref, o_ref, x_s, x_sem):
    step, n_steps = pl.program_id(0), pl.num_programs(0)

    @pl.when(step == 0)
    def _():
        for k in range(MERGE_X_BUFS - 1):
            _x_block_copy(x_hbm, x_s, x_sem, k).start()

    ahead = step + (MERGE_X_BUFS - 1)

    @pl.when(ahead < n_steps)
    def _():
        _x_block_copy(x_hbm, x_s, x_sem, ahead).start()

    _x_block_copy(x_hbm, x_s, x_sem, step).wait()
    x_ref = x_s.at[step % MERGE_X_BUFS]
    wpa, wpb, wo = wpa_ref[...].astype(BF16), wpb_ref[...].astype(BF16), wo_ref[...].astype(BF16)
    for r in range(TM_OUT // MERGE_ROWS):
        rows = slice(MERGE_ROWS * r, MERGE_ROWS * (r + 1))
        pa = jnp.dot(ya_ref[rows, :], wpa, preferred_element_type=F32)
        pb = jnp.dot(yb_ref[rows, :], wpb, preferred_element_type=F32)
        merged = _sigmoid(ga_ref[rows, :].astype(F32)) * pa + _sigmoid(gb_ref[rows, :].astype(F32)) * pb
        h = x_ref[rows, :] + jnp.dot(merged.astype(BF16), wo, preferred_element_type=F32)
        ms = jnp.mean(h * h, axis=-1, keepdims=True)
        o_ref[rows, :] = h * lax.rsqrt(ms + NORM_EPS) * fnw_ref[...]


def _merge(x2, ya, yb, pbf, w_pa, w_pb, w_out, fnw):
    t = x2.shape[0]
    assert t % TM_OUT == 0 and t // TM_OUT >= MERGE_X_BUFS - 1
    return pl.pallas_call(
        _merge_kernel,
        grid=(t // TM_OUT,),
        in_specs=[
            pl.BlockSpec(memory_space=pl.ANY),
            pl.BlockSpec((TM_OUT, SSM_WIDTH), lambda i: (i, 0)),
            pl.BlockSpec((TM_OUT, HGRN_WIDTH), lambda i: (i, 0)),
            pl.BlockSpec((TM_OUT, D_MODEL), lambda i: (i, 0)),
            pl.BlockSpec((TM_OUT, D_MODEL), lambda i: (i, 1)),
            _const_spec((SSM_WIDTH, D_MODEL)), _const_spec((HGRN_WIDTH, D_MODEL)),
            _const_spec((D_MODEL, D_MODEL)), _const_spec((1, D_MODEL)),
        ],
        out_specs=pl.BlockSpec((TM_OUT, D_MODEL), lambda i: (i, 0)),
        out_shape=jax.ShapeDtypeStruct((t, D_MODEL), F32),
        scratch_shapes=[pltpu.VMEM((MERGE_X_BUFS, TM_OUT, D_MODEL), F32),
                        pltpu.SemaphoreType.DMA((MERGE_X_BUFS,))],
        compiler_params=pltpu.CompilerParams(
            dimension_semantics=("arbitrary",), vmem_limit_bytes=VMEM_LIMIT),
        name="merge",
    )(x2, ya, yb, pbf, pbf, w_pa, w_pb, w_out, fnw)


def kernel(x, norm_w, w_in, ssm_lambda_re, ssm_lambda_im, ssm_b_re, ssm_b_im, ssm_c_re, ssm_c_im, ssm_d,
           ssm_log_dt, ssm_w_glu, ssm_b_glu, hgrn_lb_logits, hgrn_norm_w, w_proj_a, w_proj_b, w_out,
           final_norm_w):
    batch, seqlen, _ = x.shape
    assert norm_w.shape[0] == 1, "single-layer block"
    assert seqlen % S5_TB == 0 and seqlen % HG_TB == 0 and HG_TB % S5_SEG_ROWS == 0
    x2 = x.reshape(batch * seqlen, D_MODEL)
    u32, pbf, yb = _proj_hgrn(x2, norm_w[0][None, :], w_in[0], hgrn_lb_logits,
                              hgrn_norm_w[0][None, :], seqlen)

    kb, winp, woutp, ptre, ptim = _s5_prep(ssm_lambda_re[0], ssm_lambda_im[0], ssm_b_re[0], ssm_b_im[0],
                                           ssm_c_re[0], ssm_c_im[0], ssm_log_dt[0])
    ya = _s5(u32, pbf, kb, winp, woutp, ptre, ptim, ssm_d[0].reshape(1, SSM_WIDTH),
             ssm_w_glu[0], ssm_b_glu[0][None, :], batch, seqlen)
    out = _merge(x2, ya, yb, pbf, w_proj_a[0], w_proj_b[0], w_out[0], final_norm_w[None, :])
    return out.reshape(batch, seqlen, D_MODEL)
```

```python
import functools

import jax
import jax.numpy as jnp
from jax import lax
from jax.experimental import pallas as pl
from jax.experimental.pallas import tpu as pltpu

F32 = jnp.float32
BF16 = jnp.bfloat16

D_MODEL = 1024
SSM_WIDTH = 512
SSM_GROUP = 16
SSM_GROUPS = 32
SSM_STATE = 64
HGRN_WIDTH = 512
HEAD_DIM = 128
HEADS = 4
NORM_EPS = 1e-6
LAMBDA_RE_MAX = -1e-4

IN_WIDTH = 2 * SSM_WIDTH + 5 * HGRN_WIDTH + 2 * D_MODEL
COL_ZA, COL_H, COL_G = SSM_WIDTH, 2 * SSM_WIDTH, 2 * SSM_WIDTH + 5 * HGRN_WIDTH
PH_WIDTH = 5 * HGRN_WIDTH
PH_Q, PH_F, PH_I, PH_OG, PH_ZB = 0, 1, 2, 3, 4
PBF_WIDTH = 2 * D_MODEL + SSM_WIDTH
PBF_ZA = 4

LANES = 128
SUBLANES = 8
OCT = LANES // SSM_GROUP
N_OCT = SSM_GROUPS // OCT
OCT_STATE = OCT * SSM_STATE

S5_Q = 4
S5_QP = S5_Q // 2
S5_TB = 2048
S5_NCH = S5_TB // S5_Q
S5_SEG = SUBLANES
S5_NV = S5_NCH // S5_SEG
S5_PT_ROWS = S5_NV + SUBLANES
S5_SEG_ROWS = S5_Q * S5_NV
S5_TAIL_SEGS = 2
S5_SEG_PITCH = S5_SEG_ROWS + SUBLANES
HG_TB = 512
HG_CH = 64
HG_SUB = 16
W_CHUNK = 512
TM_OUT = 1024
MERGE_ROWS = 512
MERGE_X_BUFS = 3

V7X_VMEM_BYTES = 64 * 1024 * 1024
VMEM_LIMIT = V7X_VMEM_BYTES * 7 // 8


def _sigmoid(x):
    return 0.5 * jnp.tanh(0.5 * x) + 0.5


def _const_spec(shape):
    nd = len(shape)
    return pl.BlockSpec(shape, lambda *_: (0,) * nd, pipeline_mode=pl.Buffered(1))


def _disc(lam_re, lam_im, log_dt):
    return jnp.minimum(lam_re, LAMBDA_RE_MAX), lam_im, jnp.exp(log_dt)


def _cpow(lr, li, dt, k):
    mag = jnp.exp(k * (lr * dt))
    ang = k * (li * dt)
    return mag * jnp.cos(ang), mag * jnp.sin(ang)


def _split_bf16(x):
    hi = x.astype(BF16)
    return hi, (x - hi.astype(F32)).astype(BF16)


def _dot3(a, b_hi, b_lo):
    a_hi, a_lo = _split_bf16(a)
    d = lambda p, q: jnp.dot(p, q, preferred_element_type=F32)
    return d(a_hi, b_hi) + (d(a_hi, b_lo) + d(a_lo, b_hi))


def _cmul(a_re, a_im, b_re, b_im):
    return a_re * b_re - a_im * b_im, a_re * b_im + a_im * b_re


def _s5_prep_kernel(lam_c, b_t, c_n, lam_n, lam_r,
                    kb_ref, win_ref, wout_ref, ptre_ref, ptim_ref, pw_s, pm_s, xk_s):
    @pl.when(pl.program_id(0) == 0)
    def _():
        lr, li, dt = _disc(lam_c[0], lam_c[1], lam_c[2])
        ab_re, ab_im = _cpow(lr, li, dt, 1.0)
        den = lr * lr + li * li
        nr = ab_re - 1.0
        pw_s[0] = (nr * lr + ab_im * li) / den
        pw_s[1] = (ab_im * lr - nr * li) / den
        pw_s[2] = jnp.ones_like(ab_re)
        pw_s[3] = jnp.zeros_like(ab_re)
        pw_s[4] = ab_re
        pw_s[5] = ab_im
        m_re, m_im = _cpow(*_disc(lam_n[0], lam_n[1], lam_n[2]), 1.0)
        pm_s[0] = m_re
        pm_s[1] = m_im
        pm_s[2] = m_re
        pm_s[3] = m_im
        j = lax.broadcasted_iota(jnp.int32, (S5_PT_ROWS, OCT_STATE), 0).astype(F32) * float(S5_Q)
        for a in range(N_OCT):
            t_re, t_im = _cpow(*_disc(lam_r[0, a:a + 1, :], lam_r[1, a:a + 1, :], lam_r[2, a:a + 1, :]), j)
            ptre_ref[a] = t_re
            ptim_ref[a] = t_im

    col = lax.broadcasted_iota(jnp.int32, (SSM_GROUP, SSM_WIDTH), 1)

    def strip(g, carry):
        rows = pl.ds(pl.multiple_of(g * SSM_GROUP, SSM_GROUP), SSM_GROUP)
        grow = pl.ds(g, 1)
        coef_re, coef_im = pw_s[0, grow, :], pw_s[1, grow, :]
        p_re, p_im = pw_s[2, grow, :], pw_s[3, grow, :]
        bb_re, bb_im = _cmul(coef_re, coef_im, b_t[0, rows, :], b_t[1, rows, :])
        x_re, x_im = _cmul(bb_re, bb_im, p_re, p_im)
        xk_s[0, rows, :] = x_re[:, :SSM_STATE]
        xk_s[1, rows, :] = x_im[:, :SSM_STATE]
        a = g // OCT
        lrows = pl.ds(pl.multiple_of((g % OCT) * SSM_GROUP, SSM_GROUP), SSM_GROUP)
        m_in = col // SSM_STATE == g % OCT
        win_ref[0, a, 0, lrows, :] = jnp.concatenate(
            [jnp.where(m_in, x_re, 0.0), jnp.where(m_in, x_im, 0.0)], axis=1).astype(BF16)
        return carry

    lax.fori_loop(0, SSM_GROUPS, strip, 0, unroll=4)

    cr, ci = c_n[0], c_n[1]
    kfull = _dot3(xk_s[0], *_split_bf16(cr)) - _dot3(xk_s[1], *_split_bf16(ci))
    rowk = lax.broadcasted_iota(jnp.int32, (LANES, LANES), 0)
    colk = lax.broadcasted_iota(jnp.int32, (LANES, LANES), 1)
    m_k = rowk // SSM_GROUP == colk // SSM_GROUP
    for a in range(N_OCT):
        ls = slice(LANES * a, LANES * (a + 1))
        kb_ref[0, a] = jnp.where(m_k, kfull[ls, ls], 0.0).astype(BF16)

    p1_re, p1_im = pm_s[0], pm_s[1]
    w_re, w_im = _cmul(cr, ci, p1_re, p1_im)
    rowo = lax.broadcasted_iota(jnp.int32, (OCT_STATE, LANES), 0)
    colo = lax.broadcasted_iota(jnp.int32, (OCT_STATE, LANES), 1)
    m_out = rowo // SSM_STATE == colo // SSM_GROUP
    for a in range(N_OCT):
        ls = slice(LANES * a, LANES * (a + 1))
        wout_ref[0, a, :OCT_STATE, :] = jnp.where(m_out, jnp.tile(w_re[:, ls], (OCT, 1)), 0.0).astype(BF16)
        wout_ref[0, a, OCT_STATE:, :] = jnp.where(m_out, jnp.tile(-w_im[:, ls], (OCT, 1)), 0.0).astype(BF16)

    pw_s[2], pw_s[3] = _cmul(pw_s[2], pw_s[3], pw_s[4], pw_s[5])
    pm_s[0], pm_s[1] = _cmul(p1_re, p1_im, pm_s[2], pm_s[3])


def _s5_prep(lam_re, lam_im, b_re, b_im, c_re, c_im, log_dt):
    g, n, p = SSM_GROUPS, SSM_STATE, SSM_GROUP
    lam = jnp.stack([lam_re, lam_im, jnp.broadcast_to(log_dt[:, None], (g, n))])
    b = jnp.stack([b_re, b_im])
    c = jnp.stack([c_re, c_im])
    args = (jnp.tile(lam, (1, 1, OCT)),
            jnp.tile(b.transpose(0, 1, 3, 2).reshape(2, g * p, n), (1, 1, OCT)),
            c.transpose(0, 3, 1, 2).reshape(2, n, g * p),
            jnp.repeat(lam.transpose(0, 2, 1), p, axis=2),
            lam.reshape(3, N_OCT, OCT_STATE))
    mat = (S5_QP, N_OCT, 2, LANES, 2 * OCT_STATE)
    blk = (1, N_OCT, 1, LANES, 2 * OCT_STATE)
    tab = (N_OCT, S5_PT_ROWS, OCT_STATE)
    kb, win, woutp, ptre, ptim = pl.pallas_call(
        _s5_prep_kernel,
        grid=(S5_Q,),
        in_specs=[_const_spec(a.shape) for a in args],
        out_specs=[
            pl.BlockSpec((1, N_OCT, LANES, LANES), lambda t: (t, 0, 0, 0)),
            pl.BlockSpec(blk, lambda t: ((S5_Q - 1 - t) // 2, 0, (S5_Q - 1 - t) % 2, 0, 0)),
            pl.BlockSpec((1, N_OCT, 2 * OCT_STATE, LANES), lambda t: (t // 2, 0, 0, t % 2)),
            pl.BlockSpec(tab, lambda t: (0, 0, 0)),
            pl.BlockSpec(tab, lambda t: (0, 0, 0)),
        ],
        out_shape=[
            jax.ShapeDtypeStruct((S5_Q, N_OCT, LANES, LANES), BF16),
            jax.ShapeDtypeStruct(mat, BF16),
            jax.ShapeDtypeStruct((S5_QP, N_OCT, 2 * OCT_STATE, 2 * LANES), BF16),
            jax.ShapeDtypeStruct(tab, F32),
            jax.ShapeDtypeStruct(tab, F32),
        ],
        scratch_shapes=[pltpu.VMEM((6, SSM_GROUPS, SSM_WIDTH), F32),
                        pltpu.VMEM((4, SSM_STATE, SSM_WIDTH), F32),
                        pltpu.VMEM((2, SSM_WIDTH, SSM_STATE), F32)],
        compiler_params=pltpu.CompilerParams(
            dimension_semantics=("arbitrary",), vmem_limit_bytes=VMEM_LIMIT),
        name="s5_prep",
    )(*args)

    winp = win.reshape(S5_QP, N_OCT, 2 * LANES, 2 * OCT_STATE)
    return kb, winp, woutp, ptre, ptim


def _gelu_tanh(x):
    c = 0.7978845608028654
    return x * (0.5 * (1.0 + jnp.tanh(x * (c + (c * 0.044715) * (x * x)))))


def _s5_kernel(u0_ref, u1_ref, u2_ref, u3_ref, z_ref, kb_ref, winp_ref, woutp_ref, ptre_ref, ptim_ref,
               d_ref, wglu_ref, bglu_ref, o_ref, y0_s, y1_s, y2_s, y3_s, cre_scr, cim_scr):
    u_refs = (u0_ref, u1_ref, u2_ref, u3_ref)
    y_scrs = (y0_s, y1_s, y2_s, y3_s)

    @pl.when(pl.program_id(1) == 0)
    def _():
        cre_scr[...] = jnp.zeros_like(cre_scr)
        cim_scr[...] = jnp.zeros_like(cim_scr)

    def tok(a, s):
        return jnp.concatenate(
            [u_refs[a][pl.ds(s + S5_Q * v, S5_SEG, stride=S5_SEG_PITCH), :] for v in range(S5_NV)], axis=0)

    xp = [[jnp.concatenate([tok(a, 2 * sp), tok(a, 2 * sp + 1)], axis=1).astype(BF16) for a in range(N_OCT)]
          for sp in range(S5_QP)]

    def cmul_add(b_re, b_im, m_re, m_im, x_re, x_im):
        return b_re + m_re * x_re - m_im * x_im, b_im + m_re * x_im + m_im * x_re

    hs = []
    for a in range(N_OCT):
        acc = None
        for sp in range(S5_QP):
            part = jnp.dot(xp[sp][a], winp_ref[sp, a], preferred_element_type=F32)
            acc = part if acc is None else acc + part
        blk = lambda v: (acc[S5_SEG * v:S5_SEG * (v + 1), :OCT_STATE], acc[S5_SEG * v:S5_SEG * (v + 1), OCT_STATE:])
        m_re, m_im = ptre_ref[a, 1:2, :], ptim_ref[a, 1:2, :]
        loc = [blk(0)]
        for v in range(1, S5_NV):
            loc.append(cmul_add(*blk(v), m_re, m_im, *loc[-1]))
        l_re, l_im = ptre_ref[a, S5_NV:S5_NV + 1, :], ptim_ref[a, S5_NV:S5_NV + 1, :]
        c_re, c_im = cre_scr[a:a + 1, :], cim_scr[a:a + 1, :]
        carry = []
        for r in range(S5_SEG):
            carry.append((c_re, c_im))
            c_re, c_im = cmul_add(loc[-1][0][r:r + 1, :], loc[-1][1][r:r + 1, :], l_re, l_im, c_re, c_im)
        cre_scr[a:a + 1, :] = c_re
        cim_scr[a:a + 1, :] = c_im
        cs_re = jnp.concatenate([c[0] for c in carry], axis=0)
        cs_im = jnp.concatenate([c[1] for c in carry], axis=0)
        ent = [(cs_re, cs_im)]
        for v in range(S5_NV - 1):
            ent.append(cmul_add(*loc[v], ptre_ref[a, v + 1:v + 2, :], ptim_ref[a, v + 1:v + 2, :], cs_re, cs_im))
        hs.append(jnp.concatenate([jnp.concatenate([e[0] for e in ent], axis=0),
                                   jnp.concatenate([e[1] for e in ent], axis=0)], axis=1).astype(BF16))

    def pair_tile(d, a):
        below = kb_ref[2 * d - 1, a] if d > 0 else jnp.zeros((LANES, LANES), BF16)
        return jnp.concatenate([jnp.concatenate([kb_ref[2 * d, a], kb_ref[2 * d + 1, a]], axis=1),
                                jnp.concatenate([below, kb_ref[2 * d, a]], axis=1)], axis=0)

    for a in range(N_OCT):
        tiles = [pair_tile(d, a) for d in range(S5_QP)]
        for tp in range(S5_QP):
            acc = jnp.dot(hs[a], woutp_ref[tp, a], preferred_element_type=F32)
            for sp in range(tp + 1):
                acc = acc + jnp.dot(xp[sp][a], tiles[tp - sp], preferred_element_type=F32)
            for v in range(S5_NV):
                rows = slice(S5_SEG * v, S5_SEG * (v + 1))
                y_scrs[a][pl.ds(2 * tp + S5_Q * v, S5_SEG, stride=S5_SEG_PITCH), :] = acc[rows, :LANES]
                y_scrs[a][pl.ds(2 * tp + 1 + S5_Q * v, S5_SEG, stride=S5_SEG_PITCH), :] = acc[rows, LANES:]
        d_a = d_ref[:, LANES * a:LANES * (a + 1)]
        for r in range(S5_SEG):
            prow = slice(S5_SEG_PITCH * r, S5_SEG_PITCH * r + S5_SEG_ROWS)
            y_scrs[a][prow, :] = _gelu_tanh(y_scrs[a][prow, :] + d_a * u_refs[a][prow, :])

    wglu = wglu_ref[...].astype(BF16)
    for r0 in range(0, S5_SEG, S5_TAIL_SEGS):
        segs = range(r0, r0 + S5_TAIL_SEGS)
        trow = slice(S5_SEG_ROWS * r0, S5_SEG_ROWS * (r0 + S5_TAIL_SEGS))
        y = jnp.concatenate(
            [jnp.concatenate([ref[S5_SEG_PITCH * r:S5_SEG_PITCH * r + S5_SEG_ROWS, :] for r in segs], axis=0)
             for ref in y_scrs], axis=1)
        hy = 0.5 * y
        hgate = jnp.dot(hy.astype(BF16), wglu, preferred_element_type=F32) + 0.5 * bglu_ref[...]
        hz = 0.5 * z_ref[trow, :].astype(F32)
        o_ref[trow, :] = (hy * hz * ((1.0 + jnp.tanh(hgate)) * (1.0 + jnp.tanh(hz)))).astype(BF16)


def _s5(u32, pbf, kb, winp, woutp, ptre, ptim, d_row, w_glu, b_glu, batch, seqlen):
    nb = seqlen // S5_TB
    t = batch * seqlen
    u_tile = lambda a: pl.BlockSpec((S5_SEG * S5_SEG_PITCH, LANES), lambda b, i, a=a: (b * nb + i, a))
    return pl.pallas_call(
        _s5_kernel,
        grid=(batch, nb),
        in_specs=[u_tile(a) for a in range(N_OCT)] + [
            pl.BlockSpec((S5_TB, SSM_WIDTH), lambda b, i: (b * nb + i, PBF_ZA)),
            _const_spec(kb.shape), _const_spec(winp.shape), _const_spec(woutp.shape),
            _const_spec(ptre.shape), _const_spec(ptim.shape),
            _const_spec((1, SSM_WIDTH)), _const_spec((SSM_WIDTH, SSM_WIDTH)), _const_spec((1, SSM_WIDTH)),
        ],
        out_specs=pl.BlockSpec((S5_TB, SSM_WIDTH), lambda b, i: (b * nb + i, 0)),
        out_shape=jax.ShapeDtypeStruct((t, SSM_WIDTH), BF16),
        scratch_shapes=[pltpu.VMEM((S5_SEG * S5_SEG_PITCH, LANES), F32)] * N_OCT + [
            pltpu.VMEM((SUBLANES, OCT_STATE), F32),
            pltpu.VMEM((SUBLANES, OCT_STATE), F32),
        ],
        compiler_params=pltpu.CompilerParams(
            dimension_semantics=("arbitrary", "arbitrary"), vmem_limit_bytes=VMEM_LIMIT),
        name="s5",
    )(u32, u32, u32, u32, pbf, kb, winp, woutp, ptre, ptim, d_row, w_glu, b_glu)


def _stage_weights(w_hbm, wbf_s, stage_s, sem):
    n_chunks = IN_WIDTH // W_CHUNK

    def chunk_copy(c):
        return pltpu.make_async_copy(w_hbm.at[:, pl.ds(c * W_CHUNK, W_CHUNK)], stage_s.at[c % 2], sem.at[c % 2])

    chunk_copy(0).start()
    for c in range(n_chunks):
        if c + 1 < n_chunks:
            chunk_copy(c + 1).start()
        chunk_copy(c).wait()
        wbf_s[:, c * W_CHUNK:(c + 1) * W_CHUNK] = stage_s[c % 2].astype(BF16)


def _proj_hgrn_kernel(x_ref, nw_ref, w_hbm, lbl_ref, hnw_ref, ou_ref, or_ref, o_ref,
                      wbf_s, stage_s, stage_sem, ph_s, q0_s, qc_s, kt_s, k0_s, k1_s, k2_s, ke_s, v_s, dec_s, oacc_s,
                      st_s, upd_s, sc_s, am_s,
                      *, blocks_per_seq):
    @pl.when(pl.program_id(0) == 0)
    def _():
        _stage_weights(w_hbm, wbf_s, stage_s, stage_sem)

    @pl.when(pl.program_id(0) % blocks_per_seq == 0)
    def _():
        st_s[...] = jnp.zeros_like(st_s)

    x = x_ref[...]
    ms = jnp.mean(x * x, axis=-1, keepdims=True)
    xn = (x * lax.rsqrt(ms + NORM_EPS) * nw_ref[...]).astype(BF16)
    proj = lambda lo, hi: jnp.dot(xn, wbf_s[:, lo:hi], preferred_element_type=F32)
    ph_s[...] = proj(COL_H, COL_G)
    u = proj(0, COL_ZA)
    for r in range(HG_TB // S5_SEG_ROWS):
        ou_ref[S5_SEG_PITCH * r:S5_SEG_PITCH * r + S5_SEG_ROWS, :] = u[S5_SEG_ROWS * r:S5_SEG_ROWS * (r + 1), :]
        ou_ref[S5_SEG_PITCH * r + S5_SEG_ROWS:S5_SEG_PITCH * (r + 1), :] = jnp.zeros((SUBLANES, SSM_WIDTH), F32)
    or_ref[:, :2 * D_MODEL] = proj(COL_G, IN_WIDTH).astype(BF16)
    or_ref[:, 2 * D_MODEL:] = proj(COL_ZA, COL_H).astype(BF16)
    col = lambda k: ph_s[:, HGRN_WIDTH * k:HGRN_WIDTH * (k + 1)]

    logits = lbl_ref[...]
    e = jnp.exp(logits - jnp.max(logits, axis=0, keepdims=True))
    lb = (e / jnp.sum(e, axis=0, keepdims=True))[0:1, :]

    q = col(PH_Q)
    qf = q * _sigmoid(q)
    forget = lb + (1.0 - lb) * _sigmoid(col(PH_F))
    lf = jnp.log(forget)
    key = 1.0 - forget

    row = lax.broadcasted_iota(jnp.int32, (HG_TB, HGRN_WIDTH), 0)
    r_sub = row % HG_SUB
    r_ch = row % HG_CH

    def down(x, d):
        return pltpu.roll(x, d, 0)

    def up(x, d):
        return pltpu.roll(x, HG_TB - d, 0)

    a = lf
    d = 1
    while d < HG_SUB:
        a = a + jnp.where(r_sub >= d, down(a, d), 0.0)
        d *= 2
    a3 = a.reshape(HG_TB // HG_SUB, HG_SUB, HGRN_WIDTH)
    tsub = jnp.broadcast_to(a3[:, HG_SUB - 1:HG_SUB, :], a3.shape).reshape(HG_TB, HGRN_WIDTH)
    n_sub = HG_CH // HG_SUB
    prev = [jnp.where(r_ch >= HG_SUB * k, down(tsub, HG_SUB * k), 0.0) for k in range(1, n_sub)]
    nxt = [jnp.where(r_ch < HG_CH - HG_SUB * k, up(tsub, HG_SUB * k), 0.0) for k in range(1, n_sub)]
    eprev = prev[0] + prev[1] + prev[2]
    enext = nxt[0] + nxt[1] + nxt[2]
    suf = tsub - a

    hsub = 0.5 * tsub
    q0_s[...] = (qf * jnp.exp(a - hsub)).astype(BF16)
    qc_s[...] = (qf * jnp.exp(a + eprev)).astype(BF16)
    kt_s[...] = (key * jnp.exp(hsub - a)).astype(BF16)
    k0 = key * jnp.exp(suf)
    e0, e1, e2 = (jnp.exp(0.5 * n) for n in nxt)
    k0_s[...] = (k0 * e0).astype(BF16)
    k1 = k0 * (e0 * e0)
    k1_s[...] = (k1 * e1).astype(BF16)
    k2 = k1 * (e1 * e1)
    k2_s[...] = (k2 * e2).astype(BF16)
    ke_s[...] = (k2 * (e2 * e2)).astype(BF16)
    dec_s[...] = jnp.exp(eprev + tsub + enext)
    v_s[...] = col(PH_I).astype(BF16)

    tq = lax.broadcasted_iota(jnp.int32, (HG_CH, n_sub * HG_CH), 0)
    cc = lax.broadcasted_iota(jnp.int32, (HG_CH, n_sub * HG_CH), 1)
    cls = cc // HG_CH
    ts = cc % HG_CH
    bi = tq // HG_SUB
    bj = ts // HG_SUB
    mask4 = ((cls == 0) & (bi == bj) & (ts <= tq)) | ((cls > 0) & ((bi - bj) == cls))

    n_ch = HG_TB // HG_CH
    half = n_sub * HG_CH // 2
    units = [(c, h, slice(c * HG_CH, (c + 1) * HG_CH), slice(HEAD_DIM * h, HEAD_DIM * (h + 1)))
             for c in range(n_ch) for h in range(HEADS)]
    for c, h, rows, ls in units:
        kcat = jnp.concatenate([kt_s[rows, ls], k0_s[rows, ls], k1_s[rows, ls], k2_s[rows, ls]], axis=0)
        sc_s[c * HEADS + h] = lax.dot_general(q0_s[rows, ls], kcat, (((1,), (1,)), ((), ())),
                                              preferred_element_type=F32)
    for c, h, rows, ls in units:
        upd_s[c * HEADS + h] = lax.dot_general(v_s[rows, ls], ke_s[rows, ls], (((0,), (0,)), ((), ())),
                                               preferred_element_type=F32)
    for c, h, rows, ls in units:
        sc = jnp.where(mask4, sc_s[c * HEADS + h], 0.0)
        am_s[c * HEADS + h] = (sc[:, :half] + sc[:, half:]).astype(BF16)
    for c, h, rows, ls in units:
        vv = v_s[rows, ls]
        oacc_s[rows, ls] = jnp.dot(am_s[c * HEADS + h], jnp.concatenate([vv, vv], axis=0),
                                   preferred_element_type=F32)
    for h in range(HEADS):
        ls = slice(HEAD_DIM * h, HEAD_DIM * (h + 1))
        st = st_s[h]
        for c in range(n_ch):
            rows = slice(c * HG_CH, (c + 1) * HG_CH)
            oacc_s[rows, ls] += lax.dot_general(qc_s[rows, ls], st.astype(BF16), (((1,), (1,)), ((), ())),
                                                preferred_element_type=F32)
            st = st * dec_s[c * HG_CH:c * HG_CH + 1, ls] + upd_s[c * HEADS + h]
        st_s[h] = st

    o = oacc_s[...] * _sigmoid(col(PH_OG))
    parts = []
    for h in range(HEADS):
        oh = o[:, HEAD_DIM * h:HEAD_DIM * (h + 1)]
        ms = jnp.mean(oh * oh, axis=-1, keepdims=True)
        parts.append(oh * lax.rsqrt(ms + NORM_EPS))
    o = jnp.concatenate(parts, axis=1) * hnw_ref[...]
    hz = 0.5 * col(PH_ZB)
    o_ref[...] = (o * (hz * (1.0 + jnp.tanh(hz)))).astype(BF16)


def _proj_hgrn(x2, norm_w, w, lb_logits, hgrn_norm_w, seqlen):
    t = x2.shape[0]
    wide = (HG_TB, HGRN_WIDTH)
    units = HG_TB // HG_CH * HEADS
    n_cls = HG_CH // HG_SUB
    row_blk = lambda w: pl.BlockSpec((HG_TB, w), lambda i: (i, 0))
    u_rows = HG_TB // S5_SEG_ROWS * S5_SEG_PITCH
    return pl.pallas_call(
        functools.partial(_proj_hgrn_kernel, blocks_per_seq=seqlen // HG_TB),
        grid=(t // HG_TB,),
        in_specs=[
            row_blk(D_MODEL), _const_spec((1, D_MODEL)),
            pl.BlockSpec(memory_space=pl.ANY),
            _const_spec(lb_logits.shape), _const_spec((1, HGRN_WIDTH)),
        ],
        out_specs=[pl.BlockSpec((u_rows, SSM_WIDTH), lambda i: (i, 0)), row_blk(PBF_WIDTH), row_blk(HGRN_WIDTH)],
        out_shape=[
            jax.ShapeDtypeStruct((t // HG_TB * u_rows, SSM_WIDTH), F32),
            jax.ShapeDtypeStruct((t, PBF_WIDTH), BF16),
            jax.ShapeDtypeStruct((t, HGRN_WIDTH), BF16),
        ],
        scratch_shapes=[
            pltpu.VMEM((D_MODEL, IN_WIDTH), BF16), pltpu.VMEM((2, D_MODEL, W_CHUNK), F32),
            pltpu.SemaphoreType.DMA((2,)),
            pltpu.VMEM((HG_TB, PH_WIDTH), F32)] + [pltpu.VMEM(wide, BF16)] * 8 + [
            pltpu.VMEM(wide, F32), pltpu.VMEM(wide, F32),
            pltpu.VMEM((HEADS, HEAD_DIM, HEAD_DIM), F32),
            pltpu.VMEM((units, HEAD_DIM, HEAD_DIM), F32),
            pltpu.VMEM((units, HG_CH, n_cls * HG_CH), F32),
            pltpu.VMEM((units, HG_CH, n_cls * HG_CH // 2), BF16)],
        compiler_params=pltpu.CompilerParams(
            dimension_semantics=("arbitrary",), vmem_limit_bytes=VMEM_LIMIT),
        name="proj_hgrn",
    )(x2, norm_w, w, lb_logits, hgrn_norm_w)


def _x_block_copy(src_hbm, buf, sem, step):
    slot = step % MERGE_X_BUFS
    rows = pl.ds(step * TM_OUT, TM_OUT)
    return pltpu.make_async_copy(src_hbm.at[rows, pl.ds(0, buf.shape[-1])], buf.at[slot], sem.at[slot])


def _merge_kernel(x_hbm, ya_ref, yb_ref, g_hbm, wpa_ref, wpb_ref, wo_ref, fnw_ref, o_ref, x_s, g_s, x_sem, g_sem):
    rings = ((x_hbm, x_s, x_sem), (g_hbm, g_s, g_sem))
    step, n_steps = pl.program_id(0), pl.num_programs(0)

    @pl.when(step == 0)
    def _():
        for k in range(MERGE_X_BUFS - 1):
            for ring in rings:
                _x_block_copy(*ring, k).start()

    ahead = step + (MERGE_X_BUFS - 1)

    @pl.when(ahead < n_steps)
    def _():
        for ring in rings:
            _x_block_copy(*ring, ahead).start()

    for ring in rings:
        _x_block_copy(*ring, step).wait()
    x_ref, g_ref = x_s.at[step % MERGE_X_BUFS], g_s.at[step % MERGE_X_BUFS]
    wpa, wpb, wo = wpa_ref[...].astype(BF16), wpb_ref[...].astype(BF16), wo_ref[...].astype(BF16)
    for r in range(TM_OUT // MERGE_ROWS):
        rows = slice(MERGE_ROWS * r, MERGE_ROWS * (r + 1))
        pa = jnp.dot(ya_ref[rows, :], wpa, preferred_element_type=F32)
        pb = jnp.dot(yb_ref[rows, :], wpb, preferred_element_type=F32)
        merged = (_sigmoid(g_ref[rows, :D_MODEL].astype(F32)) * pa
                  + _sigmoid(g_ref[rows, D_MODEL:].astype(F32)) * pb)
        h = x_ref[rows, :] + jnp.dot(merged.astype(BF16), wo, preferred_element_type=F32)
        ms = jnp.mean(h * h, axis=-1, keepdims=True)
        o_ref[rows, :] = h * lax.rsqrt(ms + NORM_EPS) * fnw_ref[...]


def _merge(x2, ya, yb, pbf, w_pa, w_pb, w_out, fnw):
    t = x2.shape[0]
    assert t % TM_OUT == 0 and t // TM_OUT >= MERGE_X_BUFS - 1
    return pl.pallas_call(
        _merge_kernel,
        grid=(t // TM_OUT,),
        in_specs=[
            pl.BlockSpec(memory_space=pl.ANY),
            pl.BlockSpec((TM_OUT, SSM_WIDTH), lambda i: (i, 0)),
            pl.BlockSpec((TM_OUT, HGRN_WIDTH), lambda i: (i, 0)),
            pl.BlockSpec(memory_space=pl.ANY),
            _const_spec((SSM_WIDTH, D_MODEL)), _const_spec((HGRN_WIDTH, D_MODEL)),
            _const_spec((D_MODEL, D_MODEL)), _const_spec((1, D_MODEL)),
        ],
        out_specs=pl.BlockSpec((TM_OUT, D_MODEL), lambda i: (i, 0)),
        out_shape=jax.ShapeDtypeStruct((t, D_MODEL), F32),
        scratch_shapes=[pltpu.VMEM((MERGE_X_BUFS, TM_OUT, D_MODEL), F32),
                        pltpu.VMEM((MERGE_X_BUFS, TM_OUT, 2 * D_MODEL), BF16),
                        pltpu.SemaphoreType.DMA((MERGE_X_BUFS,)), pltpu.SemaphoreType.DMA((MERGE_X_BUFS,))],
        compiler_params=pltpu.CompilerParams(
            dimension_semantics=("arbitrary",), vmem_limit_bytes=VMEM_LIMIT),
        name="merge",
    )(x2, ya, yb, pbf, w_pa, w_pb, w_out, fnw)


def kernel(x, norm_w, w_in, ssm_lambda_re, ssm_lambda_im, ssm_b_re, ssm_b_im, ssm_c_re, ssm_c_im, ssm_d,
           ssm_log_dt, ssm_w_glu, ssm_b_glu, hgrn_lb_logits, hgrn_norm_w, w_proj_a, w_proj_b, w_out,
           final_norm_w):
    batch, seqlen, _ = x.shape
    assert norm_w.shape[0] == 1, "single-layer block"
    assert seqlen % S5_TB == 0 and seqlen % HG_TB == 0 and HG_TB % S5_SEG_ROWS == 0
    x2 = x.reshape(batch * seqlen, D_MODEL)
    u32, pbf, yb = _proj_hgrn(x2, norm_w[0][None, :], w_in[0], hgrn_lb_logits,
                              hgrn_norm_w[0][None, :], seqlen)

    kb, winp, woutp, ptre, ptim = _s5_prep(ssm_lambda_re[0], ssm_lambda_im[0], ssm_b_re[0], ssm_b_im[0],
                                           ssm_c_re[0], ssm_c_im[0], ssm_log_dt[0])
    ya = _s5(u32, pbf, kb, winp, woutp, ptre, ptim, ssm_d[0].reshape(1, SSM_WIDTH),
             ssm_w_glu[0], ssm_b_glu[0][None, :], batch, seqlen)
    out = _merge(x2, ya, yb, pbf, w_proj_a[0], w_proj_b[0], w_out[0], final_norm_w[None, :])
    return out.reshape(batch, seqlen, D_MODEL)
```
